```python
import jax, jax.numpy as jnp
from jax import lax
import numpy as np


D_MODEL = 2048
BATCH = 8
SEQ = 8192
DEPTH = 4

CTX_LEN = 256
GRID_W = 64
N_MIXERS = 3
MIXER_NA = 0
MIXER_GMLP = 1
MIXER_CONV = 2
NA_HEADS = 16
HEAD_DIM = D_MODEL // NA_HEADS
WIN_H = 8
WIN_W = 16
QROWS = 2
GM_WIDTH = 2 * D_MODEL
GM_GROUPS = 16
GM_GROUP_DIM = GM_WIDTH // GM_GROUPS
GM_CHUNK = 128
FFN_DIM = 5504
N_MOD = 6
EPS = 1e-6

kernel_name = 'hybrid_na_gmlp_shortconv_dit_block'


def rms_norm(x, g):
    x32 = x.astype(jnp.float32)
    y = x32 * lax.rsqrt(jnp.mean(x32 * x32, axis=-1, keepdims=True) + EPS)
    return (y * g.astype(jnp.float32)).astype(x.dtype)


def adaln(cond, w, b):
    return jnp.split(jax.nn.silu(cond) @ w + b, N_MOD, axis=-1)


def modulate(x, shift, scale):
    return x * (1.0 + scale) + shift


def dwconv3(z, w, b=None):
    zp = jnp.pad(z, ((0, 0), (1, 1), (0, 0)))
    y = zp[:, :-2] * w[0] + zp[:, 1:-1] * w[1] + zp[:, 2:] * w[2]
    return y if b is None else y + b


def neighbourhood_attention(h, hc, w_qkv, q_g, k_g, rpb, w_o, ctx_out):
    bn, n, d = h.shape
    rows = n // GRID_W
    kh = min(WIN_H, rows)
    nb = min(kh + QROWS - 1, rows)
    n_blk = rows // QROWS
    qw = QROWS * GRID_W
    nk = nb * GRID_W
    scale = HEAD_DIM ** -0.5

    def project(t):
        q, k, v = jnp.split(t @ w_qkv, 3, axis=-1)
        sh = (t.shape[0], t.shape[1], NA_HEADS, HEAD_DIM)
        return rms_norm(q.reshape(sh), q_g), rms_norm(k.reshape(sh), k_g), v.reshape(sh)

    q, k, v = project(h)
    qc, kc, vc = project(hc)
    kg = k.reshape(bn, rows, GRID_W, NA_HEADS, HEAD_DIM)
    vg = v.reshape(bn, rows, GRID_W, NA_HEADS, HEAD_DIM)

    q_row_off = jnp.repeat(jnp.arange(QROWS), GRID_W)
    q_col = jnp.tile(jnp.arange(GRID_W), QROWS)
    k_row_off = jnp.repeat(jnp.arange(nb), GRID_W)
    k_col = jnp.tile(jnp.arange(GRID_W), nb)
    c_start = jnp.clip(q_col - WIN_W // 2, 0, GRID_W - WIN_W)
    col_ok = (k_col[None, :] >= c_start[:, None]) & (k_col[None, :] < c_start[:, None] + WIN_W)
    dc_idx = jnp.clip(k_col[None, :] - q_col[:, None], -(WIN_W - 1), WIN_W - 1) + WIN_W - 1

    def block(args):
        blk, qb = args
        r0 = blk * QROWS
        q_row = r0 + q_row_off
        r_start = jnp.clip(q_row - kh // 2, 0, rows - kh)
        band0 = jnp.minimum(jnp.clip(r0 - kh // 2, 0, rows - kh), rows - nb)
        k_row = band0 + k_row_off
        kb = lax.dynamic_slice_in_dim(kg, band0, nb, axis=1).reshape(bn, nk, NA_HEADS, HEAD_DIM)
        vb = lax.dynamic_slice_in_dim(vg, band0, nb, axis=1).reshape(bn, nk, NA_HEADS, HEAD_DIM)
        ok = col_ok & (k_row[None, :] >= r_start[:, None]) & (k_row[None, :] < r_start[:, None] + kh)
        dr_idx = jnp.clip(k_row[None, :] - q_row[:, None], -(WIN_H - 1), WIN_H - 1) + WIN_H - 1
        bias = rpb[:, dr_idx, dc_idx].astype(jnp.float32)
        s_lat = jnp.einsum('bqhd,bkhd->bhqk', qb, kb).astype(jnp.float32) * scale + bias
        s_lat = jnp.where(ok, s_lat, -jnp.inf)
        s_ctx = jnp.einsum('bqhd,bchd->bhqc', qb, kc).astype(jnp.float32) * scale
        p = jax.nn.softmax(jnp.concatenate([s_lat, s_ctx], axis=-1), axis=-1).astype(vb.dtype)
        return (jnp.einsum('bhqk,bkhd->bqhd', p[..., :nk], vb)
                + jnp.einsum('bhqc,bchd->bqhd', p[..., nk:], vc))

    q_blocks = q.reshape(bn, n_blk, qw, NA_HEADS, HEAD_DIM).transpose(1, 0, 2, 3, 4)
    o = lax.map(block, (jnp.arange(n_blk), q_blocks))
    y = o.transpose(1, 0, 2, 3, 4).reshape(bn, n, d) @ w_o
    yc = None
    if ctx_out:
        s = jnp.einsum('bqhd,bkhd->bhqk', qc, kc).astype(jnp.float32) * scale
        p = jax.nn.softmax(s, axis=-1).astype(vc.dtype)
        yc = jnp.einsum('bhqk,bkhd->bqhd', p, vc).reshape(bn, hc.shape[1], d) @ w_o
    return y, yc


def chunk_gmlp(h, w_in, v_g, w_s, b_s, w_out):
    bn, n, _ = h.shape
    u, v = jnp.split(jax.nn.gelu(h @ w_in), 2, axis=-1)
    v = rms_norm(v, v_g)
    vch = v.reshape(bn, n // GM_CHUNK, GM_CHUNK, GM_GROUPS, GM_GROUP_DIM)
    sv = jnp.einsum('gij,bnjgd->bnigd', w_s, vch) + b_s.T[None, None, :, :, None]
    return (u * sv.reshape(bn, n, GM_WIDTH)) @ w_out


def short_conv(h, w_in, conv_w, w_out):
    b_gate, c_gate, xv = jnp.split(h @ w_in, 3, axis=-1)
    return (b_gate * dwconv3(c_gate * xv, conv_w)) @ w_out


def conv_ffn(h, w_up, conv_w, conv_b, w_down):
    gate, up = jnp.split(dwconv3(h @ w_up, conv_w, conv_b), 2, axis=-1)
    return (jax.nn.silu(gate) * up) @ w_down


def _fwd_setup_inputs(seed: int = 0) -> dict:
    key = jax.random.key(seed)
    keys = iter(jax.random.split(key, 40))

    def nrm(shape, s):
        return jax.random.normal(next(keys), shape, jnp.float32) * s

    d = D_MODEL
    n_na = len(range(MIXER_NA, DEPTH, N_MIXERS))
    n_gm = len(range(MIXER_GMLP, DEPTH, N_MIXERS))
    n_sc = len(range(MIXER_CONV, DEPTH, N_MIXERS))
    return {
        'x': nrm((BATCH, SEQ, d), 1.0),
        'c': nrm((BATCH, d), 1.0),
        'ctx': nrm((BATCH, CTX_LEN, d), 1.0),
        'c_ctx': nrm((d,), 1.0),
        'norm_mix_g': 1.0 + nrm((DEPTH, d), 0.1),
        'norm_ffn_g': 1.0 + nrm((DEPTH, d), 0.1),
        'w_ada': nrm((DEPTH, d, N_MOD * d), 0.5 * d ** -0.5),
        'b_ada': nrm((DEPTH, N_MOD * d), 0.02),
        'na_w_qkv': nrm((n_na, d, 3 * d), d ** -0.5),
        'na_q_g': 1.0 + nrm((n_na, HEAD_DIM), 0.1),
        'na_k_g': 1.0 + nrm((n_na, HEAD_DIM), 0.1),
        'na_rpb': nrm((n_na, NA_HEADS, 2 * WIN_H - 1, 2 * WIN_W - 1), 0.2),
        'na_w_o': nrm((n_na, d, d), d ** -0.5),
        'gm_w_in': nrm((n_gm, d, 2 * GM_WIDTH), d ** -0.5),
        'gm_v_g': 1.0 + nrm((n_gm, GM_WIDTH), 0.1),
        'gm_w_s': nrm((n_gm, GM_GROUPS, GM_CHUNK, GM_CHUNK), GM_CHUNK ** -0.5),
        'gm_b_s': 1.0 + nrm((n_gm, GM_GROUPS, GM_CHUNK), 0.1),
        'gm_w_out': nrm((n_gm, GM_WIDTH, d), GM_WIDTH ** -0.5),
        'sc_w_in': nrm((n_sc, d, 3 * d), d ** -0.5),
        'sc_conv_w': nrm((n_sc, 3, d), 3 ** -0.5),
        'sc_w_out': nrm((n_sc, d, d), d ** -0.5),
        'ffn_w_up': nrm((DEPTH, d, 2 * FFN_DIM), d ** -0.5),
        'ffn_conv_w': nrm((DEPTH, 3, 2 * FFN_DIM), 3 ** -0.5),
        'ffn_conv_b': nrm((DEPTH, 2 * FFN_DIM), 0.02),
        'ffn_w_down': nrm((DEPTH, FFN_DIM, d), FFN_DIM ** -0.5),
    }


def _fwd_reference(x, c, ctx, c_ctx, norm_mix_g, norm_ffn_g, w_ada, b_ada,
              na_w_qkv, na_q_g, na_k_g, na_rpb, na_w_o,
              gm_w_in, gm_v_g, gm_w_s, gm_b_s, gm_w_out,
              sc_w_in, sc_conv_w, sc_w_out,
              ffn_w_up, ffn_conv_w, ffn_conv_b, ffn_w_down):
    cond_lat = c[:, None, :]
    cond_ctx = c_ctx[None, None, :]
    for i in range(DEPTH):
        last = i == DEPTH - 1
        m, j = i % N_MIXERS, i // N_MIXERS
        sh1, sc1, g1, sh2, sc2, g2 = adaln(cond_lat, w_ada[i], b_ada[i])
        h = modulate(rms_norm(x, norm_mix_g[i]), sh1, sc1)
        hc = None
        if (not last) or m == MIXER_NA:
            csh1, csc1, cg1, csh2, csc2, cg2 = adaln(cond_ctx, w_ada[i], b_ada[i])
            hc = modulate(rms_norm(ctx, norm_mix_g[i]), csh1, csc1)
        if m == MIXER_NA:
            y, yc = neighbourhood_attention(h, hc, na_w_qkv[j], na_q_g[j], na_k_g[j],
                                            na_rpb[j], na_w_o[j], not last)
        elif m == MIXER_GMLP:
            y = chunk_gmlp(h, gm_w_in[j], gm_v_g[j], gm_w_s[j], gm_b_s[j], gm_w_out[j])
            yc = None if last else chunk_gmlp(hc, gm_w_in[j], gm_v_g[j], gm_w_s[j], gm_b_s[j], gm_w_out[j])
        else:
            y = short_conv(h, sc_w_in[j], sc_conv_w[j], sc_w_out[j])
            yc = None if last else short_conv(hc, sc_w_in[j], sc_conv_w[j], sc_w_out[j])
        x = x + g1 * y
        hf = modulate(rms_norm(x, norm_ffn_g[i]), sh2, sc2)
        x = x + g2 * conv_ffn(hf, ffn_w_up[i], ffn_conv_w[i], ffn_conv_b[i], ffn_w_down[i])
        if not last:
            ctx = ctx + cg1 * yc
            hcf = modulate(rms_norm(ctx, norm_ffn_g[i]), csh2, csc2)
            ctx = ctx + cg2 * conv_ffn(hcf, ffn_w_up[i], ffn_conv_w[i], ffn_conv_b[i], ffn_w_down[i])
    return x


import jax as _jax
import jax.numpy as _jnp

TWIN_FORMAT = 'train_step'
FWD_PARAMS = ['x', 'c', 'ctx', 'c_ctx', 'norm_mix_g', 'norm_ffn_g', 'w_ada', 'b_ada', 'na_w_qkv', 'na_q_g', 'na_k_g', 'na_rpb', 'na_w_o', 'gm_w_in', 'gm_v_g', 'gm_w_s', 'gm_b_s', 'gm_w_out', 'sc_w_in', 'sc_conv_w', 'sc_w_out', 'ffn_w_up', 'ffn_conv_w', 'ffn_conv_b', 'ffn_w_down']
TWIN_WEIGHTS = ['c_ctx', 'norm_mix_g', 'norm_ffn_g', 'w_ada', 'b_ada', 'na_w_qkv', 'na_q_g', 'na_k_g', 'na_rpb', 'na_w_o', 'gm_w_in', 'gm_v_g', 'gm_w_s', 'gm_b_s', 'gm_w_out', 'sc_w_in', 'sc_conv_w', 'sc_w_out', 'ffn_w_up', 'ffn_conv_w', 'ffn_conv_b', 'ffn_w_down']
TWIN_DIFF_INPUT = 'x'
TWIN_INPUTS = ['x', 'c', 'ctx', 'c_ctx', 'norm_mix_g', 'norm_ffn_g', 'w_ada', 'b_ada', 'na_w_qkv', 'na_q_g', 'na_k_g', 'na_rpb', 'na_w_o', 'gm_w_in', 'gm_v_g', 'gm_w_s', 'gm_b_s', 'gm_w_out', 'sc_w_in', 'sc_conv_w', 'sc_w_out', 'ffn_w_up', 'ffn_conv_w', 'ffn_conv_b', 'ffn_w_down', 'loss_target', 'm_c_ctx', 'm_norm_mix_g', 'm_norm_ffn_g', 'm_w_ada', 'm_b_ada', 'm_na_w_qkv', 'm_na_q_g', 'm_na_k_g', 'm_na_rpb', 'm_na_w_o', 'm_gm_w_in', 'm_gm_v_g', 'm_gm_w_s', 'm_gm_b_s', 'm_gm_w_out', 'm_sc_w_in', 'm_sc_conv_w', 'm_sc_w_out', 'm_ffn_w_up', 'm_ffn_conv_w', 'm_ffn_conv_b', 'm_ffn_w_down', 'v_c_ctx', 'v_norm_mix_g', 'v_norm_ffn_g', 'v_w_ada', 'v_b_ada', 'v_na_w_qkv', 'v_na_q_g', 'v_na_k_g', 'v_na_rpb', 'v_na_w_o', 'v_gm_w_in', 'v_gm_v_g', 'v_gm_w_s', 'v_gm_b_s', 'v_gm_w_out', 'v_sc_w_in', 'v_sc_conv_w', 'v_sc_w_out', 'v_ffn_w_up', 'v_ffn_conv_w', 'v_ffn_conv_b', 'v_ffn_w_down']
TWIN_OUTPUTS = ['loss', 'grad_x', 'grad_c_ctx', 'grad_norm_mix_g', 'grad_norm_ffn_g', 'grad_w_ada', 'grad_b_ada', 'grad_na_w_qkv', 'grad_na_q_g', 'grad_na_k_g', 'grad_na_rpb', 'grad_na_w_o', 'grad_gm_w_in', 'grad_gm_v_g', 'grad_gm_w_s', 'grad_gm_b_s', 'grad_gm_w_out', 'grad_sc_w_in', 'grad_sc_conv_w', 'grad_sc_w_out', 'grad_ffn_w_up', 'grad_ffn_conv_w', 'grad_ffn_conv_b', 'grad_ffn_w_down', 'delta_c_ctx', 'delta_norm_mix_g', 'delta_norm_ffn_g', 'delta_w_ada', 'delta_b_ada', 'delta_na_w_qkv', 'delta_na_q_g', 'delta_na_k_g', 'delta_na_rpb', 'delta_na_w_o', 'delta_gm_w_in', 'delta_gm_v_g', 'delta_gm_w_s', 'delta_gm_b_s', 'delta_gm_w_out', 'delta_sc_w_in', 'delta_sc_conv_w', 'delta_sc_w_out', 'delta_ffn_w_up', 'delta_ffn_conv_w', 'delta_ffn_conv_b', 'delta_ffn_w_down', 'new_m_c_ctx', 'new_m_norm_mix_g', 'new_m_norm_ffn_g', 'new_m_w_ada', 'new_m_b_ada', 'new_m_na_w_qkv', 'new_m_na_q_g', 'new_m_na_k_g', 'new_m_na_rpb', 'new_m_na_w_o', 'new_m_gm_w_in', 'new_m_gm_v_g', 'new_m_gm_w_s', 'new_m_gm_b_s', 'new_m_gm_w_out', 'new_m_sc_w_in', 'new_m_sc_conv_w', 'new_m_sc_w_out', 'new_m_ffn_w_up', 'new_m_ffn_conv_w', 'new_m_ffn_conv_b', 'new_m_ffn_w_down', 'new_v_c_ctx', 'new_v_norm_mix_g', 'new_v_norm_ffn_g', 'new_v_w_ada', 'new_v_b_ada', 'new_v_na_w_qkv', 'new_v_na_q_g', 'new_v_na_k_g', 'new_v_na_rpb', 'new_v_na_w_o', 'new_v_gm_w_in', 'new_v_gm_v_g', 'new_v_gm_w_s', 'new_v_gm_b_s', 'new_v_gm_w_out', 'new_v_sc_w_in', 'new_v_sc_conv_w', 'new_v_sc_w_out', 'new_v_ffn_w_up', 'new_v_ffn_conv_w', 'new_v_ffn_conv_b', 'new_v_ffn_w_down']
TWIN_LEAF_KINDS = {'loss': 'loss', 'grad_x': 'grad_x', 'grad_c_ctx': 'grad_w', 'grad_norm_mix_g': 'grad_w', 'grad_norm_ffn_g': 'grad_w', 'grad_w_ada': 'grad_w', 'grad_b_ada': 'grad_w', 'grad_na_w_qkv': 'grad_w', 'grad_na_q_g': 'grad_w', 'grad_na_k_g': 'grad_w', 'grad_na_rpb': 'grad_w', 'grad_na_w_o': 'grad_w', 'grad_gm_w_in': 'grad_w', 'grad_gm_v_g': 'grad_w', 'grad_gm_w_s': 'grad_w', 'grad_gm_b_s': 'grad_w', 'grad_gm_w_out': 'grad_w', 'grad_sc_w_in': 'grad_w', 'grad_sc_conv_w': 'grad_w', 'grad_sc_w_out': 'grad_w', 'grad_ffn_w_up': 'grad_w', 'grad_ffn_conv_w': 'grad_w', 'grad_ffn_conv_b': 'grad_w', 'grad_ffn_w_down': 'grad_w', 'delta_c_ctx': 'delta_w', 'delta_norm_mix_g': 'delta_w', 'delta_norm_ffn_g': 'delta_w', 'delta_w_ada': 'delta_w', 'delta_b_ada': 'delta_w', 'delta_na_w_qkv': 'delta_w', 'delta_na_q_g': 'delta_w', 'delta_na_k_g': 'delta_w', 'delta_na_rpb': 'delta_w', 'delta_na_w_o': 'delta_w', 'delta_gm_w_in': 'delta_w', 'delta_gm_v_g': 'delta_w', 'delta_gm_w_s': 'delta_w', 'delta_gm_b_s': 'delta_w', 'delta_gm_w_out': 'delta_w', 'delta_sc_w_in': 'delta_w', 'delta_sc_conv_w': 'delta_w', 'delta_sc_w_out': 'delta_w', 'delta_ffn_w_up': 'delta_w', 'delta_ffn_conv_w': 'delta_w', 'delta_ffn_conv_b': 'delta_w', 'delta_ffn_w_down': 'delta_w', 'new_m_c_ctx': 'new_m', 'new_m_norm_mix_g': 'new_m', 'new_m_norm_ffn_g': 'new_m', 'new_m_w_ada': 'new_m', 'new_m_b_ada': 'new_m', 'new_m_na_w_qkv': 'new_m', 'new_m_na_q_g': 'new_m', 'new_m_na_k_g': 'new_m', 'new_m_na_rpb': 'new_m', 'new_m_na_w_o': 'new_m', 'new_m_gm_w_in': 'new_m', 'new_m_gm_v_g': 'new_m', 'new_m_gm_w_s': 'new_m', 'new_m_gm_b_s': 'new_m', 'new_m_gm_w_out': 'new_m', 'new_m_sc_w_in': 'new_m', 'new_m_sc_conv_w': 'new_m', 'new_m_sc_w_out': 'new_m', 'new_m_ffn_w_up': 'new_m', 'new_m_ffn_conv_w': 'new_m', 'new_m_ffn_conv_b': 'new_m', 'new_m_ffn_w_down': 'new_m', 'new_v_c_ctx': 'new_v', 'new_v_norm_mix_g': 'new_v', 'new_v_norm_ffn_g': 'new_v', 'new_v_w_ada': 'new_v', 'new_v_b_ada': 'new_v', 'new_v_na_w_qkv': 'new_v', 'new_v_na_q_g': 'new_v', 'new_v_na_k_g': 'new_v', 'new_v_na_rpb': 'new_v', 'new_v_na_w_o': 'new_v', 'new_v_gm_w_in': 'new_v', 'new_v_gm_v_g': 'new_v', 'new_v_gm_w_s': 'new_v', 'new_v_gm_b_s': 'new_v', 'new_v_gm_w_out': 'new_v', 'new_v_sc_w_in': 'new_v', 'new_v_sc_conv_w': 'new_v', 'new_v_sc_w_out': 'new_v', 'new_v_ffn_w_up': 'new_v', 'new_v_ffn_conv_w': 'new_v', 'new_v_ffn_conv_b': 'new_v', 'new_v_ffn_w_down': 'new_v'}


def _forward(args):
    return _fwd_reference(*[args[k] for k in FWD_PARAMS])


def _output_shape():
    def fwd():
        inp = _fwd_setup_inputs(0)
        return _fwd_reference(*[inp[k] for k in FWD_PARAMS])
    out = _jax.eval_shape(fwd)
    return out.shape, out.dtype

N_MICROBATCH = 1
ADAM_LR = 0.001
ADAM_B1 = 0.9
ADAM_B2 = 0.999
ADAM_EPS = 1e-08
ADAM_WD = 0.01
ADAM_STEP = 10
PER_EXAMPLE_BATCH_AXIS = {'x': 0, 'c': 0, 'ctx': 0, 'loss_target': 0}
SHARED_INPUTS = []
_WEIGHT_DTYPES = {'c_ctx': _jnp.float32, 'norm_mix_g': _jnp.float32, 'norm_ffn_g': _jnp.float32, 'w_ada': _jnp.float32, 'b_ada': _jnp.float32, 'na_w_qkv': _jnp.float32, 'na_q_g': _jnp.float32, 'na_k_g': _jnp.float32, 'na_rpb': _jnp.float32, 'na_w_o': _jnp.float32, 'gm_w_in': _jnp.float32, 'gm_v_g': _jnp.float32, 'gm_w_s': _jnp.float32, 'gm_b_s': _jnp.float32, 'gm_w_out': _jnp.float32, 'sc_w_in': _jnp.float32, 'sc_conv_w': _jnp.float32, 'sc_w_out': _jnp.float32, 'ffn_w_up': _jnp.float32, 'ffn_conv_w': _jnp.float32, 'ffn_conv_b': _jnp.float32, 'ffn_w_down': _jnp.float32}
MOMENT_SCALE = {'c_ctx': 4.075614e-01, 'norm_mix_g': 6.933753e+00, 'norm_ffn_g': 3.258664e+00, 'w_ada': 1.289187e+00, 'b_ada': 3.637219e+00, 'na_w_qkv': 1.211732e-01, 'na_q_g': 3.167256e-01, 'na_k_g': 3.149351e-01, 'na_rpb': 4.797817e-03, 'na_w_o': 1.856126e-01, 'gm_w_in': 1.194109e-01, 'gm_v_g': 7.778574e-01, 'gm_w_s': 7.290406e-01, 'gm_b_s': 1.641262e+00, 'gm_w_out': 3.906469e-01, 'sc_w_in': 1.824333e-01, 'sc_conv_w': 2.729965e+00, 'sc_w_out': 1.219804e-01, 'ffn_w_up': 8.511175e-02, 'ffn_conv_w': 4.988268e-01, 'ffn_conv_b': 4.157929e-01, 'ffn_w_down': 7.599990e-02}


def _to_microbatches(a, axis):
    t = _jnp.moveaxis(a, axis, 0)
    t = t.reshape((N_MICROBATCH, t.shape[0] // N_MICROBATCH) + t.shape[1:])
    return _jnp.moveaxis(t, 1, axis + 1)


def setup_inputs(seed: int = 0) -> dict:
    inp = _fwd_setup_inputs(seed)
    key = _jax.random.fold_in(_jax.random.key(seed), 7919)
    shape, _ = _output_shape()
    out = dict(inp)
    out["loss_target"] = _jax.random.normal(_jax.random.fold_in(key, 0), shape, _jnp.float32)
    for i, name in enumerate(TWIN_WEIGHTS):
        w = inp[name].astype(_jnp.float32)
        if MOMENT_SCALE is None:
            s = _jnp.sqrt(_jnp.mean(_jnp.square(w)) + 1e-30)
        else:
            s = MOMENT_SCALE[name]
        km, kv = _jax.random.split(_jax.random.fold_in(key, i + 1))
        out[name] = w
        out["m_" + name] = s * _jax.random.normal(km, w.shape, _jnp.float32)
        out["v_" + name] = (s * s) * _jax.random.uniform(kv, w.shape, _jnp.float32, 0.5, 1.5)
    if N_MICROBATCH > 1:
        for name, axis in PER_EXAMPLE_BATCH_AXIS.items():
            out[name] = _to_microbatches(out[name], axis)
    return {'x': out['x'], 'c': out['c'], 'ctx': out['ctx'], 'c_ctx': out['c_ctx'], 'norm_mix_g': out['norm_mix_g'], 'norm_ffn_g': out['norm_ffn_g'], 'w_ada': out['w_ada'], 'b_ada': out['b_ada'], 'na_w_qkv': out['na_w_qkv'], 'na_q_g': out['na_q_g'], 'na_k_g': out['na_k_g'], 'na_rpb': out['na_rpb'], 'na_w_o': out['na_w_o'], 'gm_w_in': out['gm_w_in'], 'gm_v_g': out['gm_v_g'], 'gm_w_s': out['gm_w_s'], 'gm_b_s': out['gm_b_s'], 'gm_w_out': out['gm_w_out'], 'sc_w_in': out['sc_w_in'], 'sc_conv_w': out['sc_conv_w'], 'sc_w_out': out['sc_w_out'], 'ffn_w_up': out['ffn_w_up'], 'ffn_conv_w': out['ffn_conv_w'], 'ffn_conv_b': out['ffn_conv_b'], 'ffn_w_down': out['ffn_w_down'], 'loss_target': out['loss_target'], 'm_c_ctx': out['m_c_ctx'], 'm_norm_mix_g': out['m_norm_mix_g'], 'm_norm_ffn_g': out['m_norm_ffn_g'], 'm_w_ada': out['m_w_ada'], 'm_b_ada': out['m_b_ada'], 'm_na_w_qkv': out['m_na_w_qkv'], 'm_na_q_g': out['m_na_q_g'], 'm_na_k_g': out['m_na_k_g'], 'm_na_rpb': out['m_na_rpb'], 'm_na_w_o': out['m_na_w_o'], 'm_gm_w_in': out['m_gm_w_in'], 'm_gm_v_g': out['m_gm_v_g'], 'm_gm_w_s': out['m_gm_w_s'], 'm_gm_b_s': out['m_gm_b_s'], 'm_gm_w_out': out['m_gm_w_out'], 'm_sc_w_in': out['m_sc_w_in'], 'm_sc_conv_w': out['m_sc_conv_w'], 'm_sc_w_out': out['m_sc_w_out'], 'm_ffn_w_up': out['m_ffn_w_up'], 'm_ffn_conv_w': out['m_ffn_conv_w'], 'm_ffn_conv_b': out['m_ffn_conv_b'], 'm_ffn_w_down': out['m_ffn_w_down'], 'v_c_ctx': out['v_c_ctx'], 'v_norm_mix_g': out['v_norm_mix_g'], 'v_norm_ffn_g': out['v_norm_ffn_g'], 'v_w_ada': out['v_w_ada'], 'v_b_ada': out['v_b_ada'], 'v_na_w_qkv': out['v_na_w_qkv'], 'v_na_q_g': out['v_na_q_g'], 'v_na_k_g': out['v_na_k_g'], 'v_na_rpb': out['v_na_rpb'], 'v_na_w_o': out['v_na_w_o'], 'v_gm_w_in': out['v_gm_w_in'], 'v_gm_v_g': out['v_gm_v_g'], 'v_gm_w_s': out['v_gm_w_s'], 'v_gm_b_s': out['v_gm_b_s'], 'v_gm_w_out': out['v_gm_w_out'], 'v_sc_w_in': out['v_sc_w_in'], 'v_sc_conv_w': out['v_sc_conv_w'], 'v_sc_w_out': out['v_sc_w_out'], 'v_ffn_w_up': out['v_ffn_w_up'], 'v_ffn_conv_w': out['v_ffn_conv_w'], 'v_ffn_conv_b': out['v_ffn_conv_b'], 'v_ffn_w_down': out['v_ffn_w_down']}


def _loss(weights, diff, rest, loss_target):
    with _jax.named_scope("forward"):
        args = {**rest, TWIN_DIFF_INPUT: diff, **{k: w.astype(_WEIGHT_DTYPES[k]) for k, w in weights.items()}}
        y = _forward(args)
    with _jax.named_scope("loss_head"):
        err = _jnp.square(y.astype(_jnp.float32) - loss_target)
        return 0.5 * _jnp.sum(_jnp.mean(err, axis=-1)) if err.ndim else 0.5 * err


def _adamw(w, g, m, v):
    m = ADAM_B1 * m + (1.0 - ADAM_B1) * g
    v = ADAM_B2 * v + (1.0 - ADAM_B2) * _jnp.square(g)
    m_hat = m / (1.0 - ADAM_B1 ** ADAM_STEP)
    v_hat = v / (1.0 - ADAM_B2 ** ADAM_STEP)
    delta = -ADAM_LR * (m_hat / (_jnp.sqrt(v_hat) + ADAM_EPS) + ADAM_WD * w)
    return delta, m, v


def reference(x, c, ctx, c_ctx, norm_mix_g, norm_ffn_g, w_ada, b_ada, na_w_qkv, na_q_g, na_k_g, na_rpb, na_w_o, gm_w_in, gm_v_g, gm_w_s, gm_b_s, gm_w_out, sc_w_in, sc_conv_w, sc_w_out, ffn_w_up, ffn_conv_w, ffn_conv_b, ffn_w_down, loss_target, m_c_ctx, m_norm_mix_g, m_norm_ffn_g, m_w_ada, m_b_ada, m_na_w_qkv, m_na_q_g, m_na_k_g, m_na_rpb, m_na_w_o, m_gm_w_in, m_gm_v_g, m_gm_w_s, m_gm_b_s, m_gm_w_out, m_sc_w_in, m_sc_conv_w, m_sc_w_out, m_ffn_w_up, m_ffn_conv_w, m_ffn_conv_b, m_ffn_w_down, v_c_ctx, v_norm_mix_g, v_norm_ffn_g, v_w_ada, v_b_ada, v_na_w_qkv, v_na_q_g, v_na_k_g, v_na_rpb, v_na_w_o, v_gm_w_in, v_gm_v_g, v_gm_w_s, v_gm_b_s, v_gm_w_out, v_sc_w_in, v_sc_conv_w, v_sc_w_out, v_ffn_w_up, v_ffn_conv_w, v_ffn_conv_b, v_ffn_w_down):
    given = dict(x=x, c=c, ctx=ctx, c_ctx=c_ctx, norm_mix_g=norm_mix_g, norm_ffn_g=norm_ffn_g, w_ada=w_ada, b_ada=b_ada, na_w_qkv=na_w_qkv, na_q_g=na_q_g, na_k_g=na_k_g, na_rpb=na_rpb, na_w_o=na_w_o, gm_w_in=gm_w_in, gm_v_g=gm_v_g, gm_w_s=gm_w_s, gm_b_s=gm_b_s, gm_w_out=gm_w_out, sc_w_in=sc_w_in, sc_conv_w=sc_conv_w, sc_w_out=sc_w_out, ffn_w_up=ffn_w_up, ffn_conv_w=ffn_conv_w, ffn_conv_b=ffn_conv_b, ffn_w_down=ffn_w_down, loss_target=loss_target, m_c_ctx=m_c_ctx, m_norm_mix_g=m_norm_mix_g, m_norm_ffn_g=m_norm_ffn_g, m_w_ada=m_w_ada, m_b_ada=m_b_ada, m_na_w_qkv=m_na_w_qkv, m_na_q_g=m_na_q_g, m_na_k_g=m_na_k_g, m_na_rpb=m_na_rpb, m_na_w_o=m_na_w_o, m_gm_w_in=m_gm_w_in, m_gm_v_g=m_gm_v_g, m_gm_w_s=m_gm_w_s, m_gm_b_s=m_gm_b_s, m_gm_w_out=m_gm_w_out, m_sc_w_in=m_sc_w_in, m_sc_conv_w=m_sc_conv_w, m_sc_w_out=m_sc_w_out, m_ffn_w_up=m_ffn_w_up, m_ffn_conv_w=m_ffn_conv_w, m_ffn_conv_b=m_ffn_conv_b, m_ffn_w_down=m_ffn_w_down, v_c_ctx=v_c_ctx, v_norm_mix_g=v_norm_mix_g, v_norm_ffn_g=v_norm_ffn_g, v_w_ada=v_w_ada, v_b_ada=v_b_ada, v_na_w_qkv=v_na_w_qkv, v_na_q_g=v_na_q_g, v_na_k_g=v_na_k_g, v_na_rpb=v_na_rpb, v_na_w_o=v_na_w_o, v_gm_w_in=v_gm_w_in, v_gm_v_g=v_gm_v_g, v_gm_w_s=v_gm_w_s, v_gm_b_s=v_gm_b_s, v_gm_w_out=v_gm_w_out, v_sc_w_in=v_sc_w_in, v_sc_conv_w=v_sc_conv_w, v_sc_w_out=v_sc_w_out, v_ffn_w_up=v_ffn_w_up, v_ffn_conv_w=v_ffn_conv_w, v_ffn_conv_b=v_ffn_conv_b, v_ffn_w_down=v_ffn_w_down)
    weights = {n: given[n] for n in TWIN_WEIGHTS}
    shared = {n: given[n] for n in SHARED_INPUTS}
    per_example = {n: given[n] for n in ['x', 'c', 'ctx']}
    grad_fn = _jax.value_and_grad(_loss, argnums=(0, 1))

    def one_microbatch(ex, loss_target):
        ex = dict(ex)
        diff = ex.pop(TWIN_DIFF_INPUT)
        return grad_fn(weights, diff, {**shared, **ex}, loss_target)

    if N_MICROBATCH == 1:
        loss, (grad_w, grad_x) = one_microbatch(per_example, given["loss_target"])
    else:
        def body(carry, xs):
            loss_sum, grad_sum = carry
            l_k, (gw_k, gx_k) = one_microbatch(xs[0], xs[1])
            with _jax.named_scope("update"):
                return (loss_sum + l_k, _jax.tree.map(_jnp.add, grad_sum, gw_k)), gx_k

        init = (_jnp.zeros((), _jnp.float32), _jax.tree.map(_jnp.zeros_like, weights))
        (loss, grad_w), grad_x = _jax.lax.scan(body, init, (per_example, given["loss_target"]))
    with _jax.named_scope("update"):
        delta_w, new_m, new_v = {}, {}, {}
        for n in TWIN_WEIGHTS:
            delta_w[n], new_m[n], new_v[n] = _adamw(weights[n], grad_w[n], given["m_" + n], given["v_" + n])
    return (loss, grad_x, *[grad_w[n] for n in TWIN_WEIGHTS], *[delta_w[n] for n in TWIN_WEIGHTS],
            *[new_m[n] for n in TWIN_WEIGHTS], *[new_v[n] for n in TWIN_WEIGHTS])
```

```python
import functools
import math

import numpy as np
import jax
import jax.numpy as jnp
from jax import lax
from jax.experimental import pallas as pl
from jax.experimental.pallas import tpu as pltpu

F32 = jnp.float32
BF16 = jnp.bfloat16
MESH = pl.DeviceIdType.MESH
AXES = ("x", "y", "c")
N_DEV = 8
N_MOD = 6
N_MIXERS = 3
EPS = 1e-6
GRID_W = 64
WIN_H = 8
WIN_W = 16
QROWS = 2
GM_CHUNK = 128
LANE = 128
NEG = -1e30
ADAM_LR = 0.001
ADAM_B1 = 0.9
ADAM_B2 = 0.999
ADAM_EPS = 1e-08
ADAM_WD = 0.01
ADAM_STEP = 10
VMEM_LIMIT = 56 * 1024 * 1024
HBM = pl.BlockSpec(memory_space=pltpu.HBM)

NN = (((1,), (0,)), ((), ()))
NT = (((1,), (1,)), ((), ()))
TN = (((0,), (0,)), ((), ()))


def _params(*sem):
    return pltpu.CompilerParams(dimension_semantics=sem, vmem_limit_bytes=VMEM_LIMIT)


def _tile(n, pref, mult):
    best = None
    for t in range(mult, min(n, pref) + 1, mult):
        if n % t == 0:
            best = t
    return n if best is None else best


def _full(arr):
    nd = arr.ndim
    return pl.BlockSpec(arr.shape, lambda *g: (0,) * nd)


def _logical(shape):
    return tuple(shape) if len(shape) == 2 else (shape[1], shape[0] * shape[2])


def _cspec(shape, tr, tc, rc):
    if len(shape) == 2:
        return pl.BlockSpec((tr, tc), rc)
    cpp = shape[2] // tc

    def imap(*g):
        r, c = rc(*g)
        return (c // cpp, r, c % cpp)

    return pl.BlockSpec((None, tr, tc), imap)


def _dot(a, b, dims, exact=False):
    if exact:
        return lax.dot_general(a, b, dims, precision=lax.Precision.HIGHEST, preferred_element_type=F32)
    return lax.dot_general(a.astype(BF16), b.astype(BF16), dims, preferred_element_type=F32)


def _silu(z):
    return z * jax.nn.sigmoid(z)


def _mm(name, a, b, kind, out_shape, out_dtype, tiles, a_silu=False, exact=False):
    la, lb, lo = _logical(a.shape), _logical(b.shape), _logical(out_shape)
    t0, t1, t2 = tiles
    if kind == "nn":
        grid = (lo[1] // t1, lo[0] // t0, la[1] // t2)
        a_spec = _cspec(a.shape, t0, t2, lambda j, i, k: (i, k))
        b_spec = _cspec(b.shape, t2, t1, lambda j, i, k: (k, j))
        o_spec = _cspec(out_shape, t0, t1, lambda j, i, k: (i, j))
        dims, acc = NN, (t0, t1)
    elif kind == "nt":
        grid = (lo[1] // t1, lo[0] // t0, la[1] // t2)
        a_spec = _cspec(a.shape, t0, t2, lambda p, i, r: (i, r))
        b_spec = _cspec(b.shape, t1, t2, lambda p, i, r: (p, r))
        o_spec = _cspec(out_shape, t0, t1, lambda p, i, r: (i, p))
        dims, acc = NT, (t0, t1)
    else:
        grid = (lo[1] // t1, lo[0] // t0, la[0] // t2)
        a_spec = _cspec(a.shape, t2, t0, lambda j, kk, r: (r, kk))
        b_spec = _cspec(b.shape, t2, t1, lambda j, kk, r: (r, j))
        o_spec = _cspec(out_shape, t0, t1, lambda j, kk, r: (kk, j))
        dims, acc = TN, (t0, t1)
    nk = grid[2]

    def body(a_ref, b_ref, o_ref, acc_ref):
        k = pl.program_id(2)
        av = a_ref[...]
        if a_silu:
            av = _silu(av)
        part = _dot(av, b_ref[...], dims, exact)

        @pl.when(k == 0)
        def _():
            acc_ref[...] = part

        @pl.when(k > 0)
        def _():
            acc_ref[...] += part

        @pl.when(k == nk - 1)
        def _():
            o_ref[...] = acc_ref[...].astype(o_ref.dtype)

    return pl.pallas_call(
        body, name=name, grid=grid, in_specs=[a_spec, b_spec], out_specs=o_spec,
        out_shape=jax.ShapeDtypeStruct(out_shape, out_dtype),
        scratch_shapes=[pltpu.VMEM(acc, F32)],
        compiler_params=_params("parallel", "parallel", "arbitrary"),
    )(a, b)


def mm_nn(name, a, w, out_parts=1, out_dtype=F32, **kw):
    (m, _), (_, n) = _logical(a.shape), _logical(w.shape)
    out_shape = (m, n) if out_parts == 1 else (out_parts, m, n // out_parts)
    tiles = (_tile(m, 768, 16), _tile(math.gcd(w.shape[-1], out_shape[-1]), 1536, LANE),
             _tile(math.gcd(a.shape[-1], w.shape[-2]), 2048, LANE))
    return _mm(name, a, w, "nn", out_shape, out_dtype, tiles, **kw)


def mm_nt(name, a, w, out_dtype=F32, **kw):
    (m, _), (p, _) = _logical(a.shape), _logical(w.shape)
    tiles = (_tile(m, 768, 16), _tile(w.shape[-2], 2048, LANE), _tile(math.gcd(a.shape[-1], w.shape[-1]), 1536, LANE))
    return _mm(name, a, w, "nt", (m, p), out_dtype, tiles, **kw)


def mm_tn(name, a, dy, out_parts=1, out_dtype=BF16, **kw):
    (t, k), (_, n) = _logical(a.shape), _logical(dy.shape)
    out_shape = (k, n) if out_parts == 1 else (out_parts, k, n // out_parts)
    tiles = (_tile(a.shape[-1], 1024, LANE), _tile(math.gcd(dy.shape[-1], out_shape[-1]), 1536, LANE), _tile(t, 768, 16))
    return _mm(name, a, dy, "tn", out_shape, out_dtype, tiles, **kw)


def _rows(tr, d):
    return pl.BlockSpec((tr, d), lambda i: (i, 0))


def _pick(ctx, ref):
    return jnp.where(ctx, ref[1:2, :], ref[0:1, :])


def resid_rms_mod(name, x, y, gate, g, sc, sh, n_lat, tr=256):
    t, d = x.shape
    nlt = n_lat // tr
    has_res = y is not None

    def body(*refs):
        if has_res:
            x_ref, y_ref, gate_ref, g_ref, sc_ref, sh_ref, x1_ref, h_ref = refs
        else:
            x_ref, g_ref, sc_ref, sh_ref, h_ref = refs
        ctx = pl.program_id(0) >= nlt
        xv = x_ref[...]
        if has_res:
            xv = xv + _pick(ctx, gate_ref) * y_ref[...]
            x1_ref[...] = xv
        r = lax.rsqrt(jnp.mean(xv * xv, axis=-1, keepdims=True) + EPS)
        n = xv * r * g_ref[...]
        h_ref[...] = (n * (1.0 + _pick(ctx, sc_ref)) + _pick(ctx, sh_ref)).astype(h_ref.dtype)

    row = _rows(tr, d)
    if has_res:
        ins, in_specs = (x, y, gate, g, sc, sh), [row, row, _full(gate), _full(g), _full(sc), _full(sh)]
        out_shape = (jax.ShapeDtypeStruct((t, d), F32), jax.ShapeDtypeStruct((t, d), BF16))
        out_specs = (row, row)
    else:
        ins, in_specs = (x, g, sc, sh), [row, _full(g), _full(sc), _full(sh)]
        out_shape = jax.ShapeDtypeStruct((t, d), BF16)
        out_specs = row
    return pl.pallas_call(body, name=name, grid=(t // tr,), in_specs=in_specs, out_specs=out_specs,
                          out_shape=out_shape, compiler_params=_params("parallel"))(*ins)


def _acc_rows(i, nlt, ref, val):
    @pl.when(i == 0)
    def _():
        ref[...] = jnp.zeros_like(ref)

    @pl.when(i < nlt)
    def _():
        ref[0:1, :] += val

    @pl.when(i >= nlt)
    def _():
        ref[1:2, :] += val


def gate_bwd(name, dx, y, gate, n_lat, tr=256):
    t, d = dx.shape
    nlt = n_lat // tr

    def body(dx_ref, y_ref, gate_ref, dy_ref, dg_ref):
        i = pl.program_id(0)
        dxv = dx_ref[...]
        dy_ref[...] = (_pick(i >= nlt, gate_ref) * dxv).astype(dy_ref.dtype)
        _acc_rows(i, nlt, dg_ref, jnp.sum(dxv * y_ref[...], axis=0, keepdims=True))

    row = _rows(tr, d)
    return pl.pallas_call(
        body, name=name, grid=(t // tr,), in_specs=[row, row, _full(gate)],
        out_specs=(row, pl.BlockSpec((2, d), lambda i: (0, 0))),
        out_shape=(jax.ShapeDtypeStruct((t, d), BF16), jax.ShapeDtypeStruct((2, d), F32)),
        compiler_params=_params("arbitrary"))(dx, y, gate)


def rms_mod_bwd(name, x, dh, g, sc, dres, n_lat, tr=256):
    t, d = x.shape
    nlt = n_lat // tr

    def body(x_ref, dh_ref, g_ref, sc_ref, dres_ref, dx_ref, dsh_ref, dsc_ref, dg_ref):
        i = pl.program_id(0)
        xv, dhv, gv = x_ref[...], dh_ref[...], g_ref[...]
        r = lax.rsqrt(jnp.mean(xv * xv, axis=-1, keepdims=True) + EPS)
        xhat = xv * r
        dn = dhv * (1.0 + _pick(i >= nlt, sc_ref))
        dxhat = dn * gv
        dx_ref[...] = r * (dxhat - xhat * jnp.mean(dxhat * xhat, axis=-1, keepdims=True)) + dres_ref[...]
        _acc_rows(i, nlt, dsh_ref, jnp.sum(dhv, axis=0, keepdims=True))
        _acc_rows(i, nlt, dsc_ref, jnp.sum(dhv * (xhat * gv), axis=0, keepdims=True))
        dgp = jnp.sum(dn * xhat, axis=0, keepdims=True)

        @pl.when(i == 0)
        def _():
            dg_ref[...] = dgp

        @pl.when(i > 0)
        def _():
            dg_ref[...] += dgp

    row = _rows(tr, d)
    two = pl.BlockSpec((2, d), lambda i: (0, 0))
    return pl.pallas_call(
        body, name=name, grid=(t // tr,), in_specs=[row, row, _full(g), _full(sc), row],
        out_specs=(row, two, two, pl.BlockSpec((1, d), lambda i: (0, 0))),
        out_shape=(jax.ShapeDtypeStruct((t, d), F32), jax.ShapeDtypeStruct((2, d), F32),
                   jax.ShapeDtypeStruct((2, d), F32), jax.ShapeDtypeStruct((1, d), F32)),
        compiler_params=_params("arbitrary"))(x, dh, g, sc, dres)


def loss_head(name, x1, f, gate, target, tr=256):
    t, d = x1.shape
    nlt = target.shape[0] // tr

    def body(x_ref, f_ref, gate_ref, t_ref, dx_ref, loss_ref, acc_ref):
        i = pl.program_id(0)

        @pl.when(i == 0)
        def _():
            acc_ref[...] = jnp.zeros_like(acc_ref)

        @pl.when(i < nlt)
        def _():
            e = x_ref[...] + gate_ref[0:1, :] * f_ref[...] - t_ref[...]
            dx_ref[...] = e / d
            acc_ref[...] += jnp.sum(e * e, axis=0, keepdims=True)

        @pl.when(i >= nlt)
        def _():
            dx_ref[...] = jnp.zeros_like(dx_ref)

        @pl.when(i == t // tr - 1)
        def _():
            loss_ref[...] = jnp.sum(acc_ref[...], axis=1, keepdims=True) * (0.5 / d)

    row = _rows(tr, d)
    return pl.pallas_call(
        body, name=name, grid=(t // tr,),
        in_specs=[row, row, _full(gate), pl.BlockSpec((tr, d), lambda i: (jnp.minimum(i, nlt - 1), 0))],
        out_specs=(row, pl.BlockSpec((1, 1), lambda i: (0, 0))),
        out_shape=(jax.ShapeDtypeStruct((t, d), F32), jax.ShapeDtypeStruct((1, 1), F32)),
        scratch_shapes=[pltpu.VMEM((1, d), F32)],
        compiler_params=_params("arbitrary"))(x1, f, gate, target)


HALO = 8


def _halo_specs(shape, tr, tc, col):
    hb = tr // HALO
    last = _logical(shape)[0] // HALO - 1
    main = _cspec(shape, tr, tc, lambda j, i: (i, col(j)))
    prev = _cspec(shape, HALO, tc, lambda j, i: (jnp.maximum(i * hb - 1, 0), col(j)))
    nxt = _cspec(shape, HALO, tc, lambda j, i: (jnp.minimum((i + 1) * hb, last), col(j)))
    return [prev, main, nxt]


def _seq_edges(i, nlt, nt):
    first = (i == 0) | (i == nlt)
    last = (i == nlt - 1) | (i == nt - 1)
    return first, last


def _ext(prev_ref, main_ref, next_ref, first, last):
    p = jnp.where(first, 0.0, prev_ref[...].astype(F32))
    n = jnp.where(last, 0.0, next_ref[...].astype(F32))
    return jnp.concatenate([p, main_ref[...].astype(F32), n], axis=0)


def _up(e):
    return pltpu.roll(e, 1, 0)


def _down(e):
    return pltpu.roll(e, e.shape[0] - 1, 0)


def _conv(e, w):
    return _up(e) * w[0:1, :] + e * w[1:2, :] + _down(e) * w[2:3, :]


def _conv_t(e, w):
    return _down(e) * w[0:1, :] + e * w[1:2, :] + _up(e) * w[2:3, :]


def _mid(e, tr):
    return e[HALO:HALO + tr, :]


def _acc_cols(i, ref, val):
    @pl.when(i == 0)
    def _():
        ref[...] = val

    @pl.when(i > 0)
    def _():
        ref[...] += val


def _colsum(v):
    return jnp.sum(v, axis=0, keepdims=True)


def ffn_act_fwd(name, u, cw, cb, n_lat, tr=256):
    _, t, fp = u.shape
    tc = _tile(fp, 1536, LANE)
    nlt, nt = n_lat // tr, t // tr

    def body(pg, mg, ng, pu, mu, nu, cw_ref, cb_ref, a_ref):
        first, last = _seq_edges(pl.program_id(1), nlt, nt)
        zg = _mid(_conv(_ext(pg, mg, ng, first, last), cw_ref[0]), tr) + cb_ref[0]
        zu = _mid(_conv(_ext(pu, mu, nu, first, last), cw_ref[1]), tr) + cb_ref[1]
        a_ref[...] = (_silu(zg) * zu).astype(a_ref.dtype)

    ncol = fp // tc
    specs = _halo_specs(u.shape, tr, tc, lambda j: j) + _halo_specs(u.shape, tr, tc, lambda j: j + ncol)
    specs += [pl.BlockSpec((2, 3, tc), lambda j, i: (0, 0, j)), pl.BlockSpec((2, 1, tc), lambda j, i: (0, 0, j))]
    return pl.pallas_call(
        body, name=name, grid=(ncol, nt), in_specs=specs,
        out_specs=pl.BlockSpec((tr, tc), lambda j, i: (i, j)),
        out_shape=jax.ShapeDtypeStruct((t, fp), BF16),
        compiler_params=_params("parallel", "parallel"))(u, u, u, u, u, u, cw, cb)


def ffn_act_bwd(name, u, da, cw, cb, n_lat, tr=128):
    _, t, fp = u.shape
    tc = _tile(fp, 1536, LANE)
    nlt, nt = n_lat // tr, t // tr
    ncol = fp // tc

    def body(pg, mg, ng, pu, mu, nu, pa, ma, na, cw_ref, cb_ref, du_ref, dcw_ref, dcb_ref):
        i = pl.program_id(1)
        first, last = _seq_edges(i, nlt, nt)
        wg, wu = cw_ref[0], cw_ref[1]
        ug, uu = _ext(pg, mg, ng, first, last), _ext(pu, mu, nu, first, last)
        dae = _ext(pa, ma, na, first, last)
        zg = _conv(ug, wg) + cb_ref[0]
        zu = _conv(uu, wu) + cb_ref[1]
        sg = jax.nn.sigmoid(zg)
        dzg = dae * zu * (sg * (1.0 + zg * (1.0 - sg)))
        dzu = dae * (zg * sg)
        du_ref[0] = _mid(_conv_t(dzg, wg), tr).astype(du_ref.dtype)
        du_ref[1] = _mid(_conv_t(dzu, wu), tr).astype(du_ref.dtype)
        for h, (dz, ue) in enumerate(((dzg, ug), (dzu, uu))):
            dzm = _mid(dz, tr)
            rows = [_colsum(dzm * _mid(_up(ue), tr)), _colsum(dzm * _mid(ue, tr)), _colsum(dzm * _mid(_down(ue), tr))]
            _acc_cols(i, dcw_ref.at[h], jnp.concatenate(rows, axis=0))
            _acc_cols(i, dcb_ref.at[h], _colsum(dzm))

    specs = _halo_specs(u.shape, tr, tc, lambda j: j) + _halo_specs(u.shape, tr, tc, lambda j: j + ncol)
    specs += _halo_specs(da.shape, tr, tc, lambda j: j)
    specs += [pl.BlockSpec((2, 3, tc), lambda j, i: (0, 0, j)), pl.BlockSpec((2, 1, tc), lambda j, i: (0, 0, j))]
    return pl.pallas_call(
        body, name=name, grid=(ncol, nt), in_specs=specs,
        out_specs=(pl.BlockSpec((2, tr, tc), lambda j, i: (0, i, j)),
                   pl.BlockSpec((2, 3, tc), lambda j, i: (0, 0, j)), pl.BlockSpec((2, 1, tc), lambda j, i: (0, 0, j))),
        out_shape=(jax.ShapeDtypeStruct((2, t, fp), BF16), jax.ShapeDtypeStruct((2, 3, fp), F32),
                   jax.ShapeDtypeStruct((2, 1, fp), F32)),
        compiler_params=_params("parallel", "arbitrary"))(u, u, u, u, u, u, da, da, da, cw, cb)


def sc_gate_fwd(name, tmat, cw, n_lat, tr=256):
    t, d3 = tmat.shape
    d = d3 // 3
    tc = _tile(d, 512, LANE)
    ncol = d // tc
    nlt, nt = n_lat // tr, t // tr

    def body(b_ref, pc, mc, nc, px, mx, nx, cw_ref, s_ref):
        first, last = _seq_edges(pl.program_id(1), nlt, nt)
        p = _ext(pc, mc, nc, first, last) * _ext(px, mx, nx, first, last)
        s_ref[...] = (b_ref[...] * _mid(_conv(p, cw_ref[...]), tr)).astype(s_ref.dtype)

    specs = [pl.BlockSpec((tr, tc), lambda j, i: (i, j))]
    specs += _halo_specs(tmat.shape, tr, tc, lambda j: j + ncol) + _halo_specs(tmat.shape, tr, tc, lambda j: j + 2 * ncol)
    specs += [pl.BlockSpec((3, tc), lambda j, i: (0, j))]
    return pl.pallas_call(
        body, name=name, grid=(ncol, nt), in_specs=specs, out_specs=pl.BlockSpec((tr, tc), lambda j, i: (i, j)),
        out_shape=jax.ShapeDtypeStruct((t, d), BF16),
        compiler_params=_params("parallel", "parallel"))(*([tmat] * 7), cw)


def sc_gate_bwd(name, tmat, ds, cw, n_lat, tr=128):
    t, d3 = tmat.shape
    d = d3 // 3
    tc = _tile(d, 512, LANE)
    ncol = d // tc
    nlt, nt = n_lat // tr, t // tr

    def body(pb, mb, nb, pc, mc, nc, px, mx, nx, pd, md, nd, cw_ref, dt_ref, dcw_ref):
        i = pl.program_id(1)
        first, last = _seq_edges(i, nlt, nt)
        w = cw_ref[...]
        be, ce, xe = _ext(pb, mb, nb, first, last), _ext(pc, mc, nc, first, last), _ext(px, mx, nx, first, last)
        dse = _ext(pd, md, nd, first, last)
        p = ce * xe
        dcv = dse * be
        dp = _conv_t(dcv, w)
        dt_ref[0] = _mid(dse * _conv(p, w), tr).astype(dt_ref.dtype)
        dt_ref[1] = _mid(dp * xe, tr).astype(dt_ref.dtype)
        dt_ref[2] = _mid(dp * ce, tr).astype(dt_ref.dtype)
        dm = _mid(dcv, tr)
        rows = [_colsum(dm * _mid(_up(p), tr)), _colsum(dm * _mid(p, tr)), _colsum(dm * _mid(_down(p), tr))]
        _acc_cols(i, dcw_ref, jnp.concatenate(rows, axis=0))

    specs = []
    for part in range(3):
        specs += _halo_specs(tmat.shape, tr, tc, functools.partial(lambda j, part: j + part * ncol, part=part))
    specs += _halo_specs(ds.shape, tr, tc, lambda j: j)
    specs += [pl.BlockSpec((3, tc), lambda j, i: (0, j))]
    return pl.pallas_call(
        body, name=name, grid=(ncol, nt), in_specs=specs,
        out_specs=(pl.BlockSpec((3, tr, tc), lambda j, i: (0, i, j)), pl.BlockSpec((3, tc), lambda j, i: (0, j))),
        out_shape=(jax.ShapeDtypeStruct((3, t, d), BF16), jax.ShapeDtypeStruct((3, d), F32)),
        compiler_params=_params("parallel", "arbitrary"))(*([tmat] * 9), ds, ds, ds, cw)


_GELU_K = 0.7978845608028654
_GELU_C = 0.044715


def _gelu(x):
    return 0.5 * x * (1.0 + jnp.tanh(_GELU_K * (x + _GELU_C * (x * x * x))))


def _gelu_grad(x):
    th = jnp.tanh(_GELU_K * (x + _GELU_C * (x * x * x)))
    return 0.5 * (1.0 + th) + 0.5 * x * (1.0 - th * th) * (_GELU_K * (1.0 + 3.0 * _GELU_C * (x * x)))


def gmlp_gate_fwd(name, tmat, vg, ws, bs):
    t, w2 = tmat.shape
    w = w2 // 2
    groups = ws.shape[0]
    gd = w // groups

    def body(t_ref, vg_ref, ws_ref, bs_ref, o_ref):
        v = _gelu(t_ref[:, w:])
        r = lax.rsqrt(jnp.mean(v * v, axis=-1, keepdims=True) + EPS)
        vn = (v * r * vg_ref[...]).astype(BF16)
        for g in range(groups):
            cols = slice(g * gd, (g + 1) * gd)
            sv = _dot(ws_ref[g], vn[:, cols], NN) + bs_ref[g]
            o_ref[:, cols] = (_gelu(t_ref[:, cols]) * sv).astype(o_ref.dtype)

    return pl.pallas_call(
        body, name=name, grid=(t // GM_CHUNK,),
        in_specs=[_rows(GM_CHUNK, w2), _full(vg), _full(ws), _full(bs)], out_specs=_rows(GM_CHUNK, w),
        out_shape=jax.ShapeDtypeStruct((t, w), BF16), compiler_params=_params("parallel"))(tmat, vg, ws, bs)


def gmlp_gate_bwd(name, tmat, dout, vg, ws, bs):
    t, w2 = tmat.shape
    w = w2 // 2
    groups = ws.shape[0]
    gd = w // groups

    def body(t_ref, do_ref, vg_ref, ws_ref, bs_ref, dt_ref, dvg_ref, dws_ref, dsv_ref, dvn_ref):
        i = pl.program_id(0)
        tv = t_ref[:, w:]
        v = _gelu(tv)
        r = lax.rsqrt(jnp.mean(v * v, axis=-1, keepdims=True) + EPS)
        vhat = v * r
        vn = (vhat * vg_ref[...]).astype(BF16)
        for g in range(groups):
            cols = slice(g * gd, (g + 1) * gd)
            tu = t_ref[:, cols]
            dov = do_ref[:, cols]
            sv = _dot(ws_ref[g], vn[:, cols], NN) + bs_ref[g]
            dt_ref[:, cols] = (dov * sv * _gelu_grad(tu)).astype(dt_ref.dtype)
            dsv = dov * _gelu(tu)
            _acc_cols(i, dsv_ref.at[:, cols], dsv)
            _acc_cols(i, dws_ref.at[g], _dot(dsv, vn[:, cols], NT))
            dvn_ref[:, cols] = _dot(ws_ref[g], dsv, TN)
        dvn = dvn_ref[...]
        _acc_cols(i, dvg_ref, _colsum(dvn * vhat))
        dvhat = dvn * vg_ref[...]
        dv = r * (dvhat - vhat * jnp.mean(dvhat * vhat, axis=-1, keepdims=True))
        dt_ref[:, w:] = (dv * _gelu_grad(tv)).astype(dt_ref.dtype)

    keep = lambda shape: pl.BlockSpec(shape, lambda i: (0,) * len(shape))
    return pl.pallas_call(
        body, name=name, grid=(t // GM_CHUNK,),
        in_specs=[_rows(GM_CHUNK, w2), _rows(GM_CHUNK, w), _full(vg), _full(ws), _full(bs)],
        out_specs=(_rows(GM_CHUNK, w2), keep((1, w)), keep(ws.shape), keep((GM_CHUNK, w))),
        out_shape=(jax.ShapeDtypeStruct((t, w2), BF16), jax.ShapeDtypeStruct((1, w), F32),
                   jax.ShapeDtypeStruct(ws.shape, F32), jax.ShapeDtypeStruct((GM_CHUNK, w), F32)),
        scratch_shapes=[pltpu.VMEM((GM_CHUNK, w), F32)],
        compiler_params=_params("arbitrary"))(tmat, dout, vg, ws, bs)


def qk_norm_fwd(name, qkv, qg, kg, hd, tr=128):
    t, d3 = qkv.shape
    d = d3 // 3

    def body(x_ref, qg_ref, kg_ref, q_ref, k_ref, v_ref):
        for part, (g_ref, o_ref) in enumerate(((qg_ref, q_ref), (kg_ref, k_ref))):
            for h in range(d // hd):
                xh = x_ref[:, part * d + h * hd: part * d + (h + 1) * hd]
                r = lax.rsqrt(jnp.mean(xh * xh, axis=-1, keepdims=True) + EPS)
                o_ref[:, h * hd:(h + 1) * hd] = (xh * r * g_ref[...]).astype(o_ref.dtype)
        v_ref[...] = x_ref[:, 2 * d:].astype(v_ref.dtype)

    out = jax.ShapeDtypeStruct((t, d), BF16)
    return pl.pallas_call(
        body, name=name, grid=(t // tr,), in_specs=[_rows(tr, d3), _full(qg), _full(kg)],
        out_specs=(_rows(tr, d),) * 3, out_shape=(out,) * 3, compiler_params=_params("parallel"))(qkv, qg, kg)


def qk_norm_bwd(name, qkv, dq, dk, dv, qg, kg, hd, tr=128):
    t, d3 = qkv.shape
    d = d3 // 3

    def body(x_ref, dq_ref, dk_ref, dv_ref, qg_ref, kg_ref, o_ref, dqg_ref, dkg_ref):
        i = pl.program_id(0)
        for part, (g_ref, dn_ref, dg_ref) in enumerate(((qg_ref, dq_ref, dqg_ref), (kg_ref, dk_ref, dkg_ref))):
            dg = jnp.zeros((1, hd), F32)
            for h in range(d // hd):
                xh = x_ref[:, part * d + h * hd: part * d + (h + 1) * hd]
                dn = dn_ref[:, h * hd:(h + 1) * hd]
                r = lax.rsqrt(jnp.mean(xh * xh, axis=-1, keepdims=True) + EPS)
                xhat = xh * r
                dg = dg + _colsum(dn * xhat)
                dxhat = dn * g_ref[...]
                dx = r * (dxhat - xhat * jnp.mean(dxhat * xhat, axis=-1, keepdims=True))
                o_ref[:, part * d + h * hd: part * d + (h + 1) * hd] = dx.astype(o_ref.dtype)
            _acc_cols(i, dg_ref, dg)
        o_ref[:, 2 * d:] = dv_ref[...].astype(o_ref.dtype)

    one = pl.BlockSpec((1, hd), lambda i: (0, 0))
    return pl.pallas_call(
        body, name=name, grid=(t // tr,),
        in_specs=[_rows(tr, d3), _rows(tr, d), _rows(tr, d), _rows(tr, d), _full(qg), _full(kg)],
        out_specs=(_rows(tr, d3), one, one),
        out_shape=(jax.ShapeDtypeStruct((t, d3), BF16), jax.ShapeDtypeStruct((1, hd), F32), jax.ShapeDtypeStruct((1, hd), F32)),
        compiler_params=_params("arbitrary"))(qkv, dq, dk, dv, qg, kg)


def _na_geometry(n_lat):
    rows = n_lat // GRID_W
    kh = min(WIN_H, rows)
    nb = min(kh + QROWS - 1, rows)
    n_blk = rows // QROWS
    q_row_off = np.repeat(np.arange(QROWS), GRID_W)
    q_col = np.tile(np.arange(GRID_W), QROWS)
    k_row_off = np.repeat(np.arange(nb), GRID_W)
    k_col = np.tile(np.arange(GRID_W), nb)
    c_start = np.clip(q_col - WIN_W // 2, 0, GRID_W - WIN_W)
    col_ok = (k_col[None, :] >= c_start[:, None]) & (k_col[None, :] < c_start[:, None] + WIN_W)
    dc_idx = np.clip(k_col[None, :] - q_col[:, None], -(WIN_W - 1), WIN_W - 1) + WIN_W - 1

    def block(blk):
        r0 = blk * QROWS
        q_row = r0 + q_row_off
        r_start = np.clip(q_row - kh // 2, 0, rows - kh)
        band0 = min(int(np.clip(r0 - kh // 2, 0, rows - kh)), rows - nb)
        k_row = band0 + k_row_off
        ok = col_ok & (k_row[None, :] >= r_start[:, None]) & (k_row[None, :] < r_start[:, None] + kh)
        dr_idx = np.clip(k_row[None, :] - q_row[:, None], -(WIN_H - 1), WIN_H - 1) + WIN_H - 1
        return band0, ok, dr_idx

    reps = [0, 1, 2, n_blk - 2, n_blk - 1]
    variant = lambda blk: 0 if blk == 0 else 1 if blk == 1 else 3 if blk == n_blk - 2 else 4 if blk == n_blk - 1 else 2
    geo = [block(b) for b in reps]
    for blk in range(n_blk):
        _, ok, dr = block(blk)
        assert np.array_equal(ok, geo[variant(blk)][1]) and np.array_equal(np.where(ok, dr, 0), np.where(ok, geo[variant(blk)][2], 0))
    return dict(rows=rows, kh=kh, nb=nb, n_blk=n_blk, reps=reps, ok=[g[1] for g in geo], dr=[g[2] for g in geo],
                band0=[g[0] for g in geo], dc=dc_idx)


def na_bias_table(rpb, geo):
    tabs = [jnp.where(jnp.asarray(ok)[None], rpb[:, dr, geo["dc"]], NEG) for ok, dr in zip(geo["ok"], geo["dr"])]
    return jnp.stack(tabs).astype(F32)


def _na_tile_info(qt, geo):
    n_blk, rows, kh, nb = geo["n_blk"], geo["rows"], geo["kh"], geo["nb"]
    is_ctx = qt >= n_blk
    band0 = jnp.minimum(jnp.clip(qt * QROWS - kh // 2, 0, rows - kh), rows - nb)
    band0 = jnp.where(is_ctx, 0, band0)
    return is_ctx, pl.multiple_of(band0 * GRID_W, GRID_W)


def _na_variant(qt, n_blk):
    v = jnp.minimum(qt, 2) + (qt >= n_blk - 2).astype(jnp.int32) + (qt >= n_blk - 1).astype(jnp.int32)
    return jnp.minimum(v, 4)


def _na_probs(q, kb, kc, bias, is_ctx, scale):
    s_lat = _dot(q, kb, NT) * scale + bias
    s_lat = jnp.where(is_ctx, NEG, s_lat)
    s_ctx = _dot(q, kc, NT) * scale
    m = jnp.maximum(jnp.max(s_lat, axis=-1, keepdims=True), jnp.max(s_ctx, axis=-1, keepdims=True))
    e_lat, e_ctx = jnp.exp(s_lat - m), jnp.exp(s_ctx - m)
    den = jnp.sum(e_lat, axis=-1, keepdims=True) + jnp.sum(e_ctx, axis=-1, keepdims=True)
    return e_lat / den, e_ctx / den


def na_attention_fwd(name, q, k, v, bias, geo, n_lat, hd):
    t, d = q.shape
    qw, nk = QROWS * GRID_W, geo["nb"] * GRID_W
    n_blk = geo["n_blk"]
    scale = hd ** -0.5

    def body(q_ref, k_ref, v_ref, b_ref, o_ref):
        is_ctx, start = _na_tile_info(pl.program_id(1), geo)
        kb, vb = k_ref[pl.ds(start, nk), :], v_ref[pl.ds(start, nk), :]
        kc, vc = k_ref[n_lat:, :], v_ref[n_lat:, :]
        p_lat, p_ctx = _na_probs(q_ref[...], kb, kc, b_ref[...], is_ctx, scale)
        o_ref[...] = (_dot(p_lat, vb, NN) + _dot(p_ctx, vc, NN)).astype(o_ref.dtype)

    head = pl.BlockSpec((t, hd), lambda h, i: (0, h))
    tile = pl.BlockSpec((qw, hd), lambda h, i: (i, h))
    return pl.pallas_call(
        body, name=name, grid=(d // hd, t // qw),
        in_specs=[tile, head, head, pl.BlockSpec((None, None, qw, nk), lambda h, i: (_na_variant(i, n_blk), h, 0, 0))],
        out_specs=tile, out_shape=jax.ShapeDtypeStruct((t, d), BF16),
        compiler_params=_params("parallel", "arbitrary"))(q, k, v, bias)


def na_attention_bwd(name, q, k, v, bias, do, geo, n_lat, hd):
    t, d = q.shape
    qw, nk = QROWS * GRID_W, geo["nb"] * GRID_W
    n_blk = geo["n_blk"]
    scale = hd ** -0.5

    def body(q_ref, k_ref, v_ref, b_ref, do_ref, dq_ref, dk_ref, dv_ref, db_ref):
        qt = pl.program_id(1)
        is_ctx, start = _na_tile_info(qt, geo)
        band = pl.ds(start, nk)
        qv, dov = q_ref[...], do_ref[...]
        kb, vb = k_ref[band, :], v_ref[band, :]
        kc, vc = k_ref[n_lat:, :], v_ref[n_lat:, :]
        p_lat, p_ctx = _na_probs(qv, kb, kc, b_ref[...], is_ctx, scale)
        dp_lat, dp_ctx = _dot(dov, vb, NT), _dot(dov, vc, NT)
        delta = jnp.sum(p_lat * dp_lat, axis=-1, keepdims=True) + jnp.sum(p_ctx * dp_ctx, axis=-1, keepdims=True)
        ds_lat, ds_ctx = p_lat * (dp_lat - delta), p_ctx * (dp_ctx - delta)
        fresh = (qt <= 2) | (qt == n_blk - 2) | (qt == n_blk - 1)

        @pl.when(fresh)
        def _():
            db_ref[...] = ds_lat

        @pl.when(jnp.logical_not(fresh))
        def _():
            db_ref[...] += ds_lat

        dsl, dsc = (ds_lat * scale).astype(BF16), (ds_ctx * scale).astype(BF16)
        dq_ref[...] = _dot(dsl, kb, NN) + _dot(dsc, kc, NN)

        @pl.when(qt == 0)
        def _():
            dk_ref[...] = jnp.zeros_like(dk_ref)
            dv_ref[...] = jnp.zeros_like(dv_ref)

        dk_ref[band, :] += _dot(dsl, qv, TN)
        dk_ref[n_lat:, :] += _dot(dsc, qv, TN)
        dv_ref[band, :] += _dot(p_lat, dov, TN)
        dv_ref[n_lat:, :] += _dot(p_ctx, dov, TN)

    head = pl.BlockSpec((t, hd), lambda h, i: (0, h))
    tile = pl.BlockSpec((qw, hd), lambda h, i: (i, h))
    btab = pl.BlockSpec((None, None, qw, nk), lambda h, i: (_na_variant(i, n_blk), h, 0, 0))
    full = jax.ShapeDtypeStruct((t, d), F32)
    return pl.pallas_call(
        body, name=name, grid=(d // hd, t // qw), in_specs=[tile, head, head, btab, tile],
        out_specs=(tile, head, head, btab), out_shape=(full, full, full, jax.ShapeDtypeStruct(bias.shape, F32)),
        compiler_params=_params("arbitrary", "arbitrary"))(q, k, v, bias, do)


def na_rpb_grad(name, dbias, geo, n_heads):
    nb = geo["nb"]
    nv = len(geo["reps"])
    w2 = GRID_W * GRID_W
    xmat = dbias.reshape(nv, n_heads, QROWS, GRID_W, nb, GRID_W).transpose(0, 1, 2, 4, 3, 5).reshape(nv * n_heads * QROWS * nb, w2)
    qc, kc = np.meshgrid(np.arange(GRID_W), np.arange(GRID_W), indexing="ij")
    diff = (kc - qc).reshape(-1)
    c1 = np.zeros((w2, LANE), np.float32)
    sel = np.abs(diff) <= WIN_W - 1
    c1[np.arange(w2)[sel], diff[sel] + WIN_W - 1] = 1.0
    m = xmat.shape[0]
    r = _mm(name + "_cols", xmat, jnp.asarray(c1), "nn", (m, LANE), F32, (_tile(m, 512, 8), LANE, _tile(w2, 1024, LANE)), exact=True)
    npair = nv * QROWS * nb
    kpad = -(-npair // LANE) * LANE
    r2 = r.reshape(nv, n_heads, QROWS * nb, LANE).transpose(0, 2, 1, 3).reshape(npair, n_heads * LANE)
    r2 = jnp.pad(r2, ((0, kpad - npair), (0, 0)))
    s = np.zeros((16, kpad), np.float32)
    for vi, blk in enumerate(geo["reps"]):
        for qr in range(QROWS):
            for kr in range(nb):
                dr = (geo["band0"][vi] + kr) - (blk * QROWS + qr)
                if abs(dr) <= WIN_H - 1:
                    s[dr + WIN_H - 1, (vi * QROWS + qr) * nb + kr] = 1.0
    out = _mm(name + "_rows", jnp.asarray(s), r2, "nn", (16, n_heads * LANE), F32, (16, _tile(n_heads * LANE, 1024, LANE), kpad), exact=True)
    return out[:2 * WIN_H - 1].reshape(2 * WIN_H - 1, n_heads, LANE)[:, :, :2 * WIN_W - 1].transpose(1, 0, 2)


def _me():
    return lax.axis_index("x"), lax.axis_index("y"), lax.axis_index("c")


def _slot(p):
    return 4 * p[0] + 2 * p[1] + p[2]


def all_gather(name, arrays):
    n = len(arrays)

    def body(*refs):
        ins, outs = refs[:n], refs[n:2 * n]
        send_sems, recv_sems, local_sems = refs[2 * n:]
        x, y, c = _me()
        me, sib = (x, y, c), (x, y, 1 - c)
        chips = [(1 - x, y), (x, 1 - y), (1 - x, 1 - y)]

        def copy(a, k, block, to, src=None):
            dst = outs[a].at[_slot(block)]
            return pltpu.make_async_remote_copy(src_ref=dst if src is None else src, dst_ref=dst, send_sem=send_sems.at[a, k],
                                                recv_sem=recv_sems.at[a, k], device_id=to, device_id_type=MESH)

        sends, locals_ = [], []
        for a in range(n):
            mine = pltpu.make_async_copy(ins[a], outs[a].at[_slot(me)], local_sems.at[a])
            mine.start()
            locals_.append(mine)
            first = [copy(a, 0, me, sib, src=ins[a])] + [copy(a, 1 + j, me, (*chip, c), src=ins[a]) for j, chip in enumerate(chips)]
            for cp in first:
                cp.start()
            sends += first
        for j, chip in enumerate(chips):
            for a in range(n):
                copy(a, 1 + j, (*chip, c), me).wait_recv()
                passed = copy(a, 4 + j, (*chip, c), sib)
                passed.start()
                sends.append(passed)
        for a in range(n):
            copy(a, 0, sib, me).wait_recv()
            for j, chip in enumerate(chips):
                copy(a, 4 + j, (*chip, 1 - c), me).wait_recv()
        for cp in sends:
            cp.wait_send()
        for mine in locals_:
            mine.wait()

    outs = pl.pallas_call(
        body, name=name, in_specs=[HBM] * n, out_specs=[HBM] * n,
        out_shape=[jax.ShapeDtypeStruct((N_DEV, *a.shape), a.dtype) for a in arrays],
        scratch_shapes=[pltpu.SemaphoreType.DMA((n, 7)), pltpu.SemaphoreType.DMA((n, 7)), pltpu.SemaphoreType.DMA((n,))],
    )(*arrays)
    return list(outs)


def all_to_all(name, arrays):
    n = len(arrays)

    def body(*refs):
        ins, outs = refs[:n], refs[n:2 * n]
        send_sems, recv_sems, local_sems = refs[2 * n:]
        x, y, c = _me()
        me = (x, y, c)
        peers = [((x + (k >> 2)) % 2, (y + ((k >> 1) & 1)) % 2, (c + (k & 1)) % 2) for k in range(1, N_DEV)]

        def copy(a, k, src_block, dst_block, to):
            return pltpu.make_async_remote_copy(src_ref=ins[a].at[_slot(src_block)], dst_ref=outs[a].at[_slot(dst_block)],
                                                send_sem=send_sems.at[a, k], recv_sem=recv_sems.at[a, k], device_id=to, device_id_type=MESH)

        sends, locals_ = [], []
        for a in range(n):
            mine = pltpu.make_async_copy(ins[a].at[_slot(me)], outs[a].at[_slot(me)], local_sems.at[a])
            mine.start()
            locals_.append(mine)
            for k, peer in enumerate(peers):
                cp = copy(a, k, peer, me, peer)
                cp.start()
                sends.append(cp)
        for a in range(n):
            for k, peer in enumerate(peers):
                copy(a, k, me, peer, peer).wait_recv()
        for cp in sends:
            cp.wait_send()
        for mine in locals_:
            mine.wait()

    outs = pl.pallas_call(
        body, name=name, in_specs=[HBM] * n, out_specs=[HBM] * n,
        out_shape=[jax.ShapeDtypeStruct(a.shape, a.dtype) for a in arrays],
        scratch_shapes=[pltpu.SemaphoreType.DMA((n, 7)), pltpu.SemaphoreType.DMA((n, 7)), pltpu.SemaphoreType.DMA((n,))],
    )(*arrays)
    return list(outs)


def _adamw_math(w, g, m, v):
    m = ADAM_B1 * m + (1.0 - ADAM_B1) * g
    v = ADAM_B2 * v + (1.0 - ADAM_B2) * (g * g)
    m_hat = m / (1.0 - ADAM_B1 ** ADAM_STEP)
    v_hat = v / (1.0 - ADAM_B2 ** ADAM_STEP)
    delta = -ADAM_LR * (m_hat / (jnp.sqrt(v_hat) + ADAM_EPS) + ADAM_WD * w)
    return delta, m, v


def adamw(name, parts, w, m, v):
    r, c = w.shape
    npart, _, cp = parts.shape
    tc = c if (c % LANE or cp != c) else _tile(c, 512, LANE)
    tr = _tile(r, 256, 16 if parts.dtype == BF16 else 8)
    if tr < 64:
        tr, tc = r, (tc if tc == c and cp != c else _tile(c, 256, LANE))
    tcp = cp if tc == c else tc

    def body(p_ref, w_ref, m_ref, v_ref, g_ref, d_ref, mo_ref, vo_ref):
        g = p_ref[0].astype(F32)
        for s in range(1, npart):
            g = g + p_ref[s].astype(F32)
        g = g[:, :tc]
        g_ref[...] = g
        d_ref[...], mo_ref[...], vo_ref[...] = _adamw_math(w_ref[...], g, m_ref[...], v_ref[...])

    blk = pl.BlockSpec((tr, tc), lambda i, j: (i, j))
    out = jax.ShapeDtypeStruct((r, c), F32)
    return pl.pallas_call(
        body, name=name, grid=(r // tr, c // tc),
        in_specs=[pl.BlockSpec((npart, tr, tcp), lambda i, j: (0, i, j)), blk, blk, blk],
        out_specs=(blk,) * 4, out_shape=(out,) * 4, compiler_params=_params("parallel", "parallel"))(parts, w, m, v)


def sum_parts(name, parts):
    npart, r, c = parts.shape
    tr = _tile(r, 512, 8)

    def body(p_ref, o_ref):
        g = p_ref[0]
        for s in range(1, npart):
            g = g + p_ref[s]
        o_ref[...] = g

    return pl.pallas_call(
        body, name=name, grid=(r // tr,), in_specs=[pl.BlockSpec((npart, tr, c), lambda i: (0, i, 0))],
        out_specs=pl.BlockSpec((tr, c), lambda i: (i, 0)), out_shape=jax.ShapeDtypeStruct((r, c), F32),
        compiler_params=_params("parallel"))(parts)


class _Pack:
    def __init__(self, shapes):
        self.shapes = dict(shapes)
        self.offsets, off = {}, 0
        for name, shape in self.shapes.items():
            self.offsets[name] = off
            off += -(-int(np.prod(shape)) // (8 * LANE)) * (8 * LANE)
        self.rows = off // LANE

    def pack(self, values):
        pieces = []
        for name, shape in self.shapes.items():
            size = int(np.prod(shape))
            padded = -(-size // (8 * LANE)) * (8 * LANE)
            pieces.append(jnp.pad(values[name].astype(F32).reshape(-1), (0, padded - size)))
        return jnp.concatenate(pieces).reshape(self.rows, LANE)

    def unpack(self, flat, lead=()):
        flat = flat.reshape(*lead, self.rows * LANE)
        out = {}
        for name, shape in self.shapes.items():
            size = int(np.prod(shape))
            out[name] = flat[..., self.offsets[name]:self.offsets[name] + size].reshape(*lead, *shape)
        return out


def kernel(x, c, ctx, c_ctx, norm_mix_g, norm_ffn_g, w_ada, b_ada, na_w_qkv, na_q_g, na_k_g, na_rpb, na_w_o, gm_w_in, gm_v_g, gm_w_s, gm_b_s, gm_w_out, sc_w_in, sc_conv_w, sc_w_out, ffn_w_up, ffn_conv_w, ffn_conv_b, ffn_w_down, loss_target, m_c_ctx, m_norm_mix_g, m_norm_ffn_g, m_w_ada, m_b_ada, m_na_w_qkv, m_na_q_g, m_na_k_g, m_na_rpb, m_na_w_o, m_gm_w_in, m_gm_v_g, m_gm_w_s, m_gm_b_s, m_gm_w_out, m_sc_w_in, m_sc_conv_w, m_sc_w_out, m_ffn_w_up, m_ffn_conv_w, m_ffn_conv_b, m_ffn_w_down, v_c_ctx, v_norm_mix_g, v_norm_ffn_g, v_w_ada, v_b_ada, v_na_w_qkv, v_na_q_g, v_na_k_g, v_na_rpb, v_na_w_o, v_gm_w_in, v_gm_v_g, v_gm_w_s, v_gm_b_s, v_gm_w_out, v_sc_w_in, v_sc_conv_w, v_sc_w_out, v_ffn_w_up, v_ffn_conv_w, v_ffn_conv_b, v_ffn_w_down):
    weights = dict(c_ctx=c_ctx, norm_mix_g=norm_mix_g, norm_ffn_g=norm_ffn_g, w_ada=w_ada, b_ada=b_ada, na_w_qkv=na_w_qkv,
                   na_q_g=na_q_g, na_k_g=na_k_g, na_rpb=na_rpb, na_w_o=na_w_o, gm_w_in=gm_w_in, gm_v_g=gm_v_g, gm_w_s=gm_w_s,
                   gm_b_s=gm_b_s, gm_w_out=gm_w_out, sc_w_in=sc_w_in, sc_conv_w=sc_conv_w, sc_w_out=sc_w_out,
                   ffn_w_up=ffn_w_up, ffn_conv_w=ffn_conv_w, ffn_conv_b=ffn_conv_b, ffn_w_down=ffn_w_down)
    mom_m = dict(c_ctx=m_c_ctx, norm_mix_g=m_norm_mix_g, norm_ffn_g=m_norm_ffn_g, w_ada=m_w_ada, b_ada=m_b_ada, na_w_qkv=m_na_w_qkv,
                 na_q_g=m_na_q_g, na_k_g=m_na_k_g, na_rpb=m_na_rpb, na_w_o=m_na_w_o, gm_w_in=m_gm_w_in, gm_v_g=m_gm_v_g,
                 gm_w_s=m_gm_w_s, gm_b_s=m_gm_b_s, gm_w_out=m_gm_w_out, sc_w_in=m_sc_w_in, sc_conv_w=m_sc_conv_w,
                 sc_w_out=m_sc_w_out, ffn_w_up=m_ffn_w_up, ffn_conv_w=m_ffn_conv_w, ffn_conv_b=m_ffn_conv_b, ffn_w_down=m_ffn_w_down)
    mom_v = dict(c_ctx=v_c_ctx, norm_mix_g=v_norm_mix_g, norm_ffn_g=v_norm_ffn_g, w_ada=v_w_ada, b_ada=v_b_ada, na_w_qkv=v_na_w_qkv,
                 na_q_g=v_na_q_g, na_k_g=v_na_k_g, na_rpb=v_na_rpb, na_w_o=v_na_w_o, gm_w_in=v_gm_w_in, gm_v_g=v_gm_v_g,
                 gm_w_s=v_gm_w_s, gm_b_s=v_gm_b_s, gm_w_out=v_gm_w_out, sc_w_in=v_sc_w_in, sc_conv_w=v_sc_conv_w,
                 sc_w_out=v_sc_w_out, ffn_w_up=v_ffn_w_up, ffn_conv_w=v_ffn_conv_w, ffn_conv_b=v_ffn_conv_b, ffn_w_down=v_ffn_w_down)
    names = list(weights)

    n_lat, d = x.shape[1], x.shape[2]
    n_ctx = ctx.shape[1]
    t = n_lat + n_ctx
    depth = norm_mix_g.shape[0]
    hd = na_q_g.shape[-1]
    n_heads = d // hd
    nup = ffn_w_up.shape[-1]
    nup_p = -(-nup // LANE) * LANE
    fdim, fp = 4 * nup, 4 * nup_p
    me = _slot(_me())
    geo = _na_geometry(n_lat)

    pad_up = lambda a: jnp.pad(a, [(0, 0)] * (a.ndim - 1) + [(0, nup_p - nup)])
    gathered = all_gather("gather_weights", [
        na_w_qkv.astype(BF16), na_w_o.astype(BF16), gm_w_in.astype(BF16), gm_w_out.astype(BF16), sc_w_in.astype(BF16),
        sc_w_out.astype(BF16), pad_up(ffn_w_up).astype(BF16), ffn_w_down.astype(BF16), pad_up(ffn_conv_w), sc_conv_w, c])
    g_qkv, g_wo, g_gin, g_gout, g_sin, g_sout, g_up, g_down, g_fcw, g_scw, c_all = gathered
    w_qkv = [g_qkv[:, j] for j in range(na_w_qkv.shape[0])]
    w_o = [g_wo[:, j].reshape(d, d) for j in range(na_w_o.shape[0])]
    w_gin = [g_gin[:, j] for j in range(gm_w_in.shape[0])]
    w_gout = [g_gout[:, j].reshape(-1, d) for j in range(gm_w_out.shape[0])]
    w_sin = [g_sin[:, j] for j in range(sc_w_in.shape[0])]
    w_sout = [g_sout[:, j].reshape(d, d) for j in range(sc_w_out.shape[0])]
    w_up = [g_up[:, i] for i in range(depth)]
    w_down = [jnp.pad(g_down[:, i].reshape(4, nup, d), ((0, 0), (0, nup_p - nup), (0, 0))).reshape(fp, d) for i in range(depth)]
    f_cw = [g_fcw[:, i].transpose(1, 0, 2).reshape(3, 2, fp).transpose(1, 0, 2) for i in range(depth)]
    cb_p = pad_up(ffn_conv_b.reshape(depth, N_DEV, nup)).reshape(depth, 2, 1, fp)
    s_cw = [g_scw[:, j].transpose(1, 0, 2).reshape(3, d) for j in range(sc_conv_w.shape[0])]

    cond = jnp.concatenate([c_all.reshape(N_DEV, d), c_ctx[None], jnp.zeros((7, d), F32)])
    mod_cols = mm_nn("ada_fwd", cond, w_ada, a_silu=True)
    (mod_all,) = all_gather("gather_mod", [mod_cols])
    ncol = w_ada.shape[-1]
    mod_all = mod_all.reshape(N_DEV, 16, depth, ncol).transpose(2, 1, 0, 3).reshape(depth, 16, N_MOD * d) + b_ada[:, None, :]
    mod_lat = lax.dynamic_index_in_dim(mod_all, me, axis=1, keepdims=False)
    mods = jnp.stack([mod_lat, mod_all[:, N_DEV]], axis=1).reshape(depth, 2, N_MOD, d)
    sh1, sc1, g1, sh2, sc2, g2 = (mods[:, :, kd] for kd in range(N_MOD))

    xs = jnp.concatenate([x[0], ctx[0]], axis=0)
    saved = []
    prev = None
    for i in range(depth):
        mixer, j = i % N_MIXERS, i // N_MIXERS
        s = {}
        if prev is None:
            s["x"] = xs
            s["h"] = resid_rms_mod(f"l{i}_norm_mix", xs, None, None, norm_mix_g[i:i + 1], sc1[i], sh1[i], n_lat)
        else:
            s["x"], s["h"] = resid_rms_mod(f"l{i}_norm_mix", prev[0], prev[1], prev[2], norm_mix_g[i:i + 1], sc1[i], sh1[i], n_lat)
        if mixer == 0:
            s["qkv"] = mm_nn(f"l{i}_qkv", s["h"], w_qkv[j])
            s["q"], s["k"], s["v"] = qk_norm_fwd(f"l{i}_qk_norm", s["qkv"], na_q_g[j:j + 1], na_k_g[j:j + 1], hd)
            s["bias"] = na_bias_table(na_rpb[j], geo)
            s["o"] = na_attention_fwd(f"l{i}_attn", s["q"], s["k"], s["v"], s["bias"], geo, n_lat, hd)
            s["y"] = mm_nn(f"l{i}_wo", s["o"], w_o[j])
        elif mixer == 1:
            s["t"] = mm_nn(f"l{i}_gm_in", s["h"], w_gin[j])
            s["o"] = gmlp_gate_fwd(f"l{i}_gm_gate", s["t"], gm_v_g[j:j + 1], gm_w_s[j], gm_b_s[j][:, :, None])
            s["y"] = mm_nn(f"l{i}_gm_out", s["o"], w_gout[j])
        else:
            s["t"] = mm_nn(f"l{i}_sc_in", s["h"], w_sin[j])
            s["o"] = sc_gate_fwd(f"l{i}_sc_gate", s["t"], s_cw[j], n_lat)
            s["y"] = mm_nn(f"l{i}_sc_out", s["o"], w_sout[j])
        s["x1"], s["hf"] = resid_rms_mod(f"l{i}_norm_ffn", s["x"], s["y"], g1[i], norm_ffn_g[i:i + 1], sc2[i], sh2[i], n_lat)
        s["u"] = mm_nn(f"l{i}_ffn_up", s["hf"], w_up[i], out_parts=2)
        s["a"] = ffn_act_fwd(f"l{i}_ffn_act", s["u"], f_cw[i], cb_p[i], n_lat)
        s["f"] = mm_nn(f"l{i}_ffn_down", s["a"], w_down[i])
        prev = (s["x1"], s["f"], g2[i])
        saved.append(s)

    dx, loss_local = loss_head("loss_head", prev[0], prev[1], prev[2], loss_target[0])
    loss = lax.psum(loss_local[0, 0], AXES)

    big = {}
    small = {}
    dmod = [None] * depth
    zeros_like_param = lambda p: [None] * p.shape[0]
    for nm in ("na_w_qkv", "na_w_o", "gm_w_in", "gm_w_out", "sc_w_in", "sc_w_out", "ffn_w_up", "ffn_w_down"):
        big[nm] = zeros_like_param(weights[nm])
    for nm in ("norm_mix_g", "norm_ffn_g", "ffn_conv_w", "ffn_conv_b", "na_q_g", "na_k_g", "na_rpb", "gm_v_g", "gm_w_s", "gm_b_s", "sc_conv_w"):
        small[nm] = zeros_like_param(weights[nm])
    for i in reversed(range(depth)):
        mixer, j = i % N_MIXERS, i // N_MIXERS
        s = saved[i]
        df, dg2 = gate_bwd(f"l{i}_b_gate_ffn", dx, s["f"], g2[i], n_lat)
        da = mm_nt(f"l{i}_b_ffn_down_x", df, w_down[i])
        dwd = mm_tn(f"l{i}_b_ffn_down_w", s["a"], df)
        big["ffn_w_down"][i] = dwd.reshape(4, nup_p, d)[:, :nup].reshape(N_DEV, fdim // N_DEV, d)
        du, dcw, dcb = ffn_act_bwd(f"l{i}_b_ffn_act", s["u"], da, f_cw[i], cb_p[i], n_lat)
        small["ffn_conv_w"][i] = dcw.transpose(1, 0, 2).reshape(3, N_DEV, nup_p)[:, :, :nup].reshape(3, 2 * fdim)
        small["ffn_conv_b"][i] = dcb.reshape(N_DEV, nup_p)[:, :nup].reshape(2 * fdim)
        dhf = mm_nt(f"l{i}_b_ffn_up_x", du, w_up[i])
        big["ffn_w_up"][i] = mm_tn(f"l{i}_b_ffn_up_w", s["hf"], du, out_parts=N_DEV)
        dx1, dsh2, dsc2, dgf = rms_mod_bwd(f"l{i}_b_norm_ffn", s["x1"], dhf, norm_ffn_g[i:i + 1], sc2[i], dx, n_lat)
        small["norm_ffn_g"][i] = dgf[0]
        dy, dg1 = gate_bwd(f"l{i}_b_gate_mix", dx1, s["y"], g1[i], n_lat)
        if mixer == 0:
            do = mm_nt(f"l{i}_b_wo_x", dy, w_o[j], out_dtype=BF16)
            big["na_w_o"][j] = mm_tn(f"l{i}_b_wo_w", s["o"], dy).reshape(N_DEV, d // N_DEV, d)
            dq, dk, dv, dbias = na_attention_bwd(f"l{i}_b_attn", s["q"], s["k"], s["v"], s["bias"], do, geo, n_lat, hd)
            small["na_rpb"][j] = na_rpb_grad(f"l{i}_b_rpb", dbias, geo, n_heads)
            dt, dqg, dkg = qk_norm_bwd(f"l{i}_b_qk_norm", s["qkv"], dq, dk, dv, na_q_g[j:j + 1], na_k_g[j:j + 1], hd)
            small["na_q_g"][j], small["na_k_g"][j] = dqg[0], dkg[0]
            dh = mm_nt(f"l{i}_b_qkv_x", dt, w_qkv[j])
            big["na_w_qkv"][j] = mm_tn(f"l{i}_b_qkv_w", s["h"], dt, out_parts=N_DEV)
        elif mixer == 1:
            do = mm_nt(f"l{i}_b_gm_out_x", dy, w_gout[j])
            big["gm_w_out"][j] = mm_tn(f"l{i}_b_gm_out_w", s["o"], dy).reshape(N_DEV, -1, d)
            dt, dvg, dws, dsv = gmlp_gate_bwd(f"l{i}_b_gm_gate", s["t"], do, gm_v_g[j:j + 1], gm_w_s[j], gm_b_s[j][:, :, None])
            groups, width = gm_w_s.shape[1], dsv.shape[1]
            group_of = np.zeros((width, LANE), np.float32)
            group_of[np.arange(width), np.arange(width) // (width // groups)] = 1.0
            dbs = _mm(f"l{i}_b_gm_bs", dsv, jnp.asarray(group_of), "nn", (GM_CHUNK, LANE), F32,
                      (GM_CHUNK, LANE, _tile(width, 2048, LANE)), exact=True)[:, :groups].T
            small["gm_v_g"][j], small["gm_w_s"][j], small["gm_b_s"][j] = dvg[0], dws, dbs
            dh = mm_nt(f"l{i}_b_gm_in_x", dt, w_gin[j])
            big["gm_w_in"][j] = mm_tn(f"l{i}_b_gm_in_w", s["h"], dt, out_parts=N_DEV)
        else:
            do = mm_nt(f"l{i}_b_sc_out_x", dy, w_sout[j])
            big["sc_w_out"][j] = mm_tn(f"l{i}_b_sc_out_w", s["o"], dy).reshape(N_DEV, d // N_DEV, d)
            dt, dscw = sc_gate_bwd(f"l{i}_b_sc_gate", s["t"], do, s_cw[j], n_lat)
            small["sc_conv_w"][j] = dscw
            dt = dt.transpose(1, 0, 2).reshape(t, 3 * d)
            dh = mm_nt(f"l{i}_b_sc_in_x", dt, w_sin[j])
            big["sc_w_in"][j] = mm_tn(f"l{i}_b_sc_in_w", s["h"], dt, out_parts=N_DEV)
        dx, dsh1, dsc1, dgm = rms_mod_bwd(f"l{i}_b_norm_mix", s["x"], dh, norm_mix_g[i:i + 1], sc1[i], dx1, n_lat)
        small["norm_mix_g"][i] = dgm[0]
        dmod[i] = jnp.stack([dsh1, dsc1, dg1, dsh2, dsc2, dg2], axis=1)
    grad_x = dx[:n_lat][None]

    small = {nm: jnp.stack(v) for nm, v in small.items()}
    small["dmod"] = jnp.stack(dmod)
    pack = _Pack({nm: v.shape for nm, v in small.items()})
    (small_all,) = all_gather("gather_small", [pack.pack(small)])
    small_sum = pack.unpack(sum_parts("sum_small", small_all))
    dmod_all = pack.unpack(small_all, lead=(N_DEV,))["dmod"]
    dmod_ctx = small_sum["dmod"][:, 1].reshape(depth, N_MOD * d)
    grads = {nm: small_sum[nm] for nm in small if nm != "dmod"}
    grads["b_ada"] = (small_sum["dmod"][:, 0] + small_sum["dmod"][:, 1]).reshape(depth, N_MOD * d)
    my_cols = lambda a, width: lax.dynamic_slice_in_dim(a, me * width, width, axis=-1)
    grads["ffn_conv_w"] = my_cols(grads["ffn_conv_w"], nup)
    grads["sc_conv_w"] = my_cols(grads["sc_conv_w"], sc_conv_w.shape[-1])

    drows = jnp.concatenate([dmod_all[:, :, 0], dmod_all[:, :, 1]]).reshape(2 * N_DEV, depth, N_MOD * d)
    drows = my_cols(drows, ncol).reshape(2 * N_DEV, depth * ncol)
    cond2 = jnp.concatenate([c_all.reshape(N_DEV, d), jnp.broadcast_to(c_ctx[None], (N_DEV, d))])
    g_w_ada = mm_tn("ada_bwd_w", cond2, drows, out_parts=depth, out_dtype=F32, a_silu=True)
    dctx_rows = jnp.pad(my_cols(dmod_ctx, ncol).reshape(1, depth * ncol), ((0, 15), (0, 0)))
    dcc = mm_nt("ada_bwd_c", dctx_rows, w_ada)[0:1]
    (dcc_all,) = all_gather("gather_c_ctx", [jnp.pad(dcc.reshape(-1, LANE), ((0, (-d // LANE) % 8), (0, 0)))])
    dcc_sum = sum_parts("sum_c_ctx", dcc_all).reshape(-1)[:d]
    sig = jax.nn.sigmoid(c_ctx)
    grads["c_ctx"] = dcc_sum * (sig * (1.0 + c_ctx * (1.0 - sig)))

    order = [(nm, idx) for nm in big for idx in range(len(big[nm]))]
    parts = all_to_all("scatter_grads", [big[nm][idx] for nm, idx in order])
    out = {nm: [None] * 4 for nm in names}
    per_big = {nm: [] for nm in big}
    for (nm, idx), p in zip(order, parts):
        w2 = weights[nm][idx]
        per_big[nm].append(adamw(f"adamw_{nm}{idx}", p.reshape(N_DEV, w2.shape[0], -1), w2, mom_m[nm][idx], mom_v[nm][idx]))
    for nm, res in per_big.items():
        out[nm] = [jnp.stack([r[q] for r in res]) for q in range(4)]
    res = [adamw(f"adamw_w_ada{i}", g_w_ada[i][None], w_ada[i], m_w_ada[i], v_w_ada[i]) for i in range(depth)]
    out["w_ada"] = [jnp.stack([r[q] for r in res]) for q in range(4)]
    small_names = [nm for nm in names if nm not in big and nm != "w_ada"]
    spack = _Pack({nm: weights[nm].shape for nm in small_names})
    flat = [spack.pack({nm: src[nm] for nm in small_names}) for src in (grads, weights, mom_m, mom_v)]
    res = adamw("adamw_small", flat[0][None], flat[1], flat[2], flat[3])
    res = [spack.unpack(r) for r in res]
    for nm in small_names:
        out[nm] = [grads[nm].reshape(weights[nm].shape)] + [res[q][nm] for q in range(1, 4)]

    return (loss, grad_x, *[out[nm][0] for nm in names], *[out[nm][1] for nm in names],
            *[out[nm][2] for nm in names], *[out[nm][3] for nm in names])
```

```python
import functools
import math

import numpy as np
import jax
import jax.numpy as jnp
from jax import lax
from jax.experimental import pallas as pl
from jax.experimental.pallas import tpu as pltpu

F32 = jnp.float32
BF16 = jnp.bfloat16
MESH = pl.DeviceIdType.MESH
AXES = ("x", "y", "c")
N_DEV = 8
N_MOD = 6
N_MIXERS = 3
EPS = 1e-6
GRID_W = 64
WIN_H = 8
WIN_W = 16
QROWS = 2
GM_CHUNK = 128
LANE = 128
NEG = -1e30
ADAM_LR = 0.001
ADAM_B1 = 0.9
ADAM_B2 = 0.999
ADAM_EPS = 1e-08
ADAM_WD = 0.01
ADAM_STEP = 10
VMEM_LIMIT = 56 * 1024 * 1024
HBM = pl.BlockSpec(memory_space=pltpu.HBM)

NN = (((1,), (0,)), ((), ()))
NT = (((1,), (1,)), ((), ()))
TN = (((0,), (0,)), ((), ()))


def _params(*sem):
    return pltpu.CompilerParams(dimension_semantics=sem, vmem_limit_bytes=VMEM_LIMIT)


def _tile(n, pref, mult):
    best = None
    for t in range(mult, min(n, pref) + 1, mult):
        if n % t == 0:
            best = t
    return n if best is None else best


def _full(arr):
    nd = arr.ndim
    return pl.BlockSpec(arr.shape, lambda *g: (0,) * nd)


def _logical(shape):
    return tuple(shape) if len(shape) == 2 else (shape[1], shape[0] * shape[2])


def _cspec(shape, tr, tc, rc):
    if len(shape) == 2:
        return pl.BlockSpec((tr, tc), rc)
    cpp = shape[2] // tc

    def imap(*g):
        r, c = rc(*g)
        return (c // cpp, r, c % cpp)

    return pl.BlockSpec((None, tr, tc), imap)


def _dot(a, b, dims, exact=False):
    if exact:
        return lax.dot_general(a, b, dims, precision=lax.Precision.HIGHEST, preferred_element_type=F32)
    return lax.dot_general(a.astype(BF16), b.astype(BF16), dims, preferred_element_type=F32)


def _silu(z):
    return z * jax.nn.sigmoid(z)


def _mm(name, a, b, kind, out_shape, out_dtype, tiles, a_silu=False, exact=False):
    la, lb, lo = _logical(a.shape), _logical(b.shape), _logical(out_shape)
    t0, t1, t2 = tiles
    if kind == "nn":
        grid = (lo[1] // t1, lo[0] // t0, la[1] // t2)
        a_spec = _cspec(a.shape, t0, t2, lambda j, i, k: (i, k))
        b_spec = _cspec(b.shape, t2, t1, lambda j, i, k: (k, j))
        o_spec = _cspec(out_shape, t0, t1, lambda j, i, k: (i, j))
        dims, acc = NN, (t0, t1)
    elif kind == "nt":
        grid = (lo[1] // t1, lo[0] // t0, la[1] // t2)
        a_spec = _cspec(a.shape, t0, t2, lambda p, i, r: (i, r))
        b_spec = _cspec(b.shape, t1, t2, lambda p, i, r: (p, r))
        o_spec = _cspec(out_shape, t0, t1, lambda p, i, r: (i, p))
        dims, acc = NT, (t0, t1)
    else:
        grid = (lo[1] // t1, lo[0] // t0, la[0] // t2)
        a_spec = _cspec(a.shape, t2, t0, lambda j, kk, r: (r, kk))
        b_spec = _cspec(b.shape, t2, t1, lambda j, kk, r: (r, j))
        o_spec = _cspec(out_shape, t0, t1, lambda j, kk, r: (kk, j))
        dims, acc = TN, (t0, t1)
    nk = grid[2]

    def body(a_ref, b_ref, o_ref, acc_ref):
        k = pl.program_id(2)
        av = a_ref[...]
        if a_silu:
            av = _silu(av)
        part = _dot(av, b_ref[...], dims, exact)

        @pl.when(k == 0)
        def _():
            acc_ref[...] = part

        @pl.when(k > 0)
        def _():
            acc_ref[...] += part

        @pl.when(k == nk - 1)
        def _():
            o_ref[...] = acc_ref[...].astype(o_ref.dtype)

    return pl.pallas_call(
        body, name=name, grid=grid, in_specs=[a_spec, b_spec], out_specs=o_spec,
        out_shape=jax.ShapeDtypeStruct(out_shape, out_dtype),
        scratch_shapes=[pltpu.VMEM(acc, F32)],
        compiler_params=_params("parallel", "parallel", "arbitrary"),
    )(a, b)


def mm_nn(name, a, w, out_parts=1, out_dtype=F32, **kw):
    (m, _), (_, n) = _logical(a.shape), _logical(w.shape)
    out_shape = (m, n) if out_parts == 1 else (out_parts, m, n // out_parts)
    tiles = (_tile(m, 768, 16), _tile(math.gcd(w.shape[-1], out_shape[-1]), 1536, LANE),
             _tile(math.gcd(a.shape[-1], w.shape[-2]), 2048, LANE))
    return _mm(name, a, w, "nn", out_shape, out_dtype, tiles, **kw)


def mm_nt(name, a, w, out_dtype=F32, **kw):
    (m, _), (p, _) = _logical(a.shape), _logical(w.shape)
    tiles = (_tile(m, 768, 16), _tile(w.shape[-2], 2048, LANE), _tile(math.gcd(a.shape[-1], w.shape[-1]), 1536, LANE))
    return _mm(name, a, w, "nt", (m, p), out_dtype, tiles, **kw)


def mm_tn(name, a, dy, out_parts=1, out_dtype=BF16, **kw):
    (t, k), (_, n) = _logical(a.shape), _logical(dy.shape)
    out_shape = (k, n) if out_parts == 1 else (out_parts, k, n // out_parts)
    tiles = (_tile(a.shape[-1], 1024, LANE), _tile(math.gcd(dy.shape[-1], out_shape[-1]), 1536, LANE), _tile(t, 768, 16))
    return _mm(name, a, dy, "tn", out_shape, out_dtype, tiles, **kw)


def _rows(tr, d):
    return pl.BlockSpec((tr, d), lambda i: (i, 0))


def _pick(ctx, ref):
    return jnp.where(ctx, ref[1:2, :], ref[0:1, :])


def resid_rms_mod(name, x, y, gate, g, sc, sh, n_lat, tr=256):
    t, d = x.shape
    nlt = n_lat // tr
    has_res = y is not None

    def body(*refs):
        if has_res:
            x_ref, y_ref, gate_ref, g_ref, sc_ref, sh_ref, x1_ref, h_ref = refs
        else:
            x_ref, g_ref, sc_ref, sh_ref, h_ref = refs
        ctx = pl.program_id(0) >= nlt
        xv = x_ref[...]
        if has_res:
            xv = xv + _pick(ctx, gate_ref) * y_ref[...]
            x1_ref[...] = xv
        r = lax.rsqrt(jnp.mean(xv * xv, axis=-1, keepdims=True) + EPS)
        n = xv * r * g_ref[...]
        h_ref[...] = (n * (1.0 + _pick(ctx, sc_ref)) + _pick(ctx, sh_ref)).astype(h_ref.dtype)

    row = _rows(tr, d)
    if has_res:
        ins, in_specs = (x, y, gate, g, sc, sh), [row, row, _full(gate), _full(g), _full(sc), _full(sh)]
        out_shape = (jax.ShapeDtypeStruct((t, d), F32), jax.ShapeDtypeStruct((t, d), BF16))
        out_specs = (row, row)
    else:
        ins, in_specs = (x, g, sc, sh), [row, _full(g), _full(sc), _full(sh)]
        out_shape = jax.ShapeDtypeStruct((t, d), BF16)
        out_specs = row
    return pl.pallas_call(body, name=name, grid=(t // tr,), in_specs=in_specs, out_specs=out_specs,
                          out_shape=out_shape, compiler_params=_params("parallel"))(*ins)


def _acc_rows(i, nlt, ref, val):
    @pl.when(i == 0)
    def _():
        ref[...] = jnp.zeros_like(ref)

    @pl.when(i < nlt)
    def _():
        ref[0:1, :] += val

    @pl.when(i >= nlt)
    def _():
        ref[1:2, :] += val


def rms_mod_bwd(name, x, dh, g, sc, dres, n_lat, branch=None, tr=256):
    t, d = x.shape
    nlt = n_lat // tr

    def body(x_ref, dh_ref, g_ref, sc_ref, dres_ref, *rest):
        if branch is None:
            dx_ref, dsh_ref, dsc_ref, dg_ref = rest
        else:
            y_ref, gate_ref, dx_ref, dsh_ref, dsc_ref, dg_ref, dy_ref, dgate_ref = rest
        i = pl.program_id(0)
        xv, dhv, gv = x_ref[...], dh_ref[...], g_ref[...]
        r = lax.rsqrt(jnp.mean(xv * xv, axis=-1, keepdims=True) + EPS)
        xhat = xv * r
        dn = dhv * (1.0 + _pick(i >= nlt, sc_ref))
        dxhat = dn * gv
        dxv = r * (dxhat - xhat * jnp.mean(dxhat * xhat, axis=-1, keepdims=True)) + dres_ref[...]
        dx_ref[...] = dxv
        if branch is not None:
            dy_ref[...] = (_pick(i >= nlt, gate_ref) * dxv).astype(dy_ref.dtype)
            _acc_rows(i, nlt, dgate_ref, jnp.sum(dxv * y_ref[...], axis=0, keepdims=True))
        _acc_rows(i, nlt, dsh_ref, jnp.sum(dhv, axis=0, keepdims=True))
        _acc_rows(i, nlt, dsc_ref, jnp.sum(dhv * (xhat * gv), axis=0, keepdims=True))
        dgp = jnp.sum(dn * xhat, axis=0, keepdims=True)

        @pl.when(i == 0)
        def _():
            dg_ref[...] = dgp

        @pl.when(i > 0)
        def _():
            dg_ref[...] += dgp

    row = _rows(tr, d)
    two = pl.BlockSpec((2, d), lambda i: (0, 0))
    two_shape = jax.ShapeDtypeStruct((2, d), F32)
    ins, in_specs = [x, dh, g, sc, dres], [row, row, _full(g), _full(sc), row]
    out_specs = [row, two, two, pl.BlockSpec((1, d), lambda i: (0, 0))]
    out_shape = [jax.ShapeDtypeStruct((t, d), F32), two_shape, two_shape, jax.ShapeDtypeStruct((1, d), F32)]
    if branch is not None:
        ins, in_specs = ins + list(branch), in_specs + [row, _full(branch[1])]
        out_specs, out_shape = out_specs + [row, two], out_shape + [jax.ShapeDtypeStruct((t, d), BF16), two_shape]
    return pl.pallas_call(body, name=name, grid=(t // tr,), in_specs=in_specs, out_specs=out_specs, out_shape=out_shape,
                          compiler_params=_params("arbitrary"))(*ins)


def loss_head(name, x1, f, gate, target, tr=256):
    t, d = x1.shape
    nlt = target.shape[0] // tr

    def body(x_ref, f_ref, gate_ref, t_ref, dx_ref, loss_ref, df_ref, dgate_ref, acc_ref):
        i = pl.program_id(0)

        @pl.when(i == 0)
        def _():
            acc_ref[...] = jnp.zeros_like(acc_ref)
            dgate_ref[...] = jnp.zeros_like(dgate_ref)

        @pl.when(i < nlt)
        def _():
            fv, gv = f_ref[...], gate_ref[0:1, :]
            e = x_ref[...] + gv * fv - t_ref[...]
            dxv = e / d
            dx_ref[...] = dxv
            df_ref[...] = (gv * dxv).astype(df_ref.dtype)
            dgate_ref[0:1, :] += jnp.sum(dxv * fv, axis=0, keepdims=True)
            acc_ref[...] += jnp.sum(e * e, axis=0, keepdims=True)

        @pl.when(i >= nlt)
        def _():
            dx_ref[...] = jnp.zeros_like(dx_ref)
            df_ref[...] = jnp.zeros_like(df_ref)

        @pl.when(i == t // tr - 1)
        def _():
            loss_ref[...] = jnp.sum(acc_ref[...], axis=1, keepdims=True) * (0.5 / d)

    row = _rows(tr, d)
    return pl.pallas_call(
        body, name=name, grid=(t // tr,),
        in_specs=[row, row, _full(gate), pl.BlockSpec((tr, d), lambda i: (jnp.minimum(i, nlt - 1), 0))],
        out_specs=(row, pl.BlockSpec((1, 1), lambda i: (0, 0)), row, pl.BlockSpec((2, d), lambda i: (0, 0))),
        out_shape=(jax.ShapeDtypeStruct((t, d), F32), jax.ShapeDtypeStruct((1, 1), F32),
                   jax.ShapeDtypeStruct((t, d), BF16), jax.ShapeDtypeStruct((2, d), F32)),
        scratch_shapes=[pltpu.VMEM((1, d), F32)],
        compiler_params=_params("arbitrary"))(x1, f, gate, target)


HALO = 8


def _halo_specs(shape, tr, tc, col):
    hb = tr // HALO
    last = _logical(shape)[0] // HALO - 1
    main = _cspec(shape, tr, tc, lambda j, i: (i, col(j)))
    prev = _cspec(shape, HALO, tc, lambda j, i: (jnp.maximum(i * hb - 1, 0), col(j)))
    nxt = _cspec(shape, HALO, tc, lambda j, i: (jnp.minimum((i + 1) * hb, last), col(j)))
    return [prev, main, nxt]


def _seq_edges(i, nlt, nt):
    first = (i == 0) | (i == nlt)
    last = (i == nlt - 1) | (i == nt - 1)
    return first, last


def _ext(prev_ref, main_ref, next_ref, first, last):
    p = jnp.where(first, 0.0, prev_ref[...].astype(F32))
    n = jnp.where(last, 0.0, next_ref[...].astype(F32))
    return jnp.concatenate([p, main_ref[...].astype(F32), n], axis=0)


def _up(e):
    return pltpu.roll(e, 1, 0)


def _down(e):
    return pltpu.roll(e, e.shape[0] - 1, 0)


def _conv(e, w):
    return _up(e) * w[0:1, :] + e * w[1:2, :] + _down(e) * w[2:3, :]


def _conv_t(e, w):
    return _down(e) * w[0:1, :] + e * w[1:2, :] + _up(e) * w[2:3, :]


def _mid(e, tr):
    return e[HALO:HALO + tr, :]


def _acc_cols(i, ref, val):
    @pl.when(i == 0)
    def _():
        ref[...] = val

    @pl.when(i > 0)
    def _():
        ref[...] += val


def _colsum(v):
    return jnp.sum(v, axis=0, keepdims=True)


def ffn_act_fwd(name, u, cw, cb, n_lat, tr=256):
    _, t, fp = u.shape
    tc = _tile(fp, 1536, LANE)
    nlt, nt = n_lat // tr, t // tr

    def body(pg, mg, ng, pu, mu, nu, cw_ref, cb_ref, a_ref):
        first, last = _seq_edges(pl.program_id(1), nlt, nt)
        zg = _mid(_conv(_ext(pg, mg, ng, first, last), cw_ref[0]), tr) + cb_ref[0]
        zu = _mid(_conv(_ext(pu, mu, nu, first, last), cw_ref[1]), tr) + cb_ref[1]
        a_ref[...] = (_silu(zg) * zu).astype(a_ref.dtype)

    ncol = fp // tc
    specs = _halo_specs(u.shape, tr, tc, lambda j: j) + _halo_specs(u.shape, tr, tc, lambda j: j + ncol)
    specs += [pl.BlockSpec((2, 3, tc), lambda j, i: (0, 0, j)), pl.BlockSpec((2, 1, tc), lambda j, i: (0, 0, j))]
    return pl.pallas_call(
        body, name=name, grid=(ncol, nt), in_specs=specs,
        out_specs=pl.BlockSpec((tr, tc), lambda j, i: (i, j)),
        out_shape=jax.ShapeDtypeStruct((t, fp), BF16),
        compiler_params=_params("parallel", "parallel"))(u, u, u, u, u, u, cw, cb)


def ffn_act_bwd(name, u, da, cw, cb, n_lat, tr=128):
    _, t, fp = u.shape
    tc = _tile(fp, 1536, LANE)
    nlt, nt = n_lat // tr, t // tr
    ncol = fp // tc

    def body(pg, mg, ng, pu, mu, nu, pa, ma, na, cw_ref, cb_ref, du_ref, dcw_ref, dcb_ref):
        i = pl.program_id(1)
        first, last = _seq_edges(i, nlt, nt)
        wg, wu = cw_ref[0], cw_ref[1]
        ug, uu = _ext(pg, mg, ng, first, last), _ext(pu, mu, nu, first, last)
        dae = _ext(pa, ma, na, first, last)
        zg = _conv(ug, wg) + cb_ref[0]
        zu = _conv(uu, wu) + cb_ref[1]
        sg = jax.nn.sigmoid(zg)
        dzg = dae * zu * (sg * (1.0 + zg * (1.0 - sg)))
        dzu = dae * (zg * sg)
        du_ref[0] = _mid(_conv_t(dzg, wg), tr).astype(du_ref.dtype)
        du_ref[1] = _mid(_conv_t(dzu, wu), tr).astype(du_ref.dtype)
        for h, (dz, ue) in enumerate(((dzg, ug), (dzu, uu))):
            dzm = _mid(dz, tr)
            rows = [_colsum(dzm * _mid(_up(ue), tr)), _colsum(dzm * _mid(ue, tr)), _colsum(dzm * _mid(_down(ue), tr))]
            _acc_cols(i, dcw_ref.at[h], jnp.concatenate(rows, axis=0))
            _acc_cols(i, dcb_ref.at[h], _colsum(dzm))

    specs = _halo_specs(u.shape, tr, tc, lambda j: j) + _halo_specs(u.shape, tr, tc, lambda j: j + ncol)
    specs += _halo_specs(da.shape, tr, tc, lambda j: j)
    specs += [pl.BlockSpec((2, 3, tc), lambda j, i: (0, 0, j)), pl.BlockSpec((2, 1, tc), lambda j, i: (0, 0, j))]
    return pl.pallas_call(
        body, name=name, grid=(ncol, nt), in_specs=specs,
        out_specs=(pl.BlockSpec((2, tr, tc), lambda j, i: (0, i, j)),
                   pl.BlockSpec((2, 3, tc), lambda j, i: (0, 0, j)), pl.BlockSpec((2, 1, tc), lambda j, i: (0, 0, j))),
        out_shape=(jax.ShapeDtypeStruct((2, t, fp), BF16), jax.ShapeDtypeStruct((2, 3, fp), F32),
                   jax.ShapeDtypeStruct((2, 1, fp), F32)),
        compiler_params=_params("parallel", "arbitrary"))(u, u, u, u, u, u, da, da, da, cw, cb)


def sc_gate_fwd(name, tmat, cw, n_lat, tr=256):
    t, d3 = tmat.shape
    d = d3 // 3
    tc = _tile(d, 512, LANE)
    ncol = d // tc
    nlt, nt = n_lat // tr, t // tr

    def body(b_ref, pc, mc, nc, px, mx, nx, cw_ref, s_ref):
        first, last = _seq_edges(pl.program_id(1), nlt, nt)
        p = _ext(pc, mc, nc, first, last) * _ext(px, mx, nx, first, last)
        s_ref[...] = (b_ref[...] * _mid(_conv(p, cw_ref[...]), tr)).astype(s_ref.dtype)

    specs = [pl.BlockSpec((tr, tc), lambda j, i: (i, j))]
    specs += _halo_specs(tmat.shape, tr, tc, lambda j: j + ncol) + _halo_specs(tmat.shape, tr, tc, lambda j: j + 2 * ncol)
    specs += [pl.BlockSpec((3, tc), lambda j, i: (0, j))]
    return pl.pallas_call(
        body, name=name, grid=(ncol, nt), in_specs=specs, out_specs=pl.BlockSpec((tr, tc), lambda j, i: (i, j)),
        out_shape=jax.ShapeDtypeStruct((t, d), BF16),
        compiler_params=_params("parallel", "parallel"))(*([tmat] * 7), cw)


def sc_gate_bwd(name, tmat, ds, cw, n_lat, tr=128):
    t, d3 = tmat.shape
    d = d3 // 3
    tc = _tile(d, 512, LANE)
    ncol = d // tc
    nlt, nt = n_lat // tr, t // tr

    def body(pb, mb, nb, pc, mc, nc, px, mx, nx, pd, md, nd, cw_ref, dt_ref, dcw_ref):
        i = pl.program_id(1)
        first, last = _seq_edges(i, nlt, nt)
        w = cw_ref[...]
        be, ce, xe = _ext(pb, mb, nb, first, last), _ext(pc, mc, nc, first, last), _ext(px, mx, nx, first, last)
        dse = _ext(pd, md, nd, first, last)
        p = ce * xe
        dcv = dse * be
        dp = _conv_t(dcv, w)
        dt_ref[0] = _mid(dse * _conv(p, w), tr).astype(dt_ref.dtype)
        dt_ref[1] = _mid(dp * xe, tr).astype(dt_ref.dtype)
        dt_ref[2] = _mid(dp * ce, tr).astype(dt_ref.dtype)
        dm = _mid(dcv, tr)
        rows = [_colsum(dm * _mid(_up(p), tr)), _colsum(dm * _mid(p, tr)), _colsum(dm * _mid(_down(p), tr))]
        _acc_cols(i, dcw_ref, jnp.concatenate(rows, axis=0))

    specs = []
    for part in range(3):
        specs += _halo_specs(tmat.shape, tr, tc, functools.partial(lambda j, part: j + part * ncol, part=part))
    specs += _halo_specs(ds.shape, tr, tc, lambda j: j)
    specs += [pl.BlockSpec((3, tc), lambda j, i: (0, j))]
    return pl.pallas_call(
        body, name=name, grid=(ncol, nt), in_specs=specs,
        out_specs=(pl.BlockSpec((3, tr, tc), lambda j, i: (0, i, j)), pl.BlockSpec((3, tc), lambda j, i: (0, j))),
        out_shape=(jax.ShapeDtypeStruct((3, t, d), BF16), jax.ShapeDtypeStruct((3, d), F32)),
        compiler_params=_params("parallel", "arbitrary"))(*([tmat] * 9), ds, ds, ds, cw)


_GELU_K = 0.7978845608028654
_GELU_C = 0.044715


def _gelu(x):
    return 0.5 * x * (1.0 + jnp.tanh(_GELU_K * (x + _GELU_C * (x * x * x))))


def _gelu_grad(x):
    th = jnp.tanh(_GELU_K * (x + _GELU_C * (x * x * x)))
    return 0.5 * (1.0 + th) + 0.5 * x * (1.0 - th * th) * (_GELU_K * (1.0 + 3.0 * _GELU_C * (x * x)))


def gmlp_gate_fwd(name, tmat, vg, ws, bs):
    t, w2 = tmat.shape
    w = w2 // 2
    groups = ws.shape[0]
    gd = w // groups

    def body(t_ref, vg_ref, ws_ref, bs_ref, o_ref):
        v = _gelu(t_ref[:, w:])
        r = lax.rsqrt(jnp.mean(v * v, axis=-1, keepdims=True) + EPS)
        vn = (v * r * vg_ref[...]).astype(BF16)
        for g in range(groups):
            cols = slice(g * gd, (g + 1) * gd)
            sv = _dot(ws_ref[g], vn[:, cols], NN) + bs_ref[g]
            o_ref[:, cols] = (_gelu(t_ref[:, cols]) * sv).astype(o_ref.dtype)

    return pl.pallas_call(
        body, name=name, grid=(t // GM_CHUNK,),
        in_specs=[_rows(GM_CHUNK, w2), _full(vg), _full(ws), _full(bs)], out_specs=_rows(GM_CHUNK, w),
        out_shape=jax.ShapeDtypeStruct((t, w), BF16), compiler_params=_params("parallel"))(tmat, vg, ws, bs)


def gmlp_gate_bwd(name, tmat, dout, vg, ws, bs):
    t, w2 = tmat.shape
    w = w2 // 2
    groups = ws.shape[0]
    gd = w // groups

    def body(t_ref, do_ref, vg_ref, ws_ref, bs_ref, dt_ref, dvg_ref, dws_ref, dsv_ref, dvn_ref):
        i = pl.program_id(0)
        tv = t_ref[:, w:]
        v = _gelu(tv)
        r = lax.rsqrt(jnp.mean(v * v, axis=-1, keepdims=True) + EPS)
        vhat = v * r
        vn = (vhat * vg_ref[...]).astype(BF16)
        for g in range(groups):
            cols = slice(g * gd, (g + 1) * gd)
            tu = t_ref[:, cols]
            dov = do_ref[:, cols]
            sv = _dot(ws_ref[g], vn[:, cols], NN) + bs_ref[g]
            dt_ref[:, cols] = (dov * sv * _gelu_grad(tu)).astype(dt_ref.dtype)
            dsv = dov * _gelu(tu)
            _acc_cols(i, dsv_ref.at[:, cols], dsv)
            _acc_cols(i, dws_ref.at[g], _dot(dsv, vn[:, cols], NT))
            dvn_ref[:, cols] = _dot(ws_ref[g], dsv, TN)
        dvn = dvn_ref[...]
        _acc_cols(i, dvg_ref, _colsum(dvn * vhat))
        dvhat = dvn * vg_ref[...]
        dv = r * (dvhat - vhat * jnp.mean(dvhat * vhat, axis=-1, keepdims=True))
        dt_ref[:, w:] = (dv * _gelu_grad(tv)).astype(dt_ref.dtype)

    keep = lambda shape: pl.BlockSpec(shape, lambda i: (0,) * len(shape))
    return pl.pallas_call(
        body, name=name, grid=(t // GM_CHUNK,),
        in_specs=[_rows(GM_CHUNK, w2), _rows(GM_CHUNK, w), _full(vg), _full(ws), _full(bs)],
        out_specs=(_rows(GM_CHUNK, w2), keep((1, w)), keep(ws.shape), keep((GM_CHUNK, w))),
        out_shape=(jax.ShapeDtypeStruct((t, w2), BF16), jax.ShapeDtypeStruct((1, w), F32),
                   jax.ShapeDtypeStruct(ws.shape, F32), jax.ShapeDtypeStruct((GM_CHUNK, w), F32)),
        scratch_shapes=[pltpu.VMEM((GM_CHUNK, w), F32)],
        compiler_params=_params("arbitrary"))(tmat, dout, vg, ws, bs)


def qk_norm_fwd(name, qkv, qg, kg, hd, tr=128):
    t, d3 = qkv.shape
    d = d3 // 3

    def body(x_ref, qg_ref, kg_ref, q_ref, k_ref, v_ref):
        for part, (g_ref, o_ref) in enumerate(((qg_ref, q_ref), (kg_ref, k_ref))):
            for h in range(d // hd):
                xh = x_ref[:, part * d + h * hd: part * d + (h + 1) * hd]
                r = lax.rsqrt(jnp.mean(xh * xh, axis=-1, keepdims=True) + EPS)
                o_ref[:, h * hd:(h + 1) * hd] = (xh * r * g_ref[...]).astype(o_ref.dtype)
        v_ref[...] = x_ref[:, 2 * d:].astype(v_ref.dtype)

    out = jax.ShapeDtypeStruct((t, d), BF16)
    return pl.pallas_call(
        body, name=name, grid=(t // tr,), in_specs=[_rows(tr, d3), _full(qg), _full(kg)],
        out_specs=(_rows(tr, d),) * 3, out_shape=(out,) * 3, compiler_params=_params("parallel"))(qkv, qg, kg)


def qk_norm_bwd(name, qkv, dq, dk, dv, qg, kg, hd, tr=128):
    t, d3 = qkv.shape
    d = d3 // 3

    def body(x_ref, dq_ref, dk_ref, dv_ref, qg_ref, kg_ref, o_ref, dqg_ref, dkg_ref):
        i = pl.program_id(0)
        for part, (g_ref, dn_ref, dg_ref) in enumerate(((qg_ref, dq_ref, dqg_ref), (kg_ref, dk_ref, dkg_ref))):
            dg = jnp.zeros((1, hd), F32)
            for h in range(d // hd):
                xh = x_ref[:, part * d + h * hd: part * d + (h + 1) * hd]
                dn = dn_ref[:, h * hd:(h + 1) * hd]
                r = lax.rsqrt(jnp.mean(xh * xh, axis=-1, keepdims=True) + EPS)
                xhat = xh * r
                dg = dg + _colsum(dn * xhat)
                dxhat = dn * g_ref[...]
                dx = r * (dxhat - xhat * jnp.mean(dxhat * xhat, axis=-1, keepdims=True))
                o_ref[:, part * d + h * hd: part * d + (h + 1) * hd] = dx.astype(o_ref.dtype)
            _acc_cols(i, dg_ref, dg)
        o_ref[:, 2 * d:] = dv_ref[...].astype(o_ref.dtype)

    one = pl.BlockSpec((1, hd), lambda i: (0, 0))
    return pl.pallas_call(
        body, name=name, grid=(t // tr,),
        in_specs=[_rows(tr, d3), _rows(tr, d), _rows(tr, d), _rows(tr, d), _full(qg), _full(kg)],
        out_specs=(_rows(tr, d3), one, one),
        out_shape=(jax.ShapeDtypeStruct((t, d3), BF16), jax.ShapeDtypeStruct((1, hd), F32), jax.ShapeDtypeStruct((1, hd), F32)),
        compiler_params=_params("arbitrary"))(qkv, dq, dk, dv, qg, kg)


def _na_geometry(n_lat):
    rows = n_lat // GRID_W
    kh = min(WIN_H, rows)
    nb = min(kh + QROWS - 1, rows)
    n_blk = rows // QROWS
    q_row_off = np.repeat(np.arange(QROWS), GRID_W)
    q_col = np.tile(np.arange(GRID_W), QROWS)
    k_row_off = np.repeat(np.arange(nb), GRID_W)
    k_col = np.tile(np.arange(GRID_W), nb)
    c_start = np.clip(q_col - WIN_W // 2, 0, GRID_W - WIN_W)
    col_ok = (k_col[None, :] >= c_start[:, None]) & (k_col[None, :] < c_start[:, None] + WIN_W)
    dc_idx = np.clip(k_col[None, :] - q_col[:, None], -(WIN_W - 1), WIN_W - 1) + WIN_W - 1

    def block(blk):
        r0 = blk * QROWS
        q_row = r0 + q_row_off
        r_start = np.clip(q_row - kh // 2, 0, rows - kh)
        band0 = min(int(np.clip(r0 - kh // 2, 0, rows - kh)), rows - nb)
        k_row = band0 + k_row_off
        ok = col_ok & (k_row[None, :] >= r_start[:, None]) & (k_row[None, :] < r_start[:, None] + kh)
        dr_idx = np.clip(k_row[None, :] - q_row[:, None], -(WIN_H - 1), WIN_H - 1) + WIN_H - 1
        return band0, ok, dr_idx

    reps = [0, 1, 2, n_blk - 2, n_blk - 1]
    variant = lambda blk: 0 if blk == 0 else 1 if blk == 1 else 3 if blk == n_blk - 2 else 4 if blk == n_blk - 1 else 2
    geo = [block(b) for b in reps]
    for blk in range(n_blk):
        _, ok, dr = block(blk)
        assert np.array_equal(ok, geo[variant(blk)][1]) and np.array_equal(np.where(ok, dr, 0), np.where(ok, geo[variant(blk)][2], 0))
    return dict(rows=rows, kh=kh, nb=nb, n_blk=n_blk, reps=reps, ok=[g[1] for g in geo], dr=[g[2] for g in geo],
                band0=[g[0] for g in geo], dc=dc_idx)


def _na_onehots(geo):
    nb = geo["nb"]
    w2 = GRID_W * GRID_W
    qc, kc = np.meshgrid(np.arange(GRID_W), np.arange(GRID_W), indexing="ij")
    diff = (kc - qc).reshape(-1)
    cols = np.zeros((w2, LANE), np.float32)
    sel = np.abs(diff) <= WIN_W - 1
    cols[np.arange(w2)[sel], diff[sel] + WIN_W - 1] = 1.0
    npair = len(geo["reps"]) * QROWS * nb
    kpad = -(-npair // LANE) * LANE
    rows = np.zeros((16, kpad), np.float32)
    for vi, blk in enumerate(geo["reps"]):
        for qr in range(QROWS):
            for kr in range(nb):
                dr = (geo["band0"][vi] + kr) - (blk * QROWS + qr)
                if abs(dr) <= WIN_H - 1:
                    rows[dr + WIN_H - 1, (vi * QROWS + qr) * nb + kr] = 1.0
    return cols, rows, npair, kpad


def na_bias_table(name, rpb, geo):
    n_heads, nb, nv = rpb.shape[0], geo["nb"], len(geo["reps"])
    cols, rows, npair, kpad = _na_onehots(geo)
    rpb_p = jnp.pad(rpb, ((0, 0), (0, 16 - rpb.shape[1]), (0, LANE - rpb.shape[2]))).transpose(1, 0, 2).reshape(16, n_heads * LANE)
    t1 = _mm(name + "_rows", jnp.asarray(rows.T), rpb_p, "nn", (kpad, n_heads * LANE), F32,
             (kpad, _tile(n_heads * LANE, 1024, LANE), 16), exact=True)
    t1 = t1[:npair].reshape(nv, QROWS * nb, n_heads, LANE).transpose(0, 2, 1, 3).reshape(nv * n_heads * QROWS * nb, LANE)
    m = t1.shape[0]
    flat = _mm(name + "_cols", t1, jnp.asarray(cols), "nt", (m, GRID_W * GRID_W), F32,
               (_tile(m, 512, 8), _tile(GRID_W * GRID_W, 2048, LANE), LANE), exact=True)
    tab = flat.reshape(nv, n_heads, QROWS, nb, GRID_W, GRID_W).transpose(0, 1, 2, 4, 3, 5).reshape(nv, n_heads, QROWS * GRID_W, nb * GRID_W)
    return jnp.where(jnp.asarray(np.stack(geo["ok"]))[:, None], tab, NEG)


def _na_tile_info(qt, geo):
    n_blk, rows, kh, nb = geo["n_blk"], geo["rows"], geo["kh"], geo["nb"]
    is_ctx = qt >= n_blk
    band0 = jnp.minimum(jnp.clip(qt * QROWS - kh // 2, 0, rows - kh), rows - nb)
    band0 = jnp.where(is_ctx, 0, band0)
    return is_ctx, pl.multiple_of(band0 * GRID_W, GRID_W)


def _na_variant(qt, n_blk):
    v = jnp.minimum(qt, 2) + (qt >= n_blk - 2).astype(jnp.int32) + (qt >= n_blk - 1).astype(jnp.int32)
    return jnp.minimum(v, 4)


def _na_probs(q, kb, kc, bias, is_ctx, scale):
    s_lat = _dot(q, kb, NT) * scale + bias
    s_lat = jnp.where(is_ctx, NEG, s_lat)
    s_ctx = _dot(q, kc, NT) * scale
    m = jnp.maximum(jnp.max(s_lat, axis=-1, keepdims=True), jnp.max(s_ctx, axis=-1, keepdims=True))
    e_lat, e_ctx = jnp.exp(s_lat - m), jnp.exp(s_ctx - m)
    den = jnp.sum(e_lat, axis=-1, keepdims=True) + jnp.sum(e_ctx, axis=-1, keepdims=True)
    return e_lat / den, e_ctx / den


def na_attention_fwd(name, q, k, v, bias, geo, n_lat, hd):
    t, d = q.shape
    qw, nk = QROWS * GRID_W, geo["nb"] * GRID_W
    n_blk = geo["n_blk"]
    scale = hd ** -0.5

    def body(q_ref, k_ref, v_ref, b_ref, o_ref):
        is_ctx, start = _na_tile_info(pl.program_id(1), geo)
        kb, vb = k_ref[pl.ds(start, nk), :], v_ref[pl.ds(start, nk), :]
        kc, vc = k_ref[n_lat:, :], v_ref[n_lat:, :]
        p_lat, p_ctx = _na_probs(q_ref[...], kb, kc, b_ref[...], is_ctx, scale)
        o_ref[...] = (_dot(p_lat, vb, NN) + _dot(p_ctx, vc, NN)).astype(o_ref.dtype)

    head = pl.BlockSpec((t, hd), lambda h, i: (0, h))
    tile = pl.BlockSpec((qw, hd), lambda h, i: (i, h))
    return pl.pallas_call(
        body, name=name, grid=(d // hd, t // qw),
        in_specs=[tile, head, head, pl.BlockSpec((None, None, qw, nk), lambda h, i: (_na_variant(i, n_blk), h, 0, 0))],
        out_specs=tile, out_shape=jax.ShapeDtypeStruct((t, d), BF16),
        compiler_params=_params("parallel", "arbitrary"))(q, k, v, bias)


def na_attention_bwd(name, q, k, v, bias, do, geo, n_lat, hd):
    t, d = q.shape
    qw, nk = QROWS * GRID_W, geo["nb"] * GRID_W
    n_blk = geo["n_blk"]
    scale = hd ** -0.5

    def body(q_ref, k_ref, v_ref, b_ref, do_ref, dq_ref, dk_ref, dv_ref, db_ref):
        qt = pl.program_id(1)
        is_ctx, start = _na_tile_info(qt, geo)
        band = pl.ds(start, nk)
        qv, dov = q_ref[...], do_ref[...]
        kb, vb = k_ref[band, :], v_ref[band, :]
        kc, vc = k_ref[n_lat:, :], v_ref[n_lat:, :]
        p_lat, p_ctx = _na_probs(qv, kb, kc, b_ref[...], is_ctx, scale)
        dp_lat, dp_ctx = _dot(dov, vb, NT), _dot(dov, vc, NT)
        delta = jnp.sum(p_lat * dp_lat, axis=-1, keepdims=True) + jnp.sum(p_ctx * dp_ctx, axis=-1, keepdims=True)
        ds_lat, ds_ctx = p_lat * (dp_lat - delta), p_ctx * (dp_ctx - delta)
        fresh = (qt <= 2) | (qt == n_blk - 2) | (qt == n_blk - 1)

        @pl.when(fresh)
        def _():
            db_ref[...] = ds_lat

        @pl.when(jnp.logical_not(fresh))
        def _():
            db_ref[...] += ds_lat

        dsl, dsc = (ds_lat * scale).astype(BF16), (ds_ctx * scale).astype(BF16)
        dq_ref[...] = _dot(dsl, kb, NN) + _dot(dsc, kc, NN)

        @pl.when(qt == 0)
        def _():
            dk_ref[...] = jnp.zeros_like(dk_ref)
            dv_ref[...] = jnp.zeros_like(dv_ref)

        dk_ref[band, :] += _dot(dsl, qv, TN)
        dk_ref[n_lat:, :] += _dot(dsc, qv, TN)
        dv_ref[band, :] += _dot(p_lat, dov, TN)
        dv_ref[n_lat:, :] += _dot(p_ctx, dov, TN)

    head = pl.BlockSpec((t, hd), lambda h, i: (0, h))
    tile = pl.BlockSpec((qw, hd), lambda h, i: (i, h))
    btab = pl.BlockSpec((None, None, qw, nk), lambda h, i: (_na_variant(i, n_blk), h, 0, 0))
    full = jax.ShapeDtypeStruct((t, d), F32)
    return pl.pallas_call(
        body, name=name, grid=(d // hd, t // qw), in_specs=[tile, head, head, btab, tile],
        out_specs=(tile, head, head, btab), out_shape=(full, full, full, jax.ShapeDtypeStruct(bias.shape, F32)),
        compiler_params=_params("arbitrary", "arbitrary"))(q, k, v, bias, do)


def na_rpb_grad(name, dbias, geo, n_heads):
    nb = geo["nb"]
    nv = len(geo["reps"])
    w2 = GRID_W * GRID_W
    cols, rows, npair, kpad = _na_onehots(geo)
    xmat = dbias.reshape(nv, n_heads, QROWS, GRID_W, nb, GRID_W).transpose(0, 1, 2, 4, 3, 5).reshape(nv * n_heads * QROWS * nb, w2)
    m = xmat.shape[0]
    r = _mm(name + "_cols", xmat, jnp.asarray(cols), "nn", (m, LANE), F32, (_tile(m, 512, 8), LANE, _tile(w2, 1024, LANE)), exact=True)
    r2 = r.reshape(nv, n_heads, QROWS * nb, LANE).transpose(0, 2, 1, 3).reshape(npair, n_heads * LANE)
    r2 = jnp.pad(r2, ((0, kpad - npair), (0, 0)))
    out = _mm(name + "_rows", jnp.asarray(rows), r2, "nn", (16, n_heads * LANE), F32, (16, _tile(n_heads * LANE, 1024, LANE), kpad), exact=True)
    return out[:2 * WIN_H - 1].reshape(2 * WIN_H - 1, n_heads, LANE)[:, :, :2 * WIN_W - 1].transpose(1, 0, 2)


def _me():
    return lax.axis_index("x"), lax.axis_index("y"), lax.axis_index("c")


def _slot(p):
    return 4 * p[0] + 2 * p[1] + p[2]


def all_gather(name, arrays):
    n = len(arrays)

    def body(*refs):
        ins, outs = refs[:n], refs[n:2 * n]
        send_sems, recv_sems, local_sems = refs[2 * n:]
        x, y, c = _me()
        me, sib = (x, y, c), (x, y, 1 - c)
        chips = [(1 - x, y), (x, 1 - y), (1 - x, 1 - y)]

        def copy(a, k, block, to, src=None):
            dst = outs[a].at[_slot(block)]
            return pltpu.make_async_remote_copy(src_ref=dst if src is None else src, dst_ref=dst, send_sem=send_sems.at[a, k],
                                                recv_sem=recv_sems.at[a, k], device_id=to, device_id_type=MESH)

        sends, locals_ = [], []
        for a in range(n):
            mine = pltpu.make_async_copy(ins[a], outs[a].at[_slot(me)], local_sems.at[a])
            mine.start()
            locals_.append(mine)
            first = [copy(a, 0, me, sib, src=ins[a])] + [copy(a, 1 + j, me, (*chip, c), src=ins[a]) for j, chip in enumerate(chips)]
            for cp in first:
                cp.start()
            sends += first
        for j, chip in enumerate(chips):
            for a in range(n):
                copy(a, 1 + j, (*chip, c), me).wait_recv()
                passed = copy(a, 4 + j, (*chip, c), sib)
                passed.start()
                sends.append(passed)
        for a in range(n):
            copy(a, 0, sib, me).wait_recv()
            for j, chip in enumerate(chips):
                copy(a, 4 + j, (*chip, 1 - c), me).wait_recv()
        for cp in sends:
            cp.wait_send()
        for mine in locals_:
            mine.wait()

    outs = pl.pallas_call(
        body, name=name, in_specs=[HBM] * n, out_specs=[HBM] * n,
        out_shape=[jax.ShapeDtypeStruct((N_DEV, *a.shape), a.dtype) for a in arrays],
        scratch_shapes=[pltpu.SemaphoreType.DMA((n, 7)), pltpu.SemaphoreType.DMA((n, 7)), pltpu.SemaphoreType.DMA((n,))],
    )(*arrays)
    return list(outs)


def all_to_all(name, arrays):
    n = len(arrays)

    def body(*refs):
        ins, outs = refs[:n], refs[n:2 * n]
        send_sems, recv_sems, local_sems = refs[2 * n:]
        x, y, c = _me()
        me = (x, y, c)
        peers = [((x + (k >> 2)) % 2, (y + ((k >> 1) & 1)) % 2, (c + (k & 1)) % 2) for k in range(1, N_DEV)]

        def copy(a, k, src_block, dst_block, to):
            return pltpu.make_async_remote_copy(src_ref=ins[a].at[_slot(src_block)], dst_ref=outs[a].at[_slot(dst_block)],
                                                send_sem=send_sems.at[a, k], recv_sem=recv_sems.at[a, k], device_id=to, device_id_type=MESH)

        sends, locals_ = [], []
        for a in range(n):
            mine = pltpu.make_async_copy(ins[a].at[_slot(me)], outs[a].at[_slot(me)], local_sems.at[a])
            mine.start()
            locals_.append(mine)
            for k, peer in enumerate(peers):
                cp = copy(a, k, peer, me, peer)
                cp.start()
                sends.append(cp)
        for a in range(n):
            for k, peer in enumerate(peers):
                copy(a, k, me, peer, peer).wait_recv()
        for cp in sends:
            cp.wait_send()
        for mine in locals_:
            mine.wait()

    outs = pl.pallas_call(
        body, name=name, in_specs=[HBM] * n, out_specs=[HBM] * n,
        out_shape=[jax.ShapeDtypeStruct(a.shape, a.dtype) for a in arrays],
        scratch_shapes=[pltpu.SemaphoreType.DMA((n, 7)), pltpu.SemaphoreType.DMA((n, 7)), pltpu.SemaphoreType.DMA((n,))],
    )(*arrays)
    return list(outs)


def _adamw_math(w, g, m, v):
    m = ADAM_B1 * m + (1.0 - ADAM_B1) * g
    v = ADAM_B2 * v + (1.0 - ADAM_B2) * (g * g)
    m_hat = m / (1.0 - ADAM_B1 ** ADAM_STEP)
    v_hat = v / (1.0 - ADAM_B2 ** ADAM_STEP)
    delta = -ADAM_LR * (m_hat / (jnp.sqrt(v_hat) + ADAM_EPS) + ADAM_WD * w)
    return delta, m, v


def adamw(name, parts, w, m, v):
    r, c = w.shape
    npart, _, cp = parts.shape
    tc = c if (c % LANE or cp != c) else _tile(c, 512, LANE)
    tr = _tile(r, 256, 16 if parts.dtype == BF16 else 8)
    if tr < 64:
        tr, tc = r, (tc if tc == c and cp != c else _tile(c, 256, LANE))
    tcp = cp if tc == c else tc

    def body(p_ref, w_ref, m_ref, v_ref, g_ref, d_ref, mo_ref, vo_ref):
        g = p_ref[0].astype(F32)
        for s in range(1, npart):
            g = g + p_ref[s].astype(F32)
        g = g[:, :tc]
        g_ref[...] = g
        d_ref[...], mo_ref[...], vo_ref[...] = _adamw_math(w_ref[...], g, m_ref[...], v_ref[...])

    blk = pl.BlockSpec((tr, tc), lambda i, j: (i, j))
    out = jax.ShapeDtypeStruct((r, c), F32)
    return pl.pallas_call(
        body, name=name, grid=(r // tr, c // tc),
        in_specs=[pl.BlockSpec((npart, tr, tcp), lambda i, j: (0, i, j)), blk, blk, blk],
        out_specs=(blk,) * 4, out_shape=(out,) * 4, compiler_params=_params("parallel", "parallel"))(parts, w, m, v)


def sum_parts(name, parts):
    npart, r, c = parts.shape
    tr = _tile(r, 512, 8)

    def body(p_ref, o_ref):
        g = p_ref[0]
        for s in range(1, npart):
            g = g + p_ref[s]
        o_ref[...] = g

    return pl.pallas_call(
        body, name=name, grid=(r // tr,), in_specs=[pl.BlockSpec((npart, tr, c), lambda i: (0, i, 0))],
        out_specs=pl.BlockSpec((tr, c), lambda i: (i, 0)), out_shape=jax.ShapeDtypeStruct((r, c), F32),
        compiler_params=_params("parallel"))(parts)


class _Pack:
    def __init__(self, shapes):
        self.shapes = dict(shapes)
        self.offsets, off = {}, 0
        for name, shape in self.shapes.items():
            self.offsets[name] = off
            off += -(-int(np.prod(shape)) // (8 * LANE)) * (8 * LANE)
        self.used = off
        self.rows = -(-off // (512 * LANE)) * 512

    def pack(self, values):
        pieces = []
        for name, shape in self.shapes.items():
            size = int(np.prod(shape))
            padded = -(-size // (8 * LANE)) * (8 * LANE)
            pieces.append(jnp.pad(values[name].astype(F32).reshape(-1), (0, padded - size)))
        pieces.append(jnp.zeros((self.rows * LANE - self.used,), F32))
        return jnp.concatenate(pieces).reshape(self.rows, LANE)

    def unpack(self, flat, lead=()):
        flat = flat.reshape(*lead, self.rows * LANE)
        out = {}
        for name, shape in self.shapes.items():
            size = int(np.prod(shape))
            out[name] = flat[..., self.offsets[name]:self.offsets[name] + size].reshape(*lead, *shape)
        return out


def kernel(x, c, ctx, c_ctx, norm_mix_g, norm_ffn_g, w_ada, b_ada, na_w_qkv, na_q_g, na_k_g, na_rpb, na_w_o, gm_w_in, gm_v_g, gm_w_s, gm_b_s, gm_w_out, sc_w_in, sc_conv_w, sc_w_out, ffn_w_up, ffn_conv_w, ffn_conv_b, ffn_w_down, loss_target, m_c_ctx, m_norm_mix_g, m_norm_ffn_g, m_w_ada, m_b_ada, m_na_w_qkv, m_na_q_g, m_na_k_g, m_na_rpb, m_na_w_o, m_gm_w_in, m_gm_v_g, m_gm_w_s, m_gm_b_s, m_gm_w_out, m_sc_w_in, m_sc_conv_w, m_sc_w_out, m_ffn_w_up, m_ffn_conv_w, m_ffn_conv_b, m_ffn_w_down, v_c_ctx, v_norm_mix_g, v_norm_ffn_g, v_w_ada, v_b_ada, v_na_w_qkv, v_na_q_g, v_na_k_g, v_na_rpb, v_na_w_o, v_gm_w_in, v_gm_v_g, v_gm_w_s, v_gm_b_s, v_gm_w_out, v_sc_w_in, v_sc_conv_w, v_sc_w_out, v_ffn_w_up, v_ffn_conv_w, v_ffn_conv_b, v_ffn_w_down):
    weights = dict(c_ctx=c_ctx, norm_mix_g=norm_mix_g, norm_ffn_g=norm_ffn_g, w_ada=w_ada, b_ada=b_ada, na_w_qkv=na_w_qkv,
                   na_q_g=na_q_g, na_k_g=na_k_g, na_rpb=na_rpb, na_w_o=na_w_o, gm_w_in=gm_w_in, gm_v_g=gm_v_g, gm_w_s=gm_w_s,
                   gm_b_s=gm_b_s, gm_w_out=gm_w_out, sc_w_in=sc_w_in, sc_conv_w=sc_conv_w, sc_w_out=sc_w_out,
                   ffn_w_up=ffn_w_up, ffn_conv_w=ffn_conv_w, ffn_conv_b=ffn_conv_b, ffn_w_down=ffn_w_down)
    mom_m = dict(c_ctx=m_c_ctx, norm_mix_g=m_norm_mix_g, norm_ffn_g=m_norm_ffn_g, w_ada=m_w_ada, b_ada=m_b_ada, na_w_qkv=m_na_w_qkv,
                 na_q_g=m_na_q_g, na_k_g=m_na_k_g, na_rpb=m_na_rpb, na_w_o=m_na_w_o, gm_w_in=m_gm_w_in, gm_v_g=m_gm_v_g,
                 gm_w_s=m_gm_w_s, gm_b_s=m_gm_b_s, gm_w_out=m_gm_w_out, sc_w_in=m_sc_w_in, sc_conv_w=m_sc_conv_w,
                 sc_w_out=m_sc_w_out, ffn_w_up=m_ffn_w_up, ffn_conv_w=m_ffn_conv_w, ffn_conv_b=m_ffn_conv_b, ffn_w_down=m_ffn_w_down)
    mom_v = dict(c_ctx=v_c_ctx, norm_mix_g=v_norm_mix_g, norm_ffn_g=v_norm_ffn_g, w_ada=v_w_ada, b_ada=v_b_ada, na_w_qkv=v_na_w_qkv,
                 na_q_g=v_na_q_g, na_k_g=v_na_k_g, na_rpb=v_na_rpb, na_w_o=v_na_w_o, gm_w_in=v_gm_w_in, gm_v_g=v_gm_v_g,
                 gm_w_s=v_gm_w_s, gm_b_s=v_gm_b_s, gm_w_out=v_gm_w_out, sc_w_in=v_sc_w_in, sc_conv_w=v_sc_conv_w,
                 sc_w_out=v_sc_w_out, ffn_w_up=v_ffn_w_up, ffn_conv_w=v_ffn_conv_w, ffn_conv_b=v_ffn_conv_b, ffn_w_down=v_ffn_w_down)
    names = list(weights)

    n_lat, d = x.shape[1], x.shape[2]
    n_ctx = ctx.shape[1]
    t = n_lat + n_ctx
    depth = norm_mix_g.shape[0]
    hd = na_q_g.shape[-1]
    n_heads = d // hd
    nup = ffn_w_up.shape[-1]
    nup_p = -(-nup // LANE) * LANE
    fdim, fp = 4 * nup, 4 * nup_p
    me = _slot(_me())
    geo = _na_geometry(n_lat)

    pad_up = lambda a: jnp.pad(a, [(0, 0)] * (a.ndim - 1) + [(0, nup_p - nup)])
    gathered = all_gather("gather_weights", [
        na_w_qkv.astype(BF16), na_w_o.astype(BF16), gm_w_in.astype(BF16), gm_w_out.astype(BF16), sc_w_in.astype(BF16),
        sc_w_out.astype(BF16), pad_up(ffn_w_up).astype(BF16), ffn_w_down.astype(BF16), pad_up(ffn_conv_w), sc_conv_w, c])
    g_qkv, g_wo, g_gin, g_gout, g_sin, g_sout, g_up, g_down, g_fcw, g_scw, c_all = gathered
    w_qkv = [g_qkv[:, j] for j in range(na_w_qkv.shape[0])]
    w_o = [g_wo[:, j].reshape(d, d) for j in range(na_w_o.shape[0])]
    w_gin = [g_gin[:, j] for j in range(gm_w_in.shape[0])]
    w_gout = [g_gout[:, j].reshape(-1, d) for j in range(gm_w_out.shape[0])]
    w_sin = [g_sin[:, j] for j in range(sc_w_in.shape[0])]
    w_sout = [g_sout[:, j].reshape(d, d) for j in range(sc_w_out.shape[0])]
    w_up = [g_up[:, i] for i in range(depth)]
    w_down = [jnp.pad(g_down[:, i].reshape(4, nup, d), ((0, 0), (0, nup_p - nup), (0, 0))).reshape(fp, d) for i in range(depth)]
    f_cw = [g_fcw[:, i].transpose(1, 0, 2).reshape(3, 2, fp).transpose(1, 0, 2) for i in range(depth)]
    cb_p = pad_up(ffn_conv_b.reshape(depth, N_DEV, nup)).reshape(depth, 2, 1, fp)
    s_cw = [g_scw[:, j].transpose(1, 0, 2).reshape(3, d) for j in range(sc_conv_w.shape[0])]

    cond = jnp.concatenate([c_all.reshape(N_DEV, d), c_ctx[None], jnp.zeros((7, d), F32)])
    mod_cols = mm_nn("ada_fwd", cond, w_ada, a_silu=True)
    (mod_all,) = all_gather("gather_mod", [mod_cols])
    ncol = w_ada.shape[-1]
    mod_all = mod_all.reshape(N_DEV, 16, depth, ncol).transpose(2, 1, 0, 3).reshape(depth, 16, N_MOD * d) + b_ada[:, None, :]
    mod_lat = lax.dynamic_index_in_dim(mod_all, me, axis=1, keepdims=False)
    mods = jnp.stack([mod_lat, mod_all[:, N_DEV]], axis=1).reshape(depth, 2, N_MOD, d)
    sh1, sc1, g1, sh2, sc2, g2 = (mods[:, :, kd] for kd in range(N_MOD))

    xs = jnp.concatenate([x[0], ctx[0]], axis=0)
    saved = []
    prev = None
    for i in range(depth):
        mixer, j = i % N_MIXERS, i // N_MIXERS
        s = {}
        if prev is None:
            s["x"] = xs
            s["h"] = resid_rms_mod(f"l{i}_norm_mix", xs, None, None, norm_mix_g[i:i + 1], sc1[i], sh1[i], n_lat)
        else:
            s["x"], s["h"] = resid_rms_mod(f"l{i}_norm_mix", prev[0], prev[1], prev[2], norm_mix_g[i:i + 1], sc1[i], sh1[i], n_lat)
        if mixer == 0:
            s["qkv"] = mm_nn(f"l{i}_qkv", s["h"], w_qkv[j])
            s["q"], s["k"], s["v"] = qk_norm_fwd(f"l{i}_qk_norm", s["qkv"], na_q_g[j:j + 1], na_k_g[j:j + 1], hd)
            s["bias"] = na_bias_table(f"l{i}_bias", na_rpb[j], geo)
            s["o"] = na_attention_fwd(f"l{i}_attn", s["q"], s["k"], s["v"], s["bias"], geo, n_lat, hd)
            s["y"] = mm_nn(f"l{i}_wo", s["o"], w_o[j])
        elif mixer == 1:
            s["t"] = mm_nn(f"l{i}_gm_in", s["h"], w_gin[j])
            s["o"] = gmlp_gate_fwd(f"l{i}_gm_gate", s["t"], gm_v_g[j:j + 1], gm_w_s[j], gm_b_s[j][:, :, None])
            s["y"] = mm_nn(f"l{i}_gm_out", s["o"], w_gout[j])
        else:
            s["t"] = mm_nn(f"l{i}_sc_in", s["h"], w_sin[j])
            s["o"] = sc_gate_fwd(f"l{i}_sc_gate", s["t"], s_cw[j], n_lat)
            s["y"] = mm_nn(f"l{i}_sc_out", s["o"], w_sout[j])
        s["x1"], s["hf"] = resid_rms_mod(f"l{i}_norm_ffn", s["x"], s["y"], g1[i], norm_ffn_g[i:i + 1], sc2[i], sh2[i], n_lat)
        s["u"] = mm_nn(f"l{i}_ffn_up", s["hf"], w_up[i], out_parts=2)
        s["a"] = ffn_act_fwd(f"l{i}_ffn_act", s["u"], f_cw[i], cb_p[i], n_lat)
        s["f"] = mm_nn(f"l{i}_ffn_down", s["a"], w_down[i])
        prev = (s["x1"], s["f"], g2[i])
        saved.append(s)

    dx, loss_local, df, dg2 = loss_head("loss_head", prev[0], prev[1], prev[2], loss_target[0])
    loss = lax.psum(loss_local[0, 0], AXES)

    big = {}
    small = {}
    dmod = [None] * depth
    zeros_like_param = lambda p: [None] * p.shape[0]
    for nm in ("na_w_qkv", "na_w_o", "gm_w_in", "gm_w_out", "sc_w_in", "sc_w_out", "ffn_w_up", "ffn_w_down"):
        big[nm] = zeros_like_param(weights[nm])
    for nm in ("norm_mix_g", "norm_ffn_g", "ffn_conv_w", "ffn_conv_b", "na_q_g", "na_k_g", "na_rpb", "gm_v_g", "gm_w_s", "gm_b_s", "sc_conv_w"):
        small[nm] = zeros_like_param(weights[nm])
    for i in reversed(range(depth)):
        mixer, j = i % N_MIXERS, i // N_MIXERS
        s = saved[i]
        da = mm_nt(f"l{i}_b_ffn_down_x", df, w_down[i])
        dwd = mm_tn(f"l{i}_b_ffn_down_w", s["a"], df)
        big["ffn_w_down"][i] = dwd.reshape(4, nup_p, d)[:, :nup].reshape(N_DEV, fdim // N_DEV, d)
        du, dcw, dcb = ffn_act_bwd(f"l{i}_b_ffn_act", s["u"], da, f_cw[i], cb_p[i], n_lat)
        small["ffn_conv_w"][i] = dcw.transpose(1, 0, 2).reshape(3, N_DEV, nup_p)[:, :, :nup].reshape(3, 2 * fdim)
        small["ffn_conv_b"][i] = dcb.reshape(N_DEV, nup_p)[:, :nup].reshape(2 * fdim)
        dhf = mm_nt(f"l{i}_b_ffn_up_x", du, w_up[i])
        big["ffn_w_up"][i] = mm_tn(f"l{i}_b_ffn_up_w", s["hf"], du, out_parts=N_DEV)
        dx1, dsh2, dsc2, dgf, dy, dg1 = rms_mod_bwd(f"l{i}_b_norm_ffn", s["x1"], dhf, norm_ffn_g[i:i + 1], sc2[i], dx, n_lat,
                                                    branch=(s["y"], g1[i]))
        small["norm_ffn_g"][i] = dgf[0]
        if mixer == 0:
            do = mm_nt(f"l{i}_b_wo_x", dy, w_o[j], out_dtype=BF16)
            big["na_w_o"][j] = mm_tn(f"l{i}_b_wo_w", s["o"], dy).reshape(N_DEV, d // N_DEV, d)
            dq, dk, dv, dbias = na_attention_bwd(f"l{i}_b_attn", s["q"], s["k"], s["v"], s["bias"], do, geo, n_lat, hd)
            small["na_rpb"][j] = na_rpb_grad(f"l{i}_b_rpb", dbias, geo, n_heads)
            dt, dqg, dkg = qk_norm_bwd(f"l{i}_b_qk_norm", s["qkv"], dq, dk, dv, na_q_g[j:j + 1], na_k_g[j:j + 1], hd)
            small["na_q_g"][j], small["na_k_g"][j] = dqg[0], dkg[0]
            dh = mm_nt(f"l{i}_b_qkv_x", dt, w_qkv[j])
            big["na_w_qkv"][j] = mm_tn(f"l{i}_b_qkv_w", s["h"], dt, out_parts=N_DEV)
        elif mixer == 1:
            do = mm_nt(f"l{i}_b_gm_out_x", dy, w_gout[j])
            big["gm_w_out"][j] = mm_tn(f"l{i}_b_gm_out_w", s["o"], dy).reshape(N_DEV, -1, d)
            dt, dvg, dws, dsv = gmlp_gate_bwd(f"l{i}_b_gm_gate", s["t"], do, gm_v_g[j:j + 1], gm_w_s[j], gm_b_s[j][:, :, None])
            groups, width = gm_w_s.shape[1], dsv.shape[1]
            group_of = np.zeros((width, LANE), np.float32)
            group_of[np.arange(width), np.arange(width) // (width // groups)] = 1.0
            dbs = _mm(f"l{i}_b_gm_bs", dsv, jnp.asarray(group_of), "nn", (GM_CHUNK, LANE), F32,
                      (GM_CHUNK, LANE, _tile(width, 2048, LANE)), exact=True)[:, :groups].T
            small["gm_v_g"][j], small["gm_w_s"][j], small["gm_b_s"][j] = dvg[0], dws, dbs
            dh = mm_nt(f"l{i}_b_gm_in_x", dt, w_gin[j])
            big["gm_w_in"][j] = mm_tn(f"l{i}_b_gm_in_w", s["h"], dt, out_parts=N_DEV)
        else:
            do = mm_nt(f"l{i}_b_sc_out_x", dy, w_sout[j])
            big["sc_w_out"][j] = mm_tn(f"l{i}_b_sc_out_w", s["o"], dy).reshape(N_DEV, d // N_DEV, d)
            dt, dscw = sc_gate_bwd(f"l{i}_b_sc_gate", s["t"], do, s_cw[j], n_lat)
            small["sc_conv_w"][j] = dscw
            dt = dt.transpose(1, 0, 2).reshape(t, 3 * d)
            dh = mm_nt(f"l{i}_b_sc_in_x", dt, w_sin[j])
            big["sc_w_in"][j] = mm_tn(f"l{i}_b_sc_in_w", s["h"], dt, out_parts=N_DEV)
        dmod_ffn = [dsh2, dsc2, dg2]
        if i > 0:
            dx, dsh1, dsc1, dgm, df, dg2 = rms_mod_bwd(f"l{i}_b_norm_mix", s["x"], dh, norm_mix_g[i:i + 1], sc1[i], dx1, n_lat,
                                                       branch=(saved[i - 1]["f"], g2[i - 1]))
        else:
            dx, dsh1, dsc1, dgm = rms_mod_bwd(f"l{i}_b_norm_mix", s["x"], dh, norm_mix_g[i:i + 1], sc1[i], dx1, n_lat)
        small["norm_mix_g"][i] = dgm[0]
        dmod[i] = jnp.stack([dsh1, dsc1, dg1] + dmod_ffn, axis=1)
    grad_x = dx[:n_lat][None]

    small = {nm: jnp.stack(v) for nm, v in small.items()}
    small["dmod"] = jnp.stack(dmod)
    pack = _Pack({nm: v.shape for nm, v in small.items()})
    (small_all,) = all_gather("gather_small", [pack.pack(small)])
    small_sum = pack.unpack(sum_parts("sum_small", small_all))
    dmod_all = pack.unpack(small_all, lead=(N_DEV,))["dmod"]
    dmod_ctx = small_sum["dmod"][:, 1].reshape(depth, N_MOD * d)
    grads = {nm: small_sum[nm] for nm in small if nm != "dmod"}
    grads["b_ada"] = (small_sum["dmod"][:, 0] + small_sum["dmod"][:, 1]).reshape(depth, N_MOD * d)
    my_cols = lambda a, width: lax.dynamic_slice_in_dim(a, me * width, width, axis=-1)
    grads["ffn_conv_w"] = my_cols(grads["ffn_conv_w"], nup)
    grads["sc_conv_w"] = my_cols(grads["sc_conv_w"], sc_conv_w.shape[-1])

    drows = jnp.concatenate([dmod_all[:, :, 0], dmod_all[:, :, 1]]).reshape(2 * N_DEV, depth, N_MOD * d)
    drows = my_cols(drows, ncol).reshape(2 * N_DEV, depth * ncol)
    cond2 = jnp.concatenate([c_all.reshape(N_DEV, d), jnp.broadcast_to(c_ctx[None], (N_DEV, d))])
    g_w_ada = mm_tn("ada_bwd_w", cond2, drows, out_parts=depth, out_dtype=F32, a_silu=True)
    dctx_rows = jnp.pad(my_cols(dmod_ctx, ncol).reshape(1, depth * ncol), ((0, 15), (0, 0)))
    dcc = mm_nt("ada_bwd_c", dctx_rows, w_ada)[0:1]
    (dcc_all,) = all_gather("gather_c_ctx", [jnp.pad(dcc.reshape(-1, LANE), ((0, (-d // LANE) % 8), (0, 0)))])
    dcc_sum = sum_parts("sum_c_ctx", dcc_all).reshape(-1)[:d]
    sig = jax.nn.sigmoid(c_ctx)
    grads["c_ctx"] = dcc_sum * (sig * (1.0 + c_ctx * (1.0 - sig)))

    order = [(nm, idx) for nm in big for idx in range(len(big[nm]))]
    parts = all_to_all("scatter_grads", [big[nm][idx] for nm, idx in order])
    out = {nm: [None] * 4 for nm in names}
    per_big = {nm: [] for nm in big}
    for (nm, idx), p in zip(order, parts):
        w2 = weights[nm][idx]
        per_big[nm].append(adamw(f"adamw_{nm}{idx}", p.reshape(N_DEV, w2.shape[0], -1), w2, mom_m[nm][idx], mom_v[nm][idx]))
    for nm, res in per_big.items():
        out[nm] = [jnp.stack([r[q] for r in res]) for q in range(4)]
    res = [adamw(f"adamw_w_ada{i}", g_w_ada[i][None], w_ada[i], m_w_ada[i], v_w_ada[i]) for i in range(depth)]
    out["w_ada"] = [jnp.stack([r[q] for r in res]) for q in range(4)]
    small_names = [nm for nm in names if nm not in big and nm != "w_ada"]
    spack = _Pack({nm: weights[nm].shape for nm in small_names})
    flat = [spack.pack({nm: src[nm] for nm in small_names}) for src in (grads, weights, mom_m, mom_v)]
    res = adamw("adamw_small", flat[0][None], flat[1], flat[2], flat[3])
    res = [spack.unpack(r) for r in res]
    for nm in small_names:
        out[nm] = [grads[nm].reshape(weights[nm].shape)] + [res[q][nm] for q in range(1, 4)]

    return (loss, grad_x, *[out[nm][0] for nm in names], *[out[nm][1] for nm in names],
            *[out[nm][2] for nm in names], *[out[nm][3] for nm in names])
```

```python
import functools
import math

import numpy as np
import jax
import jax.numpy as jnp
from jax import lax
from jax.experimental import pallas as pl
from jax.experimental.pallas import tpu as pltpu

F32 = jnp.float32
BF16 = jnp.bfloat16
MESH = pl.DeviceIdType.MESH
AXES = ("x", "y", "c")
N_DEV = 8
N_MOD = 6
N_MIXERS = 3
EPS = 1e-6
GRID_W = 64
WIN_H = 8
WIN_W = 16
QROWS = 2
GM_CHUNK = 128
LANE = 128
NEG = -1e30
ADAM_LR = 0.001
ADAM_B1 = 0.9
ADAM_B2 = 0.999
ADAM_EPS = 1e-08
ADAM_WD = 0.01
ADAM_STEP = 10
VMEM_LIMIT = 56 * 1024 * 1024
HBM = pl.BlockSpec(memory_space=pltpu.HBM)

NN = (((1,), (0,)), ((), ()))
NT = (((1,), (1,)), ((), ()))
TN = (((0,), (0,)), ((), ()))


def _params(*sem):
    return pltpu.CompilerParams(dimension_semantics=sem, vmem_limit_bytes=VMEM_LIMIT)


def _tile(n, pref, mult):
    best = None
    for t in range(mult, min(n, pref) + 1, mult):
        if n % t == 0:
            best = t
    return n if best is None else best


def _full(arr):
    nd = arr.ndim
    return pl.BlockSpec(arr.shape, lambda *g: (0,) * nd)


def _logical(shape):
    return tuple(shape) if len(shape) == 2 else (shape[1], shape[0] * shape[2])


def _cspec(shape, tr, tc, rc):
    if len(shape) == 2:
        return pl.BlockSpec((tr, tc), rc)
    cpp = shape[2] // tc

    def imap(*g):
        r, c = rc(*g)
        return (c // cpp, r, c % cpp)

    return pl.BlockSpec((None, tr, tc), imap)


def _dot(a, b, dims, exact=False):
    if exact:
        return lax.dot_general(a, b, dims, precision=lax.Precision.HIGHEST, preferred_element_type=F32)
    return lax.dot_general(a.astype(BF16), b.astype(BF16), dims, preferred_element_type=F32)


def _silu(z):
    return z * jax.nn.sigmoid(z)


def _me():
    return lax.axis_index("x"), lax.axis_index("y"), lax.axis_index("c")


def _slot(p):
    return 4 * p[0] + 2 * p[1] + p[2]


def _exchange(kind, srcs, dsts, send_sems, recv_sems, local_sems):
    x, y, c = _me()
    me = (x, y, c)
    peers = [((x + (k >> 2)) % 2, (y + ((k >> 1) & 1)) % 2, (c + (k & 1)) % 2) for k in range(1, N_DEV)]

    def src_of(a, dev):
        return srcs[a] if kind == "gather" else srcs[a].at[_slot(dev)]

    def remote(a, k, peer, src_dev, dst_dev):
        return pltpu.make_async_remote_copy(src_ref=src_of(a, src_dev), dst_ref=dsts[a].at[_slot(dst_dev)], send_sem=send_sems.at[a, k],
                                            recv_sem=recv_sems.at[a, k], device_id=peer, device_id_type=MESH)

    def local(a):
        return pltpu.make_async_copy(src_of(a, me), dsts[a].at[_slot(me)], local_sems.at[a])

    def start():
        for a in range(len(srcs)):
            local(a).start()
            for k, peer in enumerate(peers):
                remote(a, k, peer, peer, me).start()

    def wait():
        for a in range(len(srcs)):
            for k, peer in enumerate(peers):
                remote(a, k, peer, me, peer).wait_recv()
        for a in range(len(srcs)):
            for k, peer in enumerate(peers):
                remote(a, k, peer, peer, me).wait_send()
            local(a).wait()

    return start, wait


def _exchange_scratch(n):
    return [pltpu.SemaphoreType.DMA((n, N_DEV - 1)), pltpu.SemaphoreType.DMA((n, N_DEV - 1)), pltpu.SemaphoreType.DMA((n,))]


def _landing(kind, arrays):
    return [jax.ShapeDtypeStruct((N_DEV, *a.shape) if kind == "gather" else a.shape, a.dtype) for a in arrays]


def _call(body, name, grid, in_specs, out_specs, out_shape, ins, sem, scratch=(), carry=None):
    in_specs, out_specs, out_shape, scratch = list(in_specs), list(out_specs), list(out_shape), list(scratch)
    if carry is None:
        return list(pl.pallas_call(body, name=name, grid=grid, in_specs=in_specs, out_specs=out_specs, out_shape=out_shape,
                                   scratch_shapes=scratch, compiler_params=_params(*sem))(*ins))
    kind, arrays = carry
    n, ni, no, ns = len(arrays), len(in_specs), len(out_specs), len(scratch)

    def carrying(*refs):
        own_in, srcs = refs[:ni], refs[ni:ni + n]
        own_out, dsts = refs[ni + n:ni + n + no], refs[ni + n + no:ni + 2 * n + no]
        own_scratch, sems = refs[ni + 2 * n + no:ni + 2 * n + no + ns], refs[ni + 2 * n + no + ns:]
        start, wait = _exchange(kind, srcs, dsts, *sems)
        first = functools.reduce(jnp.logical_and, [pl.program_id(ax) == 0 for ax in range(len(grid))])
        last = functools.reduce(jnp.logical_and, [pl.program_id(ax) == g - 1 for ax, g in enumerate(grid)])
        pl.when(first)(start)
        body(*own_in, *own_out, *own_scratch)
        pl.when(last)(wait)

    res = pl.pallas_call(
        carrying, name=name, grid=grid, in_specs=in_specs + [HBM] * n, out_specs=out_specs + [HBM] * n,
        out_shape=out_shape + _landing(kind, arrays), scratch_shapes=scratch + _exchange_scratch(n),
        compiler_params=_params(*(["arbitrary"] * len(grid))))(*ins, *arrays)
    return list(res[:no]), list(res[no:])


def _mm(name, a, b, kind, out_shape, out_dtype, tiles, a_silu=False, exact=False, carry=None):
    la, lb, lo = _logical(a.shape), _logical(b.shape), _logical(out_shape)
    t0, t1, t2 = tiles
    if kind == "nn":
        grid = (lo[1] // t1, lo[0] // t0, la[1] // t2)
        a_spec = _cspec(a.shape, t0, t2, lambda j, i, k: (i, k))
        b_spec = _cspec(b.shape, t2, t1, lambda j, i, k: (k, j))
        o_spec = _cspec(out_shape, t0, t1, lambda j, i, k: (i, j))
        dims, acc = NN, (t0, t1)
    elif kind == "nt":
        grid = (lo[1] // t1, lo[0] // t0, la[1] // t2)
        a_spec = _cspec(a.shape, t0, t2, lambda p, i, r: (i, r))
        b_spec = _cspec(b.shape, t1, t2, lambda p, i, r: (p, r))
        o_spec = _cspec(out_shape, t0, t1, lambda p, i, r: (i, p))
        dims, acc = NT, (t0, t1)
    else:
        grid = (lo[1] // t1, lo[0] // t0, la[0] // t2)
        a_spec = _cspec(a.shape, t2, t0, lambda j, kk, r: (r, kk))
        b_spec = _cspec(b.shape, t2, t1, lambda j, kk, r: (r, j))
        o_spec = _cspec(out_shape, t0, t1, lambda j, kk, r: (kk, j))
        dims, acc = TN, (t0, t1)
    nk = grid[2]

    def body(a_ref, b_ref, o_ref, acc_ref):
        k = pl.program_id(2)
        av = a_ref[...]
        if a_silu:
            av = _silu(av)
        part = _dot(av, b_ref[...], dims, exact)

        @pl.when(k == 0)
        def _():
            acc_ref[...] = part

        @pl.when(k > 0)
        def _():
            acc_ref[...] += part

        @pl.when(k == nk - 1)
        def _():
            o_ref[...] = acc_ref[...].astype(o_ref.dtype)

    res = _call(body, name, grid, [a_spec, b_spec], [o_spec], [jax.ShapeDtypeStruct(out_shape, out_dtype)], (a, b),
                ("parallel", "parallel", "arbitrary"), scratch=[pltpu.VMEM(acc, F32)], carry=carry)
    return res[0] if carry is None else (res[0][0], res[1])


def mm_nn(name, a, w, out_parts=1, out_dtype=F32, **kw):
    (m, _), (_, n) = _logical(a.shape), _logical(w.shape)
    out_shape = (m, n) if out_parts == 1 else (out_parts, m, n // out_parts)
    tiles = (_tile(m, 768, 16), _tile(math.gcd(w.shape[-1], out_shape[-1]), 1536, LANE),
             _tile(math.gcd(a.shape[-1], w.shape[-2]), 2048, LANE))
    return _mm(name, a, w, "nn", out_shape, out_dtype, tiles, **kw)


def mm_nt(name, a, w, out_dtype=F32, **kw):
    (m, _), (p, _) = _logical(a.shape), _logical(w.shape)
    tiles = (_tile(m, 768, 16), _tile(w.shape[-2], 2048, LANE), _tile(math.gcd(a.shape[-1], w.shape[-1]), 1536, LANE))
    return _mm(name, a, w, "nt", (m, p), out_dtype, tiles, **kw)


def mm_tn(name, a, dy, out_parts=1, out_dtype=BF16, **kw):
    (t, k), (_, n) = _logical(a.shape), _logical(dy.shape)
    out_shape = (k, n) if out_parts == 1 else (out_parts, k, n // out_parts)
    tiles = (_tile(a.shape[-1], 1024, LANE), _tile(math.gcd(dy.shape[-1], out_shape[-1]), 1536, LANE), _tile(t, 768, 16))
    return _mm(name, a, dy, "tn", out_shape, out_dtype, tiles, **kw)


def _rows(tr, d):
    return pl.BlockSpec((tr, d), lambda i: (i, 0))


def _pick(ctx, ref):
    return jnp.where(ctx, ref[1:2, :], ref[0:1, :])


def resid_rms_mod(name, x, y, gate, g, sc, sh, n_lat, tr=256):
    t, d = x.shape
    nlt = n_lat // tr
    has_res = y is not None

    def body(*refs):
        if has_res:
            x_ref, y_ref, gate_ref, g_ref, sc_ref, sh_ref, x1_ref, h_ref = refs
        else:
            x_ref, g_ref, sc_ref, sh_ref, h_ref = refs
        ctx = pl.program_id(0) >= nlt
        xv = x_ref[...]
        if has_res:
            xv = xv + _pick(ctx, gate_ref) * y_ref[...]
            x1_ref[...] = xv
        r = lax.rsqrt(jnp.mean(xv * xv, axis=-1, keepdims=True) + EPS)
        n = xv * r * g_ref[...]
        h_ref[...] = (n * (1.0 + _pick(ctx, sc_ref)) + _pick(ctx, sh_ref)).astype(h_ref.dtype)

    row = _rows(tr, d)
    if has_res:
        ins, in_specs = (x, y, gate, g, sc, sh), [row, row, _full(gate), _full(g), _full(sc), _full(sh)]
        out_shape = (jax.ShapeDtypeStruct((t, d), F32), jax.ShapeDtypeStruct((t, d), BF16))
        out_specs = (row, row)
    else:
        ins, in_specs = (x, g, sc, sh), [row, _full(g), _full(sc), _full(sh)]
        out_shape = jax.ShapeDtypeStruct((t, d), BF16)
        out_specs = row
    return pl.pallas_call(body, name=name, grid=(t // tr,), in_specs=in_specs, out_specs=out_specs,
                          out_shape=out_shape, compiler_params=_params("parallel"))(*ins)


def _acc_rows(i, nlt, ref, val):
    @pl.when(i == 0)
    def _():
        ref[...] = jnp.zeros_like(ref)

    @pl.when(i < nlt)
    def _():
        ref[0:1, :] += val

    @pl.when(i >= nlt)
    def _():
        ref[1:2, :] += val


def rms_mod_bwd(name, x, dh, g, sc, dres, n_lat, branch=None, tr=256):
    t, d = x.shape
    nlt = n_lat // tr

    def body(x_ref, dh_ref, g_ref, sc_ref, dres_ref, *rest):
        if branch is None:
            dx_ref, dsh_ref, dsc_ref, dg_ref = rest
        else:
            y_ref, gate_ref, dx_ref, dsh_ref, dsc_ref, dg_ref, dy_ref, dgate_ref = rest
        i = pl.program_id(0)
        xv, dhv, gv = x_ref[...], dh_ref[...], g_ref[...]
        r = lax.rsqrt(jnp.mean(xv * xv, axis=-1, keepdims=True) + EPS)
        xhat = xv * r
        dn = dhv * (1.0 + _pick(i >= nlt, sc_ref))
        dxhat = dn * gv
        dxv = r * (dxhat - xhat * jnp.mean(dxhat * xhat, axis=-1, keepdims=True)) + dres_ref[...]
        dx_ref[...] = dxv
        if branch is not None:
            dy_ref[...] = (_pick(i >= nlt, gate_ref) * dxv).astype(dy_ref.dtype)
            _acc_rows(i, nlt, dgate_ref, jnp.sum(dxv * y_ref[...], axis=0, keepdims=True))
        _acc_rows(i, nlt, dsh_ref, jnp.sum(dhv, axis=0, keepdims=True))
        _acc_rows(i, nlt, dsc_ref, jnp.sum(dhv * (xhat * gv), axis=0, keepdims=True))
        dgp = jnp.sum(dn * xhat, axis=0, keepdims=True)

        @pl.when(i == 0)
        def _():
            dg_ref[...] = dgp

        @pl.when(i > 0)
        def _():
            dg_ref[...] += dgp

    row = _rows(tr, d)
    two = pl.BlockSpec((2, d), lambda i: (0, 0))
    two_shape = jax.ShapeDtypeStruct((2, d), F32)
    ins, in_specs = [x, dh, g, sc, dres], [row, row, _full(g), _full(sc), row]
    out_specs = [row, two, two, pl.BlockSpec((1, d), lambda i: (0, 0))]
    out_shape = [jax.ShapeDtypeStruct((t, d), F32), two_shape, two_shape, jax.ShapeDtypeStruct((1, d), F32)]
    if branch is not None:
        ins, in_specs = ins + list(branch), in_specs + [row, _full(branch[1])]
        out_specs, out_shape = out_specs + [row, two], out_shape + [jax.ShapeDtypeStruct((t, d), BF16), two_shape]
    return pl.pallas_call(body, name=name, grid=(t // tr,), in_specs=in_specs, out_specs=out_specs, out_shape=out_shape,
                          compiler_params=_params("arbitrary"))(*ins)


def loss_head(name, x1, f, gate, target, tr=256):
    t, d = x1.shape
    nlt = target.shape[0] // tr

    def body(x_ref, f_ref, gate_ref, t_ref, dx_ref, loss_ref, df_ref, dgate_ref, acc_ref):
        i = pl.program_id(0)

        @pl.when(i == 0)
        def _():
            acc_ref[...] = jnp.zeros_like(acc_ref)
            dgate_ref[...] = jnp.zeros_like(dgate_ref)

        @pl.when(i < nlt)
        def _():
            fv, gv = f_ref[...], gate_ref[0:1, :]
            e = x_ref[...] + gv * fv - t_ref[...]
            dxv = e / d
            dx_ref[...] = dxv
            df_ref[...] = (gv * dxv).astype(df_ref.dtype)
            dgate_ref[0:1, :] += jnp.sum(dxv * fv, axis=0, keepdims=True)
            acc_ref[...] += jnp.sum(e * e, axis=0, keepdims=True)

        @pl.when(i >= nlt)
        def _():
            dx_ref[...] = jnp.zeros_like(dx_ref)
            df_ref[...] = jnp.zeros_like(df_ref)

        @pl.when(i == t // tr - 1)
        def _():
            loss_ref[...] = jnp.sum(acc_ref[...], axis=1, keepdims=True) * (0.5 / d)

    row = _rows(tr, d)
    return pl.pallas_call(
        body, name=name, grid=(t // tr,),
        in_specs=[row, row, _full(gate), pl.BlockSpec((tr, d), lambda i: (jnp.minimum(i, nlt - 1), 0))],
        out_specs=(row, pl.BlockSpec((1, 1), lambda i: (0, 0)), row, pl.BlockSpec((2, d), lambda i: (0, 0))),
        out_shape=(jax.ShapeDtypeStruct((t, d), F32), jax.ShapeDtypeStruct((1, 1), F32),
                   jax.ShapeDtypeStruct((t, d), BF16), jax.ShapeDtypeStruct((2, d), F32)),
        scratch_shapes=[pltpu.VMEM((1, d), F32)],
        compiler_params=_params("arbitrary"))(x1, f, gate, target)


HALO = 8


def _halo_specs(shape, tr, tc, col):
    hb = tr // HALO
    last = _logical(shape)[0] // HALO - 1
    main = _cspec(shape, tr, tc, lambda j, i: (i, col(j)))
    prev = _cspec(shape, HALO, tc, lambda j, i: (jnp.maximum(i * hb - 1, 0), col(j)))
    nxt = _cspec(shape, HALO, tc, lambda j, i: (jnp.minimum((i + 1) * hb, last), col(j)))
    return [prev, main, nxt]


def _seq_edges(i, nlt, nt):
    first = (i == 0) | (i == nlt)
    last = (i == nlt - 1) | (i == nt - 1)
    return first, last


def _ext(prev_ref, main_ref, next_ref, first, last):
    p = jnp.where(first, 0.0, prev_ref[...].astype(F32))
    n = jnp.where(last, 0.0, next_ref[...].astype(F32))
    return jnp.concatenate([p, main_ref[...].astype(F32), n], axis=0)


def _up(e):
    return pltpu.roll(e, 1, 0)


def _down(e):
    return pltpu.roll(e, e.shape[0] - 1, 0)


def _conv(e, w):
    return _up(e) * w[0:1, :] + e * w[1:2, :] + _down(e) * w[2:3, :]


def _conv_t(e, w):
    return _down(e) * w[0:1, :] + e * w[1:2, :] + _up(e) * w[2:3, :]


def _mid(e, tr):
    return e[HALO:HALO + tr, :]


def _acc_cols(i, ref, val):
    @pl.when(i == 0)
    def _():
        ref[...] = val

    @pl.when(i > 0)
    def _():
        ref[...] += val


def _colsum(v):
    return jnp.sum(v, axis=0, keepdims=True)


def ffn_act_fwd(name, u, cw, cb, n_lat, tr=256, carry=None):
    _, t, fp = u.shape
    tc = _tile(fp, 1536, LANE)
    nlt, nt = n_lat // tr, t // tr

    def body(pg, mg, ng, pu, mu, nu, cw_ref, cb_ref, a_ref):
        first, last = _seq_edges(pl.program_id(1), nlt, nt)
        zg = _mid(_conv(_ext(pg, mg, ng, first, last), cw_ref[0]), tr) + cb_ref[0]
        zu = _mid(_conv(_ext(pu, mu, nu, first, last), cw_ref[1]), tr) + cb_ref[1]
        a_ref[...] = (_silu(zg) * zu).astype(a_ref.dtype)

    ncol = fp // tc
    specs = _halo_specs(u.shape, tr, tc, lambda j: j) + _halo_specs(u.shape, tr, tc, lambda j: j + ncol)
    specs += [pl.BlockSpec((2, 3, tc), lambda j, i: (0, 0, j)), pl.BlockSpec((2, 1, tc), lambda j, i: (0, 0, j))]
    res = _call(body, name, (ncol, nt), specs, [pl.BlockSpec((tr, tc), lambda j, i: (i, j))], [jax.ShapeDtypeStruct((t, fp), BF16)],
                (u, u, u, u, u, u, cw, cb), ("parallel", "parallel"), carry=carry)
    return res[0] if carry is None else (res[0][0], res[1])


def ffn_act_bwd(name, u, da, cw, cb, n_lat, tr=128, carry=None):
    _, t, fp = u.shape
    tc = _tile(fp, 1536, LANE)
    nlt, nt = n_lat // tr, t // tr
    ncol = fp // tc

    def body(pg, mg, ng, pu, mu, nu, pa, ma, na, cw_ref, cb_ref, du_ref, dcw_ref, dcb_ref):
        i = pl.program_id(1)
        first, last = _seq_edges(i, nlt, nt)
        wg, wu = cw_ref[0], cw_ref[1]
        ug, uu = _ext(pg, mg, ng, first, last), _ext(pu, mu, nu, first, last)
        dae = _ext(pa, ma, na, first, last)
        zg = _conv(ug, wg) + cb_ref[0]
        zu = _conv(uu, wu) + cb_ref[1]
        sg = jax.nn.sigmoid(zg)
        dzg = dae * zu * (sg * (1.0 + zg * (1.0 - sg)))
        dzu = dae * (zg * sg)
        du_ref[0] = _mid(_conv_t(dzg, wg), tr).astype(du_ref.dtype)
        du_ref[1] = _mid(_conv_t(dzu, wu), tr).astype(du_ref.dtype)
        for h, (dz, ue) in enumerate(((dzg, ug), (dzu, uu))):
            dzm = _mid(dz, tr)
            rows = [_colsum(dzm * _mid(_up(ue), tr)), _colsum(dzm * _mid(ue, tr)), _colsum(dzm * _mid(_down(ue), tr))]
            _acc_cols(i, dcw_ref.at[h], jnp.concatenate(rows, axis=0))
            _acc_cols(i, dcb_ref.at[h], _colsum(dzm))

    specs = _halo_specs(u.shape, tr, tc, lambda j: j) + _halo_specs(u.shape, tr, tc, lambda j: j + ncol)
    specs += _halo_specs(da.shape, tr, tc, lambda j: j)
    specs += [pl.BlockSpec((2, 3, tc), lambda j, i: (0, 0, j)), pl.BlockSpec((2, 1, tc), lambda j, i: (0, 0, j))]
    res = _call(body, name, (ncol, nt), specs,
                [pl.BlockSpec((2, tr, tc), lambda j, i: (0, i, j)),
                 pl.BlockSpec((2, 3, tc), lambda j, i: (0, 0, j)), pl.BlockSpec((2, 1, tc), lambda j, i: (0, 0, j))],
                [jax.ShapeDtypeStruct((2, t, fp), BF16), jax.ShapeDtypeStruct((2, 3, fp), F32), jax.ShapeDtypeStruct((2, 1, fp), F32)],
                (u, u, u, u, u, u, da, da, da, cw, cb), ("parallel", "arbitrary"), carry=carry)
    return tuple(res) if carry is None else (*res[0], res[1])


def sc_gate_fwd(name, tmat, cw, n_lat, tr=256):
    t, d3 = tmat.shape
    d = d3 // 3
    tc = _tile(d, 512, LANE)
    ncol = d // tc
    nlt, nt = n_lat // tr, t // tr

    def body(b_ref, pc, mc, nc, px, mx, nx, cw_ref, s_ref):
        first, last = _seq_edges(pl.program_id(1), nlt, nt)
        p = _ext(pc, mc, nc, first, last) * _ext(px, mx, nx, first, last)
        s_ref[...] = (b_ref[...] * _mid(_conv(p, cw_ref[...]), tr)).astype(s_ref.dtype)

    specs = [pl.BlockSpec((tr, tc), lambda j, i: (i, j))]
    specs += _halo_specs(tmat.shape, tr, tc, lambda j: j + ncol) + _halo_specs(tmat.shape, tr, tc, lambda j: j + 2 * ncol)
    specs += [pl.BlockSpec((3, tc), lambda j, i: (0, j))]
    return pl.pallas_call(
        body, name=name, grid=(ncol, nt), in_specs=specs, out_specs=pl.BlockSpec((tr, tc), lambda j, i: (i, j)),
        out_shape=jax.ShapeDtypeStruct((t, d), BF16),
        compiler_params=_params("parallel", "parallel"))(*([tmat] * 7), cw)


def sc_gate_bwd(name, tmat, ds, cw, n_lat, tr=128):
    t, d3 = tmat.shape
    d = d3 // 3
    tc = _tile(d, 512, LANE)
    ncol = d // tc
    nlt, nt = n_lat // tr, t // tr

    def body(pb, mb, nb, pc, mc, nc, px, mx, nx, pd, md, nd, cw_ref, dt_ref, dcw_ref):
        i = pl.program_id(1)
        first, last = _seq_edges(i, nlt, nt)
        w = cw_ref[...]
        be, ce, xe = _ext(pb, mb, nb, first, last), _ext(pc, mc, nc, first, last), _ext(px, mx, nx, first, last)
        dse = _ext(pd, md, nd, first, last)
        p = ce * xe
        dcv = dse * be
        dp = _conv_t(dcv, w)
        dt_ref[0] = _mid(dse * _conv(p, w), tr).astype(dt_ref.dtype)
        dt_ref[1] = _mid(dp * xe, tr).astype(dt_ref.dtype)
        dt_ref[2] = _mid(dp * ce, tr).astype(dt_ref.dtype)
        dm = _mid(dcv, tr)
        rows = [_colsum(dm * _mid(_up(p), tr)), _colsum(dm * _mid(p, tr)), _colsum(dm * _mid(_down(p), tr))]
        _acc_cols(i, dcw_ref, jnp.concatenate(rows, axis=0))

    specs = []
    for part in range(3):
        specs += _halo_specs(tmat.shape, tr, tc, functools.partial(lambda j, part: j + part * ncol, part=part))
    specs += _halo_specs(ds.shape, tr, tc, lambda j: j)
    specs += [pl.BlockSpec((3, tc), lambda j, i: (0, j))]
    return pl.pallas_call(
        body, name=name, grid=(ncol, nt), in_specs=specs,
        out_specs=(pl.BlockSpec((3, tr, tc), lambda j, i: (0, i, j)), pl.BlockSpec((3, tc), lambda j, i: (0, j))),
        out_shape=(jax.ShapeDtypeStruct((3, t, d), BF16), jax.ShapeDtypeStruct((3, d), F32)),
        compiler_params=_params("parallel", "arbitrary"))(*([tmat] * 9), ds, ds, ds, cw)


_GELU_K = 0.7978845608028654
_GELU_C = 0.044715


def _gelu(x):
    return 0.5 * x * (1.0 + jnp.tanh(_GELU_K * (x + _GELU_C * (x * x * x))))


def _gelu_grad(x):
    th = jnp.tanh(_GELU_K * (x + _GELU_C * (x * x * x)))
    return 0.5 * (1.0 + th) + 0.5 * x * (1.0 - th * th) * (_GELU_K * (1.0 + 3.0 * _GELU_C * (x * x)))


def gmlp_gate_fwd(name, tmat, vg, ws, bs):
    t, w2 = tmat.shape
    w = w2 // 2
    groups = ws.shape[0]
    gd = w // groups

    def body(t_ref, vg_ref, ws_ref, bs_ref, o_ref):
        v = _gelu(t_ref[:, w:])
        r = lax.rsqrt(jnp.mean(v * v, axis=-1, keepdims=True) + EPS)
        vn = (v * r * vg_ref[...]).astype(BF16)
        for g in range(groups):
            cols = slice(g * gd, (g + 1) * gd)
            sv = _dot(ws_ref[g], vn[:, cols], NN) + bs_ref[g]
            o_ref[:, cols] = (_gelu(t_ref[:, cols]) * sv).astype(o_ref.dtype)

    return pl.pallas_call(
        body, name=name, grid=(t // GM_CHUNK,),
        in_specs=[_rows(GM_CHUNK, w2), _full(vg), _full(ws), _full(bs)], out_specs=_rows(GM_CHUNK, w),
        out_shape=jax.ShapeDtypeStruct((t, w), BF16), compiler_params=_params("parallel"))(tmat, vg, ws, bs)


def gmlp_gate_bwd(name, tmat, dout, vg, ws, bs):
    t, w2 = tmat.shape
    w = w2 // 2
    groups = ws.shape[0]
    gd = w // groups

    def body(t_ref, do_ref, vg_ref, ws_ref, bs_ref, dt_ref, dvg_ref, dws_ref, dsv_ref, dvn_ref):
        i = pl.program_id(0)
        tv = t_ref[:, w:]
        v = _gelu(tv)
        r = lax.rsqrt(jnp.mean(v * v, axis=-1, keepdims=True) + EPS)
        vhat = v * r
        vn = (vhat * vg_ref[...]).astype(BF16)
        for g in range(groups):
            cols = slice(g * gd, (g + 1) * gd)
            tu = t_ref[:, cols]
            dov = do_ref[:, cols]
            sv = _dot(ws_ref[g], vn[:, cols], NN) + bs_ref[g]
            dt_ref[:, cols] = (dov * sv * _gelu_grad(tu)).astype(dt_ref.dtype)
            dsv = dov * _gelu(tu)
            _acc_cols(i, dsv_ref.at[:, cols], dsv)
            _acc_cols(i, dws_ref.at[g], _dot(dsv, vn[:, cols], NT))
            dvn_ref[:, cols] = _dot(ws_ref[g], dsv, TN)
        dvn = dvn_ref[...]
        _acc_cols(i, dvg_ref, _colsum(dvn * vhat))
        dvhat = dvn * vg_ref[...]
        dv = r * (dvhat - vhat * jnp.mean(dvhat * vhat, axis=-1, keepdims=True))
        dt_ref[:, w:] = (dv * _gelu_grad(tv)).astype(dt_ref.dtype)

    keep = lambda shape: pl.BlockSpec(shape, lambda i: (0,) * len(shape))
    return pl.pallas_call(
        body, name=name, grid=(t // GM_CHUNK,),
        in_specs=[_rows(GM_CHUNK, w2), _rows(GM_CHUNK, w), _full(vg), _full(ws), _full(bs)],
        out_specs=(_rows(GM_CHUNK, w2), keep((1, w)), keep(ws.shape), keep((GM_CHUNK, w))),
        out_shape=(jax.ShapeDtypeStruct((t, w2), BF16), jax.ShapeDtypeStruct((1, w), F32),
                   jax.ShapeDtypeStruct(ws.shape, F32), jax.ShapeDtypeStruct((GM_CHUNK, w), F32)),
        scratch_shapes=[pltpu.VMEM((GM_CHUNK, w), F32)],
        compiler_params=_params("arbitrary"))(tmat, dout, vg, ws, bs)


def qk_norm_fwd(name, qkv, qg, kg, hd, tr=128):
    t, d3 = qkv.shape
    d = d3 // 3

    def body(x_ref, qg_ref, kg_ref, q_ref, k_ref, v_ref):
        for part, (g_ref, o_ref) in enumerate(((qg_ref, q_ref), (kg_ref, k_ref))):
            for h in range(d // hd):
                xh = x_ref[:, part * d + h * hd: part * d + (h + 1) * hd]
                r = lax.rsqrt(jnp.mean(xh * xh, axis=-1, keepdims=True) + EPS)
                o_ref[:, h * hd:(h + 1) * hd] = (xh * r * g_ref[...]).astype(o_ref.dtype)
        v_ref[...] = x_ref[:, 2 * d:].astype(v_ref.dtype)

    out = jax.ShapeDtypeStruct((t, d), BF16)
    return pl.pallas_call(
        body, name=name, grid=(t // tr,), in_specs=[_rows(tr, d3), _full(qg), _full(kg)],
        out_specs=(_rows(tr, d),) * 3, out_shape=(out,) * 3, compiler_params=_params("parallel"))(qkv, qg, kg)


def qk_norm_bwd(name, qkv, dq, dk, dv, qg, kg, hd, tr=128):
    t, d3 = qkv.shape
    d = d3 // 3

    def body(x_ref, dq_ref, dk_ref, dv_ref, qg_ref, kg_ref, o_ref, dqg_ref, dkg_ref):
        i = pl.program_id(0)
        for part, (g_ref, dn_ref, dg_ref) in enumerate(((qg_ref, dq_ref, dqg_ref), (kg_ref, dk_ref, dkg_ref))):
            dg = jnp.zeros((1, hd), F32)
            for h in range(d // hd):
                xh = x_ref[:, part * d + h * hd: part * d + (h + 1) * hd]
                dn = dn_ref[:, h * hd:(h + 1) * hd]
                r = lax.rsqrt(jnp.mean(xh * xh, axis=-1, keepdims=True) + EPS)
                xhat = xh * r
                dg = dg + _colsum(dn * xhat)
                dxhat = dn * g_ref[...]
                dx = r * (dxhat - xhat * jnp.mean(dxhat * xhat, axis=-1, keepdims=True))
                o_ref[:, part * d + h * hd: part * d + (h + 1) * hd] = dx.astype(o_ref.dtype)
            _acc_cols(i, dg_ref, dg)
        o_ref[:, 2 * d:] = dv_ref[...].astype(o_ref.dtype)

    one = pl.BlockSpec((1, hd), lambda i: (0, 0))
    return pl.pallas_call(
        body, name=name, grid=(t // tr,),
        in_specs=[_rows(tr, d3), _rows(tr, d), _rows(tr, d), _rows(tr, d), _full(qg), _full(kg)],
        out_specs=(_rows(tr, d3), one, one),
        out_shape=(jax.ShapeDtypeStruct((t, d3), BF16), jax.ShapeDtypeStruct((1, hd), F32), jax.ShapeDtypeStruct((1, hd), F32)),
        compiler_params=_params("arbitrary"))(qkv, dq, dk, dv, qg, kg)


def _na_geometry(n_lat):
    rows = n_lat // GRID_W
    kh = min(WIN_H, rows)
    nb = min(kh + QROWS - 1, rows)
    n_blk = rows // QROWS
    q_row_off = np.repeat(np.arange(QROWS), GRID_W)
    q_col = np.tile(np.arange(GRID_W), QROWS)
    k_row_off = np.repeat(np.arange(nb), GRID_W)
    k_col = np.tile(np.arange(GRID_W), nb)
    c_start = np.clip(q_col - WIN_W // 2, 0, GRID_W - WIN_W)
    col_ok = (k_col[None, :] >= c_start[:, None]) & (k_col[None, :] < c_start[:, None] + WIN_W)
    dc_idx = np.clip(k_col[None, :] - q_col[:, None], -(WIN_W - 1), WIN_W - 1) + WIN_W - 1

    def block(blk):
        r0 = blk * QROWS
        q_row = r0 + q_row_off
        r_start = np.clip(q_row - kh // 2, 0, rows - kh)
        band0 = min(int(np.clip(r0 - kh // 2, 0, rows - kh)), rows - nb)
        k_row = band0 + k_row_off
        ok = col_ok & (k_row[None, :] >= r_start[:, None]) & (k_row[None, :] < r_start[:, None] + kh)
        dr_idx = np.clip(k_row[None, :] - q_row[:, None], -(WIN_H - 1), WIN_H - 1) + WIN_H - 1
        return band0, ok, dr_idx

    reps = [0, 1, 2, n_blk - 2, n_blk - 1]
    variant = lambda blk: 0 if blk == 0 else 1 if blk == 1 else 3 if blk == n_blk - 2 else 4 if blk == n_blk - 1 else 2
    geo = [block(b) for b in reps]
    for blk in range(n_blk):
        _, ok, dr = block(blk)
        assert np.array_equal(ok, geo[variant(blk)][1]) and np.array_equal(np.where(ok, dr, 0), np.where(ok, geo[variant(blk)][2], 0))
    return dict(rows=rows, kh=kh, nb=nb, n_blk=n_blk, reps=reps, ok=[g[1] for g in geo], dr=[g[2] for g in geo],
                band0=[g[0] for g in geo], dc=dc_idx)


def _na_onehots(geo):
    nb = geo["nb"]
    w2 = GRID_W * GRID_W
    qc, kc = np.meshgrid(np.arange(GRID_W), np.arange(GRID_W), indexing="ij")
    diff = (kc - qc).reshape(-1)
    cols = np.zeros((w2, LANE), np.float32)
    sel = np.abs(diff) <= WIN_W - 1
    cols[np.arange(w2)[sel], diff[sel] + WIN_W - 1] = 1.0
    npair = len(geo["reps"]) * QROWS * nb
    kpad = -(-npair // LANE) * LANE
    rows = np.zeros((16, kpad), np.float32)
    for vi, blk in enumerate(geo["reps"]):
        for qr in range(QROWS):
            for kr in range(nb):
                dr = (geo["band0"][vi] + kr) - (blk * QROWS + qr)
                if abs(dr) <= WIN_H - 1:
                    rows[dr + WIN_H - 1, (vi * QROWS + qr) * nb + kr] = 1.0
    return cols, rows, npair, kpad


def na_bias_table(name, rpb, geo):
    n_heads, nb, nv = rpb.shape[0], geo["nb"], len(geo["reps"])
    cols, rows, npair, kpad = _na_onehots(geo)
    rpb_p = jnp.pad(rpb, ((0, 0), (0, 16 - rpb.shape[1]), (0, LANE - rpb.shape[2]))).transpose(1, 0, 2).reshape(16, n_heads * LANE)
    t1 = _mm(name + "_rows", jnp.asarray(rows.T), rpb_p, "nn", (kpad, n_heads * LANE), F32,
             (kpad, _tile(n_heads * LANE, 1024, LANE), 16), exact=True)
    t1 = t1[:npair].reshape(nv, QROWS * nb, n_heads, LANE).transpose(0, 2, 1, 3).reshape(nv * n_heads * QROWS * nb, LANE)
    m = t1.shape[0]
    flat = _mm(name + "_cols", t1, jnp.asarray(cols), "nt", (m, GRID_W * GRID_W), F32,
               (_tile(m, 512, 8), _tile(GRID_W * GRID_W, 2048, LANE), LANE), exact=True)
    tab = flat.reshape(nv, n_heads, QROWS, nb, GRID_W, GRID_W).transpose(0, 1, 2, 4, 3, 5).reshape(nv, n_heads, QROWS * GRID_W, nb * GRID_W)
    return jnp.where(jnp.asarray(np.stack(geo["ok"]))[:, None], tab, NEG)


def _na_tile_info(qt, geo):
    n_blk, rows, kh, nb = geo["n_blk"], geo["rows"], geo["kh"], geo["nb"]
    is_ctx = qt >= n_blk
    band0 = jnp.minimum(jnp.clip(qt * QROWS - kh // 2, 0, rows - kh), rows - nb)
    band0 = jnp.where(is_ctx, 0, band0)
    return is_ctx, pl.multiple_of(band0 * GRID_W, GRID_W)


def _na_variant(qt, n_blk):
    v = jnp.minimum(qt, 2) + (qt >= n_blk - 2).astype(jnp.int32) + (qt >= n_blk - 1).astype(jnp.int32)
    return jnp.minimum(v, 4)


def _na_probs(q, kb, kc, bias, is_ctx, scale):
    s_lat = _dot(q, kb, NT) * scale + bias
    s_lat = jnp.where(is_ctx, NEG, s_lat)
    s_ctx = _dot(q, kc, NT) * scale
    m = jnp.maximum(jnp.max(s_lat, axis=-1, keepdims=True), jnp.max(s_ctx, axis=-1, keepdims=True))
    e_lat, e_ctx = jnp.exp(s_lat - m), jnp.exp(s_ctx - m)
    den = jnp.sum(e_lat, axis=-1, keepdims=True) + jnp.sum(e_ctx, axis=-1, keepdims=True)
    return e_lat / den, e_ctx / den


def na_attention_fwd(name, q, k, v, bias, geo, n_lat, hd):
    t, d = q.shape
    qw, nk = QROWS * GRID_W, geo["nb"] * GRID_W
    n_blk = geo["n_blk"]
    scale = hd ** -0.5

    def body(q_ref, k_ref, v_ref, b_ref, o_ref):
        is_ctx, start = _na_tile_info(pl.program_id(1), geo)
        kb, vb = k_ref[pl.ds(start, nk), :], v_ref[pl.ds(start, nk), :]
        kc, vc = k_ref[n_lat:, :], v_ref[n_lat:, :]
        p_lat, p_ctx = _na_probs(q_ref[...], kb, kc, b_ref[...], is_ctx, scale)
        o_ref[...] = (_dot(p_lat, vb, NN) + _dot(p_ctx, vc, NN)).astype(o_ref.dtype)

    head = pl.BlockSpec((t, hd), lambda h, i: (0, h))
    tile = pl.BlockSpec((qw, hd), lambda h, i: (i, h))
    return pl.pallas_call(
        body, name=name, grid=(d // hd, t // qw),
        in_specs=[tile, head, head, pl.BlockSpec((None, None, qw, nk), lambda h, i: (_na_variant(i, n_blk), h, 0, 0))],
        out_specs=tile, out_shape=jax.ShapeDtypeStruct((t, d), BF16),
        compiler_params=_params("parallel", "arbitrary"))(q, k, v, bias)


def na_attention_bwd(name, q, k, v, bias, do, geo, n_lat, hd):
    t, d = q.shape
    qw, nk = QROWS * GRID_W, geo["nb"] * GRID_W
    n_blk = geo["n_blk"]
    scale = hd ** -0.5

    def body(q_ref, k_ref, v_ref, b_ref, do_ref, dq_ref, dk_ref, dv_ref, db_ref):
        qt = pl.program_id(1)
        is_ctx, start = _na_tile_info(qt, geo)
        band = pl.ds(start, nk)
        qv, dov = q_ref[...], do_ref[...]
        kb, vb = k_ref[band, :], v_ref[band, :]
        kc, vc = k_ref[n_lat:, :], v_ref[n_lat:, :]
        p_lat, p_ctx = _na_probs(qv, kb, kc, b_ref[...], is_ctx, scale)
        dp_lat, dp_ctx = _dot(dov, vb, NT), _dot(dov, vc, NT)
        delta = jnp.sum(p_lat * dp_lat, axis=-1, keepdims=True) + jnp.sum(p_ctx * dp_ctx, axis=-1, keepdims=True)
        ds_lat, ds_ctx = p_lat * (dp_lat - delta), p_ctx * (dp_ctx - delta)
        fresh = (qt <= 2) | (qt == n_blk - 2) | (qt == n_blk - 1)

        @pl.when(fresh)
        def _():
            db_ref[...] = ds_lat

        @pl.when(jnp.logical_not(fresh))
        def _():
            db_ref[...] += ds_lat

        dsl, dsc = (ds_lat * scale).astype(BF16), (ds_ctx * scale).astype(BF16)
        dq_ref[...] = _dot(dsl, kb, NN) + _dot(dsc, kc, NN)

        @pl.when(qt == 0)
        def _():
            dk_ref[...] = jnp.zeros_like(dk_ref)
            dv_ref[...] = jnp.zeros_like(dv_ref)

        dk_ref[band, :] += _dot(dsl, qv, TN)
        dk_ref[n_lat:, :] += _dot(dsc, qv, TN)
        dv_ref[band, :] += _dot(p_lat, dov, TN)
        dv_ref[n_lat:, :] += _dot(p_ctx, dov, TN)

    head = pl.BlockSpec((t, hd), lambda h, i: (0, h))
    tile = pl.BlockSpec((qw, hd), lambda h, i: (i, h))
    btab = pl.BlockSpec((None, None, qw, nk), lambda h, i: (_na_variant(i, n_blk), h, 0, 0))
    full = jax.ShapeDtypeStruct((t, d), F32)
    return pl.pallas_call(
        body, name=name, grid=(d // hd, t // qw), in_specs=[tile, head, head, btab, tile],
        out_specs=(tile, head, head, btab), out_shape=(full, full, full, jax.ShapeDtypeStruct(bias.shape, F32)),
        compiler_params=_params("arbitrary", "arbitrary"))(q, k, v, bias, do)


def na_rpb_grad(name, dbias, geo, n_heads):
    nb = geo["nb"]
    nv = len(geo["reps"])
    w2 = GRID_W * GRID_W
    cols, rows, npair, kpad = _na_onehots(geo)
    xmat = dbias.reshape(nv, n_heads, QROWS, GRID_W, nb, GRID_W).transpose(0, 1, 2, 4, 3, 5).reshape(nv * n_heads * QROWS * nb, w2)
    m = xmat.shape[0]
    r = _mm(name + "_cols", xmat, jnp.asarray(cols), "nn", (m, LANE), F32, (_tile(m, 512, 8), LANE, _tile(w2, 1024, LANE)), exact=True)
    r2 = r.reshape(nv, n_heads, QROWS * nb, LANE).transpose(0, 2, 1, 3).reshape(npair, n_heads * LANE)
    r2 = jnp.pad(r2, ((0, kpad - npair), (0, 0)))
    out = _mm(name + "_rows", jnp.asarray(rows), r2, "nn", (16, n_heads * LANE), F32, (16, _tile(n_heads * LANE, 1024, LANE), kpad), exact=True)
    return out[:2 * WIN_H - 1].reshape(2 * WIN_H - 1, n_heads, LANE)[:, :, :2 * WIN_W - 1].transpose(1, 0, 2)


def all_gather(name, arrays):
    n = len(arrays)

    def body(*refs):
        ins, outs = refs[:n], refs[n:2 * n]
        send_sems, recv_sems, local_sems = refs[2 * n:]
        x, y, c = _me()
        me, sib = (x, y, c), (x, y, 1 - c)
        chips = [(1 - x, y), (x, 1 - y), (1 - x, 1 - y)]

        def copy(a, k, block, to, src=None):
            dst = outs[a].at[_slot(block)]
            return pltpu.make_async_remote_copy(src_ref=dst if src is None else src, dst_ref=dst, send_sem=send_sems.at[a, k],
                                                recv_sem=recv_sems.at[a, k], device_id=to, device_id_type=MESH)

        sends, locals_ = [], []
        for a in range(n):
            mine = pltpu.make_async_copy(ins[a], outs[a].at[_slot(me)], local_sems.at[a])
            mine.start()
            locals_.append(mine)
            first = [copy(a, 0, me, sib, src=ins[a])] + [copy(a, 1 + j, me, (*chip, c), src=ins[a]) for j, chip in enumerate(chips)]
            for cp in first:
                cp.start()
            sends += first
        for j, chip in enumerate(chips):
            for a in range(n):
                copy(a, 1 + j, (*chip, c), me).wait_recv()
                passed = copy(a, 4 + j, (*chip, c), sib)
                passed.start()
                sends.append(passed)
        for a in range(n):
            copy(a, 0, sib, me).wait_recv()
            for j, chip in enumerate(chips):
                copy(a, 4 + j, (*chip, 1 - c), me).wait_recv()
        for cp in sends:
            cp.wait_send()
        for mine in locals_:
            mine.wait()

    outs = pl.pallas_call(
        body, name=name, in_specs=[HBM] * n, out_specs=[HBM] * n,
        out_shape=[jax.ShapeDtypeStruct((N_DEV, *a.shape), a.dtype) for a in arrays],
        scratch_shapes=[pltpu.SemaphoreType.DMA((n, 7)), pltpu.SemaphoreType.DMA((n, 7)), pltpu.SemaphoreType.DMA((n,))],
    )(*arrays)
    return list(outs)


def all_to_all(name, arrays):
    n = len(arrays)

    def body(*refs):
        start, wait = _exchange("scatter", refs[:n], refs[n:2 * n], *refs[2 * n:])
        start()
        wait()

    outs = pl.pallas_call(body, name=name, in_specs=[HBM] * n, out_specs=[HBM] * n, out_shape=_landing("scatter", arrays),
                          scratch_shapes=_exchange_scratch(n))(*arrays)
    return list(outs)


def _adamw_math(w, g, m, v):
    m = ADAM_B1 * m + (1.0 - ADAM_B1) * g
    v = ADAM_B2 * v + (1.0 - ADAM_B2) * (g * g)
    m_hat = m / (1.0 - ADAM_B1 ** ADAM_STEP)
    v_hat = v / (1.0 - ADAM_B2 ** ADAM_STEP)
    delta = -ADAM_LR * (m_hat / (jnp.sqrt(v_hat) + ADAM_EPS) + ADAM_WD * w)
    return delta, m, v


def adamw(name, parts, w, m, v):
    r, c = w.shape
    npart, _, cp = parts.shape
    tc = c if (c % LANE or cp != c) else _tile(c, 512, LANE)
    tr = _tile(r, 256, 16 if parts.dtype == BF16 else 8)
    if tr < 64:
        tr, tc = r, (tc if tc == c and cp != c else _tile(c, 256, LANE))
    tcp = cp if tc == c else tc

    def body(p_ref, w_ref, m_ref, v_ref, g_ref, d_ref, mo_ref, vo_ref):
        g = p_ref[0].astype(F32)
        for s in range(1, npart):
            g = g + p_ref[s].astype(F32)
        g = g[:, :tc]
        g_ref[...] = g
        d_ref[...], mo_ref[...], vo_ref[...] = _adamw_math(w_ref[...], g, m_ref[...], v_ref[...])

    blk = pl.BlockSpec((tr, tc), lambda i, j: (i, j))
    out = jax.ShapeDtypeStruct((r, c), F32)
    return pl.pallas_call(
        body, name=name, grid=(r // tr, c // tc),
        in_specs=[pl.BlockSpec((npart, tr, tcp), lambda i, j: (0, i, j)), blk, blk, blk],
        out_specs=(blk,) * 4, out_shape=(out,) * 4, compiler_params=_params("parallel", "parallel"))(parts, w, m, v)


def sum_parts(name, parts):
    npart, r, c = parts.shape
    tr = _tile(r, 512, 8)

    def body(p_ref, o_ref):
        g = p_ref[0]
        for s in range(1, npart):
            g = g + p_ref[s]
        o_ref[...] = g

    return pl.pallas_call(
        body, name=name, grid=(r // tr,), in_specs=[pl.BlockSpec((npart, tr, c), lambda i: (0, i, 0))],
        out_specs=pl.BlockSpec((tr, c), lambda i: (i, 0)), out_shape=jax.ShapeDtypeStruct((r, c), F32),
        compiler_params=_params("parallel"))(parts)


class _Pack:
    def __init__(self, shapes):
        self.shapes = dict(shapes)
        self.offsets, off = {}, 0
        for name, shape in self.shapes.items():
            self.offsets[name] = off
            off += -(-int(np.prod(shape)) // (8 * LANE)) * (8 * LANE)
        self.used = off
        self.rows = -(-off // (512 * LANE)) * 512

    def pack(self, values):
        pieces = []
        for name, shape in self.shapes.items():
            size = int(np.prod(shape))
            padded = -(-size // (8 * LANE)) * (8 * LANE)
            pieces.append(jnp.pad(values[name].astype(F32).reshape(-1), (0, padded - size)))
        pieces.append(jnp.zeros((self.rows * LANE - self.used,), F32))
        return jnp.concatenate(pieces).reshape(self.rows, LANE)

    def unpack(self, flat, lead=()):
        flat = flat.reshape(*lead, self.rows * LANE)
        out = {}
        for name, shape in self.shapes.items():
            size = int(np.prod(shape))
            out[name] = flat[..., self.offsets[name]:self.offsets[name] + size].reshape(*lead, *shape)
        return out


def kernel(x, c, ctx, c_ctx, norm_mix_g, norm_ffn_g, w_ada, b_ada, na_w_qkv, na_q_g, na_k_g, na_rpb, na_w_o, gm_w_in, gm_v_g, gm_w_s, gm_b_s, gm_w_out, sc_w_in, sc_conv_w, sc_w_out, ffn_w_up, ffn_conv_w, ffn_conv_b, ffn_w_down, loss_target, m_c_ctx, m_norm_mix_g, m_norm_ffn_g, m_w_ada, m_b_ada, m_na_w_qkv, m_na_q_g, m_na_k_g, m_na_rpb, m_na_w_o, m_gm_w_in, m_gm_v_g, m_gm_w_s, m_gm_b_s, m_gm_w_out, m_sc_w_in, m_sc_conv_w, m_sc_w_out, m_ffn_w_up, m_ffn_conv_w, m_ffn_conv_b, m_ffn_w_down, v_c_ctx, v_norm_mix_g, v_norm_ffn_g, v_w_ada, v_b_ada, v_na_w_qkv, v_na_q_g, v_na_k_g, v_na_rpb, v_na_w_o, v_gm_w_in, v_gm_v_g, v_gm_w_s, v_gm_b_s, v_gm_w_out, v_sc_w_in, v_sc_conv_w, v_sc_w_out, v_ffn_w_up, v_ffn_conv_w, v_ffn_conv_b, v_ffn_w_down):
    weights = dict(c_ctx=c_ctx, norm_mix_g=norm_mix_g, norm_ffn_g=norm_ffn_g, w_ada=w_ada, b_ada=b_ada, na_w_qkv=na_w_qkv,
                   na_q_g=na_q_g, na_k_g=na_k_g, na_rpb=na_rpb, na_w_o=na_w_o, gm_w_in=gm_w_in, gm_v_g=gm_v_g, gm_w_s=gm_w_s,
                   gm_b_s=gm_b_s, gm_w_out=gm_w_out, sc_w_in=sc_w_in, sc_conv_w=sc_conv_w, sc_w_out=sc_w_out,
                   ffn_w_up=ffn_w_up, ffn_conv_w=ffn_conv_w, ffn_conv_b=ffn_conv_b, ffn_w_down=ffn_w_down)
    mom_m = dict(c_ctx=m_c_ctx, norm_mix_g=m_norm_mix_g, norm_ffn_g=m_norm_ffn_g, w_ada=m_w_ada, b_ada=m_b_ada, na_w_qkv=m_na_w_qkv,
                 na_q_g=m_na_q_g, na_k_g=m_na_k_g, na_rpb=m_na_rpb, na_w_o=m_na_w_o, gm_w_in=m_gm_w_in, gm_v_g=m_gm_v_g,
                 gm_w_s=m_gm_w_s, gm_b_s=m_gm_b_s, gm_w_out=m_gm_w_out, sc_w_in=m_sc_w_in, sc_conv_w=m_sc_conv_w,
                 sc_w_out=m_sc_w_out, ffn_w_up=m_ffn_w_up, ffn_conv_w=m_ffn_conv_w, ffn_conv_b=m_ffn_conv_b, ffn_w_down=m_ffn_w_down)
    mom_v = dict(c_ctx=v_c_ctx, norm_mix_g=v_norm_mix_g, norm_ffn_g=v_norm_ffn_g, w_ada=v_w_ada, b_ada=v_b_ada, na_w_qkv=v_na_w_qkv,
                 na_q_g=v_na_q_g, na_k_g=v_na_k_g, na_rpb=v_na_rpb, na_w_o=v_na_w_o, gm_w_in=v_gm_w_in, gm_v_g=v_gm_v_g,
                 gm_w_s=v_gm_w_s, gm_b_s=v_gm_b_s, gm_w_out=v_gm_w_out, sc_w_in=v_sc_w_in, sc_conv_w=v_sc_conv_w,
                 sc_w_out=v_sc_w_out, ffn_w_up=v_ffn_w_up, ffn_conv_w=v_ffn_conv_w, ffn_conv_b=v_ffn_conv_b, ffn_w_down=v_ffn_w_down)
    names = list(weights)

    n_lat, d = x.shape[1], x.shape[2]
    n_ctx = ctx.shape[1]
    t = n_lat + n_ctx
    depth = norm_mix_g.shape[0]
    hd = na_q_g.shape[-1]
    n_heads = d // hd
    nup = ffn_w_up.shape[-1]
    nup_p = -(-nup // LANE) * LANE
    fdim, fp = 4 * nup, 4 * nup_p
    me = _slot(_me())
    geo = _na_geometry(n_lat)

    pad_up = lambda a: jnp.pad(a, [(0, 0)] * (a.ndim - 1) + [(0, nup_p - nup)])
    mixer_weights = (("na_w_qkv", "na_w_o"), ("gm_w_in", "gm_w_out"), ("sc_w_in", "sc_w_out"))

    def shards(i):
        w_in, w_out = (weights[nm][i // N_MIXERS].astype(BF16) for nm in mixer_weights[i % N_MIXERS])
        return [w_in, w_out, pad_up(ffn_w_up[i]).astype(BF16), ffn_w_down[i].astype(BF16)]

    def operands(g_in, g_out, g_up, g_down):
        down = jnp.pad(g_down.reshape(4, nup, d), ((0, 0), (0, nup_p - nup), (0, 0))).reshape(fp, d)
        return dict(w_in=g_in, w_out=g_out.reshape(-1, d), w_up=g_up, w_down=down)

    *g_first, g_fcw, g_scw, c_all = all_gather("gather_first", shards(0) + [pad_up(ffn_conv_w), sc_conv_w, c])
    layer_w = [operands(*g_first)]
    f_cw = [g_fcw[:, i].transpose(1, 0, 2).reshape(3, 2, fp).transpose(1, 0, 2) for i in range(depth)]
    cb_p = pad_up(ffn_conv_b.reshape(depth, N_DEV, nup)).reshape(depth, 2, 1, fp)
    s_cw = [g_scw[:, j].transpose(1, 0, 2).reshape(3, d) for j in range(sc_conv_w.shape[0])]

    cond = jnp.concatenate([c_all.reshape(N_DEV, d), c_ctx[None], jnp.zeros((7, d), F32)])
    mod_cols = mm_nn("ada_fwd", cond, w_ada, a_silu=True)
    (mod_all,) = all_gather("gather_mod", [mod_cols])
    ncol = w_ada.shape[-1]
    mod_all = mod_all.reshape(N_DEV, 16, depth, ncol).transpose(2, 1, 0, 3).reshape(depth, 16, N_MOD * d) + b_ada[:, None, :]
    mod_lat = lax.dynamic_index_in_dim(mod_all, me, axis=1, keepdims=False)
    mods = jnp.stack([mod_lat, mod_all[:, N_DEV]], axis=1).reshape(depth, 2, N_MOD, d)
    sh1, sc1, g1, sh2, sc2, g2 = (mods[:, :, kd] for kd in range(N_MOD))

    xs = jnp.concatenate([x[0], ctx[0]], axis=0)
    saved = []
    prev = None
    for i in range(depth):
        mixer, j = i % N_MIXERS, i // N_MIXERS
        s = {}
        if prev is None:
            s["x"] = xs
            s["h"] = resid_rms_mod(f"l{i}_norm_mix", xs, None, None, norm_mix_g[i:i + 1], sc1[i], sh1[i], n_lat)
        else:
            s["x"], s["h"] = resid_rms_mod(f"l{i}_norm_mix", prev[0], prev[1], prev[2], norm_mix_g[i:i + 1], sc1[i], sh1[i], n_lat)
        lw = layer_w[i]
        if mixer == 0:
            s["qkv"] = mm_nn(f"l{i}_qkv", s["h"], lw["w_in"])
            s["q"], s["k"], s["v"] = qk_norm_fwd(f"l{i}_qk_norm", s["qkv"], na_q_g[j:j + 1], na_k_g[j:j + 1], hd)
            s["bias"] = na_bias_table(f"l{i}_bias", na_rpb[j], geo)
            s["o"] = na_attention_fwd(f"l{i}_attn", s["q"], s["k"], s["v"], s["bias"], geo, n_lat, hd)
        elif mixer == 1:
            s["t"] = mm_nn(f"l{i}_gm_in", s["h"], lw["w_in"])
            s["o"] = gmlp_gate_fwd(f"l{i}_gm_gate", s["t"], gm_v_g[j:j + 1], gm_w_s[j], gm_b_s[j][:, :, None])
        else:
            s["t"] = mm_nn(f"l{i}_sc_in", s["h"], lw["w_in"])
            s["o"] = sc_gate_fwd(f"l{i}_sc_gate", s["t"], s_cw[j], n_lat)
        s["y"] = mm_nn(f"l{i}_mix_out", s["o"], lw["w_out"])
        s["x1"], s["hf"] = resid_rms_mod(f"l{i}_norm_ffn", s["x"], s["y"], g1[i], norm_ffn_g[i:i + 1], sc2[i], sh2[i], n_lat)
        if i + 1 < depth:
            n_in, n_out, n_up, n_down = shards(i + 1)
            s["u"], (g_up,) = mm_nn(f"l{i}_ffn_up", s["hf"], lw["w_up"], out_parts=2, carry=("gather", [n_up]))
            s["a"], (g_down,) = ffn_act_fwd(f"l{i}_ffn_act", s["u"], f_cw[i], cb_p[i], n_lat, carry=("gather", [n_down]))
            s["f"], (g_in, g_out) = mm_nn(f"l{i}_ffn_down", s["a"], lw["w_down"], carry=("gather", [n_in, n_out]))
            layer_w.append(operands(g_in, g_out, g_up, g_down))
        else:
            s["u"] = mm_nn(f"l{i}_ffn_up", s["hf"], lw["w_up"], out_parts=2)
            s["a"] = ffn_act_fwd(f"l{i}_ffn_act", s["u"], f_cw[i], cb_p[i], n_lat)
            s["f"] = mm_nn(f"l{i}_ffn_down", s["a"], lw["w_down"])
        prev = (s["x1"], s["f"], g2[i])
        saved.append(s)

    dx, loss_local, df, dg2 = loss_head("loss_head", prev[0], prev[1], prev[2], loss_target[0])
    loss = lax.psum(loss_local[0, 0], AXES)

    big = {}
    small = {}
    dmod = [None] * depth
    zeros_like_param = lambda p: [None] * p.shape[0]
    for nm in ("na_w_qkv", "na_w_o", "gm_w_in", "gm_w_out", "sc_w_in", "sc_w_out", "ffn_w_up", "ffn_w_down"):
        big[nm] = zeros_like_param(weights[nm])
    for nm in ("norm_mix_g", "norm_ffn_g", "ffn_conv_w", "ffn_conv_b", "na_q_g", "na_k_g", "na_rpb", "gm_v_g", "gm_w_s", "gm_b_s", "sc_conv_w"):
        small[nm] = zeros_like_param(weights[nm])
    landed, pending = {}, []

    def take(wanted):
        keys = [k for k in pending if wanted(k)]
        for k in keys:
            pending.remove(k)
        return keys

    for i in reversed(range(depth)):
        mixer, j = i % N_MIXERS, i // N_MIXERS
        nm_in, nm_out = mixer_weights[mixer]
        s, lw = saved[i], layer_w[i]
        da = mm_nt(f"l{i}_b_ffn_down_x", df, lw["w_down"])
        dwd = mm_tn(f"l{i}_b_ffn_down_w", s["a"], df)
        big["ffn_w_down"][i] = dwd.reshape(4, nup_p, d)[:, :nup].reshape(N_DEV, fdim // N_DEV, d)
        keys = take(lambda k: k[0] == "ffn_w_up" or k[0] in [m[1] for m in mixer_weights])
        if keys:
            du, dcw, dcb, got = ffn_act_bwd(f"l{i}_b_ffn_act", s["u"], da, f_cw[i], cb_p[i], n_lat,
                                            carry=("scatter", [big[nm][idx] for nm, idx in keys]))
            landed.update(zip(keys, got))
        else:
            du, dcw, dcb = ffn_act_bwd(f"l{i}_b_ffn_act", s["u"], da, f_cw[i], cb_p[i], n_lat)
        small["ffn_conv_w"][i] = dcw.transpose(1, 0, 2).reshape(3, N_DEV, nup_p)[:, :, :nup].reshape(3, 2 * fdim)
        small["ffn_conv_b"][i] = dcb.reshape(N_DEV, nup_p)[:, :nup].reshape(2 * fdim)
        keys = [("ffn_w_down", i)] + take(lambda k: True)
        dhf, got = mm_nt(f"l{i}_b_ffn_up_x", du, lw["w_up"], carry=("scatter", [big[nm][idx] for nm, idx in keys]))
        landed.update(zip(keys, got))
        big["ffn_w_up"][i] = mm_tn(f"l{i}_b_ffn_up_w", s["hf"], du, out_parts=N_DEV)
        dx1, dsh2, dsc2, dgf, dy, dg1 = rms_mod_bwd(f"l{i}_b_norm_ffn", s["x1"], dhf, norm_ffn_g[i:i + 1], sc2[i], dx, n_lat,
                                                    branch=(s["y"], g1[i]))
        small["norm_ffn_g"][i] = dgf[0]
        do = mm_nt(f"l{i}_b_mix_out_x", dy, lw["w_out"], out_dtype=BF16 if mixer == 0 else F32)
        big[nm_out][j] = mm_tn(f"l{i}_b_mix_out_w", s["o"], dy).reshape(N_DEV, -1, d)
        if mixer == 0:
            dq, dk, dv, dbias = na_attention_bwd(f"l{i}_b_attn", s["q"], s["k"], s["v"], s["bias"], do, geo, n_lat, hd)
            small["na_rpb"][j] = na_rpb_grad(f"l{i}_b_rpb", dbias, geo, n_heads)
            dt, dqg, dkg = qk_norm_bwd(f"l{i}_b_qk_norm", s["qkv"], dq, dk, dv, na_q_g[j:j + 1], na_k_g[j:j + 1], hd)
            small["na_q_g"][j], small["na_k_g"][j] = dqg[0], dkg[0]
        elif mixer == 1:
            dt, dvg, dws, dsv = gmlp_gate_bwd(f"l{i}_b_gm_gate", s["t"], do, gm_v_g[j:j + 1], gm_w_s[j], gm_b_s[j][:, :, None])
            groups, width = gm_w_s.shape[1], dsv.shape[1]
            group_of = np.zeros((width, LANE), np.float32)
            group_of[np.arange(width), np.arange(width) // (width // groups)] = 1.0
            dbs = _mm(f"l{i}_b_gm_bs", dsv, jnp.asarray(group_of), "nn", (GM_CHUNK, LANE), F32,
                      (GM_CHUNK, LANE, _tile(width, 2048, LANE)), exact=True)[:, :groups].T
            small["gm_v_g"][j], small["gm_w_s"][j], small["gm_b_s"][j] = dvg[0], dws, dbs
        else:
            dt, dscw = sc_gate_bwd(f"l{i}_b_sc_gate", s["t"], do, s_cw[j], n_lat)
            small["sc_conv_w"][j] = dscw
            dt = dt.transpose(1, 0, 2).reshape(t, 3 * d)
        dh = mm_nt(f"l{i}_b_mix_in_x", dt, lw["w_in"])
        big[nm_in][j] = mm_tn(f"l{i}_b_mix_in_w", s["h"], dt, out_parts=N_DEV)
        pending += [("ffn_w_up", i), (nm_out, j), (nm_in, j)]
        dmod_ffn = [dsh2, dsc2, dg2]
        if i > 0:
            dx, dsh1, dsc1, dgm, df, dg2 = rms_mod_bwd(f"l{i}_b_norm_mix", s["x"], dh, norm_mix_g[i:i + 1], sc1[i], dx1, n_lat,
                                                       branch=(saved[i - 1]["f"], g2[i - 1]))
        else:
            dx, dsh1, dsc1, dgm = rms_mod_bwd(f"l{i}_b_norm_mix", s["x"], dh, norm_mix_g[i:i + 1], sc1[i], dx1, n_lat)
        small["norm_mix_g"][i] = dgm[0]
        dmod[i] = jnp.stack([dsh1, dsc1, dg1] + dmod_ffn, axis=1)
    grad_x = dx[:n_lat][None]

    small = {nm: jnp.stack(v) for nm, v in small.items()}
    small["dmod"] = jnp.stack(dmod)
    pack = _Pack({nm: v.shape for nm, v in small.items()})
    (small_all,) = all_gather("gather_small", [pack.pack(small)])
    small_sum = pack.unpack(sum_parts("sum_small", small_all))
    dmod_all = pack.unpack(small_all, lead=(N_DEV,))["dmod"]
    dmod_ctx = small_sum["dmod"][:, 1].reshape(depth, N_MOD * d)
    grads = {nm: small_sum[nm] for nm in small if nm != "dmod"}
    grads["b_ada"] = (small_sum["dmod"][:, 0] + small_sum["dmod"][:, 1]).reshape(depth, N_MOD * d)
    my_cols = lambda a, width: lax.dynamic_slice_in_dim(a, me * width, width, axis=-1)
    grads["ffn_conv_w"] = my_cols(grads["ffn_conv_w"], nup)
    grads["sc_conv_w"] = my_cols(grads["sc_conv_w"], sc_conv_w.shape[-1])

    drows = jnp.concatenate([dmod_all[:, :, 0], dmod_all[:, :, 1]]).reshape(2 * N_DEV, depth, N_MOD * d)
    drows = my_cols(drows, ncol).reshape(2 * N_DEV, depth * ncol)
    cond2 = jnp.concatenate([c_all.reshape(N_DEV, d), jnp.broadcast_to(c_ctx[None], (N_DEV, d))])
    g_w_ada = mm_tn("ada_bwd_w", cond2, drows, out_parts=depth, out_dtype=F32, a_silu=True)
    dctx_rows = jnp.pad(my_cols(dmod_ctx, ncol).reshape(1, depth * ncol), ((0, 15), (0, 0)))
    dcc = mm_nt("ada_bwd_c", dctx_rows, w_ada)[0:1]
    (dcc_all,) = all_gather("gather_c_ctx", [jnp.pad(dcc.reshape(-1, LANE), ((0, (-d // LANE) % 8), (0, 0)))])
    dcc_sum = sum_parts("sum_c_ctx", dcc_all).reshape(-1)[:d]
    sig = jax.nn.sigmoid(c_ctx)
    grads["c_ctx"] = dcc_sum * (sig * (1.0 + c_ctx * (1.0 - sig)))

    landed.update(zip(pending, all_to_all("scatter_last", [big[nm][idx] for nm, idx in pending])))
    out = {nm: [None] * 4 for nm in names}
    per_big = {nm: [] for nm in big}
    for nm, idx in [(nm, idx) for nm in big for idx in range(len(big[nm]))]:
        w2, p = weights[nm][idx], landed[nm, idx]
        per_big[nm].append(adamw(f"adamw_{nm}{idx}", p.reshape(N_DEV, w2.shape[0], -1), w2, mom_m[nm][idx], mom_v[nm][idx]))
    for nm, res in per_big.items():
        out[nm] = [jnp.stack([r[q] for r in res]) for q in range(4)]
    res = [adamw(f"adamw_w_ada{i}", g_w_ada[i][None], w_ada[i], m_w_ada[i], v_w_ada[i]) for i in range(depth)]
    out["w_ada"] = [jnp.stack([r[q] for r in res]) for q in range(4)]
    small_names = [nm for nm in names if nm not in big and nm != "w_ada"]
    spack = _Pack({nm: weights[nm].shape for nm in small_names})
    flat = [spack.pack({nm: src[nm] for nm in small_names}) for src in (grads, weights, mom_m, mom_v)]
    res = adamw("adamw_small", flat[0][None], flat[1], flat[2], flat[3])
    res = [spack.unpack(r) for r in res]
    for nm in small_names:
        out[nm] = [grads[nm].reshape(weights[nm].shape)] + [res[q][nm] for q in range(1, 4)]

    return (loss, grad_x, *[out[nm][0] for nm in names], *[out[nm][1] for nm in names],
            *[out[nm][2] for nm in names], *[out[nm][3] for nm in names])
```

```python
import functools
import math

import numpy as np
import jax
import jax.numpy as jnp
from jax import lax
from jax.experimental import pallas as pl
from jax.experimental.pallas import tpu as pltpu

F32 = jnp.float32
BF16 = jnp.bfloat16
MESH = pl.DeviceIdType.MESH
AXES = ("x", "y", "c")
N_DEV = 8
N_MOD = 6
N_MIXERS = 3
EPS = 1e-6
GRID_W = 64
WIN_H = 8
WIN_W = 16
QROWS = 2
GM_CHUNK = 128
LANE = 128
NEG = -1e30
ADAM_LR = 0.001
ADAM_B1 = 0.9
ADAM_B2 = 0.999
ADAM_EPS = 1e-08
ADAM_WD = 0.01
ADAM_STEP = 10
VMEM_LIMIT = 56 * 1024 * 1024
HBM = pl.BlockSpec(memory_space=pltpu.HBM)

NN = (((1,), (0,)), ((), ()))
NT = (((1,), (1,)), ((), ()))
TN = (((0,), (0,)), ((), ()))


def _params(*sem):
    return pltpu.CompilerParams(dimension_semantics=sem, vmem_limit_bytes=VMEM_LIMIT)


def _tile(n, pref, mult):
    best = None
    for t in range(mult, min(n, pref) + 1, mult):
        if n % t == 0:
            best = t
    return n if best is None else best


def _full(arr):
    nd = arr.ndim
    return pl.BlockSpec(arr.shape, lambda *g: (0,) * nd)


def _logical(shape):
    return tuple(shape) if len(shape) == 2 else (shape[1], shape[0] * shape[2])


def _cspec(shape, tr, tc, rc):
    if len(shape) == 2:
        return pl.BlockSpec((tr, tc), rc)
    cpp = shape[2] // tc

    def imap(*g):
        r, c = rc(*g)
        return (c // cpp, r, c % cpp)

    return pl.BlockSpec((None, tr, tc), imap)


def _dot(a, b, dims, exact=False):
    if exact:
        return lax.dot_general(a, b, dims, precision=lax.Precision.HIGHEST, preferred_element_type=F32)
    return lax.dot_general(a.astype(BF16), b.astype(BF16), dims, preferred_element_type=F32)


def _silu(z):
    return z * jax.nn.sigmoid(z)


def _me():
    return lax.axis_index("x"), lax.axis_index("y"), lax.axis_index("c")


def _slot(p):
    return 4 * p[0] + 2 * p[1] + p[2]


def _exchange(kind, srcs, dsts, send_sems, recv_sems, local_sems):
    x, y, c = _me()
    me = (x, y, c)
    peers = [((x + (k >> 2)) % 2, (y + ((k >> 1) & 1)) % 2, (c + (k & 1)) % 2) for k in range(1, N_DEV)]

    def src_of(a, dev):
        return srcs[a] if kind == "gather" else srcs[a].at[_slot(dev)]

    def remote(a, k, peer, src_dev, dst_dev):
        return pltpu.make_async_remote_copy(src_ref=src_of(a, src_dev), dst_ref=dsts[a].at[_slot(dst_dev)], send_sem=send_sems.at[a, k],
                                            recv_sem=recv_sems.at[a, k], device_id=peer, device_id_type=MESH)

    def local(a):
        return pltpu.make_async_copy(src_of(a, me), dsts[a].at[_slot(me)], local_sems.at[a])

    def start():
        for a in range(len(srcs)):
            local(a).start()
            for k, peer in enumerate(peers):
                remote(a, k, peer, peer, me).start()

    def wait():
        for a in range(len(srcs)):
            for k, peer in enumerate(peers):
                remote(a, k, peer, me, peer).wait_recv()
        for a in range(len(srcs)):
            for k, peer in enumerate(peers):
                remote(a, k, peer, peer, me).wait_send()
            local(a).wait()

    return start, wait


def _exchange_scratch(n):
    return [pltpu.SemaphoreType.DMA((n, N_DEV - 1)), pltpu.SemaphoreType.DMA((n, N_DEV - 1)), pltpu.SemaphoreType.DMA((n,))]


def _landing(kind, arrays):
    return [jax.ShapeDtypeStruct((N_DEV, *a.shape) if kind == "gather" else a.shape, a.dtype) for a in arrays]


def _call(body, name, grid, in_specs, out_specs, out_shape, ins, sem, scratch=(), carry=None):
    in_specs, out_specs, out_shape, scratch = list(in_specs), list(out_specs), list(out_shape), list(scratch)
    if carry is None:
        return list(pl.pallas_call(body, name=name, grid=grid, in_specs=in_specs, out_specs=out_specs, out_shape=out_shape,
                                   scratch_shapes=scratch, compiler_params=_params(*sem))(*ins))
    kind, arrays = carry
    n, ni, no, ns = len(arrays), len(in_specs), len(out_specs), len(scratch)

    def carrying(*refs):
        own_in, srcs = refs[:ni], refs[ni:ni + n]
        own_out, dsts = refs[ni + n:ni + n + no], refs[ni + n + no:ni + 2 * n + no]
        own_scratch, sems = refs[ni + 2 * n + no:ni + 2 * n + no + ns], refs[ni + 2 * n + no + ns:]
        start, wait = _exchange(kind, srcs, dsts, *sems)
        first = functools.reduce(jnp.logical_and, [pl.program_id(ax) == 0 for ax in range(len(grid))])
        last = functools.reduce(jnp.logical_and, [pl.program_id(ax) == g - 1 for ax, g in enumerate(grid)])
        pl.when(first)(start)
        body(*own_in, *own_out, *own_scratch)
        pl.when(last)(wait)

    res = pl.pallas_call(
        carrying, name=name, grid=grid, in_specs=in_specs + [HBM] * n, out_specs=out_specs + [HBM] * n,
        out_shape=out_shape + _landing(kind, arrays), scratch_shapes=scratch + _exchange_scratch(n),
        compiler_params=_params(*(["arbitrary"] * len(grid))))(*ins, *arrays)
    return list(res[:no]), list(res[no:])


def _mm(name, a, b, kind, out_shape, out_dtype, tiles, a_silu=False, exact=False, carry=None):
    la, lb, lo = _logical(a.shape), _logical(b.shape), _logical(out_shape)
    t0, t1, t2 = tiles
    if kind == "nn":
        grid = (lo[1] // t1, lo[0] // t0, la[1] // t2)
        a_spec = _cspec(a.shape, t0, t2, lambda j, i, k: (i, k))
        b_spec = _cspec(b.shape, t2, t1, lambda j, i, k: (k, j))
        o_spec = _cspec(out_shape, t0, t1, lambda j, i, k: (i, j))
        dims, acc = NN, (t0, t1)
    elif kind == "nt":
        grid = (lo[1] // t1, lo[0] // t0, la[1] // t2)
        a_spec = _cspec(a.shape, t0, t2, lambda p, i, r: (i, r))
        b_spec = _cspec(b.shape, t1, t2, lambda p, i, r: (p, r))
        o_spec = _cspec(out_shape, t0, t1, lambda p, i, r: (i, p))
        dims, acc = NT, (t0, t1)
    else:
        grid = (lo[1] // t1, lo[0] // t0, la[0] // t2)
        a_spec = _cspec(a.shape, t2, t0, lambda j, kk, r: (r, kk))
        b_spec = _cspec(b.shape, t2, t1, lambda j, kk, r: (r, j))
        o_spec = _cspec(out_shape, t0, t1, lambda j, kk, r: (kk, j))
        dims, acc = TN, (t0, t1)
    nk = grid[2]
    in_place = out_dtype == F32

    def body(a_ref, b_ref, o_ref, *scratch):
        acc_ref = o_ref if in_place else scratch[0]
        k = pl.program_id(2)
        av = a_ref[...]
        if a_silu:
            av = _silu(av)
        part = _dot(av, b_ref[...], dims, exact)

        @pl.when(k == 0)
        def _():
            acc_ref[...] = part

        @pl.when(k > 0)
        def _():
            acc_ref[...] += part

        if not in_place:
            @pl.when(k == nk - 1)
            def _():
                o_ref[...] = acc_ref[...].astype(o_ref.dtype)

    res = _call(body, name, grid, [a_spec, b_spec], [o_spec], [jax.ShapeDtypeStruct(out_shape, out_dtype)], (a, b),
                ("parallel", "parallel", "arbitrary"), scratch=[] if in_place else [pltpu.VMEM(acc, F32)], carry=carry)
    return res[0] if carry is None else (res[0][0], res[1])


MM_VMEM_BUDGET = 40 * 1024 * 1024
MM_TILE_CAP = 2048


def _divisors(n, mult):
    return [t for t in range(mult, min(n, MM_TILE_CAP) + 1, mult) if n % t == 0] or [n]


def _mm_tiles(c0, c1, c2, a, b, out_dtype):
    ia, ib, io = a.dtype.itemsize, b.dtype.itemsize, jnp.dtype(out_dtype).itemsize
    best, best_score = None, -1
    for t0 in c0:
        for t1 in c1:
            for t2 in c2:
                need = 2 * (t0 * t2 * ia + t1 * t2 * ib + t0 * t1 * io) + t0 * t1 * 4 * (1 if out_dtype == F32 else 2)
                score = (t0 * t1 * t2, t2)
                if need <= MM_VMEM_BUDGET and score > (best_score if best else (-1, -1)):
                    best, best_score = (t0, t1, t2), score
    return best if best else (c0[0], c1[0], c2[0])


def mm_nn(name, a, w, out_parts=1, out_dtype=F32, **kw):
    (m, _), (_, n) = _logical(a.shape), _logical(w.shape)
    out_shape = (m, n) if out_parts == 1 else (out_parts, m, n // out_parts)
    tiles = _mm_tiles(_divisors(m, 16), _divisors(math.gcd(w.shape[-1], out_shape[-1]), LANE),
                      _divisors(math.gcd(a.shape[-1], w.shape[-2]), LANE), a, w, out_dtype)
    return _mm(name, a, w, "nn", out_shape, out_dtype, tiles, **kw)


def mm_nt(name, a, w, out_dtype=F32, **kw):
    (m, _), (p, _) = _logical(a.shape), _logical(w.shape)
    tiles = _mm_tiles(_divisors(m, 16), _divisors(w.shape[-2], LANE), _divisors(math.gcd(a.shape[-1], w.shape[-1]), LANE), a, w, out_dtype)
    return _mm(name, a, w, "nt", (m, p), out_dtype, tiles, **kw)


def mm_tn(name, a, dy, out_parts=1, out_dtype=BF16, **kw):
    (t, k), (_, n) = _logical(a.shape), _logical(dy.shape)
    out_shape = (k, n) if out_parts == 1 else (out_parts, k, n // out_parts)
    tiles = _mm_tiles(_divisors(a.shape[-1], LANE), _divisors(math.gcd(dy.shape[-1], out_shape[-1]), LANE), _divisors(t, 16), a, dy, out_dtype)
    return _mm(name, a, dy, "tn", out_shape, out_dtype, tiles, **kw)


def _rows(tr, d):
    return pl.BlockSpec((tr, d), lambda i: (i, 0))


def _pick(ctx, ref):
    return jnp.where(ctx, ref[1:2, :], ref[0:1, :])


def resid_rms_mod(name, x, y, gate, g, sc, sh, n_lat, tr=256):
    t, d = x.shape
    nlt = n_lat // tr
    has_res = y is not None

    def body(*refs):
        if has_res:
            x_ref, y_ref, gate_ref, g_ref, sc_ref, sh_ref, x1_ref, h_ref = refs
        else:
            x_ref, g_ref, sc_ref, sh_ref, h_ref = refs
        ctx = pl.program_id(0) >= nlt
        xv = x_ref[...]
        if has_res:
            xv = xv + _pick(ctx, gate_ref) * y_ref[...]
            x1_ref[...] = xv
        r = lax.rsqrt(jnp.mean(xv * xv, axis=-1, keepdims=True) + EPS)
        n = xv * r * g_ref[...]
        h_ref[...] = (n * (1.0 + _pick(ctx, sc_ref)) + _pick(ctx, sh_ref)).astype(h_ref.dtype)

    row = _rows(tr, d)
    if has_res:
        ins, in_specs = (x, y, gate, g, sc, sh), [row, row, _full(gate), _full(g), _full(sc), _full(sh)]
        out_shape = (jax.ShapeDtypeStruct((t, d), F32), jax.ShapeDtypeStruct((t, d), BF16))
        out_specs = (row, row)
    else:
        ins, in_specs = (x, g, sc, sh), [row, _full(g), _full(sc), _full(sh)]
        out_shape = jax.ShapeDtypeStruct((t, d), BF16)
        out_specs = row
    return pl.pallas_call(body, name=name, grid=(t // tr,), in_specs=in_specs, out_specs=out_specs,
                          out_shape=out_shape, compiler_params=_params("parallel"))(*ins)


def _acc_rows(i, nlt, ref, val):
    @pl.when(i == 0)
    def _():
        ref[...] = jnp.zeros_like(ref)

    @pl.when(i < nlt)
    def _():
        ref[0:1, :] += val

    @pl.when(i >= nlt)
    def _():
        ref[1:2, :] += val


def rms_mod_bwd(name, x, dh, g, sc, dres, n_lat, branch=None, tr=256):
    t, d = x.shape
    nlt = n_lat // tr

    def body(x_ref, dh_ref, g_ref, sc_ref, dres_ref, *rest):
        if branch is None:
            dx_ref, dsh_ref, dsc_ref, dg_ref = rest
        else:
            y_ref, gate_ref, dx_ref, dsh_ref, dsc_ref, dg_ref, dy_ref, dgate_ref = rest
        i = pl.program_id(0)
        xv, dhv, gv = x_ref[...], dh_ref[...], g_ref[...]
        r = lax.rsqrt(jnp.mean(xv * xv, axis=-1, keepdims=True) + EPS)
        xhat = xv * r
        dn = dhv * (1.0 + _pick(i >= nlt, sc_ref))
        dxhat = dn * gv
        dxv = r * (dxhat - xhat * jnp.mean(dxhat * xhat, axis=-1, keepdims=True)) + dres_ref[...]
        dx_ref[...] = dxv
        if branch is not None:
            dy_ref[...] = (_pick(i >= nlt, gate_ref) * dxv).astype(dy_ref.dtype)
            _acc_rows(i, nlt, dgate_ref, jnp.sum(dxv * y_ref[...], axis=0, keepdims=True))
        _acc_rows(i, nlt, dsh_ref, jnp.sum(dhv, axis=0, keepdims=True))
        _acc_rows(i, nlt, dsc_ref, jnp.sum(dhv * (xhat * gv), axis=0, keepdims=True))
        dgp = jnp.sum(dn * xhat, axis=0, keepdims=True)

        @pl.when(i == 0)
        def _():
            dg_ref[...] = dgp

        @pl.when(i > 0)
        def _():
            dg_ref[...] += dgp

    row = _rows(tr, d)
    two = pl.BlockSpec((2, d), lambda i: (0, 0))
    two_shape = jax.ShapeDtypeStruct((2, d), F32)
    ins, in_specs = [x, dh, g, sc, dres], [row, row, _full(g), _full(sc), row]
    out_specs = [row, two, two, pl.BlockSpec((1, d), lambda i: (0, 0))]
    out_shape = [jax.ShapeDtypeStruct((t, d), F32), two_shape, two_shape, jax.ShapeDtypeStruct((1, d), F32)]
    if branch is not None:
        ins, in_specs = ins + list(branch), in_specs + [row, _full(branch[1])]
        out_specs, out_shape = out_specs + [row, two], out_shape + [jax.ShapeDtypeStruct((t, d), BF16), two_shape]
    return pl.pallas_call(body, name=name, grid=(t // tr,), in_specs=in_specs, out_specs=out_specs, out_shape=out_shape,
                          compiler_params=_params("arbitrary"))(*ins)


def loss_head(name, x1, f, gate, target, tr=256):
    t, d = x1.shape
    nlt = target.shape[0] // tr

    def body(x_ref, f_ref, gate_ref, t_ref, dx_ref, loss_ref, df_ref, dgate_ref, acc_ref):
        i = pl.program_id(0)

        @pl.when(i == 0)
        def _():
            acc_ref[...] = jnp.zeros_like(acc_ref)
            dgate_ref[...] = jnp.zeros_like(dgate_ref)

        @pl.when(i < nlt)
        def _():
            fv, gv = f_ref[...], gate_ref[0:1, :]
            e = x_ref[...] + gv * fv - t_ref[...]
            dxv = e / d
            dx_ref[...] = dxv
            df_ref[...] = (gv * dxv).astype(df_ref.dtype)
            dgate_ref[0:1, :] += jnp.sum(dxv * fv, axis=0, keepdims=True)
            acc_ref[...] += jnp.sum(e * e, axis=0, keepdims=True)

        @pl.when(i >= nlt)
        def _():
            dx_ref[...] = jnp.zeros_like(dx_ref)
            df_ref[...] = jnp.zeros_like(df_ref)

        @pl.when(i == t // tr - 1)
        def _():
            loss_ref[...] = jnp.sum(acc_ref[...], axis=1, keepdims=True) * (0.5 / d)

    row = _rows(tr, d)
    return pl.pallas_call(
        body, name=name, grid=(t // tr,),
        in_specs=[row, row, _full(gate), pl.BlockSpec((tr, d), lambda i: (jnp.minimum(i, nlt - 1), 0))],
        out_specs=(row, pl.BlockSpec((1, 1), lambda i: (0, 0)), row, pl.BlockSpec((2, d), lambda i: (0, 0))),
        out_shape=(jax.ShapeDtypeStruct((t, d), F32), jax.ShapeDtypeStruct((1, 1), F32),
                   jax.ShapeDtypeStruct((t, d), BF16), jax.ShapeDtypeStruct((2, d), F32)),
        scratch_shapes=[pltpu.VMEM((1, d), F32)],
        compiler_params=_params("arbitrary"))(x1, f, gate, target)


HALO = 8


def _halo_specs(shape, tr, tc, col):
    hb = tr // HALO
    last = _logical(shape)[0] // HALO - 1
    main = _cspec(shape, tr, tc, lambda j, i: (i, col(j)))
    prev = _cspec(shape, HALO, tc, lambda j, i: (jnp.maximum(i * hb - 1, 0), col(j)))
    nxt = _cspec(shape, HALO, tc, lambda j, i: (jnp.minimum((i + 1) * hb, last), col(j)))
    return [prev, main, nxt]


def _seq_edges(i, nlt, nt):
    first = (i == 0) | (i == nlt)
    last = (i == nlt - 1) | (i == nt - 1)
    return first, last


def _ext(prev_ref, main_ref, next_ref, first, last):
    p = jnp.where(first, 0.0, prev_ref[...].astype(F32))
    n = jnp.where(last, 0.0, next_ref[...].astype(F32))
    return jnp.concatenate([p, main_ref[...].astype(F32), n], axis=0)


def _up(e):
    return pltpu.roll(e, 1, 0)


def _down(e):
    return pltpu.roll(e, e.shape[0] - 1, 0)


def _conv(e, w):
    return _up(e) * w[0:1, :] + e * w[1:2, :] + _down(e) * w[2:3, :]


def _conv_t(e, w):
    return _down(e) * w[0:1, :] + e * w[1:2, :] + _up(e) * w[2:3, :]


def _mid(e, tr):
    return e[HALO:HALO + tr, :]


def _acc_cols(i, ref, val):
    @pl.when(i == 0)
    def _():
        ref[...] = val

    @pl.when(i > 0)
    def _():
        ref[...] += val


def _colsum(v):
    return jnp.sum(v, axis=0, keepdims=True)


def ffn_act_fwd(name, u, cw, cb, n_lat, tr=256, carry=None):
    _, t, fp = u.shape
    tc = _tile(fp, 1536, LANE)
    nlt, nt = n_lat // tr, t // tr

    def body(pg, mg, ng, pu, mu, nu, cw_ref, cb_ref, a_ref):
        first, last = _seq_edges(pl.program_id(1), nlt, nt)
        zg = _mid(_conv(_ext(pg, mg, ng, first, last), cw_ref[0]), tr) + cb_ref[0]
        zu = _mid(_conv(_ext(pu, mu, nu, first, last), cw_ref[1]), tr) + cb_ref[1]
        a_ref[...] = (_silu(zg) * zu).astype(a_ref.dtype)

    ncol = fp // tc
    specs = _halo_specs(u.shape, tr, tc, lambda j: j) + _halo_specs(u.shape, tr, tc, lambda j: j + ncol)
    specs += [pl.BlockSpec((2, 3, tc), lambda j, i: (0, 0, j)), pl.BlockSpec((2, 1, tc), lambda j, i: (0, 0, j))]
    res = _call(body, name, (ncol, nt), specs, [pl.BlockSpec((tr, tc), lambda j, i: (i, j))], [jax.ShapeDtypeStruct((t, fp), BF16)],
                (u, u, u, u, u, u, cw, cb), ("parallel", "parallel"), carry=carry)
    return res[0] if carry is None else (res[0][0], res[1])


def ffn_act_bwd(name, u, da, cw, cb, n_lat, tr=128, carry=None):
    _, t, fp = u.shape
    tc = _tile(fp, 1536, LANE)
    nlt, nt = n_lat // tr, t // tr
    ncol = fp // tc

    def body(pg, mg, ng, pu, mu, nu, pa, ma, na, cw_ref, cb_ref, du_ref, dcw_ref, dcb_ref):
        i = pl.program_id(1)
        first, last = _seq_edges(i, nlt, nt)
        wg, wu = cw_ref[0], cw_ref[1]
        ug, uu = _ext(pg, mg, ng, first, last), _ext(pu, mu, nu, first, last)
        dae = _ext(pa, ma, na, first, last)
        zg = _conv(ug, wg) + cb_ref[0]
        zu = _conv(uu, wu) + cb_ref[1]
        sg = jax.nn.sigmoid(zg)
        dzg = dae * zu * (sg * (1.0 + zg * (1.0 - sg)))
        dzu = dae * (zg * sg)
        du_ref[0] = _mid(_conv_t(dzg, wg), tr).astype(du_ref.dtype)
        du_ref[1] = _mid(_conv_t(dzu, wu), tr).astype(du_ref.dtype)
        for h, (dz, ue) in enumerate(((dzg, ug), (dzu, uu))):
            dzm = _mid(dz, tr)
            rows = [_colsum(dzm * _mid(_up(ue), tr)), _colsum(dzm * _mid(ue, tr)), _colsum(dzm * _mid(_down(ue), tr))]
            _acc_cols(i, dcw_ref.at[h], jnp.concatenate(rows, axis=0))
            _acc_cols(i, dcb_ref.at[h], _colsum(dzm))

    specs = _halo_specs(u.shape, tr, tc, lambda j: j) + _halo_specs(u.shape, tr, tc, lambda j: j + ncol)
    specs += _halo_specs(da.shape, tr, tc, lambda j: j)
    specs += [pl.BlockSpec((2, 3, tc), lambda j, i: (0, 0, j)), pl.BlockSpec((2, 1, tc), lambda j, i: (0, 0, j))]
    res = _call(body, name, (ncol, nt), specs,
                [pl.BlockSpec((2, tr, tc), lambda j, i: (0, i, j)),
                 pl.BlockSpec((2, 3, tc), lambda j, i: (0, 0, j)), pl.BlockSpec((2, 1, tc), lambda j, i: (0, 0, j))],
                [jax.ShapeDtypeStruct((2, t, fp), BF16), jax.ShapeDtypeStruct((2, 3, fp), F32), jax.ShapeDtypeStruct((2, 1, fp), F32)],
                (u, u, u, u, u, u, da, da, da, cw, cb), ("parallel", "arbitrary"), carry=carry)
    return tuple(res) if carry is None else (*res[0], res[1])


def sc_gate_fwd(name, tmat, cw, n_lat, tr=256):
    t, d3 = tmat.shape
    d = d3 // 3
    tc = _tile(d, 512, LANE)
    ncol = d // tc
    nlt, nt = n_lat // tr, t // tr

    def body(b_ref, pc, mc, nc, px, mx, nx, cw_ref, s_ref):
        first, last = _seq_edges(pl.program_id(1), nlt, nt)
        p = _ext(pc, mc, nc, first, last) * _ext(px, mx, nx, first, last)
        s_ref[...] = (b_ref[...] * _mid(_conv(p, cw_ref[...]), tr)).astype(s_ref.dtype)

    specs = [pl.BlockSpec((tr, tc), lambda j, i: (i, j))]
    specs += _halo_specs(tmat.shape, tr, tc, lambda j: j + ncol) + _halo_specs(tmat.shape, tr, tc, lambda j: j + 2 * ncol)
    specs += [pl.BlockSpec((3, tc), lambda j, i: (0, j))]
    return pl.pallas_call(
        body, name=name, grid=(ncol, nt), in_specs=specs, out_specs=pl.BlockSpec((tr, tc), lambda j, i: (i, j)),
        out_shape=jax.ShapeDtypeStruct((t, d), BF16),
        compiler_params=_params("parallel", "parallel"))(*([tmat] * 7), cw)


def sc_gate_bwd(name, tmat, ds, cw, n_lat, tr=128):
    t, d3 = tmat.shape
    d = d3 // 3
    tc = _tile(d, 512, LANE)
    ncol = d // tc
    nlt, nt = n_lat // tr, t // tr

    def body(pb, mb, nb, pc, mc, nc, px, mx, nx, pd, md, nd, cw_ref, dt_ref, dcw_ref):
        i = pl.program_id(1)
        first, last = _seq_edges(i, nlt, nt)
        w = cw_ref[...]
        be, ce, xe = _ext(pb, mb, nb, first, last), _ext(pc, mc, nc, first, last), _ext(px, mx, nx, first, last)
        dse = _ext(pd, md, nd, first, last)
        p = ce * xe
        dcv = dse * be
        dp = _conv_t(dcv, w)
        dt_ref[0] = _mid(dse * _conv(p, w), tr).astype(dt_ref.dtype)
        dt_ref[1] = _mid(dp * xe, tr).astype(dt_ref.dtype)
        dt_ref[2] = _mid(dp * ce, tr).astype(dt_ref.dtype)
        dm = _mid(dcv, tr)
        rows = [_colsum(dm * _mid(_up(p), tr)), _colsum(dm * _mid(p, tr)), _colsum(dm * _mid(_down(p), tr))]
        _acc_cols(i, dcw_ref, jnp.concatenate(rows, axis=0))

    specs = []
    for part in range(3):
        specs += _halo_specs(tmat.shape, tr, tc, functools.partial(lambda j, part: j + part * ncol, part=part))
    specs += _halo_specs(ds.shape, tr, tc, lambda j: j)
    specs += [pl.BlockSpec((3, tc), lambda j, i: (0, j))]
    return pl.pallas_call(
        body, name=name, grid=(ncol, nt), in_specs=specs,
        out_specs=(pl.BlockSpec((3, tr, tc), lambda j, i: (0, i, j)), pl.BlockSpec((3, tc), lambda j, i: (0, j))),
        out_shape=(jax.ShapeDtypeStruct((3, t, d), BF16), jax.ShapeDtypeStruct((3, d), F32)),
        compiler_params=_params("parallel", "arbitrary"))(*([tmat] * 9), ds, ds, ds, cw)


_GELU_K = 0.7978845608028654
_GELU_C = 0.044715


def _gelu(x):
    return 0.5 * x * (1.0 + jnp.tanh(_GELU_K * (x + _GELU_C * (x * x * x))))


def _gelu_grad(x):
    th = jnp.tanh(_GELU_K * (x + _GELU_C * (x * x * x)))
    return 0.5 * (1.0 + th) + 0.5 * x * (1.0 - th * th) * (_GELU_K * (1.0 + 3.0 * _GELU_C * (x * x)))


def gmlp_gate_fwd(name, tmat, vg, ws, bs):
    t, w2 = tmat.shape
    w = w2 // 2
    groups = ws.shape[0]
    gd = w // groups

    def body(t_ref, vg_ref, ws_ref, bs_ref, o_ref):
        v = _gelu(t_ref[:, w:])
        r = lax.rsqrt(jnp.mean(v * v, axis=-1, keepdims=True) + EPS)
        vn = (v * r * vg_ref[...]).astype(BF16)
        for g in range(groups):
            cols = slice(g * gd, (g + 1) * gd)
            sv = _dot(ws_ref[g], vn[:, cols], NN) + bs_ref[g]
            o_ref[:, cols] = (_gelu(t_ref[:, cols]) * sv).astype(o_ref.dtype)

    return pl.pallas_call(
        body, name=name, grid=(t // GM_CHUNK,),
        in_specs=[_rows(GM_CHUNK, w2), _full(vg), _full(ws), _full(bs)], out_specs=_rows(GM_CHUNK, w),
        out_shape=jax.ShapeDtypeStruct((t, w), BF16), compiler_params=_params("parallel"))(tmat, vg, ws, bs)


def gmlp_gate_bwd(name, tmat, dout, vg, ws, bs):
    t, w2 = tmat.shape
    w = w2 // 2
    groups = ws.shape[0]
    gd = w // groups

    def body(t_ref, do_ref, vg_ref, ws_ref, bs_ref, dt_ref, dvg_ref, dws_ref, dsv_ref, dvn_ref):
        i = pl.program_id(0)
        tv = t_ref[:, w:]
        v = _gelu(tv)
        r = lax.rsqrt(jnp.mean(v * v, axis=-1, keepdims=True) + EPS)
        vhat = v * r
        vn = (vhat * vg_ref[...]).astype(BF16)
        for g in range(groups):
            cols = slice(g * gd, (g + 1) * gd)
            tu = t_ref[:, cols]
            dov = do_ref[:, cols]
            sv = _dot(ws_ref[g], vn[:, cols], NN) + bs_ref[g]
            dt_ref[:, cols] = (dov * sv * _gelu_grad(tu)).astype(dt_ref.dtype)
            dsv = dov * _gelu(tu)
            _acc_cols(i, dsv_ref.at[:, cols], dsv)
            _acc_cols(i, dws_ref.at[g], _dot(dsv, vn[:, cols], NT))
            dvn_ref[:, cols] = _dot(ws_ref[g], dsv, TN)
        dvn = dvn_ref[...]
        _acc_cols(i, dvg_ref, _colsum(dvn * vhat))
        dvhat = dvn * vg_ref[...]
        dv = r * (dvhat - vhat * jnp.mean(dvhat * vhat, axis=-1, keepdims=True))
        dt_ref[:, w:] = (dv * _gelu_grad(tv)).astype(dt_ref.dtype)

    keep = lambda shape: pl.BlockSpec(shape, lambda i: (0,) * len(shape))
    return pl.pallas_call(
        body, name=name, grid=(t // GM_CHUNK,),
        in_specs=[_rows(GM_CHUNK, w2), _rows(GM_CHUNK, w), _full(vg), _full(ws), _full(bs)],
        out_specs=(_rows(GM_CHUNK, w2), keep((1, w)), keep(ws.shape), keep((GM_CHUNK, w))),
        out_shape=(jax.ShapeDtypeStruct((t, w2), BF16), jax.ShapeDtypeStruct((1, w), F32),
                   jax.ShapeDtypeStruct(ws.shape, F32), jax.ShapeDtypeStruct((GM_CHUNK, w), F32)),
        scratch_shapes=[pltpu.VMEM((GM_CHUNK, w), F32)],
        compiler_params=_params("arbitrary"))(tmat, dout, vg, ws, bs)


def qk_norm_fwd(name, qkv, qg, kg, hd, tr=128):
    t, d3 = qkv.shape
    d = d3 // 3

    def body(x_ref, qg_ref, kg_ref, q_ref, k_ref, v_ref):
        for part, (g_ref, o_ref) in enumerate(((qg_ref, q_ref), (kg_ref, k_ref))):
            for h in range(d // hd):
                xh = x_ref[:, part * d + h * hd: part * d + (h + 1) * hd]
                r = lax.rsqrt(jnp.mean(xh * xh, axis=-1, keepdims=True) + EPS)
                o_ref[:, h * hd:(h + 1) * hd] = (xh * r * g_ref[...]).astype(o_ref.dtype)
        v_ref[...] = x_ref[:, 2 * d:].astype(v_ref.dtype)

    out = jax.ShapeDtypeStruct((t, d), BF16)
    return pl.pallas_call(
        body, name=name, grid=(t // tr,), in_specs=[_rows(tr, d3), _full(qg), _full(kg)],
        out_specs=(_rows(tr, d),) * 3, out_shape=(out,) * 3, compiler_params=_params("parallel"))(qkv, qg, kg)


def qk_norm_bwd(name, qkv, dq, dk, dv, qg, kg, hd, tr=128):
    t, d3 = qkv.shape
    d = d3 // 3

    def body(x_ref, dq_ref, dk_ref, dv_ref, qg_ref, kg_ref, o_ref, dqg_ref, dkg_ref):
        i = pl.program_id(0)
        for part, (g_ref, dn_ref, dg_ref) in enumerate(((qg_ref, dq_ref, dqg_ref), (kg_ref, dk_ref, dkg_ref))):
            dg = jnp.zeros((1, hd), F32)
            for h in range(d // hd):
                xh = x_ref[:, part * d + h * hd: part * d + (h + 1) * hd]
                dn = dn_ref[:, h * hd:(h + 1) * hd]
                r = lax.rsqrt(jnp.mean(xh * xh, axis=-1, keepdims=True) + EPS)
                xhat = xh * r
                dg = dg + _colsum(dn * xhat)
                dxhat = dn * g_ref[...]
                dx = r * (dxhat - xhat * jnp.mean(dxhat * xhat, axis=-1, keepdims=True))
                o_ref[:, part * d + h * hd: part * d + (h + 1) * hd] = dx.astype(o_ref.dtype)
            _acc_cols(i, dg_ref, dg)
        o_ref[:, 2 * d:] = dv_ref[...].astype(o_ref.dtype)

    one = pl.BlockSpec((1, hd), lambda i: (0, 0))
    return pl.pallas_call(
        body, name=name, grid=(t // tr,),
        in_specs=[_rows(tr, d3), _rows(tr, d), _rows(tr, d), _rows(tr, d), _full(qg), _full(kg)],
        out_specs=(_rows(tr, d3), one, one),
        out_shape=(jax.ShapeDtypeStruct((t, d3), BF16), jax.ShapeDtypeStruct((1, hd), F32), jax.ShapeDtypeStruct((1, hd), F32)),
        compiler_params=_params("arbitrary"))(qkv, dq, dk, dv, qg, kg)


def _na_geometry(n_lat):
    rows = n_lat // GRID_W
    kh = min(WIN_H, rows)
    nb = min(kh + QROWS - 1, rows)
    n_blk = rows // QROWS
    q_row_off = np.repeat(np.arange(QROWS), GRID_W)
    q_col = np.tile(np.arange(GRID_W), QROWS)
    k_row_off = np.repeat(np.arange(nb), GRID_W)
    k_col = np.tile(np.arange(GRID_W), nb)
    c_start = np.clip(q_col - WIN_W // 2, 0, GRID_W - WIN_W)
    col_ok = (k_col[None, :] >= c_start[:, None]) & (k_col[None, :] < c_start[:, None] + WIN_W)
    dc_idx = np.clip(k_col[None, :] - q_col[:, None], -(WIN_W - 1), WIN_W - 1) + WIN_W - 1

    def block(blk):
        r0 = blk * QROWS
        q_row = r0 + q_row_off
        r_start = np.clip(q_row - kh // 2, 0, rows - kh)
        band0 = min(int(np.clip(r0 - kh // 2, 0, rows - kh)), rows - nb)
        k_row = band0 + k_row_off
        ok = col_ok & (k_row[None, :] >= r_start[:, None]) & (k_row[None, :] < r_start[:, None] + kh)
        dr_idx = np.clip(k_row[None, :] - q_row[:, None], -(WIN_H - 1), WIN_H - 1) + WIN_H - 1
        return band0, ok, dr_idx

    reps = [0, 1, 2, n_blk - 2, n_blk - 1]
    variant = lambda blk: 0 if blk == 0 else 1 if blk == 1 else 3 if blk == n_blk - 2 else 4 if blk == n_blk - 1 else 2
    geo = [block(b) for b in reps]
    for blk in range(n_blk):
        _, ok, dr = block(blk)
        assert np.array_equal(ok, geo[variant(blk)][1]) and np.array_equal(np.where(ok, dr, 0), np.where(ok, geo[variant(blk)][2], 0))
    return dict(rows=rows, kh=kh, nb=nb, n_blk=n_blk, reps=reps, ok=[g[1] for g in geo], dr=[g[2] for g in geo],
                band0=[g[0] for g in geo], dc=dc_idx)


def _na_onehots(geo):
    nb = geo["nb"]
    w2 = GRID_W * GRID_W
    qc, kc = np.meshgrid(np.arange(GRID_W), np.arange(GRID_W), indexing="ij")
    diff = (kc - qc).reshape(-1)
    cols = np.zeros((w2, LANE), np.float32)
    sel = np.abs(diff) <= WIN_W - 1
    cols[np.arange(w2)[sel], diff[sel] + WIN_W - 1] = 1.0
    npair = len(geo["reps"]) * QROWS * nb
    kpad = -(-npair // LANE) * LANE
    rows = np.zeros((16, kpad), np.float32)
    for vi, blk in enumerate(geo["reps"]):
        for qr in range(QROWS):
            for kr in range(nb):
                dr = (geo["band0"][vi] + kr) - (blk * QROWS + qr)
                if abs(dr) <= WIN_H - 1:
                    rows[dr + WIN_H - 1, (vi * QROWS + qr) * nb + kr] = 1.0
    return cols, rows, npair, kpad


def na_bias_table(name, rpb, geo):
    n_heads, nb, nv = rpb.shape[0], geo["nb"], len(geo["reps"])
    cols, rows, npair, kpad = _na_onehots(geo)
    rpb_p = jnp.pad(rpb, ((0, 0), (0, 16 - rpb.shape[1]), (0, LANE - rpb.shape[2]))).transpose(1, 0, 2).reshape(16, n_heads * LANE)
    t1 = _mm(name + "_rows", jnp.asarray(rows.T), rpb_p, "nn", (kpad, n_heads * LANE), F32,
             (kpad, _tile(n_heads * LANE, 1024, LANE), 16), exact=True)
    t1 = t1[:npair].reshape(nv, QROWS * nb, n_heads, LANE).transpose(0, 2, 1, 3).reshape(nv * n_heads * QROWS * nb, LANE)
    m = t1.shape[0]
    flat = _mm(name + "_cols", t1, jnp.asarray(cols), "nt", (m, GRID_W * GRID_W), F32,
               (_tile(m, 512, 8), _tile(GRID_W * GRID_W, 2048, LANE), LANE), exact=True)
    tab = flat.reshape(nv, n_heads, QROWS, nb, GRID_W, GRID_W).transpose(0, 1, 2, 4, 3, 5).reshape(nv, n_heads, QROWS * GRID_W, nb * GRID_W)
    return jnp.where(jnp.asarray(np.stack(geo["ok"]))[:, None], tab, NEG)


def _na_tile_info(qt, geo):
    n_blk, rows, kh, nb = geo["n_blk"], geo["rows"], geo["kh"], geo["nb"]
    is_ctx = qt >= n_blk
    band0 = jnp.minimum(jnp.clip(qt * QROWS - kh // 2, 0, rows - kh), rows - nb)
    band0 = jnp.where(is_ctx, 0, band0)
    return is_ctx, pl.multiple_of(band0 * GRID_W, GRID_W)


def _na_variant(qt, n_blk):
    v = jnp.minimum(qt, 2) + (qt >= n_blk - 2).astype(jnp.int32) + (qt >= n_blk - 1).astype(jnp.int32)
    return jnp.minimum(v, 4)


def _na_probs(q, kb, kc, bias, is_ctx, scale):
    s_lat = _dot(q, kb, NT) * scale + bias
    s_lat = jnp.where(is_ctx, NEG, s_lat)
    s_ctx = _dot(q, kc, NT) * scale
    m = jnp.maximum(jnp.max(s_lat, axis=-1, keepdims=True), jnp.max(s_ctx, axis=-1, keepdims=True))
    e_lat, e_ctx = jnp.exp(s_lat - m), jnp.exp(s_ctx - m)
    den = jnp.sum(e_lat, axis=-1, keepdims=True) + jnp.sum(e_ctx, axis=-1, keepdims=True)
    return e_lat / den, e_ctx / den


NA_TILES_PER_STEP = 2


def na_attention_fwd(name, q, k, v, bias, geo, n_lat, hd, carry=None):
    t, d = q.shape
    qw, nk = QROWS * GRID_W, geo["nb"] * GRID_W
    n_blk = geo["n_blk"]
    scale = hd ** -0.5

    def body(q_ref, k_ref, v_ref, b0_ref, b1_ref, o_ref):
        kc, vc = k_ref[n_lat:, :], v_ref[n_lat:, :]
        for half, b_ref in enumerate((b0_ref, b1_ref)):
            rows = slice(half * qw, (half + 1) * qw)
            is_ctx, start = _na_tile_info(NA_TILES_PER_STEP * pl.program_id(1) + half, geo)
            kb, vb = k_ref[pl.ds(start, nk), :], v_ref[pl.ds(start, nk), :]
            p_lat, p_ctx = _na_probs(q_ref[rows, :], kb, kc, b_ref[...], is_ctx, scale)
            o_ref[rows, :] = (_dot(p_lat, vb, NN) + _dot(p_ctx, vc, NN)).astype(o_ref.dtype)

    head = pl.BlockSpec((t, hd), lambda h, i: (0, h))
    tile = pl.BlockSpec((NA_TILES_PER_STEP * qw, hd), lambda h, i: (i, h))
    btabs = [pl.BlockSpec((None, None, qw, nk), functools.partial(lambda h, i, half: (_na_variant(NA_TILES_PER_STEP * i + half, n_blk), h, 0, 0), half=half))
             for half in range(NA_TILES_PER_STEP)]
    res = _call(body, name, (d // hd, t // (NA_TILES_PER_STEP * qw)), [tile, head, head, *btabs], [tile],
                [jax.ShapeDtypeStruct((t, d), BF16)], (q, k, v, bias, bias), ("parallel", "arbitrary"), carry=carry)
    return res[0] if carry is None else (res[0][0], res[1])


def na_attention_bwd(name, q, k, v, bias, do, geo, n_lat, hd, carry=None):
    t, d = q.shape
    qw, nk = QROWS * GRID_W, geo["nb"] * GRID_W
    n_blk = geo["n_blk"]
    scale = hd ** -0.5

    def body(q_ref, k_ref, v_ref, b0_ref, b1_ref, do_ref, dq_ref, dk_ref, dv_ref, db_ref):
        step = pl.program_id(1)

        @pl.when(step == 0)
        def _():
            dk_ref[...] = jnp.zeros_like(dk_ref)
            dv_ref[...] = jnp.zeros_like(dv_ref)
            db_ref[...] = jnp.zeros_like(db_ref)

        kc, vc = k_ref[n_lat:, :], v_ref[n_lat:, :]
        for half, b_ref in enumerate((b0_ref, b1_ref)):
            qt = NA_TILES_PER_STEP * step + half
            rows = slice(half * qw, (half + 1) * qw)
            is_ctx, start = _na_tile_info(qt, geo)
            band = pl.ds(start, nk)
            qv, dov = q_ref[rows, :], do_ref[rows, :]
            kb, vb = k_ref[band, :], v_ref[band, :]
            p_lat, p_ctx = _na_probs(qv, kb, kc, b_ref[...], is_ctx, scale)
            dp_lat, dp_ctx = _dot(dov, vb, NT), _dot(dov, vc, NT)
            delta = jnp.sum(p_lat * dp_lat, axis=-1, keepdims=True) + jnp.sum(p_ctx * dp_ctx, axis=-1, keepdims=True)
            ds_lat, ds_ctx = p_lat * (dp_lat - delta), p_ctx * (dp_ctx - delta)
            db_ref[_na_variant(qt, n_blk)] += ds_lat
            dsl, dsc = (ds_lat * scale).astype(BF16), (ds_ctx * scale).astype(BF16)
            dq_ref[rows, :] = _dot(dsl, kb, NN) + _dot(dsc, kc, NN)
            dk_ref[band, :] += _dot(dsl, qv, TN)
            dk_ref[n_lat:, :] += _dot(dsc, qv, TN)
            dv_ref[band, :] += _dot(p_lat, dov, TN)
            dv_ref[n_lat:, :] += _dot(p_ctx, dov, TN)

    nv = bias.shape[0]
    head = pl.BlockSpec((t, hd), lambda h, i: (0, h))
    tile = pl.BlockSpec((NA_TILES_PER_STEP * qw, hd), lambda h, i: (i, h))
    btabs = [pl.BlockSpec((None, None, qw, nk), functools.partial(lambda h, i, half: (_na_variant(NA_TILES_PER_STEP * i + half, n_blk), h, 0, 0), half=half))
             for half in range(NA_TILES_PER_STEP)]
    full = jax.ShapeDtypeStruct((t, d), F32)
    res = _call(body, name, (d // hd, t // (NA_TILES_PER_STEP * qw)), [tile, head, head, *btabs, tile],
                [tile, head, head, pl.BlockSpec((nv, None, qw, nk), lambda h, i: (0, h, 0, 0))],
                [full, full, full, jax.ShapeDtypeStruct(bias.shape, F32)], (q, k, v, bias, bias, do), ("arbitrary", "arbitrary"), carry=carry)
    return tuple(res) if carry is None else (*res[0], res[1])


def na_rpb_grad(name, dbias, geo, n_heads):
    nb = geo["nb"]
    nv = len(geo["reps"])
    w2 = GRID_W * GRID_W
    cols, rows, npair, kpad = _na_onehots(geo)
    xmat = dbias.reshape(nv, n_heads, QROWS, GRID_W, nb, GRID_W).transpose(0, 1, 2, 4, 3, 5).reshape(nv * n_heads * QROWS * nb, w2)
    m = xmat.shape[0]
    r = _mm(name + "_cols", xmat, jnp.asarray(cols), "nn", (m, LANE), F32, (_tile(m, 512, 8), LANE, _tile(w2, 1024, LANE)), exact=True)
    r2 = r.reshape(nv, n_heads, QROWS * nb, LANE).transpose(0, 2, 1, 3).reshape(npair, n_heads * LANE)
    r2 = jnp.pad(r2, ((0, kpad - npair), (0, 0)))
    out = _mm(name + "_rows", jnp.asarray(rows), r2, "nn", (16, n_heads * LANE), F32, (16, _tile(n_heads * LANE, 1024, LANE), kpad), exact=True)
    return out[:2 * WIN_H - 1].reshape(2 * WIN_H - 1, n_heads, LANE)[:, :, :2 * WIN_W - 1].transpose(1, 0, 2)


def all_gather(name, arrays):
    n = len(arrays)

    def body(*refs):
        ins, outs = refs[:n], refs[n:2 * n]
        send_sems, recv_sems, local_sems = refs[2 * n:]
        x, y, c = _me()
        me, sib = (x, y, c), (x, y, 1 - c)
        chips = [(1 - x, y), (x, 1 - y), (1 - x, 1 - y)]

        def copy(a, k, block, to, src=None):
            dst = outs[a].at[_slot(block)]
            return pltpu.make_async_remote_copy(src_ref=dst if src is None else src, dst_ref=dst, send_sem=send_sems.at[a, k],
                                                recv_sem=recv_sems.at[a, k], device_id=to, device_id_type=MESH)

        sends, locals_ = [], []
        for a in range(n):
            mine = pltpu.make_async_copy(ins[a], outs[a].at[_slot(me)], local_sems.at[a])
            mine.start()
            locals_.append(mine)
            first = [copy(a, 0, me, sib, src=ins[a])] + [copy(a, 1 + j, me, (*chip, c), src=ins[a]) for j, chip in enumerate(chips)]
            for cp in first:
                cp.start()
            sends += first
        for j, chip in enumerate(chips):
            for a in range(n):
                copy(a, 1 + j, (*chip, c), me).wait_recv()
                passed = copy(a, 4 + j, (*chip, c), sib)
                passed.start()
                sends.append(passed)
        for a in range(n):
            copy(a, 0, sib, me).wait_recv()
            for j, chip in enumerate(chips):
                copy(a, 4 + j, (*chip, 1 - c), me).wait_recv()
        for cp in sends:
            cp.wait_send()
        for mine in locals_:
            mine.wait()

    outs = pl.pallas_call(
        body, name=name, in_specs=[HBM] * n, out_specs=[HBM] * n,
        out_shape=[jax.ShapeDtypeStruct((N_DEV, *a.shape), a.dtype) for a in arrays],
        scratch_shapes=[pltpu.SemaphoreType.DMA((n, 7)), pltpu.SemaphoreType.DMA((n, 7)), pltpu.SemaphoreType.DMA((n,))],
    )(*arrays)
    return list(outs)


def all_to_all(name, arrays):
    n = len(arrays)

    def body(*refs):
        start, wait = _exchange("scatter", refs[:n], refs[n:2 * n], *refs[2 * n:])
        start()
        wait()

    outs = pl.pallas_call(body, name=name, in_specs=[HBM] * n, out_specs=[HBM] * n, out_shape=_landing("scatter", arrays),
                          scratch_shapes=_exchange_scratch(n))(*arrays)
    return list(outs)


def _adamw_math(w, g, m, v):
    m = ADAM_B1 * m + (1.0 - ADAM_B1) * g
    v = ADAM_B2 * v + (1.0 - ADAM_B2) * (g * g)
    m_hat = m / (1.0 - ADAM_B1 ** ADAM_STEP)
    v_hat = v / (1.0 - ADAM_B2 ** ADAM_STEP)
    delta = -ADAM_LR * (m_hat / (jnp.sqrt(v_hat) + ADAM_EPS) + ADAM_WD * w)
    return delta, m, v


def adamw(name, parts, w, m, v):
    r, c = w.shape
    npart, _, cp = parts.shape
    tc = c if (c % LANE or cp != c) else _tile(c, 512, LANE)
    tr = _tile(r, 256, 16 if parts.dtype == BF16 else 8)
    if tr < 64:
        tr, tc = r, (tc if tc == c and cp != c else _tile(c, 256, LANE))
    tcp = cp if tc == c else tc

    def body(p_ref, w_ref, m_ref, v_ref, g_ref, d_ref, mo_ref, vo_ref):
        g = p_ref[0].astype(F32)
        for s in range(1, npart):
            g = g + p_ref[s].astype(F32)
        g = g[:, :tc]
        g_ref[...] = g
        d_ref[...], mo_ref[...], vo_ref[...] = _adamw_math(w_ref[...], g, m_ref[...], v_ref[...])

    blk = pl.BlockSpec((tr, tc), lambda i, j: (i, j))
    out = jax.ShapeDtypeStruct((r, c), F32)
    return pl.pallas_call(
        body, name=name, grid=(r // tr, c // tc),
        in_specs=[pl.BlockSpec((npart, tr, tcp), lambda i, j: (0, i, j)), blk, blk, blk],
        out_specs=(blk,) * 4, out_shape=(out,) * 4, compiler_params=_params("parallel", "parallel"))(parts, w, m, v)


def sum_parts(name, parts):
    npart, r, c = parts.shape
    tr = _tile(r, 512, 8)

    def body(p_ref, o_ref):
        g = p_ref[0]
        for s in range(1, npart):
            g = g + p_ref[s]
        o_ref[...] = g

    return pl.pallas_call(
        body, name=name, grid=(r // tr,), in_specs=[pl.BlockSpec((npart, tr, c), lambda i: (0, i, 0))],
        out_specs=pl.BlockSpec((tr, c), lambda i: (i, 0)), out_shape=jax.ShapeDtypeStruct((r, c), F32),
        compiler_params=_params("parallel"))(parts)


class _Pack:
    def __init__(self, shapes):
        self.shapes = dict(shapes)
        self.offsets, off = {}, 0
        for name, shape in self.shapes.items():
            self.offsets[name] = off
            off += -(-int(np.prod(shape)) // (8 * LANE)) * (8 * LANE)
        self.used = off
        self.rows = -(-off // (512 * LANE)) * 512

    def pack(self, values):
        pieces = []
        for name, shape in self.shapes.items():
            size = int(np.prod(shape))
            padded = -(-size // (8 * LANE)) * (8 * LANE)
            pieces.append(jnp.pad(values[name].astype(F32).reshape(-1), (0, padded - size)))
        pieces.append(jnp.zeros((self.rows * LANE - self.used,), F32))
        return jnp.concatenate(pieces).reshape(self.rows, LANE)

    def unpack(self, flat, lead=()):
        flat = flat.reshape(*lead, self.rows * LANE)
        out = {}
        for name, shape in self.shapes.items():
            size = int(np.prod(shape))
            out[name] = flat[..., self.offsets[name]:self.offsets[name] + size].reshape(*lead, *shape)
        return out


def kernel(x, c, ctx, c_ctx, norm_mix_g, norm_ffn_g, w_ada, b_ada, na_w_qkv, na_q_g, na_k_g, na_rpb, na_w_o, gm_w_in, gm_v_g, gm_w_s, gm_b_s, gm_w_out, sc_w_in, sc_conv_w, sc_w_out, ffn_w_up, ffn_conv_w, ffn_conv_b, ffn_w_down, loss_target, m_c_ctx, m_norm_mix_g, m_norm_ffn_g, m_w_ada, m_b_ada, m_na_w_qkv, m_na_q_g, m_na_k_g, m_na_rpb, m_na_w_o, m_gm_w_in, m_gm_v_g, m_gm_w_s, m_gm_b_s, m_gm_w_out, m_sc_w_in, m_sc_conv_w, m_sc_w_out, m_ffn_w_up, m_ffn_conv_w, m_ffn_conv_b, m_ffn_w_down, v_c_ctx, v_norm_mix_g, v_norm_ffn_g, v_w_ada, v_b_ada, v_na_w_qkv, v_na_q_g, v_na_k_g, v_na_rpb, v_na_w_o, v_gm_w_in, v_gm_v_g, v_gm_w_s, v_gm_b_s, v_gm_w_out, v_sc_w_in, v_sc_conv_w, v_sc_w_out, v_ffn_w_up, v_ffn_conv_w, v_ffn_conv_b, v_ffn_w_down):
    weights = dict(c_ctx=c_ctx, norm_mix_g=norm_mix_g, norm_ffn_g=norm_ffn_g, w_ada=w_ada, b_ada=b_ada, na_w_qkv=na_w_qkv,
                   na_q_g=na_q_g, na_k_g=na_k_g, na_rpb=na_rpb, na_w_o=na_w_o, gm_w_in=gm_w_in, gm_v_g=gm_v_g, gm_w_s=gm_w_s,
                   gm_b_s=gm_b_s, gm_w_out=gm_w_out, sc_w_in=sc_w_in, sc_conv_w=sc_conv_w, sc_w_out=sc_w_out,
                   ffn_w_up=ffn_w_up, ffn_conv_w=ffn_conv_w, ffn_conv_b=ffn_conv_b, ffn_w_down=ffn_w_down)
    mom_m = dict(c_ctx=m_c_ctx, norm_mix_g=m_norm_mix_g, norm_ffn_g=m_norm_ffn_g, w_ada=m_w_ada, b_ada=m_b_ada, na_w_qkv=m_na_w_qkv,
                 na_q_g=m_na_q_g, na_k_g=m_na_k_g, na_rpb=m_na_rpb, na_w_o=m_na_w_o, gm_w_in=m_gm_w_in, gm_v_g=m_gm_v_g,
                 gm_w_s=m_gm_w_s, gm_b_s=m_gm_b_s, gm_w_out=m_gm_w_out, sc_w_in=m_sc_w_in, sc_conv_w=m_sc_conv_w,
                 sc_w_out=m_sc_w_out, ffn_w_up=m_ffn_w_up, ffn_conv_w=m_ffn_conv_w, ffn_conv_b=m_ffn_conv_b, ffn_w_down=m_ffn_w_down)
    mom_v = dict(c_ctx=v_c_ctx, norm_mix_g=v_norm_mix_g, norm_ffn_g=v_norm_ffn_g, w_ada=v_w_ada, b_ada=v_b_ada, na_w_qkv=v_na_w_qkv,
                 na_q_g=v_na_q_g, na_k_g=v_na_k_g, na_rpb=v_na_rpb, na_w_o=v_na_w_o, gm_w_in=v_gm_w_in, gm_v_g=v_gm_v_g,
                 gm_w_s=v_gm_w_s, gm_b_s=v_gm_b_s, gm_w_out=v_gm_w_out, sc_w_in=v_sc_w_in, sc_conv_w=v_sc_conv_w,
                 sc_w_out=v_sc_w_out, ffn_w_up=v_ffn_w_up, ffn_conv_w=v_ffn_conv_w, ffn_conv_b=v_ffn_conv_b, ffn_w_down=v_ffn_w_down)
    names = list(weights)

    n_lat, d = x.shape[1], x.shape[2]
    n_ctx = ctx.shape[1]
    t = n_lat + n_ctx
    depth = norm_mix_g.shape[0]
    hd = na_q_g.shape[-1]
    n_heads = d // hd
    nup = ffn_w_up.shape[-1]
    nup_p = -(-nup // LANE) * LANE
    fdim, fp = 4 * nup, 4 * nup_p
    me = _slot(_me())
    geo = _na_geometry(n_lat)

    pad_up = lambda a: jnp.pad(a, [(0, 0)] * (a.ndim - 1) + [(0, nup_p - nup)])
    mixer_weights = (("na_w_qkv", "na_w_o"), ("gm_w_in", "gm_w_out"), ("sc_w_in", "sc_w_out"))

    def shards(i):
        w_in, w_out = (weights[nm][i // N_MIXERS].astype(BF16) for nm in mixer_weights[i % N_MIXERS])
        return [w_in, w_out, pad_up(ffn_w_up[i]).astype(BF16), ffn_w_down[i].astype(BF16)]

    def operands(g_in, g_out, g_up, g_down):
        down = jnp.pad(g_down.reshape(4, nup, d), ((0, 0), (0, nup_p - nup), (0, 0))).reshape(fp, d)
        return dict(w_in=g_in, w_out=g_out.reshape(-1, d), w_up=g_up, w_down=down)

    first_in, *first_rest = shards(0)
    g_in0, g_fcw, g_scw, c_all = all_gather("gather_first", [first_in, pad_up(ffn_conv_w), sc_conv_w, c])
    layer_w = []
    f_cw = [g_fcw[:, i].transpose(1, 0, 2).reshape(3, 2, fp).transpose(1, 0, 2) for i in range(depth)]
    cb_p = pad_up(ffn_conv_b.reshape(depth, N_DEV, nup)).reshape(depth, 2, 1, fp)
    s_cw = [g_scw[:, j].transpose(1, 0, 2).reshape(3, d) for j in range(sc_conv_w.shape[0])]

    cond = jnp.concatenate([c_all.reshape(N_DEV, d), c_ctx[None], jnp.zeros((7, d), F32)])
    mod_cols = mm_nn("ada_fwd", cond, w_ada, a_silu=True)
    (mod_all,) = all_gather("gather_mod", [mod_cols])
    ncol = w_ada.shape[-1]
    mod_all = mod_all.reshape(N_DEV, 16, depth, ncol).transpose(2, 1, 0, 3).reshape(depth, 16, N_MOD * d) + b_ada[:, None, :]
    mod_lat = lax.dynamic_index_in_dim(mod_all, me, axis=1, keepdims=False)
    mods = jnp.stack([mod_lat, mod_all[:, N_DEV]], axis=1).reshape(depth, 2, N_MOD, d)
    sh1, sc1, g1, sh2, sc2, g2 = (mods[:, :, kd] for kd in range(N_MOD))

    xs = jnp.concatenate([x[0], ctx[0]], axis=0)
    saved = []
    prev = None
    for i in range(depth):
        mixer, j = i % N_MIXERS, i // N_MIXERS
        s = {}
        if prev is None:
            s["x"] = xs
            s["h"] = resid_rms_mod(f"l{i}_norm_mix", xs, None, None, norm_mix_g[i:i + 1], sc1[i], sh1[i], n_lat)
        else:
            s["x"], s["h"] = resid_rms_mod(f"l{i}_norm_mix", prev[0], prev[1], prev[2], norm_mix_g[i:i + 1], sc1[i], sh1[i], n_lat)
        if i == 0:
            s["qkv"], (g_out0,) = mm_nn(f"l{i}_qkv", s["h"], g_in0, carry=("gather", first_rest[:1]))
            s["q"], s["k"], s["v"] = qk_norm_fwd(f"l{i}_qk_norm", s["qkv"], na_q_g[j:j + 1], na_k_g[j:j + 1], hd)
            s["bias"] = na_bias_table(f"l{i}_bias", na_rpb[j], geo)
            s["o"], (g_up0, g_down0) = na_attention_fwd(f"l{i}_attn", s["q"], s["k"], s["v"], s["bias"], geo, n_lat, hd,
                                                        carry=("gather", first_rest[1:]))
            layer_w.append(operands(g_in0, g_out0, g_up0, g_down0))
        lw = layer_w[i]
        if i == 0:
            pass
        elif mixer == 0:
            s["qkv"] = mm_nn(f"l{i}_qkv", s["h"], lw["w_in"])
            s["q"], s["k"], s["v"] = qk_norm_fwd(f"l{i}_qk_norm", s["qkv"], na_q_g[j:j + 1], na_k_g[j:j + 1], hd)
            s["bias"] = na_bias_table(f"l{i}_bias", na_rpb[j], geo)
            s["o"] = na_attention_fwd(f"l{i}_attn", s["q"], s["k"], s["v"], s["bias"], geo, n_lat, hd)
        elif mixer == 1:
            s["t"] = mm_nn(f"l{i}_gm_in", s["h"], lw["w_in"])
            s["o"] = gmlp_gate_fwd(f"l{i}_gm_gate", s["t"], gm_v_g[j:j + 1], gm_w_s[j], gm_b_s[j][:, :, None])
        else:
            s["t"] = mm_nn(f"l{i}_sc_in", s["h"], lw["w_in"])
            s["o"] = sc_gate_fwd(f"l{i}_sc_gate", s["t"], s_cw[j], n_lat)
        s["y"] = mm_nn(f"l{i}_mix_out", s["o"], lw["w_out"])
        s["x1"], s["hf"] = resid_rms_mod(f"l{i}_norm_ffn", s["x"], s["y"], g1[i], norm_ffn_g[i:i + 1], sc2[i], sh2[i], n_lat)
        if i + 1 < depth:
            n_in, n_out, n_up, n_down = shards(i + 1)
            s["u"], (g_up,) = mm_nn(f"l{i}_ffn_up", s["hf"], lw["w_up"], out_parts=2, carry=("gather", [n_up]))
            s["a"], (g_down,) = ffn_act_fwd(f"l{i}_ffn_act", s["u"], f_cw[i], cb_p[i], n_lat, carry=("gather", [n_down]))
            s["f"], (g_in, g_out) = mm_nn(f"l{i}_ffn_down", s["a"], lw["w_down"], carry=("gather", [n_in, n_out]))
            layer_w.append(operands(g_in, g_out, g_up, g_down))
        else:
            s["u"] = mm_nn(f"l{i}_ffn_up", s["hf"], lw["w_up"], out_parts=2)
            s["a"] = ffn_act_fwd(f"l{i}_ffn_act", s["u"], f_cw[i], cb_p[i], n_lat)
            s["f"] = mm_nn(f"l{i}_ffn_down", s["a"], lw["w_down"])
        prev = (s["x1"], s["f"], g2[i])
        saved.append(s)

    dx, loss_local, df, dg2 = loss_head("loss_head", prev[0], prev[1], prev[2], loss_target[0])
    loss = lax.psum(loss_local[0, 0], AXES)

    big = {}
    small = {}
    dmod = [None] * depth
    zeros_like_param = lambda p: [None] * p.shape[0]
    for nm in ("na_w_qkv", "na_w_o", "gm_w_in", "gm_w_out", "sc_w_in", "sc_w_out", "ffn_w_up", "ffn_w_down"):
        big[nm] = zeros_like_param(weights[nm])
    for nm in ("norm_mix_g", "norm_ffn_g", "ffn_conv_w", "ffn_conv_b", "na_q_g", "na_k_g", "na_rpb", "gm_v_g", "gm_w_s", "gm_b_s", "sc_conv_w"):
        small[nm] = zeros_like_param(weights[nm])
    landed, pending = {}, []

    def take(wanted):
        keys = [k for k in pending if wanted(k)]
        for k in keys:
            pending.remove(k)
        return keys

    for i in reversed(range(depth)):
        mixer, j = i % N_MIXERS, i // N_MIXERS
        nm_in, nm_out = mixer_weights[mixer]
        s, lw = saved[i], layer_w[i]
        da = mm_nt(f"l{i}_b_ffn_down_x", df, lw["w_down"])
        dwd = mm_tn(f"l{i}_b_ffn_down_w", s["a"], df)
        big["ffn_w_down"][i] = dwd.reshape(4, nup_p, d)[:, :nup].reshape(N_DEV, fdim // N_DEV, d)
        keys = take(lambda k: k[0] == "ffn_w_up" or k[0] in [m[1] for m in mixer_weights])
        if keys:
            du, dcw, dcb, got = ffn_act_bwd(f"l{i}_b_ffn_act", s["u"], da, f_cw[i], cb_p[i], n_lat,
                                            carry=("scatter", [big[nm][idx] for nm, idx in keys]))
            landed.update(zip(keys, got))
        else:
            du, dcw, dcb = ffn_act_bwd(f"l{i}_b_ffn_act", s["u"], da, f_cw[i], cb_p[i], n_lat)
        small["ffn_conv_w"][i] = dcw.transpose(1, 0, 2).reshape(3, N_DEV, nup_p)[:, :, :nup].reshape(3, 2 * fdim)
        small["ffn_conv_b"][i] = dcb.reshape(N_DEV, nup_p)[:, :nup].reshape(2 * fdim)
        keys = [("ffn_w_down", i)] + take(lambda k: True)
        dhf, got = mm_nt(f"l{i}_b_ffn_up_x", du, lw["w_up"], carry=("scatter", [big[nm][idx] for nm, idx in keys]))
        landed.update(zip(keys, got))
        big["ffn_w_up"][i] = mm_tn(f"l{i}_b_ffn_up_w", s["hf"], du, out_parts=N_DEV)
        pending.append(("ffn_w_up", i))
        dx1, dsh2, dsc2, dgf, dy, dg1 = rms_mod_bwd(f"l{i}_b_norm_ffn", s["x1"], dhf, norm_ffn_g[i:i + 1], sc2[i], dx, n_lat,
                                                    branch=(s["y"], g1[i]))
        small["norm_ffn_g"][i] = dgf[0]
        do = mm_nt(f"l{i}_b_mix_out_x", dy, lw["w_out"], out_dtype=BF16 if mixer == 0 else F32)
        big[nm_out][j] = mm_tn(f"l{i}_b_mix_out_w", s["o"], dy).reshape(N_DEV, -1, d)
        pending.append((nm_out, j))
        if mixer == 0:
            keys = take(lambda k: True)
            dq, dk, dv, dbias, got = na_attention_bwd(f"l{i}_b_attn", s["q"], s["k"], s["v"], s["bias"], do, geo, n_lat, hd,
                                                      carry=("scatter", [big[nm][idx] for nm, idx in keys]))
            landed.update(zip(keys, got))
            small["na_rpb"][j] = na_rpb_grad(f"l{i}_b_rpb", dbias, geo, n_heads)
            dt, dqg, dkg = qk_norm_bwd(f"l{i}_b_qk_norm", s["qkv"], dq, dk, dv, na_q_g[j:j + 1], na_k_g[j:j + 1], hd)
            small["na_q_g"][j], small["na_k_g"][j] = dqg[0], dkg[0]
        elif mixer == 1:
            dt, dvg, dws, dsv = gmlp_gate_bwd(f"l{i}_b_gm_gate", s["t"], do, gm_v_g[j:j + 1], gm_w_s[j], gm_b_s[j][:, :, None])
            groups, width = gm_w_s.shape[1], dsv.shape[1]
            group_of = np.zeros((width, LANE), np.float32)
            group_of[np.arange(width), np.arange(width) // (width // groups)] = 1.0
            dbs = _mm(f"l{i}_b_gm_bs", dsv, jnp.asarray(group_of), "nn", (GM_CHUNK, LANE), F32,
                      (GM_CHUNK, LANE, _tile(width, 2048, LANE)), exact=True)[:, :groups].T
            small["gm_v_g"][j], small["gm_w_s"][j], small["gm_b_s"][j] = dvg[0], dws, dbs
        else:
            dt, dscw = sc_gate_bwd(f"l{i}_b_sc_gate", s["t"], do, s_cw[j], n_lat)
            small["sc_conv_w"][j] = dscw
            dt = dt.transpose(1, 0, 2).reshape(t, 3 * d)
        dh = mm_nt(f"l{i}_b_mix_in_x", dt, lw["w_in"])
        big[nm_in][j] = mm_tn(f"l{i}_b_mix_in_w", s["h"], dt, out_parts=N_DEV)
        pending.append((nm_in, j))
        dmod_ffn = [dsh2, dsc2, dg2]
        if i > 0:
            dx, dsh1, dsc1, dgm, df, dg2 = rms_mod_bwd(f"l{i}_b_norm_mix", s["x"], dh, norm_mix_g[i:i + 1], sc1[i], dx1, n_lat,
                                                       branch=(saved[i - 1]["f"], g2[i - 1]))
        else:
            dx, dsh1, dsc1, dgm = rms_mod_bwd(f"l{i}_b_norm_mix", s["x"], dh, norm_mix_g[i:i + 1], sc1[i], dx1, n_lat)
        small["norm_mix_g"][i] = dgm[0]
        dmod[i] = jnp.stack([dsh1, dsc1, dg1] + dmod_ffn, axis=1)
    grad_x = dx[:n_lat][None]

    small = {nm: jnp.stack(v) for nm, v in small.items()}
    small["dmod"] = jnp.stack(dmod)
    pack = _Pack({nm: v.shape for nm, v in small.items()})
    (small_all,) = all_gather("gather_small", [pack.pack(small)])
    small_sum = pack.unpack(sum_parts("sum_small", small_all))
    dmod_all = pack.unpack(small_all, lead=(N_DEV,))["dmod"]
    dmod_ctx = small_sum["dmod"][:, 1].reshape(depth, N_MOD * d)
    grads = {nm: small_sum[nm] for nm in small if nm != "dmod"}
    grads["b_ada"] = (small_sum["dmod"][:, 0] + small_sum["dmod"][:, 1]).reshape(depth, N_MOD * d)
    my_cols = lambda a, width: lax.dynamic_slice_in_dim(a, me * width, width, axis=-1)
    grads["ffn_conv_w"] = my_cols(grads["ffn_conv_w"], nup)
    grads["sc_conv_w"] = my_cols(grads["sc_conv_w"], sc_conv_w.shape[-1])

    drows = jnp.concatenate([dmod_all[:, :, 0], dmod_all[:, :, 1]]).reshape(2 * N_DEV, depth, N_MOD * d)
    drows = my_cols(drows, ncol).reshape(2 * N_DEV, depth * ncol)
    cond2 = jnp.concatenate([c_all.reshape(N_DEV, d), jnp.broadcast_to(c_ctx[None], (N_DEV, d))])
    g_w_ada = mm_tn("ada_bwd_w", cond2, drows, out_parts=depth, out_dtype=F32, a_silu=True)
    dctx_rows = jnp.pad(my_cols(dmod_ctx, ncol).reshape(1, depth * ncol), ((0, 15), (0, 0)))
    dcc = mm_nt("ada_bwd_c", dctx_rows, w_ada)[0:1]
    (dcc_all,) = all_gather("gather_c_ctx", [jnp.pad(dcc.reshape(-1, LANE), ((0, (-d // LANE) % 8), (0, 0)))])
    dcc_sum = sum_parts("sum_c_ctx", dcc_all).reshape(-1)[:d]
    sig = jax.nn.sigmoid(c_ctx)
    grads["c_ctx"] = dcc_sum * (sig * (1.0 + c_ctx * (1.0 - sig)))

    landed.update(zip(pending, all_to_all("scatter_last", [big[nm][idx] for nm, idx in pending])))
    out = {nm: [None] * 4 for nm in names}
    per_big = {nm: [] for nm in big}
    for nm, idx in [(nm, idx) for nm in big for idx in range(len(big[nm]))]:
        w2, p = weights[nm][idx], landed[nm, idx]
        per_big[nm].append(adamw(f"adamw_{nm}{idx}", p.reshape(N_DEV, w2.shape[0], -1), w2, mom_m[nm][idx], mom_v[nm][idx]))
    for nm, res in per_big.items():
        out[nm] = [jnp.stack([r[q] for r in res]) for q in range(4)]
    res = [adamw(f"adamw_w_ada{i}", g_w_ada[i][None], w_ada[i], m_w_ada[i], v_w_ada[i]) for i in range(depth)]
    out["w_ada"] = [jnp.stack([r[q] for r in res]) for q in range(4)]
    small_names = [nm for nm in names if nm not in big and nm != "w_ada"]
    spack = _Pack({nm: weights[nm].shape for nm in small_names})
    flat = [spack.pack({nm: src[nm] for nm in small_names}) for src in (grads, weights, mom_m, mom_v)]
    res = adamw("adamw_small", flat[0][None], flat[1], flat[2], flat[3])
    res = [spack.unpack(r) for r in res]
    for nm in small_names:
        out[nm] = [grads[nm].reshape(weights[nm].shape)] + [res[q][nm] for q in range(1, 4)]

    return (loss, grad_x, *[out[nm][0] for nm in names], *[out[nm][1] for nm in names],
            *[out[nm][2] for nm in names], *[out[nm][3] for nm in names])
```

```python
import functools
import math

import numpy as np
import jax
import jax.numpy as jnp
from jax import lax
from jax.experimental import pallas as pl
from jax.experimental.pallas import tpu as pltpu

F32 = jnp.float32
BF16 = jnp.bfloat16
MESH = pl.DeviceIdType.MESH
AXES = ("x", "y", "c")
N_DEV = 8
N_MOD = 6
N_MIXERS = 3
EPS = 1e-6
GRID_W = 64
WIN_H = 8
WIN_W = 16
QROWS = 2
GM_CHUNK = 128
LANE = 128
NEG = -1e30
ADAM_LR = 0.001
ADAM_B1 = 0.9
ADAM_B2 = 0.999
ADAM_EPS = 1e-08
ADAM_WD = 0.01
ADAM_STEP = 10
VMEM_LIMIT = 56 * 1024 * 1024
HBM = pl.BlockSpec(memory_space=pltpu.HBM)

NN = (((1,), (0,)), ((), ()))
NT = (((1,), (1,)), ((), ()))
TN = (((0,), (0,)), ((), ()))


def _params(*sem):
    return pltpu.CompilerParams(dimension_semantics=sem, vmem_limit_bytes=VMEM_LIMIT)


def _tile(n, pref, mult):
    best = None
    for t in range(mult, min(n, pref) + 1, mult):
        if n % t == 0:
            best = t
    return n if best is None else best


def _full(arr):
    nd = arr.ndim
    return pl.BlockSpec(arr.shape, lambda *g: (0,) * nd)


def _logical(shape):
    return tuple(shape) if len(shape) == 2 else (shape[1], shape[0] * shape[2])


def _cspec(shape, tr, tc, rc):
    if len(shape) == 2:
        return pl.BlockSpec((tr, tc), rc)
    cpp = shape[2] // tc

    def imap(*g):
        r, c = rc(*g)
        return (c // cpp, r, c % cpp)

    return pl.BlockSpec((None, tr, tc), imap)


def _dot(a, b, dims, exact=False):
    if exact:
        return lax.dot_general(a, b, dims, precision=lax.Precision.HIGHEST, preferred_element_type=F32)
    return lax.dot_general(a.astype(BF16), b.astype(BF16), dims, preferred_element_type=F32)


def _silu(z):
    return z * jax.nn.sigmoid(z)


def _me():
    return lax.axis_index("x"), lax.axis_index("y"), lax.axis_index("c")


def _slot(p):
    return 4 * p[0] + 2 * p[1] + p[2]


def _scatter_direct(srcs, dsts, send_sems, recv_sems, local_sems):
    x, y, c = _me()
    me = (x, y, c)
    peers = [((x + (k >> 2)) % 2, (y + ((k >> 1) & 1)) % 2, (c + (k & 1)) % 2) for k in range(1, N_DEV)]

    def remote(a, k, peer, src_dev, dst_dev):
        return pltpu.make_async_remote_copy(src_ref=srcs[a].at[_slot(src_dev)], dst_ref=dsts[a].at[_slot(dst_dev)], send_sem=send_sems.at[a, k],
                                            recv_sem=recv_sems.at[a, k], device_id=peer, device_id_type=MESH)

    def local(a):
        return pltpu.make_async_copy(srcs[a].at[_slot(me)], dsts[a].at[_slot(me)], local_sems.at[a])

    def start():
        for a in range(len(srcs)):
            local(a).start()
            for k, peer in enumerate(peers):
                remote(a, k, peer, peer, me).start()

    def wait():
        for a in range(len(srcs)):
            for k, peer in enumerate(peers):
                remote(a, k, peer, me, peer).wait_recv()
        for a in range(len(srcs)):
            for k, peer in enumerate(peers):
                remote(a, k, peer, peer, me).wait_send()
            local(a).wait()

    return start, lambda: None, wait


def _gather_two_level(srcs, dsts, send_sems, recv_sems, local_sems):
    x, y, c = _me()
    me, sib = (x, y, c), (x, y, 1 - c)
    chips = [(1 - x, y), (x, 1 - y), (1 - x, 1 - y)]

    def copy(a, k, block, to, src=None):
        dst = dsts[a].at[_slot(block)]
        return pltpu.make_async_remote_copy(src_ref=dst if src is None else src, dst_ref=dst, send_sem=send_sems.at[a, k],
                                            recv_sem=recv_sems.at[a, k], device_id=to, device_id_type=MESH)

    def mine(a):
        return pltpu.make_async_copy(srcs[a], dsts[a].at[_slot(me)], local_sems.at[a])

    def first(a):
        return [copy(a, 0, me, sib, src=srcs[a])] + [copy(a, 1 + j, me, (*chip, c), src=srcs[a]) for j, chip in enumerate(chips)]

    def passed(a):
        return [copy(a, 4 + j, (*chip, c), sib) for j, chip in enumerate(chips)]

    def start():
        for a in range(len(srcs)):
            mine(a).start()
            for cp in first(a):
                cp.start()

    def pass_on():
        for j, chip in enumerate(chips):
            for a in range(len(srcs)):
                copy(a, 1 + j, (*chip, c), me).wait_recv()
                passed(a)[j].start()

    def wait():
        for a in range(len(srcs)):
            copy(a, 0, sib, me).wait_recv()
            for j, chip in enumerate(chips):
                copy(a, 4 + j, (*chip, 1 - c), me).wait_recv()
        for a in range(len(srcs)):
            for cp in first(a) + passed(a):
                cp.wait_send()
            mine(a).wait()

    return start, pass_on, wait


def _exchange_scratch(n):
    return [pltpu.SemaphoreType.DMA((n, N_DEV - 1)), pltpu.SemaphoreType.DMA((n, N_DEV - 1)), pltpu.SemaphoreType.DMA((n,))]


def _landing(kind, arrays):
    return [jax.ShapeDtypeStruct((N_DEV, *a.shape) if kind == "gather" else a.shape, a.dtype) for a in arrays]


def _call(body, name, grid, in_specs, out_specs, out_shape, ins, sem, scratch=(), carry=None):
    in_specs, out_specs, out_shape, scratch = list(in_specs), list(out_specs), list(out_shape), list(scratch)
    if carry is None:
        return list(pl.pallas_call(body, name=name, grid=grid, in_specs=in_specs, out_specs=out_specs, out_shape=out_shape,
                                   scratch_shapes=scratch, compiler_params=_params(*sem))(*ins))
    kind, arrays = carry
    n, ni, no, ns = len(arrays), len(in_specs), len(out_specs), len(scratch)

    def carrying(*refs):
        own_in, srcs = refs[:ni], refs[ni:ni + n]
        own_out, dsts = refs[ni + n:ni + n + no], refs[ni + n + no:ni + 2 * n + no]
        own_scratch, sems = refs[ni + 2 * n + no:ni + 2 * n + no + ns], refs[ni + 2 * n + no + ns:]
        start, pass_on, wait = (_gather_two_level if kind == "gather" else _scatter_direct)(srcs, dsts, *sems)
        step = functools.reduce(lambda lin, ax: lin * grid[ax] + pl.program_id(ax), range(len(grid)), 0)
        steps = math.prod(grid)
        pl.when(step == 0)(start)
        body(*own_in, *own_out, *own_scratch)
        if kind == "gather":
            pl.when(step == steps // 2)(pass_on)
        pl.when(step == steps - 1)(wait)

    res = pl.pallas_call(
        carrying, name=name, grid=grid, in_specs=in_specs + [HBM] * n, out_specs=out_specs + [HBM] * n,
        out_shape=out_shape + _landing(kind, arrays), scratch_shapes=scratch + _exchange_scratch(n),
        compiler_params=_params(*(["arbitrary"] * len(grid))))(*ins, *arrays)
    return list(res[:no]), list(res[no:])


def _mm(name, a, b, kind, out_shape, out_dtype, tiles, a_silu=False, exact=False, carry=None):
    la, lb, lo = _logical(a.shape), _logical(b.shape), _logical(out_shape)
    t0, t1, t2 = tiles
    if kind == "nn":
        grid = (lo[1] // t1, lo[0] // t0, la[1] // t2)
        a_spec = _cspec(a.shape, t0, t2, lambda j, i, k: (i, k))
        b_spec = _cspec(b.shape, t2, t1, lambda j, i, k: (k, j))
        o_spec = _cspec(out_shape, t0, t1, lambda j, i, k: (i, j))
        dims, acc = NN, (t0, t1)
    elif kind == "nt":
        grid = (lo[1] // t1, lo[0] // t0, la[1] // t2)
        a_spec = _cspec(a.shape, t0, t2, lambda p, i, r: (i, r))
        b_spec = _cspec(b.shape, t1, t2, lambda p, i, r: (p, r))
        o_spec = _cspec(out_shape, t0, t1, lambda p, i, r: (i, p))
        dims, acc = NT, (t0, t1)
    else:
        grid = (lo[1] // t1, lo[0] // t0, la[0] // t2)
        a_spec = _cspec(a.shape, t2, t0, lambda j, kk, r: (r, kk))
        b_spec = _cspec(b.shape, t2, t1, lambda j, kk, r: (r, j))
        o_spec = _cspec(out_shape, t0, t1, lambda j, kk, r: (kk, j))
        dims, acc = TN, (t0, t1)
    nk = grid[2]
    in_place = out_dtype == F32

    def body(a_ref, b_ref, o_ref, *scratch):
        acc_ref = o_ref if in_place else scratch[0]
        k = pl.program_id(2)
        av = a_ref[...]
        if a_silu:
            av = _silu(av)
        part = _dot(av, b_ref[...], dims, exact)

        @pl.when(k == 0)
        def _():
            acc_ref[...] = part

        @pl.when(k > 0)
        def _():
            acc_ref[...] += part

        if not in_place:
            @pl.when(k == nk - 1)
            def _():
                o_ref[...] = acc_ref[...].astype(o_ref.dtype)

    res = _call(body, name, grid, [a_spec, b_spec], [o_spec], [jax.ShapeDtypeStruct(out_shape, out_dtype)], (a, b),
                ("parallel", "parallel", "arbitrary"), scratch=[] if in_place else [pltpu.VMEM(acc, F32)], carry=carry)
    return res[0] if carry is None else (res[0][0], res[1])


MM_VMEM_BUDGET = 40 * 1024 * 1024
MM_TILE_CAP = 2048


def _divisors(n, mult):
    return [t for t in range(mult, min(n, MM_TILE_CAP) + 1, mult) if n % t == 0] or [n]


def _mm_tiles(c0, c1, c2, a, b, out_dtype):
    ia, ib, io = a.dtype.itemsize, b.dtype.itemsize, jnp.dtype(out_dtype).itemsize
    best, best_score = None, -1
    for t0 in c0:
        for t1 in c1:
            for t2 in c2:
                need = 2 * (t0 * t2 * ia + t1 * t2 * ib + t0 * t1 * io) + t0 * t1 * 4 * (1 if out_dtype == F32 else 2)
                score = (t0 * t1 * t2, t2)
                if need <= MM_VMEM_BUDGET and score > (best_score if best else (-1, -1)):
                    best, best_score = (t0, t1, t2), score
    return best if best else (c0[0], c1[0], c2[0])


def mm_nn(name, a, w, out_parts=1, out_dtype=F32, **kw):
    (m, _), (_, n) = _logical(a.shape), _logical(w.shape)
    out_shape = (m, n) if out_parts == 1 else (out_parts, m, n // out_parts)
    tiles = _mm_tiles(_divisors(m, 16), _divisors(math.gcd(w.shape[-1], out_shape[-1]), LANE),
                      _divisors(math.gcd(a.shape[-1], w.shape[-2]), LANE), a, w, out_dtype)
    return _mm(name, a, w, "nn", out_shape, out_dtype, tiles, **kw)


def mm_nt(name, a, w, out_dtype=F32, **kw):
    (m, _), (p, _) = _logical(a.shape), _logical(w.shape)
    tiles = _mm_tiles(_divisors(m, 16), _divisors(w.shape[-2], LANE), _divisors(math.gcd(a.shape[-1], w.shape[-1]), LANE), a, w, out_dtype)
    return _mm(name, a, w, "nt", (m, p), out_dtype, tiles, **kw)


def mm_tn(name, a, dy, out_parts=1, out_dtype=BF16, **kw):
    (t, k), (_, n) = _logical(a.shape), _logical(dy.shape)
    out_shape = (k, n) if out_parts == 1 else (out_parts, k, n // out_parts)
    tiles = _mm_tiles(_divisors(a.shape[-1], LANE), _divisors(math.gcd(dy.shape[-1], out_shape[-1]), LANE), _divisors(t, 16), a, dy, out_dtype)
    return _mm(name, a, dy, "tn", out_shape, out_dtype, tiles, **kw)


def _rows(tr, d):
    return pl.BlockSpec((tr, d), lambda i: (i, 0))


def _pick(ctx, ref):
    return jnp.where(ctx, ref[1:2, :], ref[0:1, :])


def resid_rms_mod(name, x, y, gate, g, sc, sh, n_lat, tr=256):
    t, d = x.shape
    nlt = n_lat // tr
    has_res = y is not None

    def body(*refs):
        if has_res:
            x_ref, y_ref, gate_ref, g_ref, sc_ref, sh_ref, x1_ref, h_ref = refs
        else:
            x_ref, g_ref, sc_ref, sh_ref, h_ref = refs
        ctx = pl.program_id(0) >= nlt
        xv = x_ref[...]
        if has_res:
            xv = xv + _pick(ctx, gate_ref) * y_ref[...]
            x1_ref[...] = xv
        r = lax.rsqrt(jnp.mean(xv * xv, axis=-1, keepdims=True) + EPS)
        n = xv * r * g_ref[...]
        h_ref[...] = (n * (1.0 + _pick(ctx, sc_ref)) + _pick(ctx, sh_ref)).astype(h_ref.dtype)

    row = _rows(tr, d)
    if has_res:
        ins, in_specs = (x, y, gate, g, sc, sh), [row, row, _full(gate), _full(g), _full(sc), _full(sh)]
        out_shape = (jax.ShapeDtypeStruct((t, d), F32), jax.ShapeDtypeStruct((t, d), BF16))
        out_specs = (row, row)
    else:
        ins, in_specs = (x, g, sc, sh), [row, _full(g), _full(sc), _full(sh)]
        out_shape = jax.ShapeDtypeStruct((t, d), BF16)
        out_specs = row
    return pl.pallas_call(body, name=name, grid=(t // tr,), in_specs=in_specs, out_specs=out_specs,
                          out_shape=out_shape, compiler_params=_params("parallel"))(*ins)


def _acc_rows(i, nlt, ref, val):
    @pl.when(i == 0)
    def _():
        ref[...] = jnp.zeros_like(ref)

    @pl.when(i < nlt)
    def _():
        ref[0:1, :] += val

    @pl.when(i >= nlt)
    def _():
        ref[1:2, :] += val


def rms_mod_bwd(name, x, dh, g, sc, dres, n_lat, branch=None, tr=256):
    t, d = x.shape
    nlt = n_lat // tr

    def body(x_ref, dh_ref, g_ref, sc_ref, dres_ref, *rest):
        if branch is None:
            dx_ref, dsh_ref, dsc_ref, dg_ref = rest
        else:
            y_ref, gate_ref, dx_ref, dsh_ref, dsc_ref, dg_ref, dy_ref, dgate_ref = rest
        i = pl.program_id(0)
        xv, dhv, gv = x_ref[...], dh_ref[...], g_ref[...]
        r = lax.rsqrt(jnp.mean(xv * xv, axis=-1, keepdims=True) + EPS)
        xhat = xv * r
        dn = dhv * (1.0 + _pick(i >= nlt, sc_ref))
        dxhat = dn * gv
        dxv = r * (dxhat - xhat * jnp.mean(dxhat * xhat, axis=-1, keepdims=True)) + dres_ref[...]
        dx_ref[...] = dxv
        if branch is not None:
            dy_ref[...] = (_pick(i >= nlt, gate_ref) * dxv).astype(dy_ref.dtype)
            _acc_rows(i, nlt, dgate_ref, jnp.sum(dxv * y_ref[...], axis=0, keepdims=True))
        _acc_rows(i, nlt, dsh_ref, jnp.sum(dhv, axis=0, keepdims=True))
        _acc_rows(i, nlt, dsc_ref, jnp.sum(dhv * (xhat * gv), axis=0, keepdims=True))
        dgp = jnp.sum(dn * xhat, axis=0, keepdims=True)

        @pl.when(i == 0)
        def _():
            dg_ref[...] = dgp

        @pl.when(i > 0)
        def _():
            dg_ref[...] += dgp

    row = _rows(tr, d)
    two = pl.BlockSpec((2, d), lambda i: (0, 0))
    two_shape = jax.ShapeDtypeStruct((2, d), F32)
    ins, in_specs = [x, dh, g, sc, dres], [row, row, _full(g), _full(sc), row]
    out_specs = [row, two, two, pl.BlockSpec((1, d), lambda i: (0, 0))]
    out_shape = [jax.ShapeDtypeStruct((t, d), F32), two_shape, two_shape, jax.ShapeDtypeStruct((1, d), F32)]
    if branch is not None:
        ins, in_specs = ins + list(branch), in_specs + [row, _full(branch[1])]
        out_specs, out_shape = out_specs + [row, two], out_shape + [jax.ShapeDtypeStruct((t, d), BF16), two_shape]
    return pl.pallas_call(body, name=name, grid=(t // tr,), in_specs=in_specs, out_specs=out_specs, out_shape=out_shape,
                          compiler_params=_params("arbitrary"))(*ins)


def loss_head(name, x1, f, gate, target, tr=256):
    t, d = x1.shape
    nlt = target.shape[0] // tr

    def body(x_ref, f_ref, gate_ref, t_ref, dx_ref, loss_ref, df_ref, dgate_ref, acc_ref):
        i = pl.program_id(0)

        @pl.when(i == 0)
        def _():
            acc_ref[...] = jnp.zeros_like(acc_ref)
            dgate_ref[...] = jnp.zeros_like(dgate_ref)

        @pl.when(i < nlt)
        def _():
            fv, gv = f_ref[...], gate_ref[0:1, :]
            e = x_ref[...] + gv * fv - t_ref[...]
            dxv = e / d
            dx_ref[...] = dxv
            df_ref[...] = (gv * dxv).astype(df_ref.dtype)
            dgate_ref[0:1, :] += jnp.sum(dxv * fv, axis=0, keepdims=True)
            acc_ref[...] += jnp.sum(e * e, axis=0, keepdims=True)

        @pl.when(i >= nlt)
        def _():
            dx_ref[...] = jnp.zeros_like(dx_ref)
            df_ref[...] = jnp.zeros_like(df_ref)

        @pl.when(i == t // tr - 1)
        def _():
            loss_ref[...] = jnp.sum(acc_ref[...], axis=1, keepdims=True) * (0.5 / d)

    row = _rows(tr, d)
    return pl.pallas_call(
        body, name=name, grid=(t // tr,),
        in_specs=[row, row, _full(gate), pl.BlockSpec((tr, d), lambda i: (jnp.minimum(i, nlt - 1), 0))],
        out_specs=(row, pl.BlockSpec((1, 1), lambda i: (0, 0)), row, pl.BlockSpec((2, d), lambda i: (0, 0))),
        out_shape=(jax.ShapeDtypeStruct((t, d), F32), jax.ShapeDtypeStruct((1, 1), F32),
                   jax.ShapeDtypeStruct((t, d), BF16), jax.ShapeDtypeStruct((2, d), F32)),
        scratch_shapes=[pltpu.VMEM((1, d), F32)],
        compiler_params=_params("arbitrary"))(x1, f, gate, target)


HALO = 8


def _halo_specs(shape, tr, tc, col):
    hb = tr // HALO
    last = _logical(shape)[0] // HALO - 1
    main = _cspec(shape, tr, tc, lambda j, i: (i, col(j)))
    prev = _cspec(shape, HALO, tc, lambda j, i: (jnp.maximum(i * hb - 1, 0), col(j)))
    nxt = _cspec(shape, HALO, tc, lambda j, i: (jnp.minimum((i + 1) * hb, last), col(j)))
    return [prev, main, nxt]


def _seq_edges(i, nlt, nt):
    first = (i == 0) | (i == nlt)
    last = (i == nlt - 1) | (i == nt - 1)
    return first, last


def _ext(prev_ref, main_ref, next_ref, first, last):
    p = jnp.where(first, 0.0, prev_ref[...].astype(F32))
    n = jnp.where(last, 0.0, next_ref[...].astype(F32))
    return jnp.concatenate([p, main_ref[...].astype(F32), n], axis=0)


def _up(e):
    return pltpu.roll(e, 1, 0)


def _down(e):
    return pltpu.roll(e, e.shape[0] - 1, 0)


def _conv(e, w):
    return _up(e) * w[0:1, :] + e * w[1:2, :] + _down(e) * w[2:3, :]


def _conv_t(e, w):
    return _down(e) * w[0:1, :] + e * w[1:2, :] + _up(e) * w[2:3, :]


def _mid(e, tr):
    return e[HALO:HALO + tr, :]


def _acc_cols(i, ref, val):
    @pl.when(i == 0)
    def _():
        ref[...] = val

    @pl.when(i > 0)
    def _():
        ref[...] += val


def _colsum(v):
    return jnp.sum(v, axis=0, keepdims=True)


def ffn_act_fwd(name, u, cw, cb, n_lat, tr=256, carry=None):
    _, t, fp = u.shape
    tc = _tile(fp, 1536, LANE)
    nlt, nt = n_lat // tr, t // tr

    def body(pg, mg, ng, pu, mu, nu, cw_ref, cb_ref, z_ref, a_ref):
        first, last = _seq_edges(pl.program_id(1), nlt, nt)
        zg = _mid(_conv(_ext(pg, mg, ng, first, last), cw_ref[0]), tr) + cb_ref[0]
        zu = _mid(_conv(_ext(pu, mu, nu, first, last), cw_ref[1]), tr) + cb_ref[1]
        z_ref[0] = zg
        z_ref[1] = zu
        a_ref[...] = (_silu(zg) * zu).astype(a_ref.dtype)

    ncol = fp // tc
    specs = _halo_specs(u.shape, tr, tc, lambda j: j) + _halo_specs(u.shape, tr, tc, lambda j: j + ncol)
    specs += [pl.BlockSpec((2, 3, tc), lambda j, i: (0, 0, j)), pl.BlockSpec((2, 1, tc), lambda j, i: (0, 0, j))]
    res = _call(body, name, (ncol, nt), specs,
                [pl.BlockSpec((2, tr, tc), lambda j, i: (0, i, j)), pl.BlockSpec((tr, tc), lambda j, i: (i, j))],
                [jax.ShapeDtypeStruct((2, t, fp), F32), jax.ShapeDtypeStruct((t, fp), BF16)],
                (u, u, u, u, u, u, cw, cb), ("parallel", "parallel"), carry=carry)
    return tuple(res) if carry is None else (*res[0], res[1])


def ffn_act_bwd(name, z, u, da, cw, n_lat, tr=128, carry=None):
    _, t, fp = u.shape
    tc = _tile(fp, 1536, LANE)
    nlt, nt = n_lat // tr, t // tr
    ncol = fp // tc

    def body(pg, mg, ng, pu, mu, nu, u_ref, pa, ma, na, cw_ref, du_ref, dcw_ref, dcb_ref):
        i = pl.program_id(1)
        first, last = _seq_edges(i, nlt, nt)
        zg, zu = _ext(pg, mg, ng, first, last), _ext(pu, mu, nu, first, last)
        dae = _ext(pa, ma, na, first, last)
        sg = jax.nn.sigmoid(zg)
        dzs = (dae * zu * (sg * (1.0 + zg * (1.0 - sg))), dae * (zg * sg))
        for h, dz in enumerate(dzs):
            w, um = cw_ref[h], u_ref[h]
            after, before = _down(dz), _up(dz)
            du_ref[h] = _mid(after * w[0:1, :] + dz * w[1:2, :] + before * w[2:3, :], tr).astype(du_ref.dtype)
            rows = [_colsum(_mid(after, tr) * um), _colsum(_mid(dz, tr) * um), _colsum(_mid(before, tr) * um)]
            _acc_cols(i, dcw_ref.at[h], jnp.concatenate(rows, axis=0))
            _acc_cols(i, dcb_ref.at[h], _colsum(_mid(dz, tr)))

    both = pl.BlockSpec((2, tr, tc), lambda j, i: (0, i, j))
    taps = pl.BlockSpec((2, 3, tc), lambda j, i: (0, 0, j))
    specs = _halo_specs(z.shape, tr, tc, lambda j: j) + _halo_specs(z.shape, tr, tc, lambda j: j + ncol) + [both]
    specs += _halo_specs(da.shape, tr, tc, lambda j: j) + [taps]
    res = _call(body, name, (ncol, nt), specs, [both, taps, pl.BlockSpec((2, 1, tc), lambda j, i: (0, 0, j))],
                [jax.ShapeDtypeStruct((2, t, fp), BF16), jax.ShapeDtypeStruct((2, 3, fp), F32), jax.ShapeDtypeStruct((2, 1, fp), F32)],
                (z, z, z, z, z, z, u, da, da, da, cw), ("parallel", "arbitrary"), carry=carry)
    return tuple(res) if carry is None else (*res[0], res[1])


def sc_gate_fwd(name, tmat, cw, n_lat, tr=256):
    t, d3 = tmat.shape
    d = d3 // 3
    tc = _tile(d, 512, LANE)
    ncol = d // tc
    nlt, nt = n_lat // tr, t // tr

    def body(b_ref, pc, mc, nc, px, mx, nx, cw_ref, s_ref):
        first, last = _seq_edges(pl.program_id(1), nlt, nt)
        p = _ext(pc, mc, nc, first, last) * _ext(px, mx, nx, first, last)
        s_ref[...] = (b_ref[...] * _mid(_conv(p, cw_ref[...]), tr)).astype(s_ref.dtype)

    specs = [pl.BlockSpec((tr, tc), lambda j, i: (i, j))]
    specs += _halo_specs(tmat.shape, tr, tc, lambda j: j + ncol) + _halo_specs(tmat.shape, tr, tc, lambda j: j + 2 * ncol)
    specs += [pl.BlockSpec((3, tc), lambda j, i: (0, j))]
    return pl.pallas_call(
        body, name=name, grid=(ncol, nt), in_specs=specs, out_specs=pl.BlockSpec((tr, tc), lambda j, i: (i, j)),
        out_shape=jax.ShapeDtypeStruct((t, d), BF16),
        compiler_params=_params("parallel", "parallel"))(*([tmat] * 7), cw)


def sc_gate_bwd(name, tmat, ds, cw, n_lat, tr=128):
    t, d3 = tmat.shape
    d = d3 // 3
    tc = _tile(d, 512, LANE)
    ncol = d // tc
    nlt, nt = n_lat // tr, t // tr

    def body(pb, mb, nb, pc, mc, nc, px, mx, nx, pd, md, nd, cw_ref, dt_ref, dcw_ref):
        i = pl.program_id(1)
        first, last = _seq_edges(i, nlt, nt)
        w = cw_ref[...]
        be, ce, xe = _ext(pb, mb, nb, first, last), _ext(pc, mc, nc, first, last), _ext(px, mx, nx, first, last)
        dse = _ext(pd, md, nd, first, last)
        p = ce * xe
        dcv = dse * be
        dp = _conv_t(dcv, w)
        dt_ref[0] = _mid(dse * _conv(p, w), tr).astype(dt_ref.dtype)
        dt_ref[1] = _mid(dp * xe, tr).astype(dt_ref.dtype)
        dt_ref[2] = _mid(dp * ce, tr).astype(dt_ref.dtype)
        dm = _mid(dcv, tr)
        rows = [_colsum(dm * _mid(_up(p), tr)), _colsum(dm * _mid(p, tr)), _colsum(dm * _mid(_down(p), tr))]
        _acc_cols(i, dcw_ref, jnp.concatenate(rows, axis=0))

    specs = []
    for part in range(3):
        specs += _halo_specs(tmat.shape, tr, tc, functools.partial(lambda j, part: j + part * ncol, part=part))
    specs += _halo_specs(ds.shape, tr, tc, lambda j: j)
    specs += [pl.BlockSpec((3, tc), lambda j, i: (0, j))]
    return pl.pallas_call(
        body, name=name, grid=(ncol, nt), in_specs=specs,
        out_specs=(pl.BlockSpec((3, tr, tc), lambda j, i: (0, i, j)), pl.BlockSpec((3, tc), lambda j, i: (0, j))),
        out_shape=(jax.ShapeDtypeStruct((3, t, d), BF16), jax.ShapeDtypeStruct((3, d), F32)),
        compiler_params=_params("parallel", "arbitrary"))(*([tmat] * 9), ds, ds, ds, cw)


_GELU_K = 0.7978845608028654
_GELU_C = 0.044715


def _gelu(x):
    return 0.5 * x * (1.0 + jnp.tanh(_GELU_K * (x + _GELU_C * (x * x * x))))


def _gelu_grad(x):
    th = jnp.tanh(_GELU_K * (x + _GELU_C * (x * x * x)))
    return 0.5 * (1.0 + th) + 0.5 * x * (1.0 - th * th) * (_GELU_K * (1.0 + 3.0 * _GELU_C * (x * x)))


def gmlp_gate_fwd(name, tmat, vg, ws, bs):
    t, w2 = tmat.shape
    w = w2 // 2
    groups = ws.shape[0]
    gd = w // groups

    def body(t_ref, vg_ref, ws_ref, bs_ref, o_ref):
        v = _gelu(t_ref[:, w:])
        r = lax.rsqrt(jnp.mean(v * v, axis=-1, keepdims=True) + EPS)
        vn = (v * r * vg_ref[...]).astype(BF16)
        for g in range(groups):
            cols = slice(g * gd, (g + 1) * gd)
            sv = _dot(ws_ref[g], vn[:, cols], NN) + bs_ref[g]
            o_ref[:, cols] = (_gelu(t_ref[:, cols]) * sv).astype(o_ref.dtype)

    return pl.pallas_call(
        body, name=name, grid=(t // GM_CHUNK,),
        in_specs=[_rows(GM_CHUNK, w2), _full(vg), _full(ws), _full(bs)], out_specs=_rows(GM_CHUNK, w),
        out_shape=jax.ShapeDtypeStruct((t, w), BF16), compiler_params=_params("parallel"))(tmat, vg, ws, bs)


def gmlp_gate_bwd(name, tmat, dout, vg, ws, bs):
    t, w2 = tmat.shape
    w = w2 // 2
    groups = ws.shape[0]
    gd = w // groups

    def body(t_ref, do_ref, vg_ref, ws_ref, bs_ref, dt_ref, dvg_ref, dws_ref, dsv_ref, dvn_ref):
        i = pl.program_id(0)
        tv = t_ref[:, w:]
        v = _gelu(tv)
        r = lax.rsqrt(jnp.mean(v * v, axis=-1, keepdims=True) + EPS)
        vhat = v * r
        vn = (vhat * vg_ref[...]).astype(BF16)
        for g in range(groups):
            cols = slice(g * gd, (g + 1) * gd)
            tu = t_ref[:, cols]
            dov = do_ref[:, cols]
            sv = _dot(ws_ref[g], vn[:, cols], NN) + bs_ref[g]
            dt_ref[:, cols] = (dov * sv * _gelu_grad(tu)).astype(dt_ref.dtype)
            dsv = dov * _gelu(tu)
            _acc_cols(i, dsv_ref.at[:, cols], dsv)
            _acc_cols(i, dws_ref.at[g], _dot(dsv, vn[:, cols], NT))
            dvn_ref[:, cols] = _dot(ws_ref[g], dsv, TN)
        dvn = dvn_ref[...]
        _acc_cols(i, dvg_ref, _colsum(dvn * vhat))
        dvhat = dvn * vg_ref[...]
        dv = r * (dvhat - vhat * jnp.mean(dvhat * vhat, axis=-1, keepdims=True))
        dt_ref[:, w:] = (dv * _gelu_grad(tv)).astype(dt_ref.dtype)

    keep = lambda shape: pl.BlockSpec(shape, lambda i: (0,) * len(shape))
    return pl.pallas_call(
        body, name=name, grid=(t // GM_CHUNK,),
        in_specs=[_rows(GM_CHUNK, w2), _rows(GM_CHUNK, w), _full(vg), _full(ws), _full(bs)],
        out_specs=(_rows(GM_CHUNK, w2), keep((1, w)), keep(ws.shape), keep((GM_CHUNK, w))),
        out_shape=(jax.ShapeDtypeStruct((t, w2), BF16), jax.ShapeDtypeStruct((1, w), F32),
                   jax.ShapeDtypeStruct(ws.shape, F32), jax.ShapeDtypeStruct((GM_CHUNK, w), F32)),
        scratch_shapes=[pltpu.VMEM((GM_CHUNK, w), F32)],
        compiler_params=_params("arbitrary"))(tmat, dout, vg, ws, bs)


def qk_norm_fwd(name, qkv, qg, kg, hd, tr=128):
    t, d3 = qkv.shape
    d = d3 // 3

    def body(x_ref, qg_ref, kg_ref, q_ref, k_ref, v_ref):
        for part, (g_ref, o_ref) in enumerate(((qg_ref, q_ref), (kg_ref, k_ref))):
            for h in range(d // hd):
                xh = x_ref[:, part * d + h * hd: part * d + (h + 1) * hd]
                r = lax.rsqrt(jnp.mean(xh * xh, axis=-1, keepdims=True) + EPS)
                o_ref[:, h * hd:(h + 1) * hd] = (xh * r * g_ref[...]).astype(o_ref.dtype)
        v_ref[...] = x_ref[:, 2 * d:].astype(v_ref.dtype)

    out = jax.ShapeDtypeStruct((t, d), BF16)
    return pl.pallas_call(
        body, name=name, grid=(t // tr,), in_specs=[_rows(tr, d3), _full(qg), _full(kg)],
        out_specs=(_rows(tr, d),) * 3, out_shape=(out,) * 3, compiler_params=_params("parallel"))(qkv, qg, kg)


def qk_norm_bwd(name, qkv, dq, dk, dv, qg, kg, hd, tr=128):
    t, d3 = qkv.shape
    d = d3 // 3

    def body(x_ref, dq_ref, dk_ref, dv_ref, qg_ref, kg_ref, o_ref, dqg_ref, dkg_ref):
        i = pl.program_id(0)
        for part, (g_ref, dn_ref, dg_ref) in enumerate(((qg_ref, dq_ref, dqg_ref), (kg_ref, dk_ref, dkg_ref))):
            dg = jnp.zeros((1, hd), F32)
            for h in range(d // hd):
                xh = x_ref[:, part * d + h * hd: part * d + (h + 1) * hd]
                dn = dn_ref[:, h * hd:(h + 1) * hd]
                r = lax.rsqrt(jnp.mean(xh * xh, axis=-1, keepdims=True) + EPS)
                xhat = xh * r
                dg = dg + _colsum(dn * xhat)
                dxhat = dn * g_ref[...]
                dx = r * (dxhat - xhat * jnp.mean(dxhat * xhat, axis=-1, keepdims=True))
                o_ref[:, part * d + h * hd: part * d + (h + 1) * hd] = dx.astype(o_ref.dtype)
            _acc_cols(i, dg_ref, dg)
        o_ref[:, 2 * d:] = dv_ref[...].astype(o_ref.dtype)

    one = pl.BlockSpec((1, hd), lambda i: (0, 0))
    return pl.pallas_call(
        body, name=name, grid=(t // tr,),
        in_specs=[_rows(tr, d3), _rows(tr, d), _rows(tr, d), _rows(tr, d), _full(qg), _full(kg)],
        out_specs=(_rows(tr, d3), one, one),
        out_shape=(jax.ShapeDtypeStruct((t, d3), BF16), jax.ShapeDtypeStruct((1, hd), F32), jax.ShapeDtypeStruct((1, hd), F32)),
        compiler_params=_params("arbitrary"))(qkv, dq, dk, dv, qg, kg)


def _na_geometry(n_lat):
    rows = n_lat // GRID_W
    kh = min(WIN_H, rows)
    nb = min(kh + QROWS - 1, rows)
    n_blk = rows // QROWS
    q_row_off = np.repeat(np.arange(QROWS), GRID_W)
    q_col = np.tile(np.arange(GRID_W), QROWS)
    k_row_off = np.repeat(np.arange(nb), GRID_W)
    k_col = np.tile(np.arange(GRID_W), nb)
    c_start = np.clip(q_col - WIN_W // 2, 0, GRID_W - WIN_W)
    col_ok = (k_col[None, :] >= c_start[:, None]) & (k_col[None, :] < c_start[:, None] + WIN_W)
    dc_idx = np.clip(k_col[None, :] - q_col[:, None], -(WIN_W - 1), WIN_W - 1) + WIN_W - 1

    def block(blk):
        r0 = blk * QROWS
        q_row = r0 + q_row_off
        r_start = np.clip(q_row - kh // 2, 0, rows - kh)
        band0 = min(int(np.clip(r0 - kh // 2, 0, rows - kh)), rows - nb)
        k_row = band0 + k_row_off
        ok = col_ok & (k_row[None, :] >= r_start[:, None]) & (k_row[None, :] < r_start[:, None] + kh)
        dr_idx = np.clip(k_row[None, :] - q_row[:, None], -(WIN_H - 1), WIN_H - 1) + WIN_H - 1
        return band0, ok, dr_idx

    reps = [0, 1, 2, n_blk - 2, n_blk - 1]
    variant = lambda blk: 0 if blk == 0 else 1 if blk == 1 else 3 if blk == n_blk - 2 else 4 if blk == n_blk - 1 else 2
    geo = [block(b) for b in reps]
    for blk in range(n_blk):
        _, ok, dr = block(blk)
        assert np.array_equal(ok, geo[variant(blk)][1]) and np.array_equal(np.where(ok, dr, 0), np.where(ok, geo[variant(blk)][2], 0))
    return dict(rows=rows, kh=kh, nb=nb, n_blk=n_blk, reps=reps, ok=[g[1] for g in geo], dr=[g[2] for g in geo],
                band0=[g[0] for g in geo], dc=dc_idx)


def _na_onehots(geo):
    nb = geo["nb"]
    w2 = GRID_W * GRID_W
    qc, kc = np.meshgrid(np.arange(GRID_W), np.arange(GRID_W), indexing="ij")
    diff = (kc - qc).reshape(-1)
    cols = np.zeros((w2, LANE), np.float32)
    sel = np.abs(diff) <= WIN_W - 1
    cols[np.arange(w2)[sel], diff[sel] + WIN_W - 1] = 1.0
    npair = len(geo["reps"]) * QROWS * nb
    kpad = -(-npair // LANE) * LANE
    rows = np.zeros((16, kpad), np.float32)
    for vi, blk in enumerate(geo["reps"]):
        for qr in range(QROWS):
            for kr in range(nb):
                dr = (geo["band0"][vi] + kr) - (blk * QROWS + qr)
                if abs(dr) <= WIN_H - 1:
                    rows[dr + WIN_H - 1, (vi * QROWS + qr) * nb + kr] = 1.0
    return cols, rows, npair, kpad


def na_bias_table(name, rpb, geo):
    n_heads, nb, nv = rpb.shape[0], geo["nb"], len(geo["reps"])
    cols, rows, npair, kpad = _na_onehots(geo)
    rpb_p = jnp.pad(rpb, ((0, 0), (0, 16 - rpb.shape[1]), (0, LANE - rpb.shape[2]))).transpose(1, 0, 2).reshape(16, n_heads * LANE)
    t1 = _mm(name + "_rows", jnp.asarray(rows.T), rpb_p, "nn", (kpad, n_heads * LANE), F32,
             (kpad, _tile(n_heads * LANE, 1024, LANE), 16), exact=True)
    t1 = t1[:npair].reshape(nv, QROWS * nb, n_heads, LANE).transpose(0, 2, 1, 3).reshape(nv * n_heads * QROWS * nb, LANE)
    m = t1.shape[0]
    flat = _mm(name + "_cols", t1, jnp.asarray(cols), "nt", (m, GRID_W * GRID_W), F32,
               (_tile(m, 512, 8), _tile(GRID_W * GRID_W, 2048, LANE), LANE), exact=True)
    tab = flat.reshape(nv, n_heads, QROWS, nb, GRID_W, GRID_W).transpose(0, 1, 2, 4, 3, 5).reshape(nv, n_heads, QROWS * GRID_W, nb * GRID_W)
    return jnp.where(jnp.asarray(np.stack(geo["ok"]))[:, None], tab, NEG)


def _na_tile_info(qt, geo):
    n_blk, rows, kh, nb = geo["n_blk"], geo["rows"], geo["kh"], geo["nb"]
    is_ctx = qt >= n_blk
    band0 = jnp.minimum(jnp.clip(qt * QROWS - kh // 2, 0, rows - kh), rows - nb)
    band0 = jnp.where(is_ctx, 0, band0)
    return is_ctx, pl.multiple_of(band0 * GRID_W, GRID_W)


def _na_variant(qt, n_blk):
    v = jnp.minimum(qt, 2) + (qt >= n_blk - 2).astype(jnp.int32) + (qt >= n_blk - 1).astype(jnp.int32)
    return jnp.minimum(v, 4)


def _na_probs(q, kb, kc, bias, is_ctx, scale):
    s_lat = _dot(q, kb, NT) * scale + bias
    s_lat = jnp.where(is_ctx, NEG, s_lat)
    s_ctx = _dot(q, kc, NT) * scale
    m = jnp.maximum(jnp.max(s_lat, axis=-1, keepdims=True), jnp.max(s_ctx, axis=-1, keepdims=True))
    e_lat, e_ctx = jnp.exp(s_lat - m), jnp.exp(s_ctx - m)
    den = jnp.sum(e_lat, axis=-1, keepdims=True) + jnp.sum(e_ctx, axis=-1, keepdims=True)
    return e_lat / den, e_ctx / den


NA_MAX_TILES_PER_STEP = 6


def _na_step_specs(t, qw, nk, hd, n_blk):
    per = max(n for n in range(1, NA_MAX_TILES_PER_STEP + 1) if (t // qw) % n == 0)
    tile = pl.BlockSpec((per * qw, hd), lambda h, i: (i, h))
    btabs = [pl.BlockSpec((None, None, qw, nk), functools.partial(lambda h, i, part: (_na_variant(per * i + part, n_blk), h, 0, 0), part=part))
             for part in range(per)]
    return per, tile, btabs


def na_attention_fwd(name, q, k, v, bias, geo, n_lat, hd, carry=None):
    t, d = q.shape
    qw, nk = QROWS * GRID_W, geo["nb"] * GRID_W
    scale = hd ** -0.5
    per, tile, btabs = _na_step_specs(t, qw, nk, hd, geo["n_blk"])

    def body(q_ref, k_ref, v_ref, *rest):
        b_refs, o_ref = rest[:per], rest[per]
        kc, vc = k_ref[n_lat:, :], v_ref[n_lat:, :]
        for part, b_ref in enumerate(b_refs):
            rows = slice(part * qw, (part + 1) * qw)
            is_ctx, start = _na_tile_info(per * pl.program_id(1) + part, geo)
            kb, vb = k_ref[pl.ds(start, nk), :], v_ref[pl.ds(start, nk), :]
            p_lat, p_ctx = _na_probs(q_ref[rows, :], kb, kc, b_ref[...], is_ctx, scale)
            o_ref[rows, :] = (_dot(p_lat, vb, NN) + _dot(p_ctx, vc, NN)).astype(o_ref.dtype)

    head = pl.BlockSpec((t, hd), lambda h, i: (0, h))
    res = _call(body, name, (d // hd, t // (per * qw)), [tile, head, head, *btabs], [tile],
                [jax.ShapeDtypeStruct((t, d), BF16)], (q, k, v, *([bias] * per)), ("parallel", "arbitrary"), carry=carry)
    return res[0] if carry is None else (res[0][0], res[1])


def na_attention_bwd(name, q, k, v, bias, do, geo, n_lat, hd, carry=None):
    t, d = q.shape
    qw, nk = QROWS * GRID_W, geo["nb"] * GRID_W
    n_blk = geo["n_blk"]
    scale = hd ** -0.5
    per, tile, btabs = _na_step_specs(t, qw, nk, hd, n_blk)

    def body(q_ref, k_ref, v_ref, *rest):
        b_refs, (do_ref, dq_ref, dk_ref, dv_ref, db_ref) = rest[:per], rest[per:]
        step = pl.program_id(1)

        @pl.when(step == 0)
        def _():
            dk_ref[...] = jnp.zeros_like(dk_ref)
            dv_ref[...] = jnp.zeros_like(dv_ref)
            db_ref[...] = jnp.zeros_like(db_ref)

        kc, vc = k_ref[n_lat:, :], v_ref[n_lat:, :]
        for part, b_ref in enumerate(b_refs):
            qt = per * step + part
            rows = slice(part * qw, (part + 1) * qw)
            is_ctx, start = _na_tile_info(qt, geo)
            band = pl.ds(start, nk)
            qv, dov = q_ref[rows, :], do_ref[rows, :]
            kb, vb = k_ref[band, :], v_ref[band, :]
            p_lat, p_ctx = _na_probs(qv, kb, kc, b_ref[...], is_ctx, scale)
            dp_lat, dp_ctx = _dot(dov, vb, NT), _dot(dov, vc, NT)
            delta = jnp.sum(p_lat * dp_lat, axis=-1, keepdims=True) + jnp.sum(p_ctx * dp_ctx, axis=-1, keepdims=True)
            ds_lat, ds_ctx = p_lat * (dp_lat - delta), p_ctx * (dp_ctx - delta)
            db_ref[_na_variant(qt, n_blk)] += ds_lat
            dsl, dsc = (ds_lat * scale).astype(BF16), (ds_ctx * scale).astype(BF16)
            dq_ref[rows, :] = _dot(dsl, kb, NN) + _dot(dsc, kc, NN)
            dk_ref[band, :] += _dot(dsl, qv, TN)
            dk_ref[n_lat:, :] += _dot(dsc, qv, TN)
            dv_ref[band, :] += _dot(p_lat, dov, TN)
            dv_ref[n_lat:, :] += _dot(p_ctx, dov, TN)

    nv = bias.shape[0]
    head = pl.BlockSpec((t, hd), lambda h, i: (0, h))
    full = jax.ShapeDtypeStruct((t, d), F32)
    res = _call(body, name, (d // hd, t // (per * qw)), [tile, head, head, *btabs, tile],
                [tile, head, head, pl.BlockSpec((nv, None, qw, nk), lambda h, i: (0, h, 0, 0))],
                [full, full, full, jax.ShapeDtypeStruct(bias.shape, F32)], (q, k, v, *([bias] * per), do), ("arbitrary", "arbitrary"), carry=carry)
    return tuple(res) if carry is None else (*res[0], res[1])


def na_rpb_grad(name, dbias, geo, n_heads):
    nb = geo["nb"]
    nv = len(geo["reps"])
    w2 = GRID_W * GRID_W
    cols, rows, npair, kpad = _na_onehots(geo)
    xmat = dbias.reshape(nv, n_heads, QROWS, GRID_W, nb, GRID_W).transpose(0, 1, 2, 4, 3, 5).reshape(nv * n_heads * QROWS * nb, w2)
    m = xmat.shape[0]
    r = _mm(name + "_cols", xmat, jnp.asarray(cols), "nn", (m, LANE), F32, (_tile(m, 512, 8), LANE, _tile(w2, 1024, LANE)), exact=True)
    r2 = r.reshape(nv, n_heads, QROWS * nb, LANE).transpose(0, 2, 1, 3).reshape(npair, n_heads * LANE)
    r2 = jnp.pad(r2, ((0, kpad - npair), (0, 0)))
    out = _mm(name + "_rows", jnp.asarray(rows), r2, "nn", (16, n_heads * LANE), F32, (16, _tile(n_heads * LANE, 1024, LANE), kpad), exact=True)
    return out[:2 * WIN_H - 1].reshape(2 * WIN_H - 1, n_heads, LANE)[:, :, :2 * WIN_W - 1].transpose(1, 0, 2)


def all_gather(name, arrays):
    n = len(arrays)

    def body(*refs):
        for phase in _gather_two_level(refs[:n], refs[n:2 * n], *refs[2 * n:]):
            phase()

    outs = pl.pallas_call(body, name=name, in_specs=[HBM] * n, out_specs=[HBM] * n, out_shape=_landing("gather", arrays),
                          scratch_shapes=_exchange_scratch(n))(*arrays)
    return list(outs)


def all_to_all(name, arrays):
    n = len(arrays)

    def body(*refs):
        for phase in _scatter_direct(refs[:n], refs[n:2 * n], *refs[2 * n:]):
            phase()

    outs = pl.pallas_call(body, name=name, in_specs=[HBM] * n, out_specs=[HBM] * n, out_shape=_landing("scatter", arrays),
                          scratch_shapes=_exchange_scratch(n))(*arrays)
    return list(outs)


def _adamw_math(w, g, m, v):
    m = ADAM_B1 * m + (1.0 - ADAM_B1) * g
    v = ADAM_B2 * v + (1.0 - ADAM_B2) * (g * g)
    m_hat = m / (1.0 - ADAM_B1 ** ADAM_STEP)
    v_hat = v / (1.0 - ADAM_B2 ** ADAM_STEP)
    delta = -ADAM_LR * (m_hat / (jnp.sqrt(v_hat) + ADAM_EPS) + ADAM_WD * w)
    return delta, m, v


def adamw(name, parts, w, m, v):
    r, c = w.shape
    npart, _, cp = parts.shape
    tc = c if (c % LANE or cp != c) else _tile(c, 512, LANE)
    tr = _tile(r, 256, 16 if parts.dtype == BF16 else 8)
    if tr < 64:
        tr, tc = r, (tc if tc == c and cp != c else _tile(c, 256, LANE))
    tcp = cp if tc == c else tc

    def body(p_ref, w_ref, m_ref, v_ref, g_ref, d_ref, mo_ref, vo_ref):
        g = p_ref[0].astype(F32)
        for s in range(1, npart):
            g = g + p_ref[s].astype(F32)
        g = g[:, :tc]
        g_ref[...] = g
        d_ref[...], mo_ref[...], vo_ref[...] = _adamw_math(w_ref[...], g, m_ref[...], v_ref[...])

    blk = pl.BlockSpec((tr, tc), lambda i, j: (i, j))
    out = jax.ShapeDtypeStruct((r, c), F32)
    return pl.pallas_call(
        body, name=name, grid=(r // tr, c // tc),
        in_specs=[pl.BlockSpec((npart, tr, tcp), lambda i, j: (0, i, j)), blk, blk, blk],
        out_specs=(blk,) * 4, out_shape=(out,) * 4, compiler_params=_params("parallel", "parallel"))(parts, w, m, v)


def sum_parts(name, parts):
    npart, r, c = parts.shape
    tr = _tile(r, 512, 8)

    def body(p_ref, o_ref):
        g = p_ref[0]
        for s in range(1, npart):
            g = g + p_ref[s]
        o_ref[...] = g

    return pl.pallas_call(
        body, name=name, grid=(r // tr,), in_specs=[pl.BlockSpec((npart, tr, c), lambda i: (0, i, 0))],
        out_specs=pl.BlockSpec((tr, c), lambda i: (i, 0)), out_shape=jax.ShapeDtypeStruct((r, c), F32),
        compiler_params=_params("parallel"))(parts)


class _Pack:
    def __init__(self, shapes):
        self.shapes = dict(shapes)
        self.offsets, off = {}, 0
        for name, shape in self.shapes.items():
            self.offsets[name] = off
            off += -(-int(np.prod(shape)) // (8 * LANE)) * (8 * LANE)
        self.used = off
        self.rows = -(-off // (512 * LANE)) * 512

    def pack(self, values):
        pieces = []
        for name, shape in self.shapes.items():
            size = int(np.prod(shape))
            padded = -(-size // (8 * LANE)) * (8 * LANE)
            pieces.append(jnp.pad(values[name].astype(F32).reshape(-1), (0, padded - size)))
        pieces.append(jnp.zeros((self.rows * LANE - self.used,), F32))
        return jnp.concatenate(pieces).reshape(self.rows, LANE)

    def unpack(self, flat, lead=()):
        flat = flat.reshape(*lead, self.rows * LANE)
        out = {}
        for name, shape in self.shapes.items():
            size = int(np.prod(shape))
            out[name] = flat[..., self.offsets[name]:self.offsets[name] + size].reshape(*lead, *shape)
        return out


def kernel(x, c, ctx, c_ctx, norm_mix_g, norm_ffn_g, w_ada, b_ada, na_w_qkv, na_q_g, na_k_g, na_rpb, na_w_o, gm_w_in, gm_v_g, gm_w_s, gm_b_s, gm_w_out, sc_w_in, sc_conv_w, sc_w_out, ffn_w_up, ffn_conv_w, ffn_conv_b, ffn_w_down, loss_target, m_c_ctx, m_norm_mix_g, m_norm_ffn_g, m_w_ada, m_b_ada, m_na_w_qkv, m_na_q_g, m_na_k_g, m_na_rpb, m_na_w_o, m_gm_w_in, m_gm_v_g, m_gm_w_s, m_gm_b_s, m_gm_w_out, m_sc_w_in, m_sc_conv_w, m_sc_w_out, m_ffn_w_up, m_ffn_conv_w, m_ffn_conv_b, m_ffn_w_down, v_c_ctx, v_norm_mix_g, v_norm_ffn_g, v_w_ada, v_b_ada, v_na_w_qkv, v_na_q_g, v_na_k_g, v_na_rpb, v_na_w_o, v_gm_w_in, v_gm_v_g, v_gm_w_s, v_gm_b_s, v_gm_w_out, v_sc_w_in, v_sc_conv_w, v_sc_w_out, v_ffn_w_up, v_ffn_conv_w, v_ffn_conv_b, v_ffn_w_down):
    weights = dict(c_ctx=c_ctx, norm_mix_g=norm_mix_g, norm_ffn_g=norm_ffn_g, w_ada=w_ada, b_ada=b_ada, na_w_qkv=na_w_qkv,
                   na_q_g=na_q_g, na_k_g=na_k_g, na_rpb=na_rpb, na_w_o=na_w_o, gm_w_in=gm_w_in, gm_v_g=gm_v_g, gm_w_s=gm_w_s,
                   gm_b_s=gm_b_s, gm_w_out=gm_w_out, sc_w_in=sc_w_in, sc_conv_w=sc_conv_w, sc_w_out=sc_w_out,
                   ffn_w_up=ffn_w_up, ffn_conv_w=ffn_conv_w, ffn_conv_b=ffn_conv_b, ffn_w_down=ffn_w_down)
    mom_m = dict(c_ctx=m_c_ctx, norm_mix_g=m_norm_mix_g, norm_ffn_g=m_norm_ffn_g, w_ada=m_w_ada, b_ada=m_b_ada, na_w_qkv=m_na_w_qkv,
                 na_q_g=m_na_q_g, na_k_g=m_na_k_g, na_rpb=m_na_rpb, na_w_o=m_na_w_o, gm_w_in=m_gm_w_in, gm_v_g=m_gm_v_g,
                 gm_w_s=m_gm_w_s, gm_b_s=m_gm_b_s, gm_w_out=m_gm_w_out, sc_w_in=m_sc_w_in, sc_conv_w=m_sc_conv_w,
                 sc_w_out=m_sc_w_out, ffn_w_up=m_ffn_w_up, ffn_conv_w=m_ffn_conv_w, ffn_conv_b=m_ffn_conv_b, ffn_w_down=m_ffn_w_down)
    mom_v = dict(c_ctx=v_c_ctx, norm_mix_g=v_norm_mix_g, norm_ffn_g=v_norm_ffn_g, w_ada=v_w_ada, b_ada=v_b_ada, na_w_qkv=v_na_w_qkv,
                 na_q_g=v_na_q_g, na_k_g=v_na_k_g, na_rpb=v_na_rpb, na_w_o=v_na_w_o, gm_w_in=v_gm_w_in, gm_v_g=v_gm_v_g,
                 gm_w_s=v_gm_w_s, gm_b_s=v_gm_b_s, gm_w_out=v_gm_w_out, sc_w_in=v_sc_w_in, sc_conv_w=v_sc_conv_w,
                 sc_w_out=v_sc_w_out, ffn_w_up=v_ffn_w_up, ffn_conv_w=v_ffn_conv_w, ffn_conv_b=v_ffn_conv_b, ffn_w_down=v_ffn_w_down)
    names = list(weights)

    n_lat, d = x.shape[1], x.shape[2]
    n_ctx = ctx.shape[1]
    t = n_lat + n_ctx
    depth = norm_mix_g.shape[0]
    hd = na_q_g.shape[-1]
    n_heads = d // hd
    nup = ffn_w_up.shape[-1]
    nup_p = -(-nup // LANE) * LANE
    fdim, fp = 4 * nup, 4 * nup_p
    me = _slot(_me())
    geo = _na_geometry(n_lat)

    pad_up = lambda a: jnp.pad(a, [(0, 0)] * (a.ndim - 1) + [(0, nup_p - nup)])
    mixer_weights = (("na_w_qkv", "na_w_o"), ("gm_w_in", "gm_w_out"), ("sc_w_in", "sc_w_out"))

    def shards(i):
        w_in, w_out = (weights[nm][i // N_MIXERS].astype(BF16) for nm in mixer_weights[i % N_MIXERS])
        return [w_in, w_out, pad_up(ffn_w_up[i]).astype(BF16), ffn_w_down[i].astype(BF16)]

    def operands(g_in, g_out, g_up, g_down):
        down = jnp.pad(g_down.reshape(4, nup, d), ((0, 0), (0, nup_p - nup), (0, 0))).reshape(fp, d)
        return dict(w_in=g_in, w_out=g_out.reshape(-1, d), w_up=g_up, w_down=down)

    first_in, *first_rest = shards(0)
    g_in0, g_fcw, g_scw, c_all = all_gather("gather_first", [first_in, pad_up(ffn_conv_w), sc_conv_w, c])
    layer_w = []
    f_cw = [g_fcw[:, i].transpose(1, 0, 2).reshape(3, 2, fp).transpose(1, 0, 2) for i in range(depth)]
    cb_p = pad_up(ffn_conv_b.reshape(depth, N_DEV, nup)).reshape(depth, 2, 1, fp)
    s_cw = [g_scw[:, j].transpose(1, 0, 2).reshape(3, d) for j in range(sc_conv_w.shape[0])]

    cond = jnp.concatenate([c_all.reshape(N_DEV, d), c_ctx[None], jnp.zeros((7, d), F32)])
    mod_cols = mm_nn("ada_fwd", cond, w_ada, a_silu=True)
    (mod_all,) = all_gather("gather_mod", [mod_cols])
    ncol = w_ada.shape[-1]
    mod_all = mod_all.reshape(N_DEV, 16, depth, ncol).transpose(2, 1, 0, 3).reshape(depth, 16, N_MOD * d) + b_ada[:, None, :]
    mod_lat = lax.dynamic_index_in_dim(mod_all, me, axis=1, keepdims=False)
    mods = jnp.stack([mod_lat, mod_all[:, N_DEV]], axis=1).reshape(depth, 2, N_MOD, d)
    sh1, sc1, g1, sh2, sc2, g2 = (mods[:, :, kd] for kd in range(N_MOD))

    xs = jnp.concatenate([x[0], ctx[0]], axis=0)
    saved = []
    prev = None
    for i in range(depth):
        mixer, j = i % N_MIXERS, i // N_MIXERS
        s = {}
        if prev is None:
            s["x"] = xs
            s["h"] = resid_rms_mod(f"l{i}_norm_mix", xs, None, None, norm_mix_g[i:i + 1], sc1[i], sh1[i], n_lat)
        else:
            s["x"], s["h"] = resid_rms_mod(f"l{i}_norm_mix", prev[0], prev[1], prev[2], norm_mix_g[i:i + 1], sc1[i], sh1[i], n_lat)
        if i == 0:
            s["qkv"], (g_out0,) = mm_nn(f"l{i}_qkv", s["h"], g_in0, carry=("gather", first_rest[:1]))
            s["q"], s["k"], s["v"] = qk_norm_fwd(f"l{i}_qk_norm", s["qkv"], na_q_g[j:j + 1], na_k_g[j:j + 1], hd)
            s["bias"] = na_bias_table(f"l{i}_bias", na_rpb[j], geo)
            s["o"], (g_up0, g_down0) = na_attention_fwd(f"l{i}_attn", s["q"], s["k"], s["v"], s["bias"], geo, n_lat, hd,
                                                        carry=("gather", first_rest[1:]))
            layer_w.append(operands(g_in0, g_out0, g_up0, g_down0))
        lw = layer_w[i]
        if i == 0:
            pass
        elif mixer == 0:
            s["qkv"] = mm_nn(f"l{i}_qkv", s["h"], lw["w_in"])
            s["q"], s["k"], s["v"] = qk_norm_fwd(f"l{i}_qk_norm", s["qkv"], na_q_g[j:j + 1], na_k_g[j:j + 1], hd)
            s["bias"] = na_bias_table(f"l{i}_bias", na_rpb[j], geo)
            s["o"] = na_attention_fwd(f"l{i}_attn", s["q"], s["k"], s["v"], s["bias"], geo, n_lat, hd)
        elif mixer == 1:
            s["t"] = mm_nn(f"l{i}_gm_in", s["h"], lw["w_in"])
            s["o"] = gmlp_gate_fwd(f"l{i}_gm_gate", s["t"], gm_v_g[j:j + 1], gm_w_s[j], gm_b_s[j][:, :, None])
        else:
            s["t"] = mm_nn(f"l{i}_sc_in", s["h"], lw["w_in"])
            s["o"] = sc_gate_fwd(f"l{i}_sc_gate", s["t"], s_cw[j], n_lat)
        s["y"] = mm_nn(f"l{i}_mix_out", s["o"], lw["w_out"])
        s["x1"], s["hf"] = resid_rms_mod(f"l{i}_norm_ffn", s["x"], s["y"], g1[i], norm_ffn_g[i:i + 1], sc2[i], sh2[i], n_lat)
        if i + 1 < depth:
            n_in, n_out, n_up, n_down = shards(i + 1)
            s["u"], (g_up,) = mm_nn(f"l{i}_ffn_up", s["hf"], lw["w_up"], out_parts=2, carry=("gather", [n_up]))
            s["z"], s["a"], (g_down,) = ffn_act_fwd(f"l{i}_ffn_act", s["u"], f_cw[i], cb_p[i], n_lat, carry=("gather", [n_down]))
            s["f"], (g_in, g_out) = mm_nn(f"l{i}_ffn_down", s["a"], lw["w_down"], carry=("gather", [n_in, n_out]))
            layer_w.append(operands(g_in, g_out, g_up, g_down))
        else:
            s["u"] = mm_nn(f"l{i}_ffn_up", s["hf"], lw["w_up"], out_parts=2)
            s["z"], s["a"] = ffn_act_fwd(f"l{i}_ffn_act", s["u"], f_cw[i], cb_p[i], n_lat)
            s["f"] = mm_nn(f"l{i}_ffn_down", s["a"], lw["w_down"])
        prev = (s["x1"], s["f"], g2[i])
        saved.append(s)

    dx, loss_local, df, dg2 = loss_head("loss_head", prev[0], prev[1], prev[2], loss_target[0])
    loss = lax.psum(loss_local[0, 0], AXES)

    big = {}
    small = {}
    dmod = [None] * depth
    zeros_like_param = lambda p: [None] * p.shape[0]
    for nm in ("na_w_qkv", "na_w_o", "gm_w_in", "gm_w_out", "sc_w_in", "sc_w_out", "ffn_w_up", "ffn_w_down"):
        big[nm] = zeros_like_param(weights[nm])
    for nm in ("norm_mix_g", "norm_ffn_g", "ffn_conv_w", "ffn_conv_b", "na_q_g", "na_k_g", "na_rpb", "gm_v_g", "gm_w_s", "gm_b_s", "sc_conv_w"):
        small[nm] = zeros_like_param(weights[nm])
    landed, pending = {}, []

    def take(wanted):
        keys = [k for k in pending if wanted(k)]
        for k in keys:
            pending.remove(k)
        return keys

    for i in reversed(range(depth)):
        mixer, j = i % N_MIXERS, i // N_MIXERS
        nm_in, nm_out = mixer_weights[mixer]
        s, lw = saved[i], layer_w[i]
        da = mm_nt(f"l{i}_b_ffn_down_x", df, lw["w_down"])
        dwd = mm_tn(f"l{i}_b_ffn_down_w", s["a"], df)
        big["ffn_w_down"][i] = dwd.reshape(4, nup_p, d)[:, :nup].reshape(N_DEV, fdim // N_DEV, d)
        keys = take(lambda k: k[0] == "ffn_w_up" or k[0] in [m[1] for m in mixer_weights])
        if keys:
            du, dcw, dcb, got = ffn_act_bwd(f"l{i}_b_ffn_act", s["z"], s["u"], da, f_cw[i], n_lat,
                                            carry=("scatter", [big[nm][idx] for nm, idx in keys]))
            landed.update(zip(keys, got))
        else:
            du, dcw, dcb = ffn_act_bwd(f"l{i}_b_ffn_act", s["z"], s["u"], da, f_cw[i], n_lat)
        small["ffn_conv_w"][i] = dcw.transpose(1, 0, 2).reshape(3, N_DEV, nup_p)[:, :, :nup].reshape(3, 2 * fdim)
        small["ffn_conv_b"][i] = dcb.reshape(N_DEV, nup_p)[:, :nup].reshape(2 * fdim)
        keys = [("ffn_w_down", i)] + take(lambda k: True)
        dhf, got = mm_nt(f"l{i}_b_ffn_up_x", du, lw["w_up"], carry=("scatter", [big[nm][idx] for nm, idx in keys]))
        landed.update(zip(keys, got))
        big["ffn_w_up"][i] = mm_tn(f"l{i}_b_ffn_up_w", s["hf"], du, out_parts=N_DEV)
        pending.append(("ffn_w_up", i))
        dx1, dsh2, dsc2, dgf, dy, dg1 = rms_mod_bwd(f"l{i}_b_norm_ffn", s["x1"], dhf, norm_ffn_g[i:i + 1], sc2[i], dx, n_lat,
                                                    branch=(s["y"], g1[i]))
        small["norm_ffn_g"][i] = dgf[0]
        do = mm_nt(f"l{i}_b_mix_out_x", dy, lw["w_out"], out_dtype=BF16 if mixer == 0 else F32)
        big[nm_out][j] = mm_tn(f"l{i}_b_mix_out_w", s["o"], dy).reshape(N_DEV, -1, d)
        pending.append((nm_out, j))
        if mixer == 0:
            keys = take(lambda k: True)
            dq, dk, dv, dbias, got = na_attention_bwd(f"l{i}_b_attn", s["q"], s["k"], s["v"], s["bias"], do, geo, n_lat, hd,
                                                      carry=("scatter", [big[nm][idx] for nm, idx in keys]))
            landed.update(zip(keys, got))
            small["na_rpb"][j] = na_rpb_grad(f"l{i}_b_rpb", dbias, geo, n_heads)
            dt, dqg, dkg = qk_norm_bwd(f"l{i}_b_qk_norm", s["qkv"], dq, dk, dv, na_q_g[j:j + 1], na_k_g[j:j + 1], hd)
            small["na_q_g"][j], small["na_k_g"][j] = dqg[0], dkg[0]
        elif mixer == 1:
            dt, dvg, dws, dsv = gmlp_gate_bwd(f"l{i}_b_gm_gate", s["t"], do, gm_v_g[j:j + 1], gm_w_s[j], gm_b_s[j][:, :, None])
            groups, width = gm_w_s.shape[1], dsv.shape[1]
            group_of = np.zeros((width, LANE), np.float32)
            group_of[np.arange(width), np.arange(width) // (width // groups)] = 1.0
            dbs = _mm(f"l{i}_b_gm_bs", dsv, jnp.asarray(group_of), "nn", (GM_CHUNK, LANE), F32,
                      (GM_CHUNK, LANE, _tile(width, 2048, LANE)), exact=True)[:, :groups].T
            small["gm_v_g"][j], small["gm_w_s"][j], small["gm_b_s"][j] = dvg[0], dws, dbs
        else:
            dt, dscw = sc_gate_bwd(f"l{i}_b_sc_gate", s["t"], do, s_cw[j], n_lat)
            small["sc_conv_w"][j] = dscw
            dt = dt.transpose(1, 0, 2).reshape(t, 3 * d)
        dh = mm_nt(f"l{i}_b_mix_in_x", dt, lw["w_in"])
        big[nm_in][j] = mm_tn(f"l{i}_b_mix_in_w", s["h"], dt, out_parts=N_DEV)
        pending.append((nm_in, j))
        dmod_ffn = [dsh2, dsc2, dg2]
        if i > 0:
            dx, dsh1, dsc1, dgm, df, dg2 = rms_mod_bwd(f"l{i}_b_norm_mix", s["x"], dh, norm_mix_g[i:i + 1], sc1[i], dx1, n_lat,
                                                       branch=(saved[i - 1]["f"], g2[i - 1]))
        else:
            dx, dsh1, dsc1, dgm = rms_mod_bwd(f"l{i}_b_norm_mix", s["x"], dh, norm_mix_g[i:i + 1], sc1[i], dx1, n_lat)
        small["norm_mix_g"][i] = dgm[0]
        dmod[i] = jnp.stack([dsh1, dsc1, dg1] + dmod_ffn, axis=1)
    grad_x = dx[:n_lat][None]

    small = {nm: jnp.stack(v) for nm, v in small.items()}
    small["dmod"] = jnp.stack(dmod)
    pack = _Pack({nm: v.shape for nm, v in small.items()})
    (small_all,) = all_gather("gather_small", [pack.pack(small)])
    small_sum = pack.unpack(sum_parts("sum_small", small_all))
    dmod_all = pack.unpack(small_all, lead=(N_DEV,))["dmod"]
    dmod_ctx = small_sum["dmod"][:, 1].reshape(depth, N_MOD * d)
    grads = {nm: small_sum[nm] for nm in small if nm != "dmod"}
    grads["b_ada"] = (small_sum["dmod"][:, 0] + small_sum["dmod"][:, 1]).reshape(depth, N_MOD * d)
    my_cols = lambda a, width: lax.dynamic_slice_in_dim(a, me * width, width, axis=-1)
    grads["ffn_conv_w"] = my_cols(grads["ffn_conv_w"], nup)
    grads["sc_conv_w"] = my_cols(grads["sc_conv_w"], sc_conv_w.shape[-1])

    drows = jnp.concatenate([dmod_all[:, :, 0], dmod_all[:, :, 1]]).reshape(2 * N_DEV, depth, N_MOD * d)
    drows = my_cols(drows, ncol).reshape(2 * N_DEV, depth * ncol)
    cond2 = jnp.concatenate([c_all.reshape(N_DEV, d), jnp.broadcast_to(c_ctx[None], (N_DEV, d))])
    g_w_ada = mm_tn("ada_bwd_w", cond2, drows, out_parts=depth, out_dtype=F32, a_silu=True)
    dctx_rows = jnp.pad(my_cols(dmod_ctx, ncol).reshape(1, depth * ncol), ((0, 15), (0, 0)))
    dcc = mm_nt("ada_bwd_c", dctx_rows, w_ada)[0:1]
    (dcc_all,) = all_gather("gather_c_ctx", [jnp.pad(dcc.reshape(-1, LANE), ((0, (-d // LANE) % 8), (0, 0)))])
    dcc_sum = sum_parts("sum_c_ctx", dcc_all).reshape(-1)[:d]
    sig = jax.nn.sigmoid(c_ctx)
    grads["c_ctx"] = dcc_sum * (sig * (1.0 + c_ctx * (1.0 - sig)))

    landed.update(zip(pending, all_to_all("scatter_last", [big[nm][idx] for nm, idx in pending])))
    out = {nm: [None] * 4 for nm in names}
    per_big = {nm: [] for nm in big}
    for nm, idx in [(nm, idx) for nm in big for idx in range(len(big[nm]))]:
        w2, p = weights[nm][idx], landed[nm, idx]
        per_big[nm].append(adamw(f"adamw_{nm}{idx}", p.reshape(N_DEV, w2.shape[0], -1), w2, mom_m[nm][idx], mom_v[nm][idx]))
    for nm, res in per_big.items():
        out[nm] = [jnp.stack([r[q] for r in res]) for q in range(4)]
    res = [adamw(f"adamw_w_ada{i}", g_w_ada[i][None], w_ada[i], m_w_ada[i], v_w_ada[i]) for i in range(depth)]
    out["w_ada"] = [jnp.stack([r[q] for r in res]) for q in range(4)]
    small_names = [nm for nm in names if nm not in big and nm != "w_ada"]
    spack = _Pack({nm: weights[nm].shape for nm in small_names})
    flat = [spack.pack({nm: src[nm] for nm in small_names}) for src in (grads, weights, mom_m, mom_v)]
    res = adamw("adamw_small", flat[0][None], flat[1], flat[2], flat[3])
    res = [spack.unpack(r) for r in res]
    for nm in small_names:
        out[nm] = [grads[nm].reshape(weights[nm].shape)] + [res[q][nm] for q in range(1, 4)]

    return (loss, grad_x, *[out[nm][0] for nm in names], *[out[nm][1] for nm in names],
            *[out[nm][2] for nm in names], *[out[nm][3] for nm in names])
```

```python
import functools
import math

import numpy as np
import jax
import jax.numpy as jnp
from jax import lax
from jax.experimental import pallas as pl
from jax.experimental.pallas import tpu as pltpu

F32 = jnp.float32
BF16 = jnp.bfloat16
MESH = pl.DeviceIdType.MESH
AXES = ("x", "y", "c")
N_DEV = 8
N_MOD = 6
N_MIXERS = 3
EPS = 1e-6
GRID_W = 64
WIN_H = 8
WIN_W = 16
QROWS = 2
GM_CHUNK = 128
LANE = 128
NEG = -1e30
ADAM_LR = 0.001
ADAM_B1 = 0.9
ADAM_B2 = 0.999
ADAM_EPS = 1e-08
ADAM_WD = 0.01
ADAM_STEP = 10
VMEM_LIMIT = 56 * 1024 * 1024
HBM = pl.BlockSpec(memory_space=pltpu.HBM)

NN = (((1,), (0,)), ((), ()))
NT = (((1,), (1,)), ((), ()))
TN = (((0,), (0,)), ((), ()))


def _params(*sem):
    return pltpu.CompilerParams(dimension_semantics=sem, vmem_limit_bytes=VMEM_LIMIT)


def _tile(n, pref, mult):
    best = None
    for t in range(mult, min(n, pref) + 1, mult):
        if n % t == 0:
            best = t
    return n if best is None else best


def _full(arr):
    nd = arr.ndim
    return pl.BlockSpec(arr.shape, lambda *g: (0,) * nd)


def _logical(shape):
    return tuple(shape) if len(shape) == 2 else (shape[1], shape[0] * shape[2])


def _cspec(shape, tr, tc, rc):
    if len(shape) == 2:
        return pl.BlockSpec((tr, tc), rc)
    cpp = shape[2] // tc

    def imap(*g):
        r, c = rc(*g)
        return (c // cpp, r, c % cpp)

    return pl.BlockSpec((None, tr, tc), imap)


def _dot(a, b, dims, exact=False):
    if exact:
        return lax.dot_general(a, b, dims, precision=lax.Precision.HIGHEST, preferred_element_type=F32)
    return lax.dot_general(a.astype(BF16), b.astype(BF16), dims, preferred_element_type=F32)


def _silu(z):
    return z * jax.nn.sigmoid(z)


def _me():
    return lax.axis_index("x"), lax.axis_index("y"), lax.axis_index("c")


def _slot(p):
    return 4 * p[0] + 2 * p[1] + p[2]


def _scatter_direct(srcs, dsts, send_sems, recv_sems, local_sems):
    x, y, c = _me()
    me = (x, y, c)
    peers = [((x + (k >> 2)) % 2, (y + ((k >> 1) & 1)) % 2, (c + (k & 1)) % 2) for k in range(1, N_DEV)]

    def remote(a, k, peer, src_dev, dst_dev):
        return pltpu.make_async_remote_copy(src_ref=srcs[a].at[_slot(src_dev)], dst_ref=dsts[a].at[_slot(dst_dev)], send_sem=send_sems.at[a, k],
                                            recv_sem=recv_sems.at[a, k], device_id=peer, device_id_type=MESH)

    def local(a):
        return pltpu.make_async_copy(srcs[a].at[_slot(me)], dsts[a].at[_slot(me)], local_sems.at[a])

    def start():
        for a in range(len(srcs)):
            local(a).start()
            for k, peer in enumerate(peers):
                remote(a, k, peer, peer, me).start()

    def wait():
        for a in range(len(srcs)):
            for k, peer in enumerate(peers):
                remote(a, k, peer, me, peer).wait_recv()
        for a in range(len(srcs)):
            for k, peer in enumerate(peers):
                remote(a, k, peer, peer, me).wait_send()
            local(a).wait()

    return start, lambda: None, wait


def _gather_two_level(srcs, dsts, send_sems, recv_sems, local_sems):
    x, y, c = _me()
    me, sib = (x, y, c), (x, y, 1 - c)
    chips = [(1 - x, y), (x, 1 - y), (1 - x, 1 - y)]

    def copy(a, k, block, to, src=None):
        dst = dsts[a].at[_slot(block)]
        return pltpu.make_async_remote_copy(src_ref=dst if src is None else src, dst_ref=dst, send_sem=send_sems.at[a, k],
                                            recv_sem=recv_sems.at[a, k], device_id=to, device_id_type=MESH)

    def mine(a):
        return pltpu.make_async_copy(srcs[a], dsts[a].at[_slot(me)], local_sems.at[a])

    def first(a):
        return [copy(a, 0, me, sib, src=srcs[a])] + [copy(a, 1 + j, me, (*chip, c), src=srcs[a]) for j, chip in enumerate(chips)]

    def passed(a, j):
        return copy(a, 4 + j, (*chips[j], c), sib)

    def start():
        for a in range(len(srcs)):
            mine(a).start()
            for cp in first(a):
                cp.start()

    def pass_on():
        for j, chip in enumerate(chips):
            for a in range(len(srcs)):
                copy(a, 1 + j, (*chip, c), me).wait_recv()
                passed(a, j).start()

    def wait():
        for a in range(len(srcs)):
            copy(a, 0, sib, me).wait_recv()
            for j, chip in enumerate(chips):
                copy(a, 4 + j, (*chip, 1 - c), me).wait_recv()
        for a in range(len(srcs)):
            for cp in first(a) + [passed(a, j) for j in range(len(chips))]:
                cp.wait_send()
            mine(a).wait()

    return start, pass_on, wait


def _exchange_scratch(n):
    return [pltpu.SemaphoreType.DMA((n, N_DEV - 1)), pltpu.SemaphoreType.DMA((n, N_DEV - 1)), pltpu.SemaphoreType.DMA((n,))]


def _landing(kind, arrays):
    return [jax.ShapeDtypeStruct((N_DEV, *a.shape) if kind == "gather" else a.shape, a.dtype) for a in arrays]


def _call(body, name, grid, in_specs, out_specs, out_shape, ins, sem, scratch=(), carry=None):
    in_specs, out_specs, out_shape, scratch = list(in_specs), list(out_specs), list(out_shape), list(scratch)
    if carry is None:
        return list(pl.pallas_call(body, name=name, grid=grid, in_specs=in_specs, out_specs=out_specs, out_shape=out_shape,
                                   scratch_shapes=scratch, compiler_params=_params(*sem))(*ins))
    kind, arrays = carry
    n, ni, no, ns = len(arrays), len(in_specs), len(out_specs), len(scratch)

    def carrying(*refs):
        own_in, srcs = refs[:ni], refs[ni:ni + n]
        own_out, dsts = refs[ni + n:ni + n + no], refs[ni + n + no:ni + 2 * n + no]
        own_scratch, sems = refs[ni + 2 * n + no:ni + 2 * n + no + ns], refs[ni + 2 * n + no + ns:]
        start, pass_on, wait = (_gather_two_level if kind == "gather" else _scatter_direct)(srcs, dsts, *sems)
        step = functools.reduce(lambda lin, ax: lin * grid[ax] + pl.program_id(ax), range(len(grid)), 0)
        steps = math.prod(grid)
        pl.when(step == 0)(start)
        body(*own_in, *own_out, *own_scratch)
        if kind == "gather":
            pl.when(step == steps // 2)(pass_on)
        pl.when(step == steps - 1)(wait)

    res = pl.pallas_call(
        carrying, name=name, grid=grid, in_specs=in_specs + [HBM] * n, out_specs=out_specs + [HBM] * n,
        out_shape=out_shape + _landing(kind, arrays), scratch_shapes=scratch + _exchange_scratch(n),
        compiler_params=_params(*(["arbitrary"] * len(grid))))(*ins, *arrays)
    return list(res[:no]), list(res[no:])


def _mm(name, a, b, kind, out_shape, out_dtype, tiles, a_silu=False, exact=False, carry=None):
    la, lb, lo = _logical(a.shape), _logical(b.shape), _logical(out_shape)
    t0, t1, t2 = tiles
    if kind == "nn":
        grid = (lo[1] // t1, lo[0] // t0, la[1] // t2)
        a_spec = _cspec(a.shape, t0, t2, lambda j, i, k: (i, k))
        b_spec = _cspec(b.shape, t2, t1, lambda j, i, k: (k, j))
        o_spec = _cspec(out_shape, t0, t1, lambda j, i, k: (i, j))
        dims, acc = NN, (t0, t1)
    elif kind == "nt":
        grid = (lo[1] // t1, lo[0] // t0, la[1] // t2)
        a_spec = _cspec(a.shape, t0, t2, lambda p, i, r: (i, r))
        b_spec = _cspec(b.shape, t1, t2, lambda p, i, r: (p, r))
        o_spec = _cspec(out_shape, t0, t1, lambda p, i, r: (i, p))
        dims, acc = NT, (t0, t1)
    else:
        grid = (lo[1] // t1, lo[0] // t0, la[0] // t2)
        a_spec = _cspec(a.shape, t2, t0, lambda j, kk, r: (r, kk))
        b_spec = _cspec(b.shape, t2, t1, lambda j, kk, r: (r, j))
        o_spec = _cspec(out_shape, t0, t1, lambda j, kk, r: (kk, j))
        dims, acc = TN, (t0, t1)
    nk = grid[2]
    in_place = out_dtype == F32

    def body(a_ref, b_ref, o_ref, *scratch):
        acc_ref = o_ref if in_place else scratch[0]
        k = pl.program_id(2)
        av = a_ref[...]
        if a_silu:
            av = _silu(av)
        part = _dot(av, b_ref[...], dims, exact)

        @pl.when(k == 0)
        def _():
            acc_ref[...] = part

        @pl.when(k > 0)
        def _():
            acc_ref[...] += part

        if not in_place:
            @pl.when(k == nk - 1)
            def _():
                o_ref[...] = acc_ref[...].astype(o_ref.dtype)

    res = _call(body, name, grid, [a_spec, b_spec], [o_spec], [jax.ShapeDtypeStruct(out_shape, out_dtype)], (a, b),
                ("parallel", "parallel", "arbitrary"), scratch=[] if in_place else [pltpu.VMEM(acc, F32)], carry=carry)
    return res[0] if carry is None else (res[0][0], res[1])


MM_VMEM_BUDGET = 40 * 1024 * 1024
MM_TILE_CAP = 2048


def _divisors(n, mult):
    return [t for t in range(mult, min(n, MM_TILE_CAP) + 1, mult) if n % t == 0] or [n]


def _mm_tiles(c0, c1, c2, a, b, out_dtype):
    ia, ib, io = a.dtype.itemsize, b.dtype.itemsize, jnp.dtype(out_dtype).itemsize
    best, best_score = None, -1
    for t0 in c0:
        for t1 in c1:
            for t2 in c2:
                need = 2 * (t0 * t2 * ia + t1 * t2 * ib + t0 * t1 * io) + t0 * t1 * 4 * (1 if out_dtype == F32 else 2)
                score = (t0 * t1 * t2, t2)
                if need <= MM_VMEM_BUDGET and score > (best_score if best else (-1, -1)):
                    best, best_score = (t0, t1, t2), score
    return best if best else (c0[0], c1[0], c2[0])


def mm_nn(name, a, w, out_parts=1, out_dtype=F32, **kw):
    (m, _), (_, n) = _logical(a.shape), _logical(w.shape)
    out_shape = (m, n) if out_parts == 1 else (out_parts, m, n // out_parts)
    tiles = _mm_tiles(_divisors(m, 16), _divisors(math.gcd(w.shape[-1], out_shape[-1]), LANE),
                      _divisors(math.gcd(a.shape[-1], w.shape[-2]), LANE), a, w, out_dtype)
    return _mm(name, a, w, "nn", out_shape, out_dtype, tiles, **kw)


def mm_nt(name, a, w, out_dtype=F32, **kw):
    (m, _), (p, _) = _logical(a.shape), _logical(w.shape)
    tiles = _mm_tiles(_divisors(m, 16), _divisors(w.shape[-2], LANE), _divisors(math.gcd(a.shape[-1], w.shape[-1]), LANE), a, w, out_dtype)
    return _mm(name, a, w, "nt", (m, p), out_dtype, tiles, **kw)


def mm_tn(name, a, dy, out_parts=1, out_dtype=BF16, **kw):
    (t, k), (_, n) = _logical(a.shape), _logical(dy.shape)
    out_shape = (k, n) if out_parts == 1 else (out_parts, k, n // out_parts)
    tiles = _mm_tiles(_divisors(a.shape[-1], LANE), _divisors(math.gcd(dy.shape[-1], out_shape[-1]), LANE), _divisors(t, 16), a, dy, out_dtype)
    return _mm(name, a, dy, "tn", out_shape, out_dtype, tiles, **kw)


def _rows(tr, d):
    return pl.BlockSpec((tr, d), lambda i: (i, 0))


def _pick(ctx, ref):
    return jnp.where(ctx, ref[1:2, :], ref[0:1, :])


def resid_rms_mod(name, x, y, gate, g, sc, sh, n_lat, tr=256):
    t, d = x.shape
    nlt = n_lat // tr
    has_res = y is not None

    def body(*refs):
        if has_res:
            x_ref, y_ref, gate_ref, g_ref, sc_ref, sh_ref, x1_ref, h_ref = refs
        else:
            x_ref, g_ref, sc_ref, sh_ref, h_ref = refs
        ctx = pl.program_id(0) >= nlt
        xv = x_ref[...]
        if has_res:
            xv = xv + _pick(ctx, gate_ref) * y_ref[...]
            x1_ref[...] = xv
        r = lax.rsqrt(jnp.mean(xv * xv, axis=-1, keepdims=True) + EPS)
        n = xv * r * g_ref[...]
        h_ref[...] = (n * (1.0 + _pick(ctx, sc_ref)) + _pick(ctx, sh_ref)).astype(h_ref.dtype)

    row = _rows(tr, d)
    if has_res:
        ins, in_specs = (x, y, gate, g, sc, sh), [row, row, _full(gate), _full(g), _full(sc), _full(sh)]
        out_shape = (jax.ShapeDtypeStruct((t, d), F32), jax.ShapeDtypeStruct((t, d), BF16))
        out_specs = (row, row)
    else:
        ins, in_specs = (x, g, sc, sh), [row, _full(g), _full(sc), _full(sh)]
        out_shape = jax.ShapeDtypeStruct((t, d), BF16)
        out_specs = row
    return pl.pallas_call(body, name=name, grid=(t // tr,), in_specs=in_specs, out_specs=out_specs,
                          out_shape=out_shape, compiler_params=_params("parallel"))(*ins)


def _acc_rows(i, nlt, ref, val):
    @pl.when(i == 0)
    def _():
        ref[...] = jnp.zeros_like(ref)

    @pl.when(i < nlt)
    def _():
        ref[0:1, :] += val

    @pl.when(i >= nlt)
    def _():
        ref[1:2, :] += val


def rms_mod_bwd(name, x, dh, g, sc, dres, n_lat, branch=None, tr=256, carry=None):
    t, d = x.shape
    nlt = n_lat // tr

    def body(x_ref, dh_ref, g_ref, sc_ref, dres_ref, *rest):
        if branch is None:
            dx_ref, dsh_ref, dsc_ref, dg_ref = rest
        else:
            y_ref, gate_ref, dx_ref, dsh_ref, dsc_ref, dg_ref, dy_ref, dgate_ref = rest
        i = pl.program_id(0)
        xv, dhv, gv = x_ref[...], dh_ref[...], g_ref[...]
        r = lax.rsqrt(jnp.mean(xv * xv, axis=-1, keepdims=True) + EPS)
        xhat = xv * r
        dn = dhv * (1.0 + _pick(i >= nlt, sc_ref))
        dxhat = dn * gv
        dxv = r * (dxhat - xhat * jnp.mean(dxhat * xhat, axis=-1, keepdims=True)) + dres_ref[...]
        dx_ref[...] = dxv
        if branch is not None:
            dy_ref[...] = (_pick(i >= nlt, gate_ref) * dxv).astype(dy_ref.dtype)
            _acc_rows(i, nlt, dgate_ref, jnp.sum(dxv * y_ref[...], axis=0, keepdims=True))
        _acc_rows(i, nlt, dsh_ref, jnp.sum(dhv, axis=0, keepdims=True))
        _acc_rows(i, nlt, dsc_ref, jnp.sum(dhv * (xhat * gv), axis=0, keepdims=True))
        dgp = jnp.sum(dn * xhat, axis=0, keepdims=True)

        @pl.when(i == 0)
        def _():
            dg_ref[...] = dgp

        @pl.when(i > 0)
        def _():
            dg_ref[...] += dgp

    row = _rows(tr, d)
    two = pl.BlockSpec((2, d), lambda i: (0, 0))
    two_shape = jax.ShapeDtypeStruct((2, d), F32)
    ins, in_specs = [x, dh, g, sc, dres], [row, row, _full(g), _full(sc), row]
    out_specs = [row, two, two, pl.BlockSpec((1, d), lambda i: (0, 0))]
    out_shape = [jax.ShapeDtypeStruct((t, d), F32), two_shape, two_shape, jax.ShapeDtypeStruct((1, d), F32)]
    if branch is not None:
        ins, in_specs = ins + list(branch), in_specs + [row, _full(branch[1])]
        out_specs, out_shape = out_specs + [row, two], out_shape + [jax.ShapeDtypeStruct((t, d), BF16), two_shape]
    res = _call(body, name, (t // tr,), in_specs, out_specs, out_shape, ins, ("arbitrary",), carry=carry)
    return tuple(res) if carry is None else (*res[0], res[1])


def loss_head(name, x1, f, gate, target, tr=256):
    t, d = x1.shape
    nlt = target.shape[0] // tr

    def body(x_ref, f_ref, gate_ref, t_ref, dx_ref, loss_ref, df_ref, dgate_ref, acc_ref):
        i = pl.program_id(0)

        @pl.when(i == 0)
        def _():
            acc_ref[...] = jnp.zeros_like(acc_ref)
            dgate_ref[...] = jnp.zeros_like(dgate_ref)

        @pl.when(i < nlt)
        def _():
            fv, gv = f_ref[...], gate_ref[0:1, :]
            e = x_ref[...] + gv * fv - t_ref[...]
            dxv = e / d
            dx_ref[...] = dxv
            df_ref[...] = (gv * dxv).astype(df_ref.dtype)
            dgate_ref[0:1, :] += jnp.sum(dxv * fv, axis=0, keepdims=True)
            acc_ref[...] += jnp.sum(e * e, axis=0, keepdims=True)

        @pl.when(i >= nlt)
        def _():
            dx_ref[...] = jnp.zeros_like(dx_ref)
            df_ref[...] = jnp.zeros_like(df_ref)

        @pl.when(i == t // tr - 1)
        def _():
            loss_ref[...] = jnp.sum(acc_ref[...], axis=1, keepdims=True) * (0.5 / d)

    row = _rows(tr, d)
    return pl.pallas_call(
        body, name=name, grid=(t // tr,),
        in_specs=[row, row, _full(gate), pl.BlockSpec((tr, d), lambda i: (jnp.minimum(i, nlt - 1), 0))],
        out_specs=(row, pl.BlockSpec((1, 1), lambda i: (0, 0)), row, pl.BlockSpec((2, d), lambda i: (0, 0))),
        out_shape=(jax.ShapeDtypeStruct((t, d), F32), jax.ShapeDtypeStruct((1, 1), F32),
                   jax.ShapeDtypeStruct((t, d), BF16), jax.ShapeDtypeStruct((2, d), F32)),
        scratch_shapes=[pltpu.VMEM((1, d), F32)],
        compiler_params=_params("arbitrary"))(x1, f, gate, target)


HALO = 8


def _halo_specs(shape, tr, tc, col):
    hb = tr // HALO
    last = _logical(shape)[0] // HALO - 1
    main = _cspec(shape, tr, tc, lambda j, i: (i, col(j)))
    prev = _cspec(shape, HALO, tc, lambda j, i: (jnp.maximum(i * hb - 1, 0), col(j)))
    nxt = _cspec(shape, HALO, tc, lambda j, i: (jnp.minimum((i + 1) * hb, last), col(j)))
    return [prev, main, nxt]


def _seq_edges(i, nlt, nt):
    first = (i == 0) | (i == nlt)
    last = (i == nlt - 1) | (i == nt - 1)
    return first, last


def _ext(prev_ref, main_ref, next_ref, first, last):
    p = jnp.where(first, 0.0, prev_ref[...].astype(F32))
    n = jnp.where(last, 0.0, next_ref[...].astype(F32))
    return jnp.concatenate([p, main_ref[...].astype(F32), n], axis=0)


def _up(e):
    return pltpu.roll(e, 1, 0)


def _down(e):
    return pltpu.roll(e, e.shape[0] - 1, 0)


def _conv(e, w):
    return _up(e) * w[0:1, :] + e * w[1:2, :] + _down(e) * w[2:3, :]


def _conv_t(e, w):
    return _down(e) * w[0:1, :] + e * w[1:2, :] + _up(e) * w[2:3, :]


def _mid(e, tr):
    return e[HALO:HALO + tr, :]


def _acc_cols(i, ref, val):
    @pl.when(i == 0)
    def _():
        ref[...] = val

    @pl.when(i > 0)
    def _():
        ref[...] += val


def _colsum(v):
    return jnp.sum(v, axis=0, keepdims=True)


def ffn_act_fwd(name, u, cw, cb, n_lat, tr=256, carry=None):
    _, t, fp = u.shape
    tc = _tile(fp, 1536, LANE)
    nlt, nt = n_lat // tr, t // tr

    def body(pg, mg, ng, pu, mu, nu, cw_ref, cb_ref, z_ref, a_ref):
        first, last = _seq_edges(pl.program_id(1), nlt, nt)
        zg = _mid(_conv(_ext(pg, mg, ng, first, last), cw_ref[0]), tr) + cb_ref[0]
        zu = _mid(_conv(_ext(pu, mu, nu, first, last), cw_ref[1]), tr) + cb_ref[1]
        z_ref[0] = zg
        z_ref[1] = zu
        a_ref[...] = (_silu(zg) * zu).astype(a_ref.dtype)

    ncol = fp // tc
    specs = _halo_specs(u.shape, tr, tc, lambda j: j) + _halo_specs(u.shape, tr, tc, lambda j: j + ncol)
    specs += [pl.BlockSpec((2, 3, tc), lambda j, i: (0, 0, j)), pl.BlockSpec((2, 1, tc), lambda j, i: (0, 0, j))]
    res = _call(body, name, (ncol, nt), specs,
                [pl.BlockSpec((2, tr, tc), lambda j, i: (0, i, j)), pl.BlockSpec((tr, tc), lambda j, i: (i, j))],
                [jax.ShapeDtypeStruct((2, t, fp), F32), jax.ShapeDtypeStruct((t, fp), BF16)],
                (u, u, u, u, u, u, cw, cb), ("parallel", "parallel"), carry=carry)
    return tuple(res) if carry is None else (*res[0], res[1])


def ffn_act_bwd(name, z, u, da, cw, n_lat, tr=128, carry=None):
    _, t, fp = u.shape
    tc = _tile(fp, 1536, LANE)
    nlt, nt = n_lat // tr, t // tr
    ncol = fp // tc

    def body(pg, mg, ng, pu, mu, nu, u_ref, pa, ma, na, cw_ref, du_ref, dcw_ref, dcb_ref):
        i = pl.program_id(1)
        first, last = _seq_edges(i, nlt, nt)
        zg, zu = _ext(pg, mg, ng, first, last), _ext(pu, mu, nu, first, last)
        dae = _ext(pa, ma, na, first, last)
        sg = jax.nn.sigmoid(zg)
        dzs = (dae * zu * (sg * (1.0 + zg * (1.0 - sg))), dae * (zg * sg))
        for h, dz in enumerate(dzs):
            w, um = cw_ref[h], u_ref[h]
            after, before = _down(dz), _up(dz)
            du_ref[h] = _mid(after * w[0:1, :] + dz * w[1:2, :] + before * w[2:3, :], tr).astype(du_ref.dtype)
            rows = [_colsum(_mid(after, tr) * um), _colsum(_mid(dz, tr) * um), _colsum(_mid(before, tr) * um)]
            _acc_cols(i, dcw_ref.at[h], jnp.concatenate(rows, axis=0))
            _acc_cols(i, dcb_ref.at[h], _colsum(_mid(dz, tr)))

    both = pl.BlockSpec((2, tr, tc), lambda j, i: (0, i, j))
    taps = pl.BlockSpec((2, 3, tc), lambda j, i: (0, 0, j))
    specs = _halo_specs(z.shape, tr, tc, lambda j: j) + _halo_specs(z.shape, tr, tc, lambda j: j + ncol) + [both]
    specs += _halo_specs(da.shape, tr, tc, lambda j: j) + [taps]
    res = _call(body, name, (ncol, nt), specs, [both, taps, pl.BlockSpec((2, 1, tc), lambda j, i: (0, 0, j))],
                [jax.ShapeDtypeStruct((2, t, fp), BF16), jax.ShapeDtypeStruct((2, 3, fp), F32), jax.ShapeDtypeStruct((2, 1, fp), F32)],
                (z, z, z, z, z, z, u, da, da, da, cw), ("parallel", "arbitrary"), carry=carry)
    return tuple(res) if carry is None else (*res[0], res[1])


def sc_gate_fwd(name, tmat, cw, n_lat, tr=256):
    t, d3 = tmat.shape
    d = d3 // 3
    tc = _tile(d, 512, LANE)
    ncol = d // tc
    nlt, nt = n_lat // tr, t // tr

    def body(b_ref, pc, mc, nc, px, mx, nx, cw_ref, s_ref):
        first, last = _seq_edges(pl.program_id(1), nlt, nt)
        p = _ext(pc, mc, nc, first, last) * _ext(px, mx, nx, first, last)
        s_ref[...] = (b_ref[...] * _mid(_conv(p, cw_ref[...]), tr)).astype(s_ref.dtype)

    specs = [pl.BlockSpec((tr, tc), lambda j, i: (i, j))]
    specs += _halo_specs(tmat.shape, tr, tc, lambda j: j + ncol) + _halo_specs(tmat.shape, tr, tc, lambda j: j + 2 * ncol)
    specs += [pl.BlockSpec((3, tc), lambda j, i: (0, j))]
    return pl.pallas_call(
        body, name=name, grid=(ncol, nt), in_specs=specs, out_specs=pl.BlockSpec((tr, tc), lambda j, i: (i, j)),
        out_shape=jax.ShapeDtypeStruct((t, d), BF16),
        compiler_params=_params("parallel", "parallel"))(*([tmat] * 7), cw)


def sc_gate_bwd(name, tmat, ds, cw, n_lat, tr=128):
    t, d3 = tmat.shape
    d = d3 // 3
    tc = _tile(d, 512, LANE)
    ncol = d // tc
    nlt, nt = n_lat // tr, t // tr

    def body(pb, mb, nb, pc, mc, nc, px, mx, nx, pd, md, nd, cw_ref, dt_ref, dcw_ref):
        i = pl.program_id(1)
        first, last = _seq_edges(i, nlt, nt)
        w = cw_ref[...]
        be, ce, xe = _ext(pb, mb, nb, first, last), _ext(pc, mc, nc, first, last), _ext(px, mx, nx, first, last)
        dse = _ext(pd, md, nd, first, last)
        p = ce * xe
        dcv = dse * be
        dp = _conv_t(dcv, w)
        dt_ref[0] = _mid(dse * _conv(p, w), tr).astype(dt_ref.dtype)
        dt_ref[1] = _mid(dp * xe, tr).astype(dt_ref.dtype)
        dt_ref[2] = _mid(dp * ce, tr).astype(dt_ref.dtype)
        dm = _mid(dcv, tr)
        rows = [_colsum(dm * _mid(_up(p), tr)), _colsum(dm * _mid(p, tr)), _colsum(dm * _mid(_down(p), tr))]
        _acc_cols(i, dcw_ref, jnp.concatenate(rows, axis=0))

    specs = []
    for part in range(3):
        specs += _halo_specs(tmat.shape, tr, tc, functools.partial(lambda j, part: j + part * ncol, part=part))
    specs += _halo_specs(ds.shape, tr, tc, lambda j: j)
    specs += [pl.BlockSpec((3, tc), lambda j, i: (0, j))]
    return pl.pallas_call(
        body, name=name, grid=(ncol, nt), in_specs=specs,
        out_specs=(pl.BlockSpec((3, tr, tc), lambda j, i: (0, i, j)), pl.BlockSpec((3, tc), lambda j, i: (0, j))),
        out_shape=(jax.ShapeDtypeStruct((3, t, d), BF16), jax.ShapeDtypeStruct((3, d), F32)),
        compiler_params=_params("parallel", "arbitrary"))(*([tmat] * 9), ds, ds, ds, cw)


_GELU_K = 0.7978845608028654
_GELU_C = 0.044715


def _gelu(x):
    return 0.5 * x * (1.0 + jnp.tanh(_GELU_K * (x + _GELU_C * (x * x * x))))


def _gelu_grad(x):
    th = jnp.tanh(_GELU_K * (x + _GELU_C * (x * x * x)))
    return 0.5 * (1.0 + th) + 0.5 * x * (1.0 - th * th) * (_GELU_K * (1.0 + 3.0 * _GELU_C * (x * x)))


def gmlp_gate_fwd(name, tmat, vg, ws, bs):
    t, w2 = tmat.shape
    w = w2 // 2
    groups = ws.shape[0]
    gd = w // groups

    def body(t_ref, vg_ref, ws_ref, bs_ref, o_ref):
        v = _gelu(t_ref[:, w:])
        r = lax.rsqrt(jnp.mean(v * v, axis=-1, keepdims=True) + EPS)
        vn = (v * r * vg_ref[...]).astype(BF16)
        for g in range(groups):
            cols = slice(g * gd, (g + 1) * gd)
            sv = _dot(ws_ref[g], vn[:, cols], NN) + bs_ref[g]
            o_ref[:, cols] = (_gelu(t_ref[:, cols]) * sv).astype(o_ref.dtype)

    return pl.pallas_call(
        body, name=name, grid=(t // GM_CHUNK,),
        in_specs=[_rows(GM_CHUNK, w2), _full(vg), _full(ws), _full(bs)], out_specs=_rows(GM_CHUNK, w),
        out_shape=jax.ShapeDtypeStruct((t, w), BF16), compiler_params=_params("parallel"))(tmat, vg, ws, bs)


def gmlp_gate_bwd(name, tmat, dout, vg, ws, bs):
    t, w2 = tmat.shape
    w = w2 // 2
    groups = ws.shape[0]
    gd = w // groups

    def body(t_ref, do_ref, vg_ref, ws_ref, bs_ref, dt_ref, dvg_ref, dws_ref, dsv_ref, dvn_ref):
        i = pl.program_id(0)
        tv = t_ref[:, w:]
        v = _gelu(tv)
        r = lax.rsqrt(jnp.mean(v * v, axis=-1, keepdims=True) + EPS)
        vhat = v * r
        vn = (vhat * vg_ref[...]).astype(BF16)
        for g in range(groups):
            cols = slice(g * gd, (g + 1) * gd)
            tu = t_ref[:, cols]
            dov = do_ref[:, cols]
            sv = _dot(ws_ref[g], vn[:, cols], NN) + bs_ref[g]
            dt_ref[:, cols] = (dov * sv * _gelu_grad(tu)).astype(dt_ref.dtype)
            dsv = dov * _gelu(tu)
            _acc_cols(i, dsv_ref.at[:, cols], dsv)
            _acc_cols(i, dws_ref.at[g], _dot(dsv, vn[:, cols], NT))
            dvn_ref[:, cols] = _dot(ws_ref[g], dsv, TN)
        dvn = dvn_ref[...]
        _acc_cols(i, dvg_ref, _colsum(dvn * vhat))
        dvhat = dvn * vg_ref[...]
        dv = r * (dvhat - vhat * jnp.mean(dvhat * vhat, axis=-1, keepdims=True))
        dt_ref[:, w:] = (dv * _gelu_grad(tv)).astype(dt_ref.dtype)

    keep = lambda shape: pl.BlockSpec(shape, lambda i: (0,) * len(shape))
    return pl.pallas_call(
        body, name=name, grid=(t // GM_CHUNK,),
        in_specs=[_rows(GM_CHUNK, w2), _rows(GM_CHUNK, w), _full(vg), _full(ws), _full(bs)],
        out_specs=(_rows(GM_CHUNK, w2), keep((1, w)), keep(ws.shape), keep((GM_CHUNK, w))),
        out_shape=(jax.ShapeDtypeStruct((t, w2), BF16), jax.ShapeDtypeStruct((1, w), F32),
                   jax.ShapeDtypeStruct(ws.shape, F32), jax.ShapeDtypeStruct((GM_CHUNK, w), F32)),
        scratch_shapes=[pltpu.VMEM((GM_CHUNK, w), F32)],
        compiler_params=_params("arbitrary"))(tmat, dout, vg, ws, bs)


def qk_norm_fwd(name, qkv, qg, kg, hd, tr=128):
    t, d3 = qkv.shape
    d = d3 // 3

    def body(x_ref, qg_ref, kg_ref, q_ref, k_ref, v_ref):
        for part, (g_ref, o_ref) in enumerate(((qg_ref, q_ref), (kg_ref, k_ref))):
            for h in range(d // hd):
                xh = x_ref[:, part * d + h * hd: part * d + (h + 1) * hd]
                r = lax.rsqrt(jnp.mean(xh * xh, axis=-1, keepdims=True) + EPS)
                o_ref[:, h * hd:(h + 1) * hd] = (xh * r * g_ref[...]).astype(o_ref.dtype)
        v_ref[...] = x_ref[:, 2 * d:].astype(v_ref.dtype)

    out = jax.ShapeDtypeStruct((t, d), BF16)
    return pl.pallas_call(
        body, name=name, grid=(t // tr,), in_specs=[_rows(tr, d3), _full(qg), _full(kg)],
        out_specs=(_rows(tr, d),) * 3, out_shape=(out,) * 3, compiler_params=_params("parallel"))(qkv, qg, kg)


def qk_norm_bwd(name, qkv, dq, dk, dv, qg, kg, hd, tr=128):
    t, d3 = qkv.shape
    d = d3 // 3

    def body(x_ref, dq_ref, dk_ref, dv_ref, qg_ref, kg_ref, o_ref, dqg_ref, dkg_ref):
        i = pl.program_id(0)
        for part, (g_ref, dn_ref, dg_ref) in enumerate(((qg_ref, dq_ref, dqg_ref), (kg_ref, dk_ref, dkg_ref))):
            dg = jnp.zeros((1, hd), F32)
            for h in range(d // hd):
                xh = x_ref[:, part * d + h * hd: part * d + (h + 1) * hd]
                dn = dn_ref[:, h * hd:(h + 1) * hd]
                r = lax.rsqrt(jnp.mean(xh * xh, axis=-1, keepdims=True) + EPS)
                xhat = xh * r
                dg = dg + _colsum(dn * xhat)
                dxhat = dn * g_ref[...]
                dx = r * (dxhat - xhat * jnp.mean(dxhat * xhat, axis=-1, keepdims=True))
                o_ref[:, part * d + h * hd: part * d + (h + 1) * hd] = dx.astype(o_ref.dtype)
            _acc_cols(i, dg_ref, dg)
        o_ref[:, 2 * d:] = dv_ref[...].astype(o_ref.dtype)

    one = pl.BlockSpec((1, hd), lambda i: (0, 0))
    return pl.pallas_call(
        body, name=name, grid=(t // tr,),
        in_specs=[_rows(tr, d3), _rows(tr, d), _rows(tr, d), _rows(tr, d), _full(qg), _full(kg)],
        out_specs=(_rows(tr, d3), one, one),
        out_shape=(jax.ShapeDtypeStruct((t, d3), BF16), jax.ShapeDtypeStruct((1, hd), F32), jax.ShapeDtypeStruct((1, hd), F32)),
        compiler_params=_params("arbitrary"))(qkv, dq, dk, dv, qg, kg)


def _na_geometry(n_lat):
    rows = n_lat // GRID_W
    kh = min(WIN_H, rows)
    nb = min(kh + QROWS - 1, rows)
    n_blk = rows // QROWS
    q_row_off = np.repeat(np.arange(QROWS), GRID_W)
    q_col = np.tile(np.arange(GRID_W), QROWS)
    k_row_off = np.repeat(np.arange(nb), GRID_W)
    k_col = np.tile(np.arange(GRID_W), nb)
    c_start = np.clip(q_col - WIN_W // 2, 0, GRID_W - WIN_W)
    col_ok = (k_col[None, :] >= c_start[:, None]) & (k_col[None, :] < c_start[:, None] + WIN_W)
    dc_idx = np.clip(k_col[None, :] - q_col[:, None], -(WIN_W - 1), WIN_W - 1) + WIN_W - 1

    def block(blk):
        r0 = blk * QROWS
        q_row = r0 + q_row_off
        r_start = np.clip(q_row - kh // 2, 0, rows - kh)
        band0 = min(int(np.clip(r0 - kh // 2, 0, rows - kh)), rows - nb)
        k_row = band0 + k_row_off
        ok = col_ok & (k_row[None, :] >= r_start[:, None]) & (k_row[None, :] < r_start[:, None] + kh)
        dr_idx = np.clip(k_row[None, :] - q_row[:, None], -(WIN_H - 1), WIN_H - 1) + WIN_H - 1
        return band0, ok, dr_idx

    reps = [0, 1, 2, n_blk - 2, n_blk - 1]
    variant = lambda blk: 0 if blk == 0 else 1 if blk == 1 else 3 if blk == n_blk - 2 else 4 if blk == n_blk - 1 else 2
    geo = [block(b) for b in reps]
    for blk in range(n_blk):
        _, ok, dr = block(blk)
        assert np.array_equal(ok, geo[variant(blk)][1]) and np.array_equal(np.where(ok, dr, 0), np.where(ok, geo[variant(blk)][2], 0))
    return dict(rows=rows, kh=kh, nb=nb, n_blk=n_blk, reps=reps, ok=[g[1] for g in geo], dr=[g[2] for g in geo],
                band0=[g[0] for g in geo], dc=dc_idx)


def _na_onehots(geo):
    nb = geo["nb"]
    w2 = GRID_W * GRID_W
    qc, kc = np.meshgrid(np.arange(GRID_W), np.arange(GRID_W), indexing="ij")
    diff = (kc - qc).reshape(-1)
    cols = np.zeros((w2, LANE), np.float32)
    sel = np.abs(diff) <= WIN_W - 1
    cols[np.arange(w2)[sel], diff[sel] + WIN_W - 1] = 1.0
    npair = len(geo["reps"]) * QROWS * nb
    kpad = -(-npair // LANE) * LANE
    rows = np.zeros((16, kpad), np.float32)
    for vi, blk in enumerate(geo["reps"]):
        for qr in range(QROWS):
            for kr in range(nb):
                dr = (geo["band0"][vi] + kr) - (blk * QROWS + qr)
                if abs(dr) <= WIN_H - 1:
                    rows[dr + WIN_H - 1, (vi * QROWS + qr) * nb + kr] = 1.0
    return cols, rows, npair, kpad


def na_bias_table(name, rpb, geo):
    n_heads, nb, nv = rpb.shape[0], geo["nb"], len(geo["reps"])
    cols, rows, npair, kpad = _na_onehots(geo)
    rpb_p = jnp.pad(rpb, ((0, 0), (0, 16 - rpb.shape[1]), (0, LANE - rpb.shape[2]))).transpose(1, 0, 2).reshape(16, n_heads * LANE)
    t1 = _mm(name + "_rows", jnp.asarray(rows.T), rpb_p, "nn", (kpad, n_heads * LANE), F32,
             (kpad, _tile(n_heads * LANE, 1024, LANE), 16), exact=True)
    t1 = t1[:npair].reshape(nv, QROWS * nb, n_heads, LANE).transpose(0, 2, 1, 3).reshape(nv * n_heads * QROWS * nb, LANE)
    m = t1.shape[0]
    flat = _mm(name + "_cols", t1, jnp.asarray(cols), "nt", (m, GRID_W * GRID_W), F32,
               (_tile(m, 512, 8), _tile(GRID_W * GRID_W, 2048, LANE), LANE), exact=True)
    tab = flat.reshape(nv, n_heads, QROWS, nb, GRID_W, GRID_W).transpose(0, 1, 2, 4, 3, 5).reshape(nv, n_heads, QROWS * GRID_W, nb * GRID_W)
    return jnp.where(jnp.asarray(np.stack(geo["ok"]))[:, None], tab, NEG)


def _na_tile_info(qt, geo):
    n_blk, rows, kh, nb = geo["n_blk"], geo["rows"], geo["kh"], geo["nb"]
    is_ctx = qt >= n_blk
    band0 = jnp.minimum(jnp.clip(qt * QROWS - kh // 2, 0, rows - kh), rows - nb)
    band0 = jnp.where(is_ctx, 0, band0)
    return is_ctx, pl.multiple_of(band0 * GRID_W, GRID_W)


def _na_variant(qt, n_blk):
    v = jnp.minimum(qt, 2) + (qt >= n_blk - 2).astype(jnp.int32) + (qt >= n_blk - 1).astype(jnp.int32)
    return jnp.minimum(v, 4)


def _na_probs(q, kb, kc, bias, is_ctx, scale):
    s_lat = _dot(q, kb, NT) * scale + bias
    s_lat = jnp.where(is_ctx, NEG, s_lat)
    s_ctx = _dot(q, kc, NT) * scale
    m = jnp.maximum(jnp.max(s_lat, axis=-1, keepdims=True), jnp.max(s_ctx, axis=-1, keepdims=True))
    e_lat, e_ctx = jnp.exp(s_lat - m), jnp.exp(s_ctx - m)
    den = jnp.sum(e_lat, axis=-1, keepdims=True) + jnp.sum(e_ctx, axis=-1, keepdims=True)
    return e_lat / den, e_ctx / den


NA_MAX_TILES_PER_STEP = 6


def _na_step_specs(t, qw, nk, hd, n_blk):
    per = max(n for n in range(1, NA_MAX_TILES_PER_STEP + 1) if (t // qw) % n == 0)
    tile = pl.BlockSpec((per * qw, hd), lambda h, i: (i, h))
    btabs = [pl.BlockSpec((None, None, qw, nk), functools.partial(lambda h, i, part: (_na_variant(per * i + part, n_blk), h, 0, 0), part=part))
             for part in range(per)]
    return per, tile, btabs


def na_attention_fwd(name, q, k, v, bias, geo, n_lat, hd, carry=None):
    t, d = q.shape
    qw, nk = QROWS * GRID_W, geo["nb"] * GRID_W
    scale = hd ** -0.5
    per, tile, btabs = _na_step_specs(t, qw, nk, hd, geo["n_blk"])

    def body(q_ref, k_ref, v_ref, *rest):
        b_refs, o_ref = rest[:per], rest[per]
        kc, vc = k_ref[n_lat:, :], v_ref[n_lat:, :]
        for part, b_ref in enumerate(b_refs):
            rows = slice(part * qw, (part + 1) * qw)
            is_ctx, start = _na_tile_info(per * pl.program_id(1) + part, geo)
            kb, vb = k_ref[pl.ds(start, nk), :], v_ref[pl.ds(start, nk), :]
            p_lat, p_ctx = _na_probs(q_ref[rows, :], kb, kc, b_ref[...], is_ctx, scale)
            o_ref[rows, :] = (_dot(p_lat, vb, NN) + _dot(p_ctx, vc, NN)).astype(o_ref.dtype)

    head = pl.BlockSpec((t, hd), lambda h, i: (0, h))
    res = _call(body, name, (d // hd, t // (per * qw)), [tile, head, head, *btabs], [tile],
                [jax.ShapeDtypeStruct((t, d), BF16)], (q, k, v, *([bias] * per)), ("parallel", "arbitrary"), carry=carry)
    return res[0] if carry is None else (res[0][0], res[1])


def na_attention_bwd(name, q, k, v, bias, do, geo, n_lat, hd, carry=None):
    t, d = q.shape
    qw, nk = QROWS * GRID_W, geo["nb"] * GRID_W
    n_blk = geo["n_blk"]
    scale = hd ** -0.5
    per, tile, btabs = _na_step_specs(t, qw, nk, hd, n_blk)

    def body(q_ref, k_ref, v_ref, *rest):
        b_refs, (do_ref, dq_ref, dk_ref, dv_ref, db_ref) = rest[:per], rest[per:]
        step = pl.program_id(1)

        @pl.when(step == 0)
        def _():
            dk_ref[...] = jnp.zeros_like(dk_ref)
            dv_ref[...] = jnp.zeros_like(dv_ref)
            db_ref[...] = jnp.zeros_like(db_ref)

        kc, vc = k_ref[n_lat:, :], v_ref[n_lat:, :]
        for part, b_ref in enumerate(b_refs):
            qt = per * step + part
            rows = slice(part * qw, (part + 1) * qw)
            is_ctx, start = _na_tile_info(qt, geo)
            band = pl.ds(start, nk)
            qv, dov = q_ref[rows, :], do_ref[rows, :]
            kb, vb = k_ref[band, :], v_ref[band, :]
            p_lat, p_ctx = _na_probs(qv, kb, kc, b_ref[...], is_ctx, scale)
            dp_lat, dp_ctx = _dot(dov, vb, NT), _dot(dov, vc, NT)
            delta = jnp.sum(p_lat * dp_lat, axis=-1, keepdims=True) + jnp.sum(p_ctx * dp_ctx, axis=-1, keepdims=True)
            ds_lat, ds_ctx = p_lat * (dp_lat - delta), p_ctx * (dp_ctx - delta)
            db_ref[_na_variant(qt, n_blk)] += ds_lat
            dsl, dsc = (ds_lat * scale).astype(BF16), (ds_ctx * scale).astype(BF16)
            dq_ref[rows, :] = _dot(dsl, kb, NN) + _dot(dsc, kc, NN)
            dk_ref[band, :] += _dot(dsl, qv, TN)
            dk_ref[n_lat:, :] += _dot(dsc, qv, TN)
            dv_ref[band, :] += _dot(p_lat, dov, TN)
            dv_ref[n_lat:, :] += _dot(p_ctx, dov, TN)

    nv = bias.shape[0]
    head = pl.BlockSpec((t, hd), lambda h, i: (0, h))
    full = jax.ShapeDtypeStruct((t, d), F32)
    res = _call(body, name, (d // hd, t // (per * qw)), [tile, head, head, *btabs, tile],
                [tile, head, head, pl.BlockSpec((nv, None, qw, nk), lambda h, i: (0, h, 0, 0))],
                [full, full, full, jax.ShapeDtypeStruct(bias.shape, F32)], (q, k, v, *([bias] * per), do), ("arbitrary", "arbitrary"), carry=carry)
    return tuple(res) if carry is None else (*res[0], res[1])


def na_rpb_grad(name, dbias, geo, n_heads):
    nb = geo["nb"]
    nv = len(geo["reps"])
    w2 = GRID_W * GRID_W
    cols, rows, npair, kpad = _na_onehots(geo)
    xmat = dbias.reshape(nv, n_heads, QROWS, GRID_W, nb, GRID_W).transpose(0, 1, 2, 4, 3, 5).reshape(nv * n_heads * QROWS * nb, w2)
    m = xmat.shape[0]
    r = _mm(name + "_cols", xmat, jnp.asarray(cols), "nn", (m, LANE), F32, (_tile(m, 512, 8), LANE, _tile(w2, 1024, LANE)), exact=True)
    r2 = r.reshape(nv, n_heads, QROWS * nb, LANE).transpose(0, 2, 1, 3).reshape(npair, n_heads * LANE)
    r2 = jnp.pad(r2, ((0, kpad - npair), (0, 0)))
    out = _mm(name + "_rows", jnp.asarray(rows), r2, "nn", (16, n_heads * LANE), F32, (16, _tile(n_heads * LANE, 1024, LANE), kpad), exact=True)
    return out[:2 * WIN_H - 1].reshape(2 * WIN_H - 1, n_heads, LANE)[:, :, :2 * WIN_W - 1].transpose(1, 0, 2)


def all_gather(name, arrays):
    n = len(arrays)

    def body(*refs):
        for phase in _gather_two_level(refs[:n], refs[n:2 * n], *refs[2 * n:]):
            phase()

    outs = pl.pallas_call(body, name=name, in_specs=[HBM] * n, out_specs=[HBM] * n, out_shape=_landing("gather", arrays),
                          scratch_shapes=_exchange_scratch(n))(*arrays)
    return list(outs)


def _adamw_math(w, g, m, v):
    m = ADAM_B1 * m + (1.0 - ADAM_B1) * g
    v = ADAM_B2 * v + (1.0 - ADAM_B2) * (g * g)
    m_hat = m / (1.0 - ADAM_B1 ** ADAM_STEP)
    v_hat = v / (1.0 - ADAM_B2 ** ADAM_STEP)
    delta = -ADAM_LR * (m_hat / (jnp.sqrt(v_hat) + ADAM_EPS) + ADAM_WD * w)
    return delta, m, v


def adamw(name, parts, w, m, v):
    r, c = w.shape
    npart, _, cp = parts.shape
    tc = c if (c % LANE or cp != c) else _tile(c, 512, LANE)
    tr = _tile(r, 256, 16 if parts.dtype == BF16 else 8)
    if tr < 64:
        tr, tc = r, (tc if tc == c and cp != c else _tile(c, 256, LANE))
    tcp = cp if tc == c else tc

    def body(p_ref, w_ref, m_ref, v_ref, g_ref, d_ref, mo_ref, vo_ref):
        g = p_ref[0].astype(F32)
        for s in range(1, npart):
            g = g + p_ref[s].astype(F32)
        g = g[:, :tc]
        g_ref[...] = g
        d_ref[...], mo_ref[...], vo_ref[...] = _adamw_math(w_ref[...], g, m_ref[...], v_ref[...])

    blk = pl.BlockSpec((tr, tc), lambda i, j: (i, j))
    out = jax.ShapeDtypeStruct((r, c), F32)
    return pl.pallas_call(
        body, name=name, grid=(r // tr, c // tc),
        in_specs=[pl.BlockSpec((npart, tr, tcp), lambda i, j: (0, i, j)), blk, blk, blk],
        out_specs=(blk,) * 4, out_shape=(out,) * 4, compiler_params=_params("parallel", "parallel"))(parts, w, m, v)


def sum_parts(name, parts):
    npart, r, c = parts.shape
    tr = _tile(r, 512, 8)

    def body(p_ref, o_ref):
        g = p_ref[0]
        for s in range(1, npart):
            g = g + p_ref[s]
        o_ref[...] = g

    return pl.pallas_call(
        body, name=name, grid=(r // tr,), in_specs=[pl.BlockSpec((npart, tr, c), lambda i: (0, i, 0))],
        out_specs=pl.BlockSpec((tr, c), lambda i: (i, 0)), out_shape=jax.ShapeDtypeStruct((r, c), F32),
        compiler_params=_params("parallel"))(parts)


class _Pack:
    def __init__(self, shapes):
        self.shapes = dict(shapes)
        self.offsets, off = {}, 0
        for name, shape in self.shapes.items():
            self.offsets[name] = off
            off += -(-int(np.prod(shape)) // (8 * LANE)) * (8 * LANE)
        self.used = off
        self.rows = -(-off // (512 * LANE)) * 512

    def pack(self, values):
        pieces = []
        for name, shape in self.shapes.items():
            size = int(np.prod(shape))
            padded = -(-size // (8 * LANE)) * (8 * LANE)
            pieces.append(jnp.pad(values[name].astype(F32).reshape(-1), (0, padded - size)))
        pieces.append(jnp.zeros((self.rows * LANE - self.used,), F32))
        return jnp.concatenate(pieces).reshape(self.rows, LANE)

    def unpack(self, flat, lead=()):
        flat = flat.reshape(*lead, self.rows * LANE)
        out = {}
        for name, shape in self.shapes.items():
            size = int(np.prod(shape))
            out[name] = flat[..., self.offsets[name]:self.offsets[name] + size].reshape(*lead, *shape)
        return out


def kernel(x, c, ctx, c_ctx, norm_mix_g, norm_ffn_g, w_ada, b_ada, na_w_qkv, na_q_g, na_k_g, na_rpb, na_w_o, gm_w_in, gm_v_g, gm_w_s, gm_b_s, gm_w_out, sc_w_in, sc_conv_w, sc_w_out, ffn_w_up, ffn_conv_w, ffn_conv_b, ffn_w_down, loss_target, m_c_ctx, m_norm_mix_g, m_norm_ffn_g, m_w_ada, m_b_ada, m_na_w_qkv, m_na_q_g, m_na_k_g, m_na_rpb, m_na_w_o, m_gm_w_in, m_gm_v_g, m_gm_w_s, m_gm_b_s, m_gm_w_out, m_sc_w_in, m_sc_conv_w, m_sc_w_out, m_ffn_w_up, m_ffn_conv_w, m_ffn_conv_b, m_ffn_w_down, v_c_ctx, v_norm_mix_g, v_norm_ffn_g, v_w_ada, v_b_ada, v_na_w_qkv, v_na_q_g, v_na_k_g, v_na_rpb, v_na_w_o, v_gm_w_in, v_gm_v_g, v_gm_w_s, v_gm_b_s, v_gm_w_out, v_sc_w_in, v_sc_conv_w, v_sc_w_out, v_ffn_w_up, v_ffn_conv_w, v_ffn_conv_b, v_ffn_w_down):
    weights = dict(c_ctx=c_ctx, norm_mix_g=norm_mix_g, norm_ffn_g=norm_ffn_g, w_ada=w_ada, b_ada=b_ada, na_w_qkv=na_w_qkv,
                   na_q_g=na_q_g, na_k_g=na_k_g, na_rpb=na_rpb, na_w_o=na_w_o, gm_w_in=gm_w_in, gm_v_g=gm_v_g, gm_w_s=gm_w_s,
                   gm_b_s=gm_b_s, gm_w_out=gm_w_out, sc_w_in=sc_w_in, sc_conv_w=sc_conv_w, sc_w_out=sc_w_out,
                   ffn_w_up=ffn_w_up, ffn_conv_w=ffn_conv_w, ffn_conv_b=ffn_conv_b, ffn_w_down=ffn_w_down)
    mom_m = dict(c_ctx=m_c_ctx, norm_mix_g=m_norm_mix_g, norm_ffn_g=m_norm_ffn_g, w_ada=m_w_ada, b_ada=m_b_ada, na_w_qkv=m_na_w_qkv,
                 na_q_g=m_na_q_g, na_k_g=m_na_k_g, na_rpb=m_na_rpb, na_w_o=m_na_w_o, gm_w_in=m_gm_w_in, gm_v_g=m_gm_v_g,
                 gm_w_s=m_gm_w_s, gm_b_s=m_gm_b_s, gm_w_out=m_gm_w_out, sc_w_in=m_sc_w_in, sc_conv_w=m_sc_conv_w,
                 sc_w_out=m_sc_w_out, ffn_w_up=m_ffn_w_up, ffn_conv_w=m_ffn_conv_w, ffn_conv_b=m_ffn_conv_b, ffn_w_down=m_ffn_w_down)
    mom_v = dict(c_ctx=v_c_ctx, norm_mix_g=v_norm_mix_g, norm_ffn_g=v_norm_ffn_g, w_ada=v_w_ada, b_ada=v_b_ada, na_w_qkv=v_na_w_qkv,
                 na_q_g=v_na_q_g, na_k_g=v_na_k_g, na_rpb=v_na_rpb, na_w_o=v_na_w_o, gm_w_in=v_gm_w_in, gm_v_g=v_gm_v_g,
                 gm_w_s=v_gm_w_s, gm_b_s=v_gm_b_s, gm_w_out=v_gm_w_out, sc_w_in=v_sc_w_in, sc_conv_w=v_sc_conv_w,
                 sc_w_out=v_sc_w_out, ffn_w_up=v_ffn_w_up, ffn_conv_w=v_ffn_conv_w, ffn_conv_b=v_ffn_conv_b, ffn_w_down=v_ffn_w_down)
    names = list(weights)

    n_lat, d = x.shape[1], x.shape[2]
    n_ctx = ctx.shape[1]
    t = n_lat + n_ctx
    depth = norm_mix_g.shape[0]
    hd = na_q_g.shape[-1]
    n_heads = d // hd
    nup = ffn_w_up.shape[-1]
    nup_p = -(-nup // LANE) * LANE
    fdim, fp = 4 * nup, 4 * nup_p
    me = _slot(_me())
    geo = _na_geometry(n_lat)

    pad_up = lambda a: jnp.pad(a, [(0, 0)] * (a.ndim - 1) + [(0, nup_p - nup)])
    mixer_weights = (("na_w_qkv", "na_w_o"), ("gm_w_in", "gm_w_out"), ("sc_w_in", "sc_w_out"))

    def shards(i):
        w_in, w_out = (weights[nm][i // N_MIXERS].astype(BF16) for nm in mixer_weights[i % N_MIXERS])
        return [w_in, w_out, pad_up(ffn_w_up[i]).astype(BF16), ffn_w_down[i].astype(BF16)]

    def operands(g_in, g_out, g_up, g_down):
        down = jnp.pad(g_down.reshape(4, nup, d), ((0, 0), (0, nup_p - nup), (0, 0))).reshape(fp, d)
        return dict(w_in=g_in, w_out=g_out.reshape(-1, d), w_up=g_up, w_down=down)

    first_in, *first_rest = shards(0)
    g_in0, g_fcw, g_scw, c_all = all_gather("gather_first", [first_in, pad_up(ffn_conv_w), sc_conv_w, c])
    layer_w = []
    f_cw = [g_fcw[:, i].transpose(1, 0, 2).reshape(3, 2, fp).transpose(1, 0, 2) for i in range(depth)]
    cb_p = pad_up(ffn_conv_b.reshape(depth, N_DEV, nup)).reshape(depth, 2, 1, fp)
    s_cw = [g_scw[:, j].transpose(1, 0, 2).reshape(3, d) for j in range(sc_conv_w.shape[0])]

    cond = jnp.concatenate([c_all.reshape(N_DEV, d), c_ctx[None], jnp.zeros((7, d), F32)])
    mod_cols = mm_nn("ada_fwd", cond, w_ada, a_silu=True)
    (mod_all,) = all_gather("gather_mod", [mod_cols])
    ncol = w_ada.shape[-1]
    mod_all = mod_all.reshape(N_DEV, 16, depth, ncol).transpose(2, 1, 0, 3).reshape(depth, 16, N_MOD * d) + b_ada[:, None, :]
    mod_lat = lax.dynamic_index_in_dim(mod_all, me, axis=1, keepdims=False)
    mods = jnp.stack([mod_lat, mod_all[:, N_DEV]], axis=1).reshape(depth, 2, N_MOD, d)
    sh1, sc1, g1, sh2, sc2, g2 = (mods[:, :, kd] for kd in range(N_MOD))

    xs = jnp.concatenate([x[0], ctx[0]], axis=0)
    saved = []
    prev = None
    for i in range(depth):
        mixer, j = i % N_MIXERS, i // N_MIXERS
        s = {}
        if prev is None:
            s["x"] = xs
            s["h"] = resid_rms_mod(f"l{i}_norm_mix", xs, None, None, norm_mix_g[i:i + 1], sc1[i], sh1[i], n_lat)
        else:
            s["x"], s["h"] = resid_rms_mod(f"l{i}_norm_mix", prev[0], prev[1], prev[2], norm_mix_g[i:i + 1], sc1[i], sh1[i], n_lat)
        nxt = shards(i + 1) if i + 1 < depth else None
        ride = (first_rest[:1] if i == 0 else []) + ([nxt[1]] if nxt else [])
        w_in = g_in0 if i == 0 else layer_w[i]["w_in"]
        in_name = f"l{i}_" + ("qkv", "gm_in", "sc_in")[mixer]
        if ride:
            pre, rode = mm_nn(in_name, s["h"], w_in, carry=("gather", ride))
        else:
            pre, rode = mm_nn(in_name, s["h"], w_in), []
        if mixer == 0:
            s["qkv"] = pre
            s["q"], s["k"], s["v"] = qk_norm_fwd(f"l{i}_qk_norm", s["qkv"], na_q_g[j:j + 1], na_k_g[j:j + 1], hd)
            s["bias"] = na_bias_table(f"l{i}_bias", na_rpb[j], geo)
            if i == 0:
                s["o"], (g_up0, g_down0) = na_attention_fwd(f"l{i}_attn", s["q"], s["k"], s["v"], s["bias"], geo, n_lat, hd,
                                                            carry=("gather", first_rest[1:]))
                layer_w.append(operands(g_in0, rode[0], g_up0, g_down0))
            else:
                s["o"] = na_attention_fwd(f"l{i}_attn", s["q"], s["k"], s["v"], s["bias"], geo, n_lat, hd)
        elif mixer == 1:
            s["t"] = pre
            s["o"] = gmlp_gate_fwd(f"l{i}_gm_gate", s["t"], gm_v_g[j:j + 1], gm_w_s[j], gm_b_s[j][:, :, None])
        else:
            s["t"] = pre
            s["o"] = sc_gate_fwd(f"l{i}_sc_gate", s["t"], s_cw[j], n_lat)
        lw = layer_w[i]
        s["y"] = mm_nn(f"l{i}_mix_out", s["o"], lw["w_out"])
        s["x1"], s["hf"] = resid_rms_mod(f"l{i}_norm_ffn", s["x"], s["y"], g1[i], norm_ffn_g[i:i + 1], sc2[i], sh2[i], n_lat)
        if nxt:
            n_in, _, n_up, n_down = nxt
            s["u"], (g_up,) = mm_nn(f"l{i}_ffn_up", s["hf"], lw["w_up"], out_parts=2, carry=("gather", [n_up]))
            s["z"], s["a"], (g_down,) = ffn_act_fwd(f"l{i}_ffn_act", s["u"], f_cw[i], cb_p[i], n_lat, carry=("gather", [n_down]))
            s["f"], (g_in,) = mm_nn(f"l{i}_ffn_down", s["a"], lw["w_down"], carry=("gather", [n_in]))
            layer_w.append(operands(g_in, rode[-1], g_up, g_down))
        else:
            s["u"] = mm_nn(f"l{i}_ffn_up", s["hf"], lw["w_up"], out_parts=2)
            s["z"], s["a"] = ffn_act_fwd(f"l{i}_ffn_act", s["u"], f_cw[i], cb_p[i], n_lat)
            s["f"] = mm_nn(f"l{i}_ffn_down", s["a"], lw["w_down"])
        prev = (s["x1"], s["f"], g2[i])
        saved.append(s)

    dx, loss_local, df, dg2 = loss_head("loss_head", prev[0], prev[1], prev[2], loss_target[0])
    loss = lax.psum(loss_local[0, 0], AXES)

    big = {}
    small = {}
    dmod = [None] * depth
    zeros_like_param = lambda p: [None] * p.shape[0]
    for nm in ("na_w_qkv", "na_w_o", "gm_w_in", "gm_w_out", "sc_w_in", "sc_w_out", "ffn_w_up", "ffn_w_down"):
        big[nm] = zeros_like_param(weights[nm])
    for nm in ("norm_mix_g", "norm_ffn_g", "ffn_conv_w", "ffn_conv_b", "na_q_g", "na_k_g", "na_rpb", "gm_v_g", "gm_w_s", "gm_b_s", "sc_conv_w"):
        small[nm] = zeros_like_param(weights[nm])
    landed, pending = {}, []

    def take(wanted):
        keys = [k for k in pending if wanted(k)]
        for k in keys:
            pending.remove(k)
        return keys

    def riding(keys, fn, *args, **kw):
        if not keys:
            return fn(*args, **kw)
        *res, got = fn(*args, carry=("scatter", [big[nm][idx] for nm, idx in keys]), **kw)
        landed.update(zip(keys, got))
        return res[0] if len(res) == 1 else tuple(res)

    for i in reversed(range(depth)):
        mixer, j = i % N_MIXERS, i // N_MIXERS
        nm_in, nm_out = mixer_weights[mixer]
        s, lw = saved[i], layer_w[i]
        da = riding(take(lambda k: k[0] in [m[0] for m in mixer_weights]), mm_nt, f"l{i}_b_ffn_down_x", df, lw["w_down"])
        dwd = mm_tn(f"l{i}_b_ffn_down_w", s["a"], df)
        big["ffn_w_down"][i] = dwd.reshape(4, nup_p, d)[:, :nup].reshape(N_DEV, fdim // N_DEV, d)
        du, dcw, dcb = riding(take(lambda k: k[0] in [m[1] for m in mixer_weights]), ffn_act_bwd,
                              f"l{i}_b_ffn_act", s["z"], s["u"], da, f_cw[i], n_lat)
        small["ffn_conv_w"][i] = dcw.transpose(1, 0, 2).reshape(3, N_DEV, nup_p)[:, :, :nup].reshape(3, 2 * fdim)
        small["ffn_conv_b"][i] = dcb.reshape(N_DEV, nup_p)[:, :nup].reshape(2 * fdim)
        dhf = riding([("ffn_w_down", i)], mm_nt, f"l{i}_b_ffn_up_x", du, lw["w_up"])
        big["ffn_w_up"][i] = riding(take(lambda k: True), mm_tn, f"l{i}_b_ffn_up_w", s["hf"], du, out_parts=N_DEV)
        pending.append(("ffn_w_up", i))
        dx1, dsh2, dsc2, dgf, dy, dg1 = rms_mod_bwd(f"l{i}_b_norm_ffn", s["x1"], dhf, norm_ffn_g[i:i + 1], sc2[i], dx, n_lat,
                                                    branch=(s["y"], g1[i]))
        small["norm_ffn_g"][i] = dgf[0]
        do = mm_nt(f"l{i}_b_mix_out_x", dy, lw["w_out"], out_dtype=BF16 if mixer == 0 else F32)
        big[nm_out][j] = mm_tn(f"l{i}_b_mix_out_w", s["o"], dy).reshape(N_DEV, -1, d)
        pending.append((nm_out, j))
        if mixer == 0:
            dq, dk, dv, dbias = riding(take(lambda k: True), na_attention_bwd,
                                       f"l{i}_b_attn", s["q"], s["k"], s["v"], s["bias"], do, geo, n_lat, hd)
            small["na_rpb"][j] = na_rpb_grad(f"l{i}_b_rpb", dbias, geo, n_heads)
            dt, dqg, dkg = qk_norm_bwd(f"l{i}_b_qk_norm", s["qkv"], dq, dk, dv, na_q_g[j:j + 1], na_k_g[j:j + 1], hd)
            small["na_q_g"][j], small["na_k_g"][j] = dqg[0], dkg[0]
        elif mixer == 1:
            dt, dvg, dws, dsv = gmlp_gate_bwd(f"l{i}_b_gm_gate", s["t"], do, gm_v_g[j:j + 1], gm_w_s[j], gm_b_s[j][:, :, None])
            groups, width = gm_w_s.shape[1], dsv.shape[1]
            group_of = np.zeros((width, LANE), np.float32)
            group_of[np.arange(width), np.arange(width) // (width // groups)] = 1.0
            dbs = _mm(f"l{i}_b_gm_bs", dsv, jnp.asarray(group_of), "nn", (GM_CHUNK, LANE), F32,
                      (GM_CHUNK, LANE, _tile(width, 2048, LANE)), exact=True)[:, :groups].T
            small["gm_v_g"][j], small["gm_w_s"][j], small["gm_b_s"][j] = dvg[0], dws, dbs
        else:
            dt, dscw = sc_gate_bwd(f"l{i}_b_sc_gate", s["t"], do, s_cw[j], n_lat)
            small["sc_conv_w"][j] = dscw
            dt = dt.transpose(1, 0, 2).reshape(t, 3 * d)
        dh = mm_nt(f"l{i}_b_mix_in_x", dt, lw["w_in"])
        big[nm_in][j] = mm_tn(f"l{i}_b_mix_in_w", s["h"], dt, out_parts=N_DEV)
        pending.append((nm_in, j))
        dmod_ffn = [dsh2, dsc2, dg2]
        if i > 0:
            dx, dsh1, dsc1, dgm, df, dg2 = rms_mod_bwd(f"l{i}_b_norm_mix", s["x"], dh, norm_mix_g[i:i + 1], sc1[i], dx1, n_lat,
                                                       branch=(saved[i - 1]["f"], g2[i - 1]))
        else:
            dx, dsh1, dsc1, dgm = riding(take(lambda k: True), rms_mod_bwd,
                                         f"l{i}_b_norm_mix", s["x"], dh, norm_mix_g[i:i + 1], sc1[i], dx1, n_lat)
        small["norm_mix_g"][i] = dgm[0]
        dmod[i] = jnp.stack([dsh1, dsc1, dg1] + dmod_ffn, axis=1)
    grad_x = dx[:n_lat][None]

    small = {nm: jnp.stack(v) for nm, v in small.items()}
    small["dmod"] = jnp.stack(dmod)
    pack = _Pack({nm: v.shape for nm, v in small.items()})
    (small_all,) = all_gather("gather_small", [pack.pack(small)])
    small_sum = pack.unpack(sum_parts("sum_small", small_all))
    dmod_all = pack.unpack(small_all, lead=(N_DEV,))["dmod"]
    dmod_ctx = small_sum["dmod"][:, 1].reshape(depth, N_MOD * d)
    grads = {nm: small_sum[nm] for nm in small if nm != "dmod"}
    grads["b_ada"] = (small_sum["dmod"][:, 0] + small_sum["dmod"][:, 1]).reshape(depth, N_MOD * d)
    my_cols = lambda a, width: lax.dynamic_slice_in_dim(a, me * width, width, axis=-1)
    grads["ffn_conv_w"] = my_cols(grads["ffn_conv_w"], nup)
    grads["sc_conv_w"] = my_cols(grads["sc_conv_w"], sc_conv_w.shape[-1])

    drows = jnp.concatenate([dmod_all[:, :, 0], dmod_all[:, :, 1]]).reshape(2 * N_DEV, depth, N_MOD * d)
    drows = my_cols(drows, ncol).reshape(2 * N_DEV, depth * ncol)
    cond2 = jnp.concatenate([c_all.reshape(N_DEV, d), jnp.broadcast_to(c_ctx[None], (N_DEV, d))])
    g_w_ada = mm_tn("ada_bwd_w", cond2, drows, out_parts=depth, out_dtype=F32, a_silu=True)
    dctx_rows = jnp.pad(my_cols(dmod_ctx, ncol).reshape(1, depth * ncol), ((0, 15), (0, 0)))
    dcc = mm_nt("ada_bwd_c", dctx_rows, w_ada)[0:1]
    (dcc_all,) = all_gather("gather_c_ctx", [jnp.pad(dcc.reshape(-1, LANE), ((0, (-d // LANE) % 8), (0, 0)))])
    dcc_sum = sum_parts("sum_c_ctx", dcc_all).reshape(-1)[:d]
    sig = jax.nn.sigmoid(c_ctx)
    grads["c_ctx"] = dcc_sum * (sig * (1.0 + c_ctx * (1.0 - sig)))

    assert not pending
    out = {nm: [None] * 4 for nm in names}
    per_big = {nm: [] for nm in big}
    for nm, idx in [(nm, idx) for nm in big for idx in range(len(big[nm]))]:
        w2, p = weights[nm][idx], landed[nm, idx]
        per_big[nm].append(adamw(f"adamw_{nm}{idx}", p.reshape(N_DEV, w2.shape[0], -1), w2, mom_m[nm][idx], mom_v[nm][idx]))
    for nm, res in per_big.items():
        out[nm] = [jnp.stack([r[q] for r in res]) for q in range(4)]
    res = [adamw(f"adamw_w_ada{i}", g_w_ada[i][None], w_ada[i], m_w_ada[i], v_w_ada[i]) for i in range(depth)]
    out["w_ada"] = [jnp.stack([r[q] for r in res]) for q in range(4)]
    small_names = [nm for nm in names if nm not in big and nm != "w_ada"]
    spack = _Pack({nm: weights[nm].shape for nm in small_names})
    flat = [spack.pack({nm: src[nm] for nm in small_names}) for src in (grads, weights, mom_m, mom_v)]
    res = adamw("adamw_small", flat[0][None], flat[1], flat[2], flat[3])
    res = [spack.unpack(r) for r in res]
    for nm in small_names:
        out[nm] = [grads[nm].reshape(weights[nm].shape)] + [res[q][nm] for q in range(1, 4)]

    return (loss, grad_x, *[out[nm][0] for nm in names], *[out[nm][1] for nm in names],
            *[out[nm][2] for nm in names], *[out[nm][3] for nm in names])
```

```python
import functools
import math

import numpy as np
import jax
import jax.numpy as jnp
from jax import lax
from jax.experimental import pallas as pl
from jax.experimental.pallas import tpu as pltpu

F32 = jnp.float32
BF16 = jnp.bfloat16
MESH = pl.DeviceIdType.MESH
AXES = ("x", "y", "c")
N_DEV = 8
N_MOD = 6
N_MIXERS = 3
EPS = 1e-6
GRID_W = 64
WIN_H = 8
WIN_W = 16
QROWS = 2
GM_CHUNK = 128
LANE = 128
NEG = -1e30
ADAM_LR = 0.001
ADAM_B1 = 0.9
ADAM_B2 = 0.999
ADAM_EPS = 1e-08
ADAM_WD = 0.01
ADAM_STEP = 10
VMEM_LIMIT = 56 * 1024 * 1024
HBM = pl.BlockSpec(memory_space=pltpu.HBM)

NN = (((1,), (0,)), ((), ()))
NT = (((1,), (1,)), ((), ()))
TN = (((0,), (0,)), ((), ()))


def _params(*sem):
    return pltpu.CompilerParams(dimension_semantics=sem, vmem_limit_bytes=VMEM_LIMIT)


def _tile(n, pref, mult):
    best = None
    for t in range(mult, min(n, pref) + 1, mult):
        if n % t == 0:
            best = t
    return n if best is None else best


def _full(arr):
    nd = arr.ndim
    return pl.BlockSpec(arr.shape, lambda *g: (0,) * nd)


def _logical(shape):
    return tuple(shape) if len(shape) == 2 else (shape[1], shape[0] * shape[2])


def _cspec(shape, tr, tc, rc):
    if len(shape) == 2:
        return pl.BlockSpec((tr, tc), rc)
    cpp = shape[2] // tc

    def imap(*g):
        r, c = rc(*g)
        return (c // cpp, r, c % cpp)

    return pl.BlockSpec((None, tr, tc), imap)


def _dot(a, b, dims, exact=False):
    if exact:
        return lax.dot_general(a, b, dims, precision=lax.Precision.HIGHEST, preferred_element_type=F32)
    return lax.dot_general(a.astype(BF16), b.astype(BF16), dims, preferred_element_type=F32)


def _silu(z):
    return z * jax.nn.sigmoid(z)


def _me():
    return lax.axis_index("x"), lax.axis_index("y"), lax.axis_index("c")


def _slot(p):
    return 4 * p[0] + 2 * p[1] + p[2]


def _scatter_direct(srcs, dsts, send_sems, recv_sems, local_sems):
    x, y, c = _me()
    me = (x, y, c)
    peers = [((x + (k >> 2)) % 2, (y + ((k >> 1) & 1)) % 2, (c + (k & 1)) % 2) for k in range(1, N_DEV)]

    def remote(a, k, peer, src_dev, dst_dev):
        return pltpu.make_async_remote_copy(src_ref=srcs[a].at[_slot(src_dev)], dst_ref=dsts[a].at[_slot(dst_dev)], send_sem=send_sems.at[a, k],
                                            recv_sem=recv_sems.at[a, k], device_id=peer, device_id_type=MESH)

    def local(a):
        return pltpu.make_async_copy(srcs[a].at[_slot(me)], dsts[a].at[_slot(me)], local_sems.at[a])

    def start():
        for a in range(len(srcs)):
            local(a).start()
            for k, peer in enumerate(peers):
                remote(a, k, peer, peer, me).start()

    def wait():
        for a in range(len(srcs)):
            for k, peer in enumerate(peers):
                remote(a, k, peer, me, peer).wait_recv()
        for a in range(len(srcs)):
            for k, peer in enumerate(peers):
                remote(a, k, peer, peer, me).wait_send()
            local(a).wait()

    return start, lambda: None, wait


def _gather_two_level(srcs, dsts, send_sems, recv_sems, local_sems):
    x, y, c = _me()
    me, sib = (x, y, c), (x, y, 1 - c)
    chips = [(1 - x, y), (x, 1 - y), (1 - x, 1 - y)]

    def copy(a, k, block, to, src=None):
        dst = dsts[a].at[_slot(block)]
        return pltpu.make_async_remote_copy(src_ref=dst if src is None else src, dst_ref=dst, send_sem=send_sems.at[a, k],
                                            recv_sem=recv_sems.at[a, k], device_id=to, device_id_type=MESH)

    def mine(a):
        return pltpu.make_async_copy(srcs[a], dsts[a].at[_slot(me)], local_sems.at[a])

    def first(a):
        return [copy(a, 0, me, sib, src=srcs[a])] + [copy(a, 1 + j, me, (*chip, c), src=srcs[a]) for j, chip in enumerate(chips)]

    def passed(a, j):
        return copy(a, 4 + j, (*chips[j], c), sib)

    def start():
        for a in range(len(srcs)):
            mine(a).start()
            for cp in first(a):
                cp.start()

    def pass_on():
        for j, chip in enumerate(chips):
            for a in range(len(srcs)):
                copy(a, 1 + j, (*chip, c), me).wait_recv()
                passed(a, j).start()

    def wait():
        for a in range(len(srcs)):
            copy(a, 0, sib, me).wait_recv()
            for j, chip in enumerate(chips):
                copy(a, 4 + j, (*chip, 1 - c), me).wait_recv()
        for a in range(len(srcs)):
            for cp in first(a) + [passed(a, j) for j in range(len(chips))]:
                cp.wait_send()
            mine(a).wait()

    return start, pass_on, wait


def _exchange_scratch(n):
    return [pltpu.SemaphoreType.DMA((n, N_DEV - 1)), pltpu.SemaphoreType.DMA((n, N_DEV - 1)), pltpu.SemaphoreType.DMA((n,))]


def _landing(kind, arrays):
    return [jax.ShapeDtypeStruct((N_DEV, *a.shape) if kind == "gather" else a.shape, a.dtype) for a in arrays]


def _call(body, name, grid, in_specs, out_specs, out_shape, ins, sem, scratch=(), carry=None):
    in_specs, out_specs, out_shape, scratch = list(in_specs), list(out_specs), list(out_shape), list(scratch)
    if carry is None:
        return list(pl.pallas_call(body, name=name, grid=grid, in_specs=in_specs, out_specs=out_specs, out_shape=out_shape,
                                   scratch_shapes=scratch, compiler_params=_params(*sem))(*ins))
    kind, arrays = carry
    n, ni, no, ns = len(arrays), len(in_specs), len(out_specs), len(scratch)

    def carrying(*refs):
        own_in, srcs = refs[:ni], refs[ni:ni + n]
        own_out, dsts = refs[ni + n:ni + n + no], refs[ni + n + no:ni + 2 * n + no]
        own_scratch, sems = refs[ni + 2 * n + no:ni + 2 * n + no + ns], refs[ni + 2 * n + no + ns:]
        start, pass_on, wait = (_gather_two_level if kind == "gather" else _scatter_direct)(srcs, dsts, *sems)
        step = functools.reduce(lambda lin, ax: lin * grid[ax] + pl.program_id(ax), range(len(grid)), 0)
        steps = math.prod(grid)
        pl.when(step == 0)(start)
        body(*own_in, *own_out, *own_scratch)
        if kind == "gather":
            pl.when(step == steps // 2)(pass_on)
        pl.when(step == steps - 1)(wait)

    res = pl.pallas_call(
        carrying, name=name, grid=grid, in_specs=in_specs + [HBM] * n, out_specs=out_specs + [HBM] * n,
        out_shape=out_shape + _landing(kind, arrays), scratch_shapes=scratch + _exchange_scratch(n),
        compiler_params=_params(*(["arbitrary"] * len(grid))))(*ins, *arrays)
    return list(res[:no]), list(res[no:])


def _mm(name, a, b, kind, out_shape, out_dtype, tiles, a_silu=False, exact=False, carry=None):
    la, lb, lo = _logical(a.shape), _logical(b.shape), _logical(out_shape)
    t0, t1, t2 = tiles
    if kind == "nn":
        grid = (lo[1] // t1, lo[0] // t0, la[1] // t2)
        a_spec = _cspec(a.shape, t0, t2, lambda j, i, k: (i, k))
        b_spec = _cspec(b.shape, t2, t1, lambda j, i, k: (k, j))
        o_spec = _cspec(out_shape, t0, t1, lambda j, i, k: (i, j))
        dims, acc = NN, (t0, t1)
    elif kind == "nt":
        grid = (lo[1] // t1, lo[0] // t0, la[1] // t2)
        a_spec = _cspec(a.shape, t0, t2, lambda p, i, r: (i, r))
        b_spec = _cspec(b.shape, t1, t2, lambda p, i, r: (p, r))
        o_spec = _cspec(out_shape, t0, t1, lambda p, i, r: (i, p))
        dims, acc = NT, (t0, t1)
    else:
        grid = (lo[1] // t1, lo[0] // t0, la[0] // t2)
        a_spec = _cspec(a.shape, t2, t0, lambda j, kk, r: (r, kk))
        b_spec = _cspec(b.shape, t2, t1, lambda j, kk, r: (r, j))
        o_spec = _cspec(out_shape, t0, t1, lambda j, kk, r: (kk, j))
        dims, acc = TN, (t0, t1)
    nk = grid[2]
    in_place = out_dtype == F32

    def body(a_ref, b_ref, o_ref, *scratch):
        acc_ref = o_ref if in_place else scratch[0]
        k = pl.program_id(2)
        av = a_ref[...]
        if a_silu:
            av = _silu(av)
        part = _dot(av, b_ref[...], dims, exact)

        @pl.when(k == 0)
        def _():
            acc_ref[...] = part

        @pl.when(k > 0)
        def _():
            acc_ref[...] += part

        if not in_place:
            @pl.when(k == nk - 1)
            def _():
                o_ref[...] = acc_ref[...].astype(o_ref.dtype)

    res = _call(body, name, grid, [a_spec, b_spec], [o_spec], [jax.ShapeDtypeStruct(out_shape, out_dtype)], (a, b),
                ("parallel", "parallel", "arbitrary"), scratch=[] if in_place else [pltpu.VMEM(acc, F32)], carry=carry)
    return res[0] if carry is None else (res[0][0], res[1])


MM_VMEM_BUDGET = 40 * 1024 * 1024
MM_TILE_CAP = 2048


def _divisors(n, mult):
    return [t for t in range(mult, min(n, MM_TILE_CAP) + 1, mult) if n % t == 0] or [n]


def _mm_tiles(c0, c1, c2, a, b, out_dtype):
    ia, ib, io = a.dtype.itemsize, b.dtype.itemsize, jnp.dtype(out_dtype).itemsize
    best, best_score = None, -1
    for t0 in c0:
        for t1 in c1:
            for t2 in c2:
                need = 2 * (t0 * t2 * ia + t1 * t2 * ib + t0 * t1 * io) + t0 * t1 * 4 * (1 if out_dtype == F32 else 2)
                score = (t0 * t1 * t2, t2)
                if need <= MM_VMEM_BUDGET and score > (best_score if best else (-1, -1)):
                    best, best_score = (t0, t1, t2), score
    return best if best else (c0[0], c1[0], c2[0])


def mm_nn(name, a, w, out_parts=1, out_dtype=F32, **kw):
    (m, _), (_, n) = _logical(a.shape), _logical(w.shape)
    out_shape = (m, n) if out_parts == 1 else (out_parts, m, n // out_parts)
    tiles = _mm_tiles(_divisors(m, 16), _divisors(math.gcd(w.shape[-1], out_shape[-1]), LANE),
                      _divisors(math.gcd(a.shape[-1], w.shape[-2]), LANE), a, w, out_dtype)
    return _mm(name, a, w, "nn", out_shape, out_dtype, tiles, **kw)


def mm_nt(name, a, w, out_dtype=F32, **kw):
    (m, _), (p, _) = _logical(a.shape), _logical(w.shape)
    tiles = _mm_tiles(_divisors(m, 16), _divisors(w.shape[-2], LANE), _divisors(math.gcd(a.shape[-1], w.shape[-1]), LANE), a, w, out_dtype)
    return _mm(name, a, w, "nt", (m, p), out_dtype, tiles, **kw)


def mm_tn(name, a, dy, out_parts=1, out_dtype=BF16, **kw):
    (t, k), (_, n) = _logical(a.shape), _logical(dy.shape)
    out_shape = (k, n) if out_parts == 1 else (out_parts, k, n // out_parts)
    tiles = _mm_tiles(_divisors(a.shape[-1], LANE), _divisors(math.gcd(dy.shape[-1], out_shape[-1]), LANE), _divisors(t, 16), a, dy, out_dtype)
    return _mm(name, a, dy, "tn", out_shape, out_dtype, tiles, **kw)


def _rows(tr, d):
    return pl.BlockSpec((tr, d), lambda i: (i, 0))


def _pick(ctx, ref):
    return jnp.where(ctx, ref[1:2, :], ref[0:1, :])


def resid_rms_mod(name, x, y, gate, g, sc, sh, n_lat, tr=256):
    t, d = x.shape
    nlt = n_lat // tr
    has_res = y is not None

    def body(*refs):
        if has_res:
            x_ref, y_ref, gate_ref, g_ref, sc_ref, sh_ref, x1_ref, h_ref = refs
        else:
            x_ref, g_ref, sc_ref, sh_ref, h_ref = refs
        ctx = pl.program_id(0) >= nlt
        xv = x_ref[...]
        if has_res:
            xv = xv + _pick(ctx, gate_ref) * y_ref[...]
            x1_ref[...] = xv
        r = lax.rsqrt(jnp.mean(xv * xv, axis=-1, keepdims=True) + EPS)
        n = xv * r * g_ref[...]
        h_ref[...] = (n * (1.0 + _pick(ctx, sc_ref)) + _pick(ctx, sh_ref)).astype(h_ref.dtype)

    row = _rows(tr, d)
    if has_res:
        ins, in_specs = (x, y, gate, g, sc, sh), [row, row, _full(gate), _full(g), _full(sc), _full(sh)]
        out_shape = (jax.ShapeDtypeStruct((t, d), F32), jax.ShapeDtypeStruct((t, d), BF16))
        out_specs = (row, row)
    else:
        ins, in_specs = (x, g, sc, sh), [row, _full(g), _full(sc), _full(sh)]
        out_shape = jax.ShapeDtypeStruct((t, d), BF16)
        out_specs = row
    return pl.pallas_call(body, name=name, grid=(t // tr,), in_specs=in_specs, out_specs=out_specs,
                          out_shape=out_shape, compiler_params=_params("parallel"))(*ins)


def _acc_rows(i, nlt, ref, val):
    @pl.when(i == 0)
    def _():
        ref[...] = jnp.zeros_like(ref)

    @pl.when(i < nlt)
    def _():
        ref[0:1, :] += val

    @pl.when(i >= nlt)
    def _():
        ref[1:2, :] += val


def rms_mod_bwd(name, x, dh, g, sc, dres, n_lat, branch=None, tr=256, carry=None):
    t, d = x.shape
    nlt = n_lat // tr

    def body(x_ref, dh_ref, g_ref, sc_ref, dres_ref, *rest):
        if branch is None:
            dx_ref, dsh_ref, dsc_ref, dg_ref = rest
        else:
            y_ref, gate_ref, dx_ref, dsh_ref, dsc_ref, dg_ref, dy_ref, dgate_ref = rest
        i = pl.program_id(0)
        xv, dhv, gv = x_ref[...], dh_ref[...], g_ref[...]
        r = lax.rsqrt(jnp.mean(xv * xv, axis=-1, keepdims=True) + EPS)
        xhat = xv * r
        dn = dhv * (1.0 + _pick(i >= nlt, sc_ref))
        dxhat = dn * gv
        dxv = r * (dxhat - xhat * jnp.mean(dxhat * xhat, axis=-1, keepdims=True)) + dres_ref[...]
        dx_ref[...] = dxv
        if branch is not None:
            dy_ref[...] = (_pick(i >= nlt, gate_ref) * dxv).astype(dy_ref.dtype)
            _acc_rows(i, nlt, dgate_ref, jnp.sum(dxv * y_ref[...], axis=0, keepdims=True))
        _acc_rows(i, nlt, dsh_ref, jnp.sum(dhv, axis=0, keepdims=True))
        _acc_rows(i, nlt, dsc_ref, jnp.sum(dhv * (xhat * gv), axis=0, keepdims=True))
        dgp = jnp.sum(dn * xhat, axis=0, keepdims=True)

        @pl.when(i == 0)
        def _():
            dg_ref[...] = dgp

        @pl.when(i > 0)
        def _():
            dg_ref[...] += dgp

    row = _rows(tr, d)
    two = pl.BlockSpec((2, d), lambda i: (0, 0))
    two_shape = jax.ShapeDtypeStruct((2, d), F32)
    ins, in_specs = [x, dh, g, sc, dres], [row, row, _full(g), _full(sc), row]
    out_specs = [row, two, two, pl.BlockSpec((1, d), lambda i: (0, 0))]
    out_shape = [jax.ShapeDtypeStruct((t, d), F32), two_shape, two_shape, jax.ShapeDtypeStruct((1, d), F32)]
    if branch is not None:
        ins, in_specs = ins + list(branch), in_specs + [row, _full(branch[1])]
        out_specs, out_shape = out_specs + [row, two], out_shape + [jax.ShapeDtypeStruct((t, d), BF16), two_shape]
    res = _call(body, name, (t // tr,), in_specs, out_specs, out_shape, ins, ("arbitrary",), carry=carry)
    return tuple(res) if carry is None else (*res[0], res[1])


def loss_head(name, x1, f, gate, target, tr=256):
    t, d = x1.shape
    nlt = target.shape[0] // tr

    def body(x_ref, f_ref, gate_ref, t_ref, dx_ref, loss_ref, df_ref, dgate_ref, acc_ref):
        i = pl.program_id(0)

        @pl.when(i == 0)
        def _():
            acc_ref[...] = jnp.zeros_like(acc_ref)
            dgate_ref[...] = jnp.zeros_like(dgate_ref)

        @pl.when(i < nlt)
        def _():
            fv, gv = f_ref[...], gate_ref[0:1, :]
            e = x_ref[...] + gv * fv - t_ref[...]
            dxv = e / d
            dx_ref[...] = dxv
            df_ref[...] = (gv * dxv).astype(df_ref.dtype)
            dgate_ref[0:1, :] += jnp.sum(dxv * fv, axis=0, keepdims=True)
            acc_ref[...] += jnp.sum(e * e, axis=0, keepdims=True)

        @pl.when(i >= nlt)
        def _():
            dx_ref[...] = jnp.zeros_like(dx_ref)
            df_ref[...] = jnp.zeros_like(df_ref)

        @pl.when(i == t // tr - 1)
        def _():
            loss_ref[...] = jnp.sum(acc_ref[...], axis=1, keepdims=True) * (0.5 / d)

    row = _rows(tr, d)
    return pl.pallas_call(
        body, name=name, grid=(t // tr,),
        in_specs=[row, row, _full(gate), pl.BlockSpec((tr, d), lambda i: (jnp.minimum(i, nlt - 1), 0))],
        out_specs=(row, pl.BlockSpec((1, 1), lambda i: (0, 0)), row, pl.BlockSpec((2, d), lambda i: (0, 0))),
        out_shape=(jax.ShapeDtypeStruct((t, d), F32), jax.ShapeDtypeStruct((1, 1), F32),
                   jax.ShapeDtypeStruct((t, d), BF16), jax.ShapeDtypeStruct((2, d), F32)),
        scratch_shapes=[pltpu.VMEM((1, d), F32)],
        compiler_params=_params("arbitrary"))(x1, f, gate, target)


HALO = 8


def _halo_specs(shape, tr, tc, col):
    hb = tr // HALO
    last = _logical(shape)[0] // HALO - 1
    main = _cspec(shape, tr, tc, lambda j, i: (i, col(j)))
    prev = _cspec(shape, HALO, tc, lambda j, i: (jnp.maximum(i * hb - 1, 0), col(j)))
    nxt = _cspec(shape, HALO, tc, lambda j, i: (jnp.minimum((i + 1) * hb, last), col(j)))
    return [prev, main, nxt]


def _seq_edges(i, nlt, nt):
    first = (i == 0) | (i == nlt)
    last = (i == nlt - 1) | (i == nt - 1)
    return first, last


def _ext(prev_ref, main_ref, next_ref, first, last):
    p = jnp.where(first, 0.0, prev_ref[...].astype(F32))
    n = jnp.where(last, 0.0, next_ref[...].astype(F32))
    return jnp.concatenate([p, main_ref[...].astype(F32), n], axis=0)


def _up(e):
    return pltpu.roll(e, 1, 0)


def _down(e):
    return pltpu.roll(e, e.shape[0] - 1, 0)


def _conv(e, w):
    return _up(e) * w[0:1, :] + e * w[1:2, :] + _down(e) * w[2:3, :]


def _conv_t(e, w):
    return _down(e) * w[0:1, :] + e * w[1:2, :] + _up(e) * w[2:3, :]


def _mid(e, tr):
    return e[HALO:HALO + tr, :]


def _acc_cols(i, ref, val):
    @pl.when(i == 0)
    def _():
        ref[...] = val

    @pl.when(i > 0)
    def _():
        ref[...] += val


def _colsum(v):
    return jnp.sum(v, axis=0, keepdims=True)


def ffn_act_fwd(name, u, cw, cb, n_lat, tr=256, carry=None):
    _, t, fp = u.shape
    tc = _tile(fp, 1536, LANE)
    nlt, nt = n_lat // tr, t // tr

    def body(pg, mg, ng, pu, mu, nu, cw_ref, cb_ref, z_ref, a_ref):
        first, last = _seq_edges(pl.program_id(1), nlt, nt)
        zg = _mid(_conv(_ext(pg, mg, ng, first, last), cw_ref[0]), tr) + cb_ref[0]
        zu = _mid(_conv(_ext(pu, mu, nu, first, last), cw_ref[1]), tr) + cb_ref[1]
        z_ref[0] = zg
        z_ref[1] = zu
        a_ref[...] = (_silu(zg) * zu).astype(a_ref.dtype)

    ncol = fp // tc
    specs = _halo_specs(u.shape, tr, tc, lambda j: j) + _halo_specs(u.shape, tr, tc, lambda j: j + ncol)
    specs += [pl.BlockSpec((2, 3, tc), lambda j, i: (0, 0, j)), pl.BlockSpec((2, 1, tc), lambda j, i: (0, 0, j))]
    res = _call(body, name, (ncol, nt), specs,
                [pl.BlockSpec((2, tr, tc), lambda j, i: (0, i, j)), pl.BlockSpec((tr, tc), lambda j, i: (i, j))],
                [jax.ShapeDtypeStruct((2, t, fp), F32), jax.ShapeDtypeStruct((t, fp), BF16)],
                (u, u, u, u, u, u, cw, cb), ("parallel", "parallel"), carry=carry)
    return tuple(res) if carry is None else (*res[0], res[1])


def ffn_act_bwd(name, z, u, da, cw, n_lat, tr=128, carry=None):
    _, t, fp = u.shape
    tc = _tile(fp, 1536, LANE)
    nlt, nt = n_lat // tr, t // tr
    ncol = fp // tc

    def body(pg, mg, ng, pu, mu, nu, u_ref, pa, ma, na, cw_ref, du_ref, dcw_ref, dcb_ref):
        i = pl.program_id(1)
        first, last = _seq_edges(i, nlt, nt)
        zg, zu = _ext(pg, mg, ng, first, last), _ext(pu, mu, nu, first, last)
        dae = _ext(pa, ma, na, first, last)
        sg = jax.nn.sigmoid(zg)
        dzs = (dae * zu * (sg * (1.0 + zg * (1.0 - sg))), dae * (zg * sg))
        for h, dz in enumerate(dzs):
            w, um = cw_ref[h], u_ref[h]
            after, before = _down(dz), _up(dz)
            du_ref[h] = _mid(after * w[0:1, :] + dz * w[1:2, :] + before * w[2:3, :], tr).astype(du_ref.dtype)
            rows = [_colsum(_mid(after, tr) * um), _colsum(_mid(dz, tr) * um), _colsum(_mid(before, tr) * um)]
            _acc_cols(i, dcw_ref.at[h], jnp.concatenate(rows, axis=0))
            _acc_cols(i, dcb_ref.at[h], _colsum(_mid(dz, tr)))

    both = pl.BlockSpec((2, tr, tc), lambda j, i: (0, i, j))
    taps = pl.BlockSpec((2, 3, tc), lambda j, i: (0, 0, j))
    specs = _halo_specs(z.shape, tr, tc, lambda j: j) + _halo_specs(z.shape, tr, tc, lambda j: j + ncol) + [both]
    specs += _halo_specs(da.shape, tr, tc, lambda j: j) + [taps]
    res = _call(body, name, (ncol, nt), specs, [both, taps, pl.BlockSpec((2, 1, tc), lambda j, i: (0, 0, j))],
                [jax.ShapeDtypeStruct((2, t, fp), BF16), jax.ShapeDtypeStruct((2, 3, fp), F32), jax.ShapeDtypeStruct((2, 1, fp), F32)],
                (z, z, z, z, z, z, u, da, da, da, cw), ("parallel", "arbitrary"), carry=carry)
    return tuple(res) if carry is None else (*res[0], res[1])


def sc_gate_fwd(name, tmat, cw, n_lat, tr=256):
    t, d3 = tmat.shape
    d = d3 // 3
    tc = _tile(d, 512, LANE)
    ncol = d // tc
    nlt, nt = n_lat // tr, t // tr

    def body(b_ref, pc, mc, nc, px, mx, nx, cw_ref, s_ref):
        first, last = _seq_edges(pl.program_id(1), nlt, nt)
        p = _ext(pc, mc, nc, first, last) * _ext(px, mx, nx, first, last)
        s_ref[...] = (b_ref[...] * _mid(_conv(p, cw_ref[...]), tr)).astype(s_ref.dtype)

    specs = [pl.BlockSpec((tr, tc), lambda j, i: (i, j))]
    specs += _halo_specs(tmat.shape, tr, tc, lambda j: j + ncol) + _halo_specs(tmat.shape, tr, tc, lambda j: j + 2 * ncol)
    specs += [pl.BlockSpec((3, tc), lambda j, i: (0, j))]
    return pl.pallas_call(
        body, name=name, grid=(ncol, nt), in_specs=specs, out_specs=pl.BlockSpec((tr, tc), lambda j, i: (i, j)),
        out_shape=jax.ShapeDtypeStruct((t, d), BF16),
        compiler_params=_params("parallel", "parallel"))(*([tmat] * 7), cw)


def sc_gate_bwd(name, tmat, ds, cw, n_lat, tr=128):
    t, d3 = tmat.shape
    d = d3 // 3
    tc = _tile(d, 512, LANE)
    ncol = d // tc
    nlt, nt = n_lat // tr, t // tr

    def body(pb, mb, nb, pc, mc, nc, px, mx, nx, pd, md, nd, cw_ref, dt_ref, dcw_ref):
        i = pl.program_id(1)
        first, last = _seq_edges(i, nlt, nt)
        w = cw_ref[...]
        be, ce, xe = _ext(pb, mb, nb, first, last), _ext(pc, mc, nc, first, last), _ext(px, mx, nx, first, last)
        dse = _ext(pd, md, nd, first, last)
        p = ce * xe
        dcv = dse * be
        dp = _conv_t(dcv, w)
        dt_ref[0] = _mid(dse * _conv(p, w), tr).astype(dt_ref.dtype)
        dt_ref[1] = _mid(dp * xe, tr).astype(dt_ref.dtype)
        dt_ref[2] = _mid(dp * ce, tr).astype(dt_ref.dtype)
        dm = _mid(dcv, tr)
        rows = [_colsum(dm * _mid(_up(p), tr)), _colsum(dm * _mid(p, tr)), _colsum(dm * _mid(_down(p), tr))]
        _acc_cols(i, dcw_ref, jnp.concatenate(rows, axis=0))

    specs = []
    for part in range(3):
        specs += _halo_specs(tmat.shape, tr, tc, functools.partial(lambda j, part: j + part * ncol, part=part))
    specs += _halo_specs(ds.shape, tr, tc, lambda j: j)
    specs += [pl.BlockSpec((3, tc), lambda j, i: (0, j))]
    return pl.pallas_call(
        body, name=name, grid=(ncol, nt), in_specs=specs,
        out_specs=(pl.BlockSpec((3, tr, tc), lambda j, i: (0, i, j)), pl.BlockSpec((3, tc), lambda j, i: (0, j))),
        out_shape=(jax.ShapeDtypeStruct((3, t, d), BF16), jax.ShapeDtypeStruct((3, d), F32)),
        compiler_params=_params("parallel", "arbitrary"))(*([tmat] * 9), ds, ds, ds, cw)


_GELU_K = 0.7978845608028654
_GELU_C = 0.044715


def _gelu(x):
    return 0.5 * x * (1.0 + jnp.tanh(_GELU_K * (x + _GELU_C * (x * x * x))))


def _gelu_grad(x):
    th = jnp.tanh(_GELU_K * (x + _GELU_C * (x * x * x)))
    return 0.5 * (1.0 + th) + 0.5 * x * (1.0 - th * th) * (_GELU_K * (1.0 + 3.0 * _GELU_C * (x * x)))


def gmlp_gate_fwd(name, tmat, vg, ws, bs):
    t, w2 = tmat.shape
    w = w2 // 2
    groups = ws.shape[0]
    gd = w // groups

    def body(t_ref, vg_ref, ws_ref, bs_ref, o_ref):
        v = _gelu(t_ref[:, w:])
        r = lax.rsqrt(jnp.mean(v * v, axis=-1, keepdims=True) + EPS)
        vn = (v * r * vg_ref[...]).astype(BF16)
        for g in range(groups):
            cols = slice(g * gd, (g + 1) * gd)
            sv = _dot(ws_ref[g], vn[:, cols], NN) + bs_ref[g]
            o_ref[:, cols] = (_gelu(t_ref[:, cols]) * sv).astype(o_ref.dtype)

    return pl.pallas_call(
        body, name=name, grid=(t // GM_CHUNK,),
        in_specs=[_rows(GM_CHUNK, w2), _full(vg), _full(ws), _full(bs)], out_specs=_rows(GM_CHUNK, w),
        out_shape=jax.ShapeDtypeStruct((t, w), BF16), compiler_params=_params("parallel"))(tmat, vg, ws, bs)


def gmlp_gate_bwd(name, tmat, dout, vg, ws, bs):
    t, w2 = tmat.shape
    w = w2 // 2
    groups = ws.shape[0]
    gd = w // groups

    def body(t_ref, do_ref, vg_ref, ws_ref, bs_ref, dt_ref, dvg_ref, dws_ref, dsv_ref, dvn_ref):
        i = pl.program_id(0)
        tv = t_ref[:, w:]
        v = _gelu(tv)
        r = lax.rsqrt(jnp.mean(v * v, axis=-1, keepdims=True) + EPS)
        vhat = v * r
        vn = (vhat * vg_ref[...]).astype(BF16)
        for g in range(groups):
            cols = slice(g * gd, (g + 1) * gd)
            tu = t_ref[:, cols]
            dov = do_ref[:, cols]
            sv = _dot(ws_ref[g], vn[:, cols], NN) + bs_ref[g]
            dt_ref[:, cols] = (dov * sv * _gelu_grad(tu)).astype(dt_ref.dtype)
            dsv = dov * _gelu(tu)
            _acc_cols(i, dsv_ref.at[:, cols], dsv)
            _acc_cols(i, dws_ref.at[g], _dot(dsv, vn[:, cols], NT))
            dvn_ref[:, cols] = _dot(ws_ref[g], dsv, TN)
        dvn = dvn_ref[...]
        _acc_cols(i, dvg_ref, _colsum(dvn * vhat))
        dvhat = dvn * vg_ref[...]
        dv = r * (dvhat - vhat * jnp.mean(dvhat * vhat, axis=-1, keepdims=True))
        dt_ref[:, w:] = (dv * _gelu_grad(tv)).astype(dt_ref.dtype)

    keep = lambda shape: pl.BlockSpec(shape, lambda i: (0,) * len(shape))
    return pl.pallas_call(
        body, name=name, grid=(t // GM_CHUNK,),
        in_specs=[_rows(GM_CHUNK, w2), _rows(GM_CHUNK, w), _full(vg), _full(ws), _full(bs)],
        out_specs=(_rows(GM_CHUNK, w2), keep((1, w)), keep(ws.shape), keep((GM_CHUNK, w))),
        out_shape=(jax.ShapeDtypeStruct((t, w2), BF16), jax.ShapeDtypeStruct((1, w), F32),
                   jax.ShapeDtypeStruct(ws.shape, F32), jax.ShapeDtypeStruct((GM_CHUNK, w), F32)),
        scratch_shapes=[pltpu.VMEM((GM_CHUNK, w), F32)],
        compiler_params=_params("arbitrary"))(tmat, dout, vg, ws, bs)


def qk_norm_fwd(name, qkv, qg, kg, hd, tr=128):
    t, d3 = qkv.shape
    d = d3 // 3

    def body(x_ref, qg_ref, kg_ref, q_ref, k_ref, v_ref):
        for part, (g_ref, o_ref) in enumerate(((qg_ref, q_ref), (kg_ref, k_ref))):
            for h in range(d // hd):
                xh = x_ref[:, part * d + h * hd: part * d + (h + 1) * hd]
                r = lax.rsqrt(jnp.mean(xh * xh, axis=-1, keepdims=True) + EPS)
                o_ref[:, h * hd:(h + 1) * hd] = (xh * r * g_ref[...]).astype(o_ref.dtype)
        v_ref[...] = x_ref[:, 2 * d:].astype(v_ref.dtype)

    out = jax.ShapeDtypeStruct((t, d), BF16)
    return pl.pallas_call(
        body, name=name, grid=(t // tr,), in_specs=[_rows(tr, d3), _full(qg), _full(kg)],
        out_specs=(_rows(tr, d),) * 3, out_shape=(out,) * 3, compiler_params=_params("parallel"))(qkv, qg, kg)


def qk_norm_bwd(name, qkv, dq, dk, dv, qg, kg, hd, tr=128):
    t, d3 = qkv.shape
    d = d3 // 3

    def body(x_ref, dq_ref, dk_ref, dv_ref, qg_ref, kg_ref, o_ref, dqg_ref, dkg_ref):
        i = pl.program_id(0)
        for part, (g_ref, dn_ref, dg_ref) in enumerate(((qg_ref, dq_ref, dqg_ref), (kg_ref, dk_ref, dkg_ref))):
            dg = jnp.zeros((1, hd), F32)
            for h in range(d // hd):
                xh = x_ref[:, part * d + h * hd: part * d + (h + 1) * hd]
                dn = dn_ref[:, h * hd:(h + 1) * hd]
                r = lax.rsqrt(jnp.mean(xh * xh, axis=-1, keepdims=True) + EPS)
                xhat = xh * r
                dg = dg + _colsum(dn * xhat)
                dxhat = dn * g_ref[...]
                dx = r * (dxhat - xhat * jnp.mean(dxhat * xhat, axis=-1, keepdims=True))
                o_ref[:, part * d + h * hd: part * d + (h + 1) * hd] = dx.astype(o_ref.dtype)
            _acc_cols(i, dg_ref, dg)
        o_ref[:, 2 * d:] = dv_ref[...].astype(o_ref.dtype)

    one = pl.BlockSpec((1, hd), lambda i: (0, 0))
    return pl.pallas_call(
        body, name=name, grid=(t // tr,),
        in_specs=[_rows(tr, d3), _rows(tr, d), _rows(tr, d), _rows(tr, d), _full(qg), _full(kg)],
        out_specs=(_rows(tr, d3), one, one),
        out_shape=(jax.ShapeDtypeStruct((t, d3), BF16), jax.ShapeDtypeStruct((1, hd), F32), jax.ShapeDtypeStruct((1, hd), F32)),
        compiler_params=_params("arbitrary"))(qkv, dq, dk, dv, qg, kg)


def _na_geometry(n_lat):
    rows = n_lat // GRID_W
    kh = min(WIN_H, rows)
    nb = min(kh + QROWS - 1, rows)
    n_blk = rows // QROWS
    q_row_off = np.repeat(np.arange(QROWS), GRID_W)
    q_col = np.tile(np.arange(GRID_W), QROWS)
    k_row_off = np.repeat(np.arange(nb), GRID_W)
    k_col = np.tile(np.arange(GRID_W), nb)
    c_start = np.clip(q_col - WIN_W // 2, 0, GRID_W - WIN_W)
    col_ok = (k_col[None, :] >= c_start[:, None]) & (k_col[None, :] < c_start[:, None] + WIN_W)
    dc_idx = np.clip(k_col[None, :] - q_col[:, None], -(WIN_W - 1), WIN_W - 1) + WIN_W - 1

    def block(blk):
        r0 = blk * QROWS
        q_row = r0 + q_row_off
        r_start = np.clip(q_row - kh // 2, 0, rows - kh)
        band0 = min(int(np.clip(r0 - kh // 2, 0, rows - kh)), rows - nb)
        k_row = band0 + k_row_off
        ok = col_ok & (k_row[None, :] >= r_start[:, None]) & (k_row[None, :] < r_start[:, None] + kh)
        dr_idx = np.clip(k_row[None, :] - q_row[:, None], -(WIN_H - 1), WIN_H - 1) + WIN_H - 1
        return band0, ok, dr_idx

    reps = [0, 1, 2, n_blk - 2, n_blk - 1]
    variant = lambda blk: 0 if blk == 0 else 1 if blk == 1 else 3 if blk == n_blk - 2 else 4 if blk == n_blk - 1 else 2
    geo = [block(b) for b in reps]
    for blk in range(n_blk):
        _, ok, dr = block(blk)
        assert np.array_equal(ok, geo[variant(blk)][1]) and np.array_equal(np.where(ok, dr, 0), np.where(ok, geo[variant(blk)][2], 0))
    return dict(rows=rows, kh=kh, nb=nb, n_blk=n_blk, reps=reps, ok=[g[1] for g in geo], dr=[g[2] for g in geo],
                band0=[g[0] for g in geo], dc=dc_idx)


def _na_onehots(geo):
    nb = geo["nb"]
    w2 = GRID_W * GRID_W
    qc, kc = np.meshgrid(np.arange(GRID_W), np.arange(GRID_W), indexing="ij")
    diff = (kc - qc).reshape(-1)
    cols = np.zeros((w2, LANE), np.float32)
    sel = np.abs(diff) <= WIN_W - 1
    cols[np.arange(w2)[sel], diff[sel] + WIN_W - 1] = 1.0
    npair = len(geo["reps"]) * QROWS * nb
    kpad = -(-npair // LANE) * LANE
    rows = np.zeros((16, kpad), np.float32)
    for vi, blk in enumerate(geo["reps"]):
        for qr in range(QROWS):
            for kr in range(nb):
                dr = (geo["band0"][vi] + kr) - (blk * QROWS + qr)
                if abs(dr) <= WIN_H - 1:
                    rows[dr + WIN_H - 1, (vi * QROWS + qr) * nb + kr] = 1.0
    return cols, rows, npair, kpad


def na_bias_table(name, rpb, geo):
    n_heads, nb, nv = rpb.shape[0], geo["nb"], len(geo["reps"])
    cols, rows, npair, kpad = _na_onehots(geo)
    rpb_p = jnp.pad(rpb, ((0, 0), (0, 16 - rpb.shape[1]), (0, LANE - rpb.shape[2]))).transpose(1, 0, 2).reshape(16, n_heads * LANE)
    t1 = _mm(name + "_rows", jnp.asarray(rows.T), rpb_p, "nn", (kpad, n_heads * LANE), F32,
             (kpad, _tile(n_heads * LANE, 1024, LANE), 16), exact=True)
    t1 = t1[:npair].reshape(nv, QROWS * nb, n_heads, LANE).transpose(0, 2, 1, 3).reshape(nv * n_heads * QROWS * nb, LANE)
    m = t1.shape[0]
    flat = _mm(name + "_cols", t1, jnp.asarray(cols), "nt", (m, GRID_W * GRID_W), F32,
               (_tile(m, 512, 8), _tile(GRID_W * GRID_W, 2048, LANE), LANE), exact=True)
    tab = flat.reshape(nv, n_heads, QROWS, nb, GRID_W, GRID_W).transpose(0, 1, 2, 4, 3, 5).reshape(nv, n_heads, QROWS * GRID_W, nb * GRID_W)
    return jnp.where(jnp.asarray(np.stack(geo["ok"]))[:, None], tab, NEG)


def _na_tile_info(qt, geo):
    n_blk, rows, kh, nb = geo["n_blk"], geo["rows"], geo["kh"], geo["nb"]
    is_ctx = qt >= n_blk
    band0 = jnp.minimum(jnp.clip(qt * QROWS - kh // 2, 0, rows - kh), rows - nb)
    band0 = jnp.where(is_ctx, 0, band0)
    return is_ctx, pl.multiple_of(band0 * GRID_W, GRID_W)


def _na_variant(qt, n_blk):
    v = jnp.minimum(qt, 2) + (qt >= n_blk - 2).astype(jnp.int32) + (qt >= n_blk - 1).astype(jnp.int32)
    return jnp.minimum(v, 4)


def _na_probs(q, kb, kc, bias, is_ctx, scale):
    s_lat = _dot(q, kb, NT) * scale + bias
    s_lat = jnp.where(is_ctx, NEG, s_lat)
    s_ctx = _dot(q, kc, NT) * scale
    m = jnp.maximum(jnp.max(s_lat, axis=-1, keepdims=True), jnp.max(s_ctx, axis=-1, keepdims=True))
    e_lat, e_ctx = jnp.exp(s_lat - m), jnp.exp(s_ctx - m)
    den = jnp.sum(e_lat, axis=-1, keepdims=True) + jnp.sum(e_ctx, axis=-1, keepdims=True)
    return e_lat / den, e_ctx / den


NA_MAX_TILES_PER_STEP = 6


def _na_step_specs(t, qw, nk, hd, n_blk):
    per = max(n for n in range(1, NA_MAX_TILES_PER_STEP + 1) if (t // qw) % n == 0)
    tile = pl.BlockSpec((per * qw, hd), lambda h, i: (i, h))
    btabs = [pl.BlockSpec((None, None, qw, nk), functools.partial(lambda h, i, part: (_na_variant(per * i + part, n_blk), h, 0, 0), part=part))
             for part in range(per)]
    return per, tile, btabs


def na_attention_fwd(name, q, k, v, bias, geo, n_lat, hd, carry=None):
    t, d = q.shape
    qw, nk = QROWS * GRID_W, geo["nb"] * GRID_W
    scale = hd ** -0.5
    per, tile, btabs = _na_step_specs(t, qw, nk, hd, geo["n_blk"])

    def body(q_ref, k_ref, v_ref, *rest):
        b_refs, o_ref = rest[:per], rest[per]
        kc, vc = k_ref[n_lat:, :], v_ref[n_lat:, :]
        for part, b_ref in enumerate(b_refs):
            rows = slice(part * qw, (part + 1) * qw)
            is_ctx, start = _na_tile_info(per * pl.program_id(1) + part, geo)
            kb, vb = k_ref[pl.ds(start, nk), :], v_ref[pl.ds(start, nk), :]
            p_lat, p_ctx = _na_probs(q_ref[rows, :], kb, kc, b_ref[...], is_ctx, scale)
            o_ref[rows, :] = (_dot(p_lat, vb, NN) + _dot(p_ctx, vc, NN)).astype(o_ref.dtype)

    head = pl.BlockSpec((t, hd), lambda h, i: (0, h))
    res = _call(body, name, (d // hd, t // (per * qw)), [tile, head, head, *btabs], [tile],
                [jax.ShapeDtypeStruct((t, d), BF16)], (q, k, v, *([bias] * per)), ("parallel", "arbitrary"), carry=carry)
    return res[0] if carry is None else (res[0][0], res[1])


def na_attention_bwd(name, q, k, v, bias, do, geo, n_lat, hd, carry=None):
    t, d = q.shape
    qw, nk = QROWS * GRID_W, geo["nb"] * GRID_W
    n_blk = geo["n_blk"]
    scale = hd ** -0.5
    per, tile, btabs = _na_step_specs(t, qw, nk, hd, n_blk)

    def body(q_ref, k_ref, v_ref, *rest):
        b_refs, (do_ref, dq_ref, dk_ref, dv_ref, db_ref) = rest[:per], rest[per:]
        step = pl.program_id(1)

        @pl.when(step == 0)
        def _():
            dk_ref[...] = jnp.zeros_like(dk_ref)
            dv_ref[...] = jnp.zeros_like(dv_ref)
            db_ref[...] = jnp.zeros_like(db_ref)

        kc, vc = k_ref[n_lat:, :], v_ref[n_lat:, :]
        for part, b_ref in enumerate(b_refs):
            qt = per * step + part
            rows = slice(part * qw, (part + 1) * qw)
            is_ctx, start = _na_tile_info(qt, geo)
            band = pl.ds(start, nk)
            qv, dov = q_ref[rows, :], do_ref[rows, :]
            kb, vb = k_ref[band, :], v_ref[band, :]
            p_lat, p_ctx = _na_probs(qv, kb, kc, b_ref[...], is_ctx, scale)
            dp_lat, dp_ctx = _dot(dov, vb, NT), _dot(dov, vc, NT)
            delta = jnp.sum(p_lat * dp_lat, axis=-1, keepdims=True) + jnp.sum(p_ctx * dp_ctx, axis=-1, keepdims=True)
            ds_lat, ds_ctx = p_lat * (dp_lat - delta), p_ctx * (dp_ctx - delta)
            db_ref[_na_variant(qt, n_blk)] += ds_lat
            dsl, dsc = (ds_lat * scale).astype(BF16), (ds_ctx * scale).astype(BF16)
            dq_ref[rows, :] = _dot(dsl, kb, NN) + _dot(dsc, kc, NN)
            dk_ref[band, :] += _dot(dsl, qv, TN)
            dk_ref[n_lat:, :] += _dot(dsc, qv, TN)
            dv_ref[band, :] += _dot(p_lat, dov, TN)
            dv_ref[n_lat:, :] += _dot(p_ctx, dov, TN)

    nv = bias.shape[0]
    head = pl.BlockSpec((t, hd), lambda h, i: (0, h))
    full = jax.ShapeDtypeStruct((t, d), F32)
    res = _call(body, name, (d // hd, t // (per * qw)), [tile, head, head, *btabs, tile],
                [tile, head, head, pl.BlockSpec((nv, None, qw, nk), lambda h, i: (0, h, 0, 0))],
                [full, full, full, jax.ShapeDtypeStruct(bias.shape, F32)], (q, k, v, *([bias] * per), do), ("arbitrary", "arbitrary"), carry=carry)
    return tuple(res) if carry is None else (*res[0], res[1])


def na_rpb_grad(name, dbias, geo, n_heads):
    nb = geo["nb"]
    nv = len(geo["reps"])
    w2 = GRID_W * GRID_W
    cols, rows, npair, kpad = _na_onehots(geo)
    xmat = dbias.reshape(nv, n_heads, QROWS, GRID_W, nb, GRID_W).transpose(0, 1, 2, 4, 3, 5).reshape(nv * n_heads * QROWS * nb, w2)
    m = xmat.shape[0]
    r = _mm(name + "_cols", xmat, jnp.asarray(cols), "nn", (m, LANE), F32, (_tile(m, 512, 8), LANE, _tile(w2, 1024, LANE)), exact=True)
    r2 = r.reshape(nv, n_heads, QROWS * nb, LANE).transpose(0, 2, 1, 3).reshape(npair, n_heads * LANE)
    r2 = jnp.pad(r2, ((0, kpad - npair), (0, 0)))
    out = _mm(name + "_rows", jnp.asarray(rows), r2, "nn", (16, n_heads * LANE), F32, (16, _tile(n_heads * LANE, 1024, LANE), kpad), exact=True)
    return out[:2 * WIN_H - 1].reshape(2 * WIN_H - 1, n_heads, LANE)[:, :, :2 * WIN_W - 1].transpose(1, 0, 2)


def all_gather(name, arrays):
    n = len(arrays)

    def body(*refs):
        for phase in _gather_two_level(refs[:n], refs[n:2 * n], *refs[2 * n:]):
            phase()

    outs = pl.pallas_call(body, name=name, in_specs=[HBM] * n, out_specs=[HBM] * n, out_shape=_landing("gather", arrays),
                          scratch_shapes=_exchange_scratch(n))(*arrays)
    return list(outs)


def _adamw_math(w, g, m, v):
    m = ADAM_B1 * m + (1.0 - ADAM_B1) * g
    v = ADAM_B2 * v + (1.0 - ADAM_B2) * (g * g)
    m_hat = m / (1.0 - ADAM_B1 ** ADAM_STEP)
    v_hat = v / (1.0 - ADAM_B2 ** ADAM_STEP)
    delta = -ADAM_LR * (m_hat / (jnp.sqrt(v_hat) + ADAM_EPS) + ADAM_WD * w)
    return delta, m, v


def adamw(name, parts, w, m, v):
    r, c = w.shape
    npart, _, cp = parts.shape
    tc = c if (c % LANE or cp != c) else _tile(c, 512, LANE)
    tr = _tile(r, 256, 16 if parts.dtype == BF16 else 8)
    if tr < 64:
        tr, tc = r, (tc if tc == c and cp != c else _tile(c, 256, LANE))
    tcp = cp if tc == c else tc

    def body(p_ref, w_ref, m_ref, v_ref, g_ref, d_ref, mo_ref, vo_ref):
        g = p_ref[0].astype(F32)
        for s in range(1, npart):
            g = g + p_ref[s].astype(F32)
        g = g[:, :tc]
        g_ref[...] = g
        d_ref[...], mo_ref[...], vo_ref[...] = _adamw_math(w_ref[...], g, m_ref[...], v_ref[...])

    blk = pl.BlockSpec((tr, tc), lambda i, j: (i, j))
    out = jax.ShapeDtypeStruct((r, c), F32)
    return pl.pallas_call(
        body, name=name, grid=(r // tr, c // tc),
        in_specs=[pl.BlockSpec((npart, tr, tcp), lambda i, j: (0, i, j)), blk, blk, blk],
        out_specs=(blk,) * 4, out_shape=(out,) * 4, compiler_params=_params("parallel", "parallel"))(parts, w, m, v)


def sum_parts(name, parts):
    npart, r, c = parts.shape
    tr = _tile(r, 512, 8)

    def body(p_ref, o_ref):
        g = p_ref[0]
        for s in range(1, npart):
            g = g + p_ref[s]
        o_ref[...] = g

    return pl.pallas_call(
        body, name=name, grid=(r // tr,), in_specs=[pl.BlockSpec((npart, tr, c), lambda i: (0, i, 0))],
        out_specs=pl.BlockSpec((tr, c), lambda i: (i, 0)), out_shape=jax.ShapeDtypeStruct((r, c), F32),
        compiler_params=_params("parallel"))(parts)


class _Pack:
    def __init__(self, shapes):
        self.shapes = dict(shapes)
        self.offsets, off = {}, 0
        for name, shape in self.shapes.items():
            self.offsets[name] = off
            off += -(-int(np.prod(shape)) // (8 * LANE)) * (8 * LANE)
        self.used = off
        self.rows = -(-off // (512 * LANE)) * 512

    def pack(self, values):
        pieces = []
        for name, shape in self.shapes.items():
            size = int(np.prod(shape))
            padded = -(-size // (8 * LANE)) * (8 * LANE)
            pieces.append(jnp.pad(values[name].astype(F32).reshape(-1), (0, padded - size)))
        pieces.append(jnp.zeros((self.rows * LANE - self.used,), F32))
        return jnp.concatenate(pieces).reshape(self.rows, LANE)

    def unpack(self, flat, lead=()):
        flat = flat.reshape(*lead, self.rows * LANE)
        out = {}
        for name, shape in self.shapes.items():
            size = int(np.prod(shape))
            out[name] = flat[..., self.offsets[name]:self.offsets[name] + size].reshape(*lead, *shape)
        return out


def kernel(x, c, ctx, c_ctx, norm_mix_g, norm_ffn_g, w_ada, b_ada, na_w_qkv, na_q_g, na_k_g, na_rpb, na_w_o, gm_w_in, gm_v_g, gm_w_s, gm_b_s, gm_w_out, sc_w_in, sc_conv_w, sc_w_out, ffn_w_up, ffn_conv_w, ffn_conv_b, ffn_w_down, loss_target, m_c_ctx, m_norm_mix_g, m_norm_ffn_g, m_w_ada, m_b_ada, m_na_w_qkv, m_na_q_g, m_na_k_g, m_na_rpb, m_na_w_o, m_gm_w_in, m_gm_v_g, m_gm_w_s, m_gm_b_s, m_gm_w_out, m_sc_w_in, m_sc_conv_w, m_sc_w_out, m_ffn_w_up, m_ffn_conv_w, m_ffn_conv_b, m_ffn_w_down, v_c_ctx, v_norm_mix_g, v_norm_ffn_g, v_w_ada, v_b_ada, v_na_w_qkv, v_na_q_g, v_na_k_g, v_na_rpb, v_na_w_o, v_gm_w_in, v_gm_v_g, v_gm_w_s, v_gm_b_s, v_gm_w_out, v_sc_w_in, v_sc_conv_w, v_sc_w_out, v_ffn_w_up, v_ffn_conv_w, v_ffn_conv_b, v_ffn_w_down):
    weights = dict(c_ctx=c_ctx, norm_mix_g=norm_mix_g, norm_ffn_g=norm_ffn_g, w_ada=w_ada, b_ada=b_ada, na_w_qkv=na_w_qkv,
                   na_q_g=na_q_g, na_k_g=na_k_g, na_rpb=na_rpb, na_w_o=na_w_o, gm_w_in=gm_w_in, gm_v_g=gm_v_g, gm_w_s=gm_w_s,
                   gm_b_s=gm_b_s, gm_w_out=gm_w_out, sc_w_in=sc_w_in, sc_conv_w=sc_conv_w, sc_w_out=sc_w_out,
                   ffn_w_up=ffn_w_up, ffn_conv_w=ffn_conv_w, ffn_conv_b=ffn_conv_b, ffn_w_down=ffn_w_down)
    mom_m = dict(c_ctx=m_c_ctx, norm_mix_g=m_norm_mix_g, norm_ffn_g=m_norm_ffn_g, w_ada=m_w_ada, b_ada=m_b_ada, na_w_qkv=m_na_w_qkv,
                 na_q_g=m_na_q_g, na_k_g=m_na_k_g, na_rpb=m_na_rpb, na_w_o=m_na_w_o, gm_w_in=m_gm_w_in, gm_v_g=m_gm_v_g,
                 gm_w_s=m_gm_w_s, gm_b_s=m_gm_b_s, gm_w_out=m_gm_w_out, sc_w_in=m_sc_w_in, sc_conv_w=m_sc_conv_w,
                 sc_w_out=m_sc_w_out, ffn_w_up=m_ffn_w_up, ffn_conv_w=m_ffn_conv_w, ffn_conv_b=m_ffn_conv_b, ffn_w_down=m_ffn_w_down)
    mom_v = dict(c_ctx=v_c_ctx, norm_mix_g=v_norm_mix_g, norm_ffn_g=v_norm_ffn_g, w_ada=v_w_ada, b_ada=v_b_ada, na_w_qkv=v_na_w_qkv,
                 na_q_g=v_na_q_g, na_k_g=v_na_k_g, na_rpb=v_na_rpb, na_w_o=v_na_w_o, gm_w_in=v_gm_w_in, gm_v_g=v_gm_v_g,
                 gm_w_s=v_gm_w_s, gm_b_s=v_gm_b_s, gm_w_out=v_gm_w_out, sc_w_in=v_sc_w_in, sc_conv_w=v_sc_conv_w,
                 sc_w_out=v_sc_w_out, ffn_w_up=v_ffn_w_up, ffn_conv_w=v_ffn_conv_w, ffn_conv_b=v_ffn_conv_b, ffn_w_down=v_ffn_w_down)
    names = list(weights)

    n_lat, d = x.shape[1], x.shape[2]
    n_ctx = ctx.shape[1]
    t = n_lat + n_ctx
    depth = norm_mix_g.shape[0]
    hd = na_q_g.shape[-1]
    n_heads = d // hd
    nup = ffn_w_up.shape[-1]
    nup_p = -(-nup // LANE) * LANE
    fdim, fp = 4 * nup, 4 * nup_p
    me = _slot(_me())
    geo = _na_geometry(n_lat)

    pad_up = lambda a: jnp.pad(a, [(0, 0)] * (a.ndim - 1) + [(0, nup_p - nup)])
    mixer_weights = (("na_w_qkv", "na_w_o"), ("gm_w_in", "gm_w_out"), ("sc_w_in", "sc_w_out"))

    def shards(i):
        w_in, w_out = (weights[nm][i // N_MIXERS].astype(BF16) for nm in mixer_weights[i % N_MIXERS])
        return [w_in, w_out, pad_up(ffn_w_up[i]).astype(BF16), ffn_w_down[i].astype(BF16)]

    def operands(g_in, g_out, g_up, g_down):
        down = jnp.pad(g_down.reshape(4, nup, d), ((0, 0), (0, nup_p - nup), (0, 0))).reshape(fp, d)
        return dict(w_in=g_in, w_out=g_out.reshape(-1, d), w_up=g_up, w_down=down)

    first_in, *first_rest = shards(0)
    g_in0, g_fcw, g_scw, c_all = all_gather("gather_first", [first_in, pad_up(ffn_conv_w), sc_conv_w, c])
    layer_w = []
    f_cw = [g_fcw[:, i].transpose(1, 0, 2).reshape(3, 2, fp).transpose(1, 0, 2) for i in range(depth)]
    cb_p = pad_up(ffn_conv_b.reshape(depth, N_DEV, nup)).reshape(depth, 2, 1, fp)
    s_cw = [g_scw[:, j].transpose(1, 0, 2).reshape(3, d) for j in range(sc_conv_w.shape[0])]

    cond = jnp.concatenate([c_all.reshape(N_DEV, d), c_ctx[None], jnp.zeros((7, d), F32)])
    mod_cols = mm_nn("ada_fwd", cond, w_ada, a_silu=True)
    (mod_all,) = all_gather("gather_mod", [mod_cols])
    ncol = w_ada.shape[-1]
    mod_all = mod_all.reshape(N_DEV, 16, depth, ncol).transpose(2, 1, 0, 3).reshape(depth, 16, N_MOD * d) + b_ada[:, None, :]
    mod_lat = lax.dynamic_index_in_dim(mod_all, me, axis=1, keepdims=False)
    mods = jnp.stack([mod_lat, mod_all[:, N_DEV]], axis=1).reshape(depth, 2, N_MOD, d)
    sh1, sc1, g1, sh2, sc2, g2 = (mods[:, :, kd] for kd in range(N_MOD))

    xs = jnp.concatenate([x[0], ctx[0]], axis=0)
    saved = []
    prev = None
    for i in range(depth):
        mixer, j = i % N_MIXERS, i // N_MIXERS
        s = {}
        if prev is None:
            s["x"] = xs
            s["h"] = resid_rms_mod(f"l{i}_norm_mix", xs, None, None, norm_mix_g[i:i + 1], sc1[i], sh1[i], n_lat)
        else:
            s["x"], s["h"] = resid_rms_mod(f"l{i}_norm_mix", prev[0], prev[1], prev[2], norm_mix_g[i:i + 1], sc1[i], sh1[i], n_lat)
        nxt = shards(i + 1) if i + 1 < depth else None
        ride = (first_rest[:1] if i == 0 else []) + ([nxt[1]] if nxt else [])
        w_in = g_in0 if i == 0 else layer_w[i]["w_in"]
        in_name = f"l{i}_" + ("qkv", "gm_in", "sc_in")[mixer]
        if ride:
            pre, rode = mm_nn(in_name, s["h"], w_in, carry=("gather", ride))
        else:
            pre, rode = mm_nn(in_name, s["h"], w_in), []
        if mixer == 0:
            s["qkv"] = pre
            s["q"], s["k"], s["v"] = qk_norm_fwd(f"l{i}_qk_norm", s["qkv"], na_q_g[j:j + 1], na_k_g[j:j + 1], hd)
            s["bias"] = na_bias_table(f"l{i}_bias", na_rpb[j], geo)
            if i == 0:
                s["o"], (g_up0, g_down0) = na_attention_fwd(f"l{i}_attn", s["q"], s["k"], s["v"], s["bias"], geo, n_lat, hd,
                                                            carry=("gather", first_rest[1:]))
                layer_w.append(operands(g_in0, rode[0], g_up0, g_down0))
            else:
                s["o"] = na_attention_fwd(f"l{i}_attn", s["q"], s["k"], s["v"], s["bias"], geo, n_lat, hd)
        elif mixer == 1:
            s["t"] = pre
            s["o"] = gmlp_gate_fwd(f"l{i}_gm_gate", s["t"], gm_v_g[j:j + 1], gm_w_s[j], gm_b_s[j][:, :, None])
        else:
            s["t"] = pre
            s["o"] = sc_gate_fwd(f"l{i}_sc_gate", s["t"], s_cw[j], n_lat)
        lw = layer_w[i]
        s["y"] = mm_nn(f"l{i}_mix_out", s["o"], lw["w_out"])
        s["x1"], s["hf"] = resid_rms_mod(f"l{i}_norm_ffn", s["x"], s["y"], g1[i], norm_ffn_g[i:i + 1], sc2[i], sh2[i], n_lat)
        if nxt:
            n_in, _, n_up, n_down = nxt
            s["u"], (g_up,) = mm_nn(f"l{i}_ffn_up", s["hf"], lw["w_up"], out_parts=2, carry=("gather", [n_up]))
            s["z"], s["a"], (g_down,) = ffn_act_fwd(f"l{i}_ffn_act", s["u"], f_cw[i], cb_p[i], n_lat, carry=("gather", [n_down]))
            s["f"], (g_in,) = mm_nn(f"l{i}_ffn_down", s["a"], lw["w_down"], carry=("gather", [n_in]))
            layer_w.append(operands(g_in, rode[-1], g_up, g_down))
        else:
            s["u"] = mm_nn(f"l{i}_ffn_up", s["hf"], lw["w_up"], out_parts=2)
            s["z"], s["a"] = ffn_act_fwd(f"l{i}_ffn_act", s["u"], f_cw[i], cb_p[i], n_lat)
            s["f"] = mm_nn(f"l{i}_ffn_down", s["a"], lw["w_down"])
        prev = (s["x1"], s["f"], g2[i])
        saved.append(s)

    dx, loss_local, df, dg2 = loss_head("loss_head", prev[0], prev[1], prev[2], loss_target[0])
    loss = lax.psum(loss_local[0, 0], AXES)

    big = {}
    small = {}
    dmod = [None] * depth
    zeros_like_param = lambda p: [None] * p.shape[0]
    for nm in ("na_w_qkv", "na_w_o", "gm_w_in", "gm_w_out", "sc_w_in", "sc_w_out", "ffn_w_up", "ffn_w_down"):
        big[nm] = zeros_like_param(weights[nm])
    for nm in ("norm_mix_g", "norm_ffn_g", "ffn_conv_w", "ffn_conv_b", "na_q_g", "na_k_g", "na_rpb", "gm_v_g", "gm_w_s", "gm_b_s", "sc_conv_w"):
        small[nm] = zeros_like_param(weights[nm])
    landed, pending = {}, []

    def take(wanted):
        keys = [k for k in pending if wanted(k)]
        for k in keys:
            pending.remove(k)
        return keys

    def riding(keys, fn, *args, **kw):
        if not keys:
            return fn(*args, **kw)
        *res, got = fn(*args, carry=("scatter", [big[nm][idx] for nm, idx in keys]), **kw)
        landed.update(zip(keys, got))
        return res[0] if len(res) == 1 else tuple(res)

    for i in reversed(range(depth)):
        mixer, j = i % N_MIXERS, i // N_MIXERS
        nm_in, nm_out = mixer_weights[mixer]
        s, lw = saved[i], layer_w[i]
        da = riding(take(lambda k: k[0] in [m[1] for m in mixer_weights]), mm_nt, f"l{i}_b_ffn_down_x", df, lw["w_down"])
        dwd = mm_tn(f"l{i}_b_ffn_down_w", s["a"], df)
        big["ffn_w_down"][i] = dwd.reshape(4, nup_p, d)[:, :nup].reshape(N_DEV, fdim // N_DEV, d)
        du, dcw, dcb = riding(take(lambda k: k[0] in [m[0] for m in mixer_weights]), ffn_act_bwd,
                              f"l{i}_b_ffn_act", s["z"], s["u"], da, f_cw[i], n_lat)
        small["ffn_conv_w"][i] = dcw.transpose(1, 0, 2).reshape(3, N_DEV, nup_p)[:, :, :nup].reshape(3, 2 * fdim)
        small["ffn_conv_b"][i] = dcb.reshape(N_DEV, nup_p)[:, :nup].reshape(2 * fdim)
        dhf = riding([("ffn_w_down", i)], mm_nt, f"l{i}_b_ffn_up_x", du, lw["w_up"])
        big["ffn_w_up"][i] = riding(take(lambda k: True), mm_tn, f"l{i}_b_ffn_up_w", s["hf"], du, out_parts=N_DEV)
        pending.append(("ffn_w_up", i))
        dx1, dsh2, dsc2, dgf, dy, dg1 = rms_mod_bwd(f"l{i}_b_norm_ffn", s["x1"], dhf, norm_ffn_g[i:i + 1], sc2[i], dx, n_lat,
                                                    branch=(s["y"], g1[i]))
        small["norm_ffn_g"][i] = dgf[0]
        do = mm_nt(f"l{i}_b_mix_out_x", dy, lw["w_out"], out_dtype=BF16 if mixer == 0 else F32)
        big[nm_out][j] = mm_tn(f"l{i}_b_mix_out_w", s["o"], dy).reshape(N_DEV, -1, d)
        pending.append((nm_out, j))
        if mixer == 0:
            dq, dk, dv, dbias = riding(take(lambda k: True), na_attention_bwd,
                                       f"l{i}_b_attn", s["q"], s["k"], s["v"], s["bias"], do, geo, n_lat, hd)
            small["na_rpb"][j] = na_rpb_grad(f"l{i}_b_rpb", dbias, geo, n_heads)
            dt, dqg, dkg = qk_norm_bwd(f"l{i}_b_qk_norm", s["qkv"], dq, dk, dv, na_q_g[j:j + 1], na_k_g[j:j + 1], hd)
            small["na_q_g"][j], small["na_k_g"][j] = dqg[0], dkg[0]
        elif mixer == 1:
            dt, dvg, dws, dsv = gmlp_gate_bwd(f"l{i}_b_gm_gate", s["t"], do, gm_v_g[j:j + 1], gm_w_s[j], gm_b_s[j][:, :, None])
            groups, width = gm_w_s.shape[1], dsv.shape[1]
            group_of = np.zeros((width, LANE), np.float32)
            group_of[np.arange(width), np.arange(width) // (width // groups)] = 1.0
            dbs = _mm(f"l{i}_b_gm_bs", dsv, jnp.asarray(group_of), "nn", (GM_CHUNK, LANE), F32,
                      (GM_CHUNK, LANE, _tile(width, 2048, LANE)), exact=True)[:, :groups].T
            small["gm_v_g"][j], small["gm_w_s"][j], small["gm_b_s"][j] = dvg[0], dws, dbs
        else:
            dt, dscw = sc_gate_bwd(f"l{i}_b_sc_gate", s["t"], do, s_cw[j], n_lat)
            small["sc_conv_w"][j] = dscw
            dt = dt.transpose(1, 0, 2).reshape(t, 3 * d)
        dh = mm_nt(f"l{i}_b_mix_in_x", dt, lw["w_in"])
        big[nm_in][j] = mm_tn(f"l{i}_b_mix_in_w", s["h"], dt, out_parts=N_DEV)
        pending.append((nm_in, j))
        dmod_ffn = [dsh2, dsc2, dg2]
        if i > 0:
            dx, dsh1, dsc1, dgm, df, dg2 = rms_mod_bwd(f"l{i}_b_norm_mix", s["x"], dh, norm_mix_g[i:i + 1], sc1[i], dx1, n_lat,
                                                       branch=(saved[i - 1]["f"], g2[i - 1]))
        else:
            dx, dsh1, dsc1, dgm = riding(take(lambda k: True), rms_mod_bwd,
                                         f"l{i}_b_norm_mix", s["x"], dh, norm_mix_g[i:i + 1], sc1[i], dx1, n_lat)
        small["norm_mix_g"][i] = dgm[0]
        dmod[i] = jnp.stack([dsh1, dsc1, dg1] + dmod_ffn, axis=1)
    grad_x = dx[:n_lat][None]

    small = {nm: jnp.stack(v) for nm, v in small.items()}
    small["dmod"] = jnp.stack(dmod)
    pack = _Pack({nm: v.shape for nm, v in small.items()})
    (small_all,) = all_gather("gather_small", [pack.pack(small)])
    small_sum = pack.unpack(sum_parts("sum_small", small_all))
    dmod_all = pack.unpack(small_all, lead=(N_DEV,))["dmod"]
    dmod_ctx = small_sum["dmod"][:, 1].reshape(depth, N_MOD * d)
    grads = {nm: small_sum[nm] for nm in small if nm != "dmod"}
    grads["b_ada"] = (small_sum["dmod"][:, 0] + small_sum["dmod"][:, 1]).reshape(depth, N_MOD * d)
    my_cols = lambda a, width: lax.dynamic_slice_in_dim(a, me * width, width, axis=-1)
    grads["ffn_conv_w"] = my_cols(grads["ffn_conv_w"], nup)
    grads["sc_conv_w"] = my_cols(grads["sc_conv_w"], sc_conv_w.shape[-1])

    drows = jnp.concatenate([dmod_all[:, :, 0], dmod_all[:, :, 1]]).reshape(2 * N_DEV, depth, N_MOD * d)
    drows = my_cols(drows, ncol).reshape(2 * N_DEV, depth * ncol)
    cond2 = jnp.concatenate([c_all.reshape(N_DEV, d), jnp.broadcast_to(c_ctx[None], (N_DEV, d))])
    g_w_ada = mm_tn("ada_bwd_w", cond2, drows, out_parts=depth, out_dtype=F32, a_silu=True)
    dctx_rows = jnp.pad(my_cols(dmod_ctx, ncol).reshape(1, depth * ncol), ((0, 15), (0, 0)))
    dcc = mm_nt("ada_bwd_c", dctx_rows, w_ada)[0:1]
    (dcc_all,) = all_gather("gather_c_ctx", [jnp.pad(dcc.reshape(-1, LANE), ((0, (-d // LANE) % 8), (0, 0)))])
    dcc_sum = sum_parts("sum_c_ctx", dcc_all).reshape(-1)[:d]
    sig = jax.nn.sigmoid(c_ctx)
    grads["c_ctx"] = dcc_sum * (sig * (1.0 + c_ctx * (1.0 - sig)))

    assert not pending
    out = {nm: [None] * 4 for nm in names}
    per_big = {nm: [] for nm in big}
    for nm, idx in [(nm, idx) for nm in big for idx in range(len(big[nm]))]:
        w2, p = weights[nm][idx], landed[nm, idx]
        per_big[nm].append(adamw(f"adamw_{nm}{idx}", p.reshape(N_DEV, w2.shape[0], -1), w2, mom_m[nm][idx], mom_v[nm][idx]))
    for nm, res in per_big.items():
        out[nm] = [jnp.stack([r[q] for r in res]) for q in range(4)]
    res = [adamw(f"adamw_w_ada{i}", g_w_ada[i][None], w_ada[i], m_w_ada[i], v_w_ada[i]) for i in range(depth)]
    out["w_ada"] = [jnp.stack([r[q] for r in res]) for q in range(4)]
    small_names = [nm for nm in names if nm not in big and nm != "w_ada"]
    spack = _Pack({nm: weights[nm].shape for nm in small_names})
    flat = [spack.pack({nm: src[nm] for nm in small_names}) for src in (grads, weights, mom_m, mom_v)]
    res = adamw("adamw_small", flat[0][None], flat[1], flat[2], flat[3])
    res = [spack.unpack(r) for r in res]
    for nm in small_names:
        out[nm] = [grads[nm].reshape(weights[nm].shape)] + [res[q][nm] for q in range(1, 4)]

    return (loss, grad_x, *[out[nm][0] for nm in names], *[out[nm][1] for nm in names],
            *[out[nm][2] for nm in names], *[out[nm][3] for nm in names])
```

```python
import functools
import math

import numpy as np
import jax
import jax.numpy as jnp
from jax import lax
from jax.experimental import pallas as pl
from jax.experimental.pallas import tpu as pltpu

F32 = jnp.float32
BF16 = jnp.bfloat16
MESH = pl.DeviceIdType.MESH
AXES = ("x", "y", "c")
N_DEV = 8
N_MOD = 6
N_MIXERS = 3
EPS = 1e-6
GRID_W = 64
WIN_H = 8
WIN_W = 16
QROWS = 2
GM_CHUNK = 128
LANE = 128
NEG = -1e30
ADAM_LR = 0.001
ADAM_B1 = 0.9
ADAM_B2 = 0.999
ADAM_EPS = 1e-08
ADAM_WD = 0.01
ADAM_STEP = 10
VMEM_LIMIT = 56 * 1024 * 1024
HBM = pl.BlockSpec(memory_space=pltpu.HBM)

NN = (((1,), (0,)), ((), ()))
NT = (((1,), (1,)), ((), ()))
TN = (((0,), (0,)), ((), ()))


def _params(*sem):
    return pltpu.CompilerParams(dimension_semantics=sem, vmem_limit_bytes=VMEM_LIMIT)


def _tile(n, pref, mult):
    best = None
    for t in range(mult, min(n, pref) + 1, mult):
        if n % t == 0:
            best = t
    return n if best is None else best


def _full(arr):
    nd = arr.ndim
    return pl.BlockSpec(arr.shape, lambda *g: (0,) * nd)


def _logical(shape):
    return tuple(shape) if len(shape) == 2 else (shape[1], shape[0] * shape[2])


def _cspec(shape, tr, tc, rc):
    if len(shape) == 2:
        return pl.BlockSpec((tr, tc), rc)
    cpp = shape[2] // tc

    def imap(*g):
        r, c = rc(*g)
        return (c // cpp, r, c % cpp)

    return pl.BlockSpec((None, tr, tc), imap)


def _dot(a, b, dims, exact=False):
    if exact:
        return lax.dot_general(a, b, dims, precision=lax.Precision.HIGHEST, preferred_element_type=F32)
    return lax.dot_general(a.astype(BF16), b.astype(BF16), dims, preferred_element_type=F32)


def _silu(z):
    return z * jax.nn.sigmoid(z)


def _me():
    return lax.axis_index("x"), lax.axis_index("y"), lax.axis_index("c")


def _slot(p):
    return 4 * p[0] + 2 * p[1] + p[2]


def _scatter_direct(srcs, dsts, send_sems, recv_sems, local_sems):
    x, y, c = _me()
    me = (x, y, c)
    peers = [((x + (k >> 2)) % 2, (y + ((k >> 1) & 1)) % 2, (c + (k & 1)) % 2) for k in range(1, N_DEV)]

    def remote(a, k, peer, src_dev, dst_dev):
        return pltpu.make_async_remote_copy(src_ref=srcs[a].at[_slot(src_dev)], dst_ref=dsts[a].at[_slot(dst_dev)], send_sem=send_sems.at[a, k],
                                            recv_sem=recv_sems.at[a, k], device_id=peer, device_id_type=MESH)

    def local(a):
        return pltpu.make_async_copy(srcs[a].at[_slot(me)], dsts[a].at[_slot(me)], local_sems.at[a])

    def start():
        for a in range(len(srcs)):
            local(a).start()
            for k, peer in enumerate(peers):
                remote(a, k, peer, peer, me).start()

    def wait():
        for a in range(len(srcs)):
            for k, peer in enumerate(peers):
                remote(a, k, peer, me, peer).wait_recv()
        for a in range(len(srcs)):
            for k, peer in enumerate(peers):
                remote(a, k, peer, peer, me).wait_send()
            local(a).wait()

    return start, lambda: None, wait


def _gather_two_level(srcs, dsts, send_sems, recv_sems, local_sems):
    x, y, c = _me()
    me, sib = (x, y, c), (x, y, 1 - c)
    chips = [(1 - x, y), (x, 1 - y), (1 - x, 1 - y)]

    def copy(a, k, block, to, src=None):
        dst = dsts[a].at[_slot(block)]
        return pltpu.make_async_remote_copy(src_ref=dst if src is None else src, dst_ref=dst, send_sem=send_sems.at[a, k],
                                            recv_sem=recv_sems.at[a, k], device_id=to, device_id_type=MESH)

    def mine(a):
        return pltpu.make_async_copy(srcs[a], dsts[a].at[_slot(me)], local_sems.at[a])

    def first(a):
        return [copy(a, 0, me, sib, src=srcs[a])] + [copy(a, 1 + j, me, (*chip, c), src=srcs[a]) for j, chip in enumerate(chips)]

    def passed(a, j):
        return copy(a, 4 + j, (*chips[j], c), sib)

    def start():
        for a in range(len(srcs)):
            mine(a).start()
            for cp in first(a):
                cp.start()

    def pass_on():
        for j, chip in enumerate(chips):
            for a in range(len(srcs)):
                copy(a, 1 + j, (*chip, c), me).wait_recv()
                passed(a, j).start()

    def wait():
        for a in range(len(srcs)):
            copy(a, 0, sib, me).wait_recv()
            for j, chip in enumerate(chips):
                copy(a, 4 + j, (*chip, 1 - c), me).wait_recv()
        for a in range(len(srcs)):
            for cp in first(a) + [passed(a, j) for j in range(len(chips))]:
                cp.wait_send()
            mine(a).wait()

    return start, pass_on, wait


def _exchange_scratch(n):
    return [pltpu.SemaphoreType.DMA((n, N_DEV - 1)), pltpu.SemaphoreType.DMA((n, N_DEV - 1)), pltpu.SemaphoreType.DMA((n,))]


def _landing(kind, arrays):
    return [jax.ShapeDtypeStruct((N_DEV, *a.shape) if kind == "gather" else a.shape, a.dtype) for a in arrays]


def _call(body, name, grid, in_specs, out_specs, out_shape, ins, sem, scratch=(), carry=None):
    in_specs, out_specs, out_shape, scratch = list(in_specs), list(out_specs), list(out_shape), list(scratch)
    if carry is None:
        return list(pl.pallas_call(body, name=name, grid=grid, in_specs=in_specs, out_specs=out_specs, out_shape=out_shape,
                                   scratch_shapes=scratch, compiler_params=_params(*sem))(*ins))
    kind, arrays = carry
    n, ni, no, ns = len(arrays), len(in_specs), len(out_specs), len(scratch)

    def carrying(*refs):
        own_in, srcs = refs[:ni], refs[ni:ni + n]
        own_out, dsts = refs[ni + n:ni + n + no], refs[ni + n + no:ni + 2 * n + no]
        own_scratch, sems = refs[ni + 2 * n + no:ni + 2 * n + no + ns], refs[ni + 2 * n + no + ns:]
        start, pass_on, wait = (_gather_two_level if kind == "gather" else _scatter_direct)(srcs, dsts, *sems)
        step = functools.reduce(lambda lin, ax: lin * grid[ax] + pl.program_id(ax), range(len(grid)), 0)
        steps = math.prod(grid)
        pl.when(step == 0)(start)
        body(*own_in, *own_out, *own_scratch)
        if kind == "gather":
            pl.when(step == steps // 2)(pass_on)
        pl.when(step == steps - 1)(wait)

    res = pl.pallas_call(
        carrying, name=name, grid=grid, in_specs=in_specs + [HBM] * n, out_specs=out_specs + [HBM] * n,
        out_shape=out_shape + _landing(kind, arrays), scratch_shapes=scratch + _exchange_scratch(n),
        compiler_params=_params(*(["arbitrary"] * len(grid))))(*ins, *arrays)
    return list(res[:no]), list(res[no:])


def _mm(name, a, b, kind, out_shape, out_dtype, tiles, a_silu=False, exact=False, carry=None):
    la, lb, lo = _logical(a.shape), _logical(b.shape), _logical(out_shape)
    t0, t1, t2 = tiles
    if kind == "nn":
        grid = (lo[1] // t1, lo[0] // t0, la[1] // t2)
        a_spec = _cspec(a.shape, t0, t2, lambda j, i, k: (i, k))
        b_spec = _cspec(b.shape, t2, t1, lambda j, i, k: (k, j))
        o_spec = _cspec(out_shape, t0, t1, lambda j, i, k: (i, j))
        dims, acc = NN, (t0, t1)
    elif kind == "nt":
        grid = (lo[1] // t1, lo[0] // t0, la[1] // t2)
        a_spec = _cspec(a.shape, t0, t2, lambda p, i, r: (i, r))
        b_spec = _cspec(b.shape, t1, t2, lambda p, i, r: (p, r))
        o_spec = _cspec(out_shape, t0, t1, lambda p, i, r: (i, p))
        dims, acc = NT, (t0, t1)
    else:
        grid = (lo[1] // t1, lo[0] // t0, la[0] // t2)
        a_spec = _cspec(a.shape, t2, t0, lambda j, kk, r: (r, kk))
        b_spec = _cspec(b.shape, t2, t1, lambda j, kk, r: (r, j))
        o_spec = _cspec(out_shape, t0, t1, lambda j, kk, r: (kk, j))
        dims, acc = TN, (t0, t1)
    nk = grid[2]
    in_place = out_dtype == F32

    def body(a_ref, b_ref, o_ref, *scratch):
        acc_ref = o_ref if in_place else scratch[0]
        k = pl.program_id(2)
        av = a_ref[...]
        if a_silu:
            av = _silu(av)
        part = _dot(av, b_ref[...], dims, exact)

        @pl.when(k == 0)
        def _():
            acc_ref[...] = part

        @pl.when(k > 0)
        def _():
            acc_ref[...] += part

        if not in_place:
            @pl.when(k == nk - 1)
            def _():
                o_ref[...] = acc_ref[...].astype(o_ref.dtype)

    res = _call(body, name, grid, [a_spec, b_spec], [o_spec], [jax.ShapeDtypeStruct(out_shape, out_dtype)], (a, b),
                ("parallel", "parallel", "arbitrary"), scratch=[] if in_place else [pltpu.VMEM(acc, F32)], carry=carry)
    return res[0] if carry is None else (res[0][0], res[1])


MM_VMEM_BUDGET = 40 * 1024 * 1024
MM_TILE_CAP = 2048


def _divisors(n, mult):
    return [t for t in range(mult, min(n, MM_TILE_CAP) + 1, mult) if n % t == 0] or [n]


def _mm_tiles(c0, c1, c2, a, b, out_dtype):
    ia, ib, io = a.dtype.itemsize, b.dtype.itemsize, jnp.dtype(out_dtype).itemsize
    best, best_score = None, -1
    for t0 in c0:
        for t1 in c1:
            for t2 in c2:
                need = 2 * (t0 * t2 * ia + t1 * t2 * ib + t0 * t1 * io) + t0 * t1 * 4 * (1 if out_dtype == F32 else 2)
                score = (t0 * t1 * t2, t2)
                if need <= MM_VMEM_BUDGET and score > (best_score if best else (-1, -1)):
                    best, best_score = (t0, t1, t2), score
    return best if best else (c0[0], c1[0], c2[0])


def mm_nn(name, a, w, out_parts=1, out_dtype=F32, **kw):
    (m, _), (_, n) = _logical(a.shape), _logical(w.shape)
    out_shape = (m, n) if out_parts == 1 else (out_parts, m, n // out_parts)
    tiles = _mm_tiles(_divisors(m, 16), _divisors(math.gcd(w.shape[-1], out_shape[-1]), LANE),
                      _divisors(math.gcd(a.shape[-1], w.shape[-2]), LANE), a, w, out_dtype)
    return _mm(name, a, w, "nn", out_shape, out_dtype, tiles, **kw)


def mm_nt(name, a, w, out_dtype=F32, **kw):
    (m, _), (p, _) = _logical(a.shape), _logical(w.shape)
    tiles = _mm_tiles(_divisors(m, 16), _divisors(w.shape[-2], LANE), _divisors(math.gcd(a.shape[-1], w.shape[-1]), LANE), a, w, out_dtype)
    return _mm(name, a, w, "nt", (m, p), out_dtype, tiles, **kw)


def mm_tn(name, a, dy, out_parts=1, out_dtype=BF16, **kw):
    (t, k), (_, n) = _logical(a.shape), _logical(dy.shape)
    out_shape = (k, n) if out_parts == 1 else (out_parts, k, n // out_parts)
    tiles = _mm_tiles(_divisors(a.shape[-1], LANE), _divisors(math.gcd(dy.shape[-1], out_shape[-1]), LANE), _divisors(t, 16), a, dy, out_dtype)
    return _mm(name, a, dy, "tn", out_shape, out_dtype, tiles, **kw)


def _rows(tr, d):
    return pl.BlockSpec((tr, d), lambda i: (i, 0))


def _pick(ctx, ref):
    return jnp.where(ctx, ref[1:2, :], ref[0:1, :])


def resid_rms_mod(name, x, y, gate, g, sc, sh, n_lat, tr=256):
    t, d = x.shape
    nlt = n_lat // tr
    has_res = y is not None

    def body(*refs):
        if has_res:
            x_ref, y_ref, gate_ref, g_ref, sc_ref, sh_ref, x1_ref, h_ref = refs
        else:
            x_ref, g_ref, sc_ref, sh_ref, h_ref = refs
        ctx = pl.program_id(0) >= nlt
        xv = x_ref[...]
        if has_res:
            xv = xv + _pick(ctx, gate_ref) * y_ref[...]
            x1_ref[...] = xv
        r = lax.rsqrt(jnp.mean(xv * xv, axis=-1, keepdims=True) + EPS)
        n = xv * r * g_ref[...]
        h_ref[...] = (n * (1.0 + _pick(ctx, sc_ref)) + _pick(ctx, sh_ref)).astype(h_ref.dtype)

    row = _rows(tr, d)
    if has_res:
        ins, in_specs = (x, y, gate, g, sc, sh), [row, row, _full(gate), _full(g), _full(sc), _full(sh)]
        out_shape = (jax.ShapeDtypeStruct((t, d), F32), jax.ShapeDtypeStruct((t, d), BF16))
        out_specs = (row, row)
    else:
        ins, in_specs = (x, g, sc, sh), [row, _full(g), _full(sc), _full(sh)]
        out_shape = jax.ShapeDtypeStruct((t, d), BF16)
        out_specs = row
    return pl.pallas_call(body, name=name, grid=(t // tr,), in_specs=in_specs, out_specs=out_specs,
                          out_shape=out_shape, compiler_params=_params("parallel"))(*ins)


def _acc_rows(i, nlt, ref, val):
    @pl.when(i == 0)
    def _():
        ref[...] = jnp.zeros_like(ref)

    @pl.when(i < nlt)
    def _():
        ref[0:1, :] += val

    @pl.when(i >= nlt)
    def _():
        ref[1:2, :] += val


def rms_mod_bwd(name, x, dh, g, sc, dres, n_lat, branch=None, tr=256, carry=None):
    t, d = x.shape
    nlt = n_lat // tr

    def body(x_ref, dh_ref, g_ref, sc_ref, dres_ref, *rest):
        if branch is None:
            dx_ref, dsh_ref, dsc_ref, dg_ref = rest
        else:
            y_ref, gate_ref, dx_ref, dsh_ref, dsc_ref, dg_ref, dy_ref, dgate_ref = rest
        i = pl.program_id(0)
        xv, dhv, gv = x_ref[...], dh_ref[...], g_ref[...]
        r = lax.rsqrt(jnp.mean(xv * xv, axis=-1, keepdims=True) + EPS)
        xhat = xv * r
        dn = dhv * (1.0 + _pick(i >= nlt, sc_ref))
        dxhat = dn * gv
        dxv = r * (dxhat - xhat * jnp.mean(dxhat * xhat, axis=-1, keepdims=True)) + dres_ref[...]
        dx_ref[...] = dxv
        if branch is not None:
            dy_ref[...] = (_pick(i >= nlt, gate_ref) * dxv).astype(dy_ref.dtype)
            _acc_rows(i, nlt, dgate_ref, jnp.sum(dxv * y_ref[...], axis=0, keepdims=True))
        _acc_rows(i, nlt, dsh_ref, jnp.sum(dhv, axis=0, keepdims=True))
        _acc_rows(i, nlt, dsc_ref, jnp.sum(dhv * (xhat * gv), axis=0, keepdims=True))
        dgp = jnp.sum(dn * xhat, axis=0, keepdims=True)

        @pl.when(i == 0)
        def _():
            dg_ref[...] = dgp

        @pl.when(i > 0)
        def _():
            dg_ref[...] += dgp

    row = _rows(tr, d)
    two = pl.BlockSpec((2, d), lambda i: (0, 0))
    two_shape = jax.ShapeDtypeStruct((2, d), F32)
    ins, in_specs = [x, dh, g, sc, dres], [row, row, _full(g), _full(sc), row]
    out_specs = [row, two, two, pl.BlockSpec((1, d), lambda i: (0, 0))]
    out_shape = [jax.ShapeDtypeStruct((t, d), F32), two_shape, two_shape, jax.ShapeDtypeStruct((1, d), F32)]
    if branch is not None:
        ins, in_specs = ins + list(branch), in_specs + [row, _full(branch[1])]
        out_specs, out_shape = out_specs + [row, two], out_shape + [jax.ShapeDtypeStruct((t, d), BF16), two_shape]
    res = _call(body, name, (t // tr,), in_specs, out_specs, out_shape, ins, ("arbitrary",), carry=carry)
    return tuple(res) if carry is None else (*res[0], res[1])


def loss_head(name, x1, f, gate, target, tr=256):
    t, d = x1.shape
    nlt = target.shape[0] // tr

    def body(x_ref, f_ref, gate_ref, t_ref, dx_ref, loss_ref, df_ref, dgate_ref, acc_ref):
        i = pl.program_id(0)

        @pl.when(i == 0)
        def _():
            acc_ref[...] = jnp.zeros_like(acc_ref)
            dgate_ref[...] = jnp.zeros_like(dgate_ref)

        @pl.when(i < nlt)
        def _():
            fv, gv = f_ref[...], gate_ref[0:1, :]
            e = x_ref[...] + gv * fv - t_ref[...]
            dxv = e / d
            dx_ref[...] = dxv
            df_ref[...] = (gv * dxv).astype(df_ref.dtype)
            dgate_ref[0:1, :] += jnp.sum(dxv * fv, axis=0, keepdims=True)
            acc_ref[...] += jnp.sum(e * e, axis=0, keepdims=True)

        @pl.when(i >= nlt)
        def _():
            dx_ref[...] = jnp.zeros_like(dx_ref)
            df_ref[...] = jnp.zeros_like(df_ref)

        @pl.when(i == t // tr - 1)
        def _():
            loss_ref[...] = jnp.sum(acc_ref[...], axis=1, keepdims=True) * (0.5 / d)

    row = _rows(tr, d)
    return pl.pallas_call(
        body, name=name, grid=(t // tr,),
        in_specs=[row, row, _full(gate), pl.BlockSpec((tr, d), lambda i: (jnp.minimum(i, nlt - 1), 0))],
        out_specs=(row, pl.BlockSpec((1, 1), lambda i: (0, 0)), row, pl.BlockSpec((2, d), lambda i: (0, 0))),
        out_shape=(jax.ShapeDtypeStruct((t, d), F32), jax.ShapeDtypeStruct((1, 1), F32),
                   jax.ShapeDtypeStruct((t, d), BF16), jax.ShapeDtypeStruct((2, d), F32)),
        scratch_shapes=[pltpu.VMEM((1, d), F32)],
        compiler_params=_params("arbitrary"))(x1, f, gate, target)


HALO = 8


def _halo_specs(shape, tr, tc, col):
    hb = tr // HALO
    last = _logical(shape)[0] // HALO - 1
    main = _cspec(shape, tr, tc, lambda j, i: (i, col(j)))
    prev = _cspec(shape, HALO, tc, lambda j, i: (jnp.maximum(i * hb - 1, 0), col(j)))
    nxt = _cspec(shape, HALO, tc, lambda j, i: (jnp.minimum((i + 1) * hb, last), col(j)))
    return [prev, main, nxt]


def _seq_edges(i, nlt, nt):
    first = (i == 0) | (i == nlt)
    last = (i == nlt - 1) | (i == nt - 1)
    return first, last


def _ext(prev_ref, main_ref, next_ref, first, last):
    p = jnp.where(first, 0.0, prev_ref[...].astype(F32))
    n = jnp.where(last, 0.0, next_ref[...].astype(F32))
    return jnp.concatenate([p, main_ref[...].astype(F32), n], axis=0)


def _up(e):
    return pltpu.roll(e, 1, 0)


def _down(e):
    return pltpu.roll(e, e.shape[0] - 1, 0)


def _conv(e, w):
    return _up(e) * w[0:1, :] + e * w[1:2, :] + _down(e) * w[2:3, :]


def _conv_t(e, w):
    return _down(e) * w[0:1, :] + e * w[1:2, :] + _up(e) * w[2:3, :]


def _mid(e, tr):
    return e[HALO:HALO + tr, :]


def _acc_cols(i, ref, val):
    @pl.when(i == 0)
    def _():
        ref[...] = val

    @pl.when(i > 0)
    def _():
        ref[...] += val


def _colsum(v):
    return jnp.sum(v, axis=0, keepdims=True)


def ffn_act_fwd(name, u, cw, cb, n_lat, tr=256, carry=None):
    _, t, fp = u.shape
    tc = _tile(fp, 1536, LANE)
    nlt, nt = n_lat // tr, t // tr

    def body(pg, mg, ng, pu, mu, nu, cw_ref, cb_ref, z_ref, a_ref):
        first, last = _seq_edges(pl.program_id(1), nlt, nt)
        zg = _mid(_conv(_ext(pg, mg, ng, first, last), cw_ref[0]), tr) + cb_ref[0]
        zu = _mid(_conv(_ext(pu, mu, nu, first, last), cw_ref[1]), tr) + cb_ref[1]
        z_ref[0] = zg
        z_ref[1] = zu
        a_ref[...] = (_silu(zg) * zu).astype(a_ref.dtype)

    ncol = fp // tc
    specs = _halo_specs(u.shape, tr, tc, lambda j: j) + _halo_specs(u.shape, tr, tc, lambda j: j + ncol)
    specs += [pl.BlockSpec((2, 3, tc), lambda j, i: (0, 0, j)), pl.BlockSpec((2, 1, tc), lambda j, i: (0, 0, j))]
    res = _call(body, name, (ncol, nt), specs,
                [pl.BlockSpec((2, tr, tc), lambda j, i: (0, i, j)), pl.BlockSpec((tr, tc), lambda j, i: (i, j))],
                [jax.ShapeDtypeStruct((2, t, fp), F32), jax.ShapeDtypeStruct((t, fp), BF16)],
                (u, u, u, u, u, u, cw, cb), ("parallel", "parallel"), carry=carry)
    return tuple(res) if carry is None else (*res[0], res[1])


def ffn_act_bwd(name, z, u, da, cw, n_lat, tr=128, carry=None):
    _, t, fp = u.shape
    tc = _tile(fp, 1536, LANE)
    nlt, nt = n_lat // tr, t // tr
    ncol = fp // tc

    def body(pg, mg, ng, pu, mu, nu, u_ref, pa, ma, na, cw_ref, du_ref, dcw_ref, dcb_ref):
        i = pl.program_id(1)
        first, last = _seq_edges(i, nlt, nt)
        zg, zu = _ext(pg, mg, ng, first, last), _ext(pu, mu, nu, first, last)
        dae = _ext(pa, ma, na, first, last)
        sg = jax.nn.sigmoid(zg)
        dzs = (dae * zu * (sg * (1.0 + zg * (1.0 - sg))), dae * (zg * sg))
        for h, dz in enumerate(dzs):
            w, um = cw_ref[h], u_ref[h]
            after, before = _down(dz), _up(dz)
            du_ref[h] = _mid(after * w[0:1, :] + dz * w[1:2, :] + before * w[2:3, :], tr).astype(du_ref.dtype)
            rows = [_colsum(_mid(after, tr) * um), _colsum(_mid(dz, tr) * um), _colsum(_mid(before, tr) * um)]
            _acc_cols(i, dcw_ref.at[h], jnp.concatenate(rows, axis=0))
            _acc_cols(i, dcb_ref.at[h], _colsum(_mid(dz, tr)))

    both = pl.BlockSpec((2, tr, tc), lambda j, i: (0, i, j))
    taps = pl.BlockSpec((2, 3, tc), lambda j, i: (0, 0, j))
    specs = _halo_specs(z.shape, tr, tc, lambda j: j) + _halo_specs(z.shape, tr, tc, lambda j: j + ncol) + [both]
    specs += _halo_specs(da.shape, tr, tc, lambda j: j) + [taps]
    res = _call(body, name, (ncol, nt), specs, [both, taps, pl.BlockSpec((2, 1, tc), lambda j, i: (0, 0, j))],
                [jax.ShapeDtypeStruct((2, t, fp), BF16), jax.ShapeDtypeStruct((2, 3, fp), F32), jax.ShapeDtypeStruct((2, 1, fp), F32)],
                (z, z, z, z, z, z, u, da, da, da, cw), ("parallel", "arbitrary"), carry=carry)
    return tuple(res) if carry is None else (*res[0], res[1])


def sc_gate_fwd(name, tmat, cw, n_lat, tr=256):
    t, d3 = tmat.shape
    d = d3 // 3
    tc = _tile(d, 512, LANE)
    ncol = d // tc
    nlt, nt = n_lat // tr, t // tr

    def body(b_ref, pc, mc, nc, px, mx, nx, cw_ref, s_ref):
        first, last = _seq_edges(pl.program_id(1), nlt, nt)
        p = _ext(pc, mc, nc, first, last) * _ext(px, mx, nx, first, last)
        s_ref[...] = (b_ref[...] * _mid(_conv(p, cw_ref[...]), tr)).astype(s_ref.dtype)

    specs = [pl.BlockSpec((tr, tc), lambda j, i: (i, j))]
    specs += _halo_specs(tmat.shape, tr, tc, lambda j: j + ncol) + _halo_specs(tmat.shape, tr, tc, lambda j: j + 2 * ncol)
    specs += [pl.BlockSpec((3, tc), lambda j, i: (0, j))]
    return pl.pallas_call(
        body, name=name, grid=(ncol, nt), in_specs=specs, out_specs=pl.BlockSpec((tr, tc), lambda j, i: (i, j)),
        out_shape=jax.ShapeDtypeStruct((t, d), BF16),
        compiler_params=_params("parallel", "parallel"))(*([tmat] * 7), cw)


def sc_gate_bwd(name, tmat, ds, cw, n_lat, tr=128):
    t, d3 = tmat.shape
    d = d3 // 3
    tc = _tile(d, 512, LANE)
    ncol = d // tc
    nlt, nt = n_lat // tr, t // tr

    def body(pb, mb, nb, pc, mc, nc, px, mx, nx, pd, md, nd, cw_ref, dt_ref, dcw_ref):
        i = pl.program_id(1)
        first, last = _seq_edges(i, nlt, nt)
        w = cw_ref[...]
        be, ce, xe = _ext(pb, mb, nb, first, last), _ext(pc, mc, nc, first, last), _ext(px, mx, nx, first, last)
        dse = _ext(pd, md, nd, first, last)
        p = ce * xe
        dcv = dse * be
        dp = _conv_t(dcv, w)
        dt_ref[0] = _mid(dse * _conv(p, w), tr).astype(dt_ref.dtype)
        dt_ref[1] = _mid(dp * xe, tr).astype(dt_ref.dtype)
        dt_ref[2] = _mid(dp * ce, tr).astype(dt_ref.dtype)
        dm = _mid(dcv, tr)
        rows = [_colsum(dm * _mid(_up(p), tr)), _colsum(dm * _mid(p, tr)), _colsum(dm * _mid(_down(p), tr))]
        _acc_cols(i, dcw_ref, jnp.concatenate(rows, axis=0))

    specs = []
    for part in range(3):
        specs += _halo_specs(tmat.shape, tr, tc, functools.partial(lambda j, part: j + part * ncol, part=part))
    specs += _halo_specs(ds.shape, tr, tc, lambda j: j)
    specs += [pl.BlockSpec((3, tc), lambda j, i: (0, j))]
    return pl.pallas_call(
        body, name=name, grid=(ncol, nt), in_specs=specs,
        out_specs=(pl.BlockSpec((3, tr, tc), lambda j, i: (0, i, j)), pl.BlockSpec((3, tc), lambda j, i: (0, j))),
        out_shape=(jax.ShapeDtypeStruct((3, t, d), BF16), jax.ShapeDtypeStruct((3, d), F32)),
        compiler_params=_params("parallel", "arbitrary"))(*([tmat] * 9), ds, ds, ds, cw)


_GELU_K = 0.7978845608028654
_GELU_C = 0.044715


def _gelu(x):
    return 0.5 * x * (1.0 + jnp.tanh(_GELU_K * (x + _GELU_C * (x * x * x))))


def _gelu_grad(x):
    th = jnp.tanh(_GELU_K * (x + _GELU_C * (x * x * x)))
    return 0.5 * (1.0 + th) + 0.5 * x * (1.0 - th * th) * (_GELU_K * (1.0 + 3.0 * _GELU_C * (x * x)))


def gmlp_gate_fwd(name, tmat, vg, ws, bs):
    t, w2 = tmat.shape
    w = w2 // 2
    groups = ws.shape[0]
    gd = w // groups

    def body(t_ref, vg_ref, ws_ref, bs_ref, o_ref):
        v = _gelu(t_ref[:, w:])
        r = lax.rsqrt(jnp.mean(v * v, axis=-1, keepdims=True) + EPS)
        vn = (v * r * vg_ref[...]).astype(BF16)
        for g in range(groups):
            cols = slice(g * gd, (g + 1) * gd)
            sv = _dot(ws_ref[g], vn[:, cols], NN) + bs_ref[g]
            o_ref[:, cols] = (_gelu(t_ref[:, cols]) * sv).astype(o_ref.dtype)

    return pl.pallas_call(
        body, name=name, grid=(t // GM_CHUNK,),
        in_specs=[_rows(GM_CHUNK, w2), _full(vg), _full(ws), _full(bs)], out_specs=_rows(GM_CHUNK, w),
        out_shape=jax.ShapeDtypeStruct((t, w), BF16), compiler_params=_params("parallel"))(tmat, vg, ws, bs)


def gmlp_gate_bwd(name, tmat, dout, vg, ws, bs):
    t, w2 = tmat.shape
    w = w2 // 2
    groups = ws.shape[0]
    gd = w // groups

    def body(t_ref, do_ref, vg_ref, ws_ref, bs_ref, dt_ref, dvg_ref, dws_ref, dsv_ref, dvn_ref):
        i = pl.program_id(0)
        tv = t_ref[:, w:]
        v = _gelu(tv)
        r = lax.rsqrt(jnp.mean(v * v, axis=-1, keepdims=True) + EPS)
        vhat = v * r
        vn = (vhat * vg_ref[...]).astype(BF16)
        for g in range(groups):
            cols = slice(g * gd, (g + 1) * gd)
            tu = t_ref[:, cols]
            dov = do_ref[:, cols]
            sv = _dot(ws_ref[g], vn[:, cols], NN) + bs_ref[g]
            dt_ref[:, cols] = (dov * sv * _gelu_grad(tu)).astype(dt_ref.dtype)
            dsv = dov * _gelu(tu)
            _acc_cols(i, dsv_ref.at[:, cols], dsv)
            _acc_cols(i, dws_ref.at[g], _dot(dsv, vn[:, cols], NT))
            dvn_ref[:, cols] = _dot(ws_ref[g], dsv, TN)
        dvn = dvn_ref[...]
        _acc_cols(i, dvg_ref, _colsum(dvn * vhat))
        dvhat = dvn * vg_ref[...]
        dv = r * (dvhat - vhat * jnp.mean(dvhat * vhat, axis=-1, keepdims=True))
        dt_ref[:, w:] = (dv * _gelu_grad(tv)).astype(dt_ref.dtype)

    keep = lambda shape: pl.BlockSpec(shape, lambda i: (0,) * len(shape))
    return pl.pallas_call(
        body, name=name, grid=(t // GM_CHUNK,),
        in_specs=[_rows(GM_CHUNK, w2), _rows(GM_CHUNK, w), _full(vg), _full(ws), _full(bs)],
        out_specs=(_rows(GM_CHUNK, w2), keep((1, w)), keep(ws.shape), keep((GM_CHUNK, w))),
        out_shape=(jax.ShapeDtypeStruct((t, w2), BF16), jax.ShapeDtypeStruct((1, w), F32),
                   jax.ShapeDtypeStruct(ws.shape, F32), jax.ShapeDtypeStruct((GM_CHUNK, w), F32)),
        scratch_shapes=[pltpu.VMEM((GM_CHUNK, w), F32)],
        compiler_params=_params("arbitrary"))(tmat, dout, vg, ws, bs)


def qk_norm_fwd(name, qkv, qg, kg, hd, tr=128):
    t, d3 = qkv.shape
    d = d3 // 3

    def body(x_ref, qg_ref, kg_ref, q_ref, k_ref, v_ref):
        for part, (g_ref, o_ref) in enumerate(((qg_ref, q_ref), (kg_ref, k_ref))):
            for h in range(d // hd):
                xh = x_ref[:, part * d + h * hd: part * d + (h + 1) * hd]
                r = lax.rsqrt(jnp.mean(xh * xh, axis=-1, keepdims=True) + EPS)
                o_ref[:, h * hd:(h + 1) * hd] = (xh * r * g_ref[...]).astype(o_ref.dtype)
        v_ref[...] = x_ref[:, 2 * d:].astype(v_ref.dtype)

    out = jax.ShapeDtypeStruct((t, d), BF16)
    return pl.pallas_call(
        body, name=name, grid=(t // tr,), in_specs=[_rows(tr, d3), _full(qg), _full(kg)],
        out_specs=(_rows(tr, d),) * 3, out_shape=(out,) * 3, compiler_params=_params("parallel"))(qkv, qg, kg)


def qk_norm_bwd(name, qkv, dq, dk, dv, qg, kg, hd, tr=128):
    t, d3 = qkv.shape
    d = d3 // 3

    def body(x_ref, dq_ref, dk_ref, dv_ref, qg_ref, kg_ref, o_ref, dqg_ref, dkg_ref):
        i = pl.program_id(0)
        for part, (g_ref, dn_ref, dg_ref) in enumerate(((qg_ref, dq_ref, dqg_ref), (kg_ref, dk_ref, dkg_ref))):
            dg = jnp.zeros((1, hd), F32)
            for h in range(d // hd):
                xh = x_ref[:, part * d + h * hd: part * d + (h + 1) * hd]
                dn = dn_ref[:, h * hd:(h + 1) * hd]
                r = lax.rsqrt(jnp.mean(xh * xh, axis=-1, keepdims=True) + EPS)
                xhat = xh * r
                dg = dg + _colsum(dn * xhat)
                dxhat = dn * g_ref[...]
                dx = r * (dxhat - xhat * jnp.mean(dxhat * xhat, axis=-1, keepdims=True))
                o_ref[:, part * d + h * hd: part * d + (h + 1) * hd] = dx.astype(o_ref.dtype)
            _acc_cols(i, dg_ref, dg)
        o_ref[:, 2 * d:] = dv_ref[...].astype(o_ref.dtype)

    one = pl.BlockSpec((1, hd), lambda i: (0, 0))
    return pl.pallas_call(
        body, name=name, grid=(t // tr,),
        in_specs=[_rows(tr, d3), _rows(tr, d), _rows(tr, d), _rows(tr, d), _full(qg), _full(kg)],
        out_specs=(_rows(tr, d3), one, one),
        out_shape=(jax.ShapeDtypeStruct((t, d3), BF16), jax.ShapeDtypeStruct((1, hd), F32), jax.ShapeDtypeStruct((1, hd), F32)),
        compiler_params=_params("arbitrary"))(qkv, dq, dk, dv, qg, kg)


def _na_geometry(n_lat):
    rows = n_lat // GRID_W
    kh = min(WIN_H, rows)
    nb = min(kh + QROWS - 1, rows)
    n_blk = rows // QROWS
    q_row_off = np.repeat(np.arange(QROWS), GRID_W)
    q_col = np.tile(np.arange(GRID_W), QROWS)
    k_row_off = np.repeat(np.arange(nb), GRID_W)
    k_col = np.tile(np.arange(GRID_W), nb)
    c_start = np.clip(q_col - WIN_W // 2, 0, GRID_W - WIN_W)
    col_ok = (k_col[None, :] >= c_start[:, None]) & (k_col[None, :] < c_start[:, None] + WIN_W)
    dc_idx = np.clip(k_col[None, :] - q_col[:, None], -(WIN_W - 1), WIN_W - 1) + WIN_W - 1

    def block(blk):
        r0 = blk * QROWS
        q_row = r0 + q_row_off
        r_start = np.clip(q_row - kh // 2, 0, rows - kh)
        band0 = min(int(np.clip(r0 - kh // 2, 0, rows - kh)), rows - nb)
        k_row = band0 + k_row_off
        ok = col_ok & (k_row[None, :] >= r_start[:, None]) & (k_row[None, :] < r_start[:, None] + kh)
        dr_idx = np.clip(k_row[None, :] - q_row[:, None], -(WIN_H - 1), WIN_H - 1) + WIN_H - 1
        return band0, ok, dr_idx

    reps = [0, 1, 2, n_blk - 2, n_blk - 1]
    variant = lambda blk: 0 if blk == 0 else 1 if blk == 1 else 3 if blk == n_blk - 2 else 4 if blk == n_blk - 1 else 2
    geo = [block(b) for b in reps]
    for blk in range(n_blk):
        _, ok, dr = block(blk)
        assert np.array_equal(ok, geo[variant(blk)][1]) and np.array_equal(np.where(ok, dr, 0), np.where(ok, geo[variant(blk)][2], 0))
    return dict(rows=rows, kh=kh, nb=nb, n_blk=n_blk, reps=reps, ok=[g[1] for g in geo], dr=[g[2] for g in geo],
                band0=[g[0] for g in geo], dc=dc_idx)


def _na_onehots(geo):
    nb = geo["nb"]
    w2 = GRID_W * GRID_W
    qc, kc = np.meshgrid(np.arange(GRID_W), np.arange(GRID_W), indexing="ij")
    diff = (kc - qc).reshape(-1)
    cols = np.zeros((w2, LANE), np.float32)
    sel = np.abs(diff) <= WIN_W - 1
    cols[np.arange(w2)[sel], diff[sel] + WIN_W - 1] = 1.0
    npair = len(geo["reps"]) * QROWS * nb
    kpad = -(-npair // LANE) * LANE
    rows = np.zeros((16, kpad), np.float32)
    for vi, blk in enumerate(geo["reps"]):
        for qr in range(QROWS):
            for kr in range(nb):
                dr = (geo["band0"][vi] + kr) - (blk * QROWS + qr)
                if abs(dr) <= WIN_H - 1:
                    rows[dr + WIN_H - 1, (vi * QROWS + qr) * nb + kr] = 1.0
    return cols, rows, npair, kpad


def na_bias_table(name, rpb, geo):
    n_heads, nb, nv = rpb.shape[0], geo["nb"], len(geo["reps"])
    cols, rows, npair, kpad = _na_onehots(geo)
    rpb_p = jnp.pad(rpb, ((0, 0), (0, 16 - rpb.shape[1]), (0, LANE - rpb.shape[2]))).transpose(1, 0, 2).reshape(16, n_heads * LANE)
    t1 = _mm(name + "_rows", jnp.asarray(rows.T), rpb_p, "nn", (kpad, n_heads * LANE), F32,
             (kpad, _tile(n_heads * LANE, 1024, LANE), 16), exact=True)
    t1 = t1[:npair].reshape(nv, QROWS * nb, n_heads, LANE).transpose(0, 2, 1, 3).reshape(nv * n_heads * QROWS * nb, LANE)
    m = t1.shape[0]
    flat = _mm(name + "_cols", t1, jnp.asarray(cols), "nt", (m, GRID_W * GRID_W), F32,
               (_tile(m, 512, 8), _tile(GRID_W * GRID_W, 2048, LANE), LANE), exact=True)
    tab = flat.reshape(nv, n_heads, QROWS, nb, GRID_W, GRID_W).transpose(0, 1, 2, 4, 3, 5).reshape(nv, n_heads, QROWS * GRID_W, nb * GRID_W)
    return jnp.where(jnp.asarray(np.stack(geo["ok"]))[:, None], tab, NEG)


def _na_tile_info(qt, geo):
    n_blk, rows, kh, nb = geo["n_blk"], geo["rows"], geo["kh"], geo["nb"]
    is_ctx = qt >= n_blk
    band0 = jnp.minimum(jnp.clip(qt * QROWS - kh // 2, 0, rows - kh), rows - nb)
    band0 = jnp.where(is_ctx, 0, band0)
    return is_ctx, pl.multiple_of(band0 * GRID_W, GRID_W)


def _na_variant(qt, n_blk):
    v = jnp.minimum(qt, 2) + (qt >= n_blk - 2).astype(jnp.int32) + (qt >= n_blk - 1).astype(jnp.int32)
    return jnp.minimum(v, 4)


def _na_probs(q, kb, kc, bias, is_ctx, scale):
    s_lat = _dot(q, kb, NT) * scale + bias
    s_lat = jnp.where(is_ctx, NEG, s_lat)
    s_ctx = _dot(q, kc, NT) * scale
    m = jnp.maximum(jnp.max(s_lat, axis=-1, keepdims=True), jnp.max(s_ctx, axis=-1, keepdims=True))
    e_lat, e_ctx = jnp.exp(s_lat - m), jnp.exp(s_ctx - m)
    den = jnp.sum(e_lat, axis=-1, keepdims=True) + jnp.sum(e_ctx, axis=-1, keepdims=True)
    return e_lat / den, e_ctx / den


NA_MAX_TILES_PER_STEP = 6


def _na_step_specs(t, qw, nk, hd, n_blk):
    per = max(n for n in range(1, NA_MAX_TILES_PER_STEP + 1) if (t // qw) % n == 0)
    tile = pl.BlockSpec((per * qw, hd), lambda h, i: (i, h))
    btabs = [pl.BlockSpec((None, None, qw, nk), functools.partial(lambda h, i, part: (_na_variant(per * i + part, n_blk), h, 0, 0), part=part))
             for part in range(per)]
    return per, tile, btabs


def na_attention_fwd(name, q, k, v, bias, geo, n_lat, hd, carry=None):
    t, d = q.shape
    qw, nk = QROWS * GRID_W, geo["nb"] * GRID_W
    scale = hd ** -0.5
    per, tile, btabs = _na_step_specs(t, qw, nk, hd, geo["n_blk"])

    def body(q_ref, k_ref, v_ref, *rest):
        b_refs, o_ref = rest[:per], rest[per]
        kc, vc = k_ref[n_lat:, :], v_ref[n_lat:, :]
        for part, b_ref in enumerate(b_refs):
            rows = slice(part * qw, (part + 1) * qw)
            is_ctx, start = _na_tile_info(per * pl.program_id(1) + part, geo)
            kb, vb = k_ref[pl.ds(start, nk), :], v_ref[pl.ds(start, nk), :]
            p_lat, p_ctx = _na_probs(q_ref[rows, :], kb, kc, b_ref[...], is_ctx, scale)
            o_ref[rows, :] = (_dot(p_lat, vb, NN) + _dot(p_ctx, vc, NN)).astype(o_ref.dtype)

    head = pl.BlockSpec((t, hd), lambda h, i: (0, h))
    res = _call(body, name, (d // hd, t // (per * qw)), [tile, head, head, *btabs], [tile],
                [jax.ShapeDtypeStruct((t, d), BF16)], (q, k, v, *([bias] * per)), ("parallel", "arbitrary"), carry=carry)
    return res[0] if carry is None else (res[0][0], res[1])


def na_attention_bwd(name, q, k, v, bias, do, geo, n_lat, hd, carry=None):
    t, d = q.shape
    qw, nk = QROWS * GRID_W, geo["nb"] * GRID_W
    n_blk = geo["n_blk"]
    scale = hd ** -0.5
    per, tile, btabs = _na_step_specs(t, qw, nk, hd, n_blk)

    def body(q_ref, k_ref, v_ref, *rest):
        b_refs, (do_ref, dq_ref, dk_ref, dv_ref, db_ref) = rest[:per], rest[per:]
        step = pl.program_id(1)

        @pl.when(step == 0)
        def _():
            dk_ref[...] = jnp.zeros_like(dk_ref)
            dv_ref[...] = jnp.zeros_like(dv_ref)
            db_ref[...] = jnp.zeros_like(db_ref)

        kc, vc = k_ref[n_lat:, :], v_ref[n_lat:, :]
        for part, b_ref in enumerate(b_refs):
            qt = per * step + part
            rows = slice(part * qw, (part + 1) * qw)
            is_ctx, start = _na_tile_info(qt, geo)
            band = pl.ds(start, nk)
            qv, dov = q_ref[rows, :], do_ref[rows, :]
            kb, vb = k_ref[band, :], v_ref[band, :]
            p_lat, p_ctx = _na_probs(qv, kb, kc, b_ref[...], is_ctx, scale)
            dp_lat, dp_ctx = _dot(dov, vb, NT), _dot(dov, vc, NT)
            delta = jnp.sum(p_lat * dp_lat, axis=-1, keepdims=True) + jnp.sum(p_ctx * dp_ctx, axis=-1, keepdims=True)
            ds_lat, ds_ctx = p_lat * (dp_lat - delta), p_ctx * (dp_ctx - delta)
            db_ref[_na_variant(qt, n_blk)] += ds_lat
            dsl, dsc = (ds_lat * scale).astype(BF16), (ds_ctx * scale).astype(BF16)
            dq_ref[rows, :] = _dot(dsl, kb, NN) + _dot(dsc, kc, NN)
            dk_ref[band, :] += _dot(dsl, qv, TN)
            dk_ref[n_lat:, :] += _dot(dsc, qv, TN)
            dv_ref[band, :] += _dot(p_lat, dov, TN)
            dv_ref[n_lat:, :] += _dot(p_ctx, dov, TN)

    nv = bias.shape[0]
    head = pl.BlockSpec((t, hd), lambda h, i: (0, h))
    full = jax.ShapeDtypeStruct((t, d), F32)
    res = _call(body, name, (d // hd, t // (per * qw)), [tile, head, head, *btabs, tile],
                [tile, head, head, pl.BlockSpec((nv, None, qw, nk), lambda h, i: (0, h, 0, 0))],
                [full, full, full, jax.ShapeDtypeStruct(bias.shape, F32)], (q, k, v, *([bias] * per), do), ("arbitrary", "arbitrary"), carry=carry)
    return tuple(res) if carry is None else (*res[0], res[1])


def na_rpb_grad(name, dbias, geo, n_heads):
    nb = geo["nb"]
    nv = len(geo["reps"])
    w2 = GRID_W * GRID_W
    cols, rows, npair, kpad = _na_onehots(geo)
    xmat = dbias.reshape(nv, n_heads, QROWS, GRID_W, nb, GRID_W).transpose(0, 1, 2, 4, 3, 5).reshape(nv * n_heads * QROWS * nb, w2)
    m = xmat.shape[0]
    r = _mm(name + "_cols", xmat, jnp.asarray(cols), "nn", (m, LANE), F32, (_tile(m, 512, 8), LANE, _tile(w2, 1024, LANE)), exact=True)
    r2 = r.reshape(nv, n_heads, QROWS * nb, LANE).transpose(0, 2, 1, 3).reshape(npair, n_heads * LANE)
    r2 = jnp.pad(r2, ((0, kpad - npair), (0, 0)))
    out = _mm(name + "_rows", jnp.asarray(rows), r2, "nn", (16, n_heads * LANE), F32, (16, _tile(n_heads * LANE, 1024, LANE), kpad), exact=True)
    return out[:2 * WIN_H - 1].reshape(2 * WIN_H - 1, n_heads, LANE)[:, :, :2 * WIN_W - 1].transpose(1, 0, 2)


def all_gather(name, arrays):
    n = len(arrays)

    def body(*refs):
        for phase in _gather_two_level(refs[:n], refs[n:2 * n], *refs[2 * n:]):
            phase()

    outs = pl.pallas_call(body, name=name, in_specs=[HBM] * n, out_specs=[HBM] * n, out_shape=_landing("gather", arrays),
                          scratch_shapes=_exchange_scratch(n))(*arrays)
    return list(outs)


def _adamw_math(w, g, m, v):
    m = ADAM_B1 * m + (1.0 - ADAM_B1) * g
    v = ADAM_B2 * v + (1.0 - ADAM_B2) * (g * g)
    m_hat = m / (1.0 - ADAM_B1 ** ADAM_STEP)
    v_hat = v / (1.0 - ADAM_B2 ** ADAM_STEP)
    delta = -ADAM_LR * (m_hat / (jnp.sqrt(v_hat) + ADAM_EPS) + ADAM_WD * w)
    return delta, m, v


def adamw(name, parts, w, m, v):
    r, c = w.shape
    npart, _, cp = parts.shape
    tc = c if (c % LANE or cp != c) else _tile(c, 512, LANE)
    tr = _tile(r, 256, 16 if parts.dtype == BF16 else 8)
    if tr < 64:
        tr, tc = r, (tc if tc == c and cp != c else _tile(c, 256, LANE))
    tcp = cp if tc == c else tc

    def body(p_ref, w_ref, m_ref, v_ref, g_ref, d_ref, mo_ref, vo_ref):
        g = p_ref[0].astype(F32)
        for s in range(1, npart):
            g = g + p_ref[s].astype(F32)
        g = g[:, :tc]
        g_ref[...] = g
        d_ref[...], mo_ref[...], vo_ref[...] = _adamw_math(w_ref[...], g, m_ref[...], v_ref[...])

    blk = pl.BlockSpec((tr, tc), lambda i, j: (i, j))
    out = jax.ShapeDtypeStruct((r, c), F32)
    return pl.pallas_call(
        body, name=name, grid=(r // tr, c // tc),
        in_specs=[pl.BlockSpec((npart, tr, tcp), lambda i, j: (0, i, j)), blk, blk, blk],
        out_specs=(blk,) * 4, out_shape=(out,) * 4, compiler_params=_params("parallel", "parallel"))(parts, w, m, v)


def sum_parts(name, parts):
    npart, r, c = parts.shape
    tr = _tile(r, 512, 8)

    def body(p_ref, o_ref):
        g = p_ref[0]
        for s in range(1, npart):
            g = g + p_ref[s]
        o_ref[...] = g

    return pl.pallas_call(
        body, name=name, grid=(r // tr,), in_specs=[pl.BlockSpec((npart, tr, c), lambda i: (0, i, 0))],
        out_specs=pl.BlockSpec((tr, c), lambda i: (i, 0)), out_shape=jax.ShapeDtypeStruct((r, c), F32),
        compiler_params=_params("parallel"))(parts)


class _Pack:
    def __init__(self, shapes):
        self.shapes = dict(shapes)
        self.offsets, off = {}, 0
        for name, shape in self.shapes.items():
            self.offsets[name] = off
            off += -(-int(np.prod(shape)) // (8 * LANE)) * (8 * LANE)
        self.used = off
        self.rows = -(-off // (512 * LANE)) * 512

    def pack(self, values):
        pieces = []
        for name, shape in self.shapes.items():
            size = int(np.prod(shape))
            padded = -(-size // (8 * LANE)) * (8 * LANE)
            pieces.append(jnp.pad(values[name].astype(F32).reshape(-1), (0, padded - size)))
        pieces.append(jnp.zeros((self.rows * LANE - self.used,), F32))
        return jnp.concatenate(pieces).reshape(self.rows, LANE)

    def unpack(self, flat, lead=()):
        flat = flat.reshape(*lead, self.rows * LANE)
        out = {}
        for name, shape in self.shapes.items():
            size = int(np.prod(shape))
            out[name] = flat[..., self.offsets[name]:self.offsets[name] + size].reshape(*lead, *shape)
        return out


def kernel(x, c, ctx, c_ctx, norm_mix_g, norm_ffn_g, w_ada, b_ada, na_w_qkv, na_q_g, na_k_g, na_rpb, na_w_o, gm_w_in, gm_v_g, gm_w_s, gm_b_s, gm_w_out, sc_w_in, sc_conv_w, sc_w_out, ffn_w_up, ffn_conv_w, ffn_conv_b, ffn_w_down, loss_target, m_c_ctx, m_norm_mix_g, m_norm_ffn_g, m_w_ada, m_b_ada, m_na_w_qkv, m_na_q_g, m_na_k_g, m_na_rpb, m_na_w_o, m_gm_w_in, m_gm_v_g, m_gm_w_s, m_gm_b_s, m_gm_w_out, m_sc_w_in, m_sc_conv_w, m_sc_w_out, m_ffn_w_up, m_ffn_conv_w, m_ffn_conv_b, m_ffn_w_down, v_c_ctx, v_norm_mix_g, v_norm_ffn_g, v_w_ada, v_b_ada, v_na_w_qkv, v_na_q_g, v_na_k_g, v_na_rpb, v_na_w_o, v_gm_w_in, v_gm_v_g, v_gm_w_s, v_gm_b_s, v_gm_w_out, v_sc_w_in, v_sc_conv_w, v_sc_w_out, v_ffn_w_up, v_ffn_conv_w, v_ffn_conv_b, v_ffn_w_down):
    weights = dict(c_ctx=c_ctx, norm_mix_g=norm_mix_g, norm_ffn_g=norm_ffn_g, w_ada=w_ada, b_ada=b_ada, na_w_qkv=na_w_qkv,
                   na_q_g=na_q_g, na_k_g=na_k_g, na_rpb=na_rpb, na_w_o=na_w_o, gm_w_in=gm_w_in, gm_v_g=gm_v_g, gm_w_s=gm_w_s,
                   gm_b_s=gm_b_s, gm_w_out=gm_w_out, sc_w_in=sc_w_in, sc_conv_w=sc_conv_w, sc_w_out=sc_w_out,
                   ffn_w_up=ffn_w_up, ffn_conv_w=ffn_conv_w, ffn_conv_b=ffn_conv_b, ffn_w_down=ffn_w_down)
    mom_m = dict(c_ctx=m_c_ctx, norm_mix_g=m_norm_mix_g, norm_ffn_g=m_norm_ffn_g, w_ada=m_w_ada, b_ada=m_b_ada, na_w_qkv=m_na_w_qkv,
                 na_q_g=m_na_q_g, na_k_g=m_na_k_g, na_rpb=m_na_rpb, na_w_o=m_na_w_o, gm_w_in=m_gm_w_in, gm_v_g=m_gm_v_g,
                 gm_w_s=m_gm_w_s, gm_b_s=m_gm_b_s, gm_w_out=m_gm_w_out, sc_w_in=m_sc_w_in, sc_conv_w=m_sc_conv_w,
                 sc_w_out=m_sc_w_out, ffn_w_up=m_ffn_w_up, ffn_conv_w=m_ffn_conv_w, ffn_conv_b=m_ffn_conv_b, ffn_w_down=m_ffn_w_down)
    mom_v = dict(c_ctx=v_c_ctx, norm_mix_g=v_norm_mix_g, norm_ffn_g=v_norm_ffn_g, w_ada=v_w_ada, b_ada=v_b_ada, na_w_qkv=v_na_w_qkv,
                 na_q_g=v_na_q_g, na_k_g=v_na_k_g, na_rpb=v_na_rpb, na_w_o=v_na_w_o, gm_w_in=v_gm_w_in, gm_v_g=v_gm_v_g,
                 gm_w_s=v_gm_w_s, gm_b_s=v_gm_b_s, gm_w_out=v_gm_w_out, sc_w_in=v_sc_w_in, sc_conv_w=v_sc_conv_w,
                 sc_w_out=v_sc_w_out, ffn_w_up=v_ffn_w_up, ffn_conv_w=v_ffn_conv_w, ffn_conv_b=v_ffn_conv_b, ffn_w_down=v_ffn_w_down)
    names = list(weights)

    n_lat, d = x.shape[1], x.shape[2]
    n_ctx = ctx.shape[1]
    t = n_lat + n_ctx
    depth = norm_mix_g.shape[0]
    hd = na_q_g.shape[-1]
    n_heads = d // hd
    nup = ffn_w_up.shape[-1]
    nup_p = -(-nup // LANE) * LANE
    fdim, fp = 4 * nup, 4 * nup_p
    me = _slot(_me())
    geo = _na_geometry(n_lat)

    pad_up = lambda a: jnp.pad(a, [(0, 0)] * (a.ndim - 1) + [(0, nup_p - nup)])
    mixer_weights = (("na_w_qkv", "na_w_o"), ("gm_w_in", "gm_w_out"), ("sc_w_in", "sc_w_out"))

    def shards(i):
        w_in, w_out = (weights[nm][i // N_MIXERS].astype(BF16) for nm in mixer_weights[i % N_MIXERS])
        return [w_in, w_out, pad_up(ffn_w_up[i]).astype(BF16), ffn_w_down[i].astype(BF16)]

    def operands(g_in, g_out, g_up, g_down):
        down = jnp.pad(g_down.reshape(4, nup, d), ((0, 0), (0, nup_p - nup), (0, 0))).reshape(fp, d)
        return dict(w_in=g_in, w_out=g_out.reshape(-1, d), w_up=g_up, w_down=down)

    first_in, *first_rest = shards(0)
    g_in0, g_fcw, g_scw, c_all = all_gather("gather_first", [first_in, pad_up(ffn_conv_w), sc_conv_w, c])
    layer_w = []
    f_cw = [g_fcw[:, i].transpose(1, 0, 2).reshape(3, 2, fp).transpose(1, 0, 2) for i in range(depth)]
    cb_p = pad_up(ffn_conv_b.reshape(depth, N_DEV, nup)).reshape(depth, 2, 1, fp)
    s_cw = [g_scw[:, j].transpose(1, 0, 2).reshape(3, d) for j in range(sc_conv_w.shape[0])]

    cond = jnp.concatenate([c_all.reshape(N_DEV, d), c_ctx[None], jnp.zeros((7, d), F32)])
    mod_cols = mm_nn("ada_fwd", cond, w_ada, a_silu=True)
    (mod_all,) = all_gather("gather_mod", [mod_cols])
    ncol = w_ada.shape[-1]
    mod_all = mod_all.reshape(N_DEV, 16, depth, ncol).transpose(2, 1, 0, 3).reshape(depth, 16, N_MOD * d) + b_ada[:, None, :]
    mod_lat = lax.dynamic_index_in_dim(mod_all, me, axis=1, keepdims=False)
    mods = jnp.stack([mod_lat, mod_all[:, N_DEV]], axis=1).reshape(depth, 2, N_MOD, d)
    sh1, sc1, g1, sh2, sc2, g2 = (mods[:, :, kd] for kd in range(N_MOD))

    xs = jnp.concatenate([x[0], ctx[0]], axis=0)
    saved = []
    prev = None
    for i in range(depth):
        mixer, j = i % N_MIXERS, i // N_MIXERS
        s = {}
        if prev is None:
            s["x"] = xs
            s["h"] = resid_rms_mod(f"l{i}_norm_mix", xs, None, None, norm_mix_g[i:i + 1], sc1[i], sh1[i], n_lat)
        else:
            s["x"], s["h"] = resid_rms_mod(f"l{i}_norm_mix", prev[0], prev[1], prev[2], norm_mix_g[i:i + 1], sc1[i], sh1[i], n_lat)
        nxt = shards(i + 1) if i + 1 < depth else None
        ride = (first_rest[:1] if i == 0 else []) + ([nxt[1]] if nxt else [])
        w_in = g_in0 if i == 0 else layer_w[i]["w_in"]
        in_name = f"l{i}_" + ("qkv", "gm_in", "sc_in")[mixer]
        if ride:
            pre, rode = mm_nn(in_name, s["h"], w_in, carry=("gather", ride))
        else:
            pre, rode = mm_nn(in_name, s["h"], w_in), []
        if mixer == 0:
            s["qkv"] = pre
            s["q"], s["k"], s["v"] = qk_norm_fwd(f"l{i}_qk_norm", s["qkv"], na_q_g[j:j + 1], na_k_g[j:j + 1], hd)
            s["bias"] = na_bias_table(f"l{i}_bias", na_rpb[j], geo)
            if i == 0:
                s["o"], (g_up0, g_down0) = na_attention_fwd(f"l{i}_attn", s["q"], s["k"], s["v"], s["bias"], geo, n_lat, hd,
                                                            carry=("gather", first_rest[1:]))
                layer_w.append(operands(g_in0, rode[0], g_up0, g_down0))
            else:
                s["o"] = na_attention_fwd(f"l{i}_attn", s["q"], s["k"], s["v"], s["bias"], geo, n_lat, hd)
        elif mixer == 1:
            s["t"] = pre
            s["o"] = gmlp_gate_fwd(f"l{i}_gm_gate", s["t"], gm_v_g[j:j + 1], gm_w_s[j], gm_b_s[j][:, :, None])
        else:
            s["t"] = pre
            s["o"] = sc_gate_fwd(f"l{i}_sc_gate", s["t"], s_cw[j], n_lat)
        lw = layer_w[i]
        s["y"] = mm_nn(f"l{i}_mix_out", s["o"], lw["w_out"])
        s["x1"], s["hf"] = resid_rms_mod(f"l{i}_norm_ffn", s["x"], s["y"], g1[i], norm_ffn_g[i:i + 1], sc2[i], sh2[i], n_lat)
        if nxt:
            n_in, _, n_up, n_down = nxt
            s["u"], (g_up,) = mm_nn(f"l{i}_ffn_up", s["hf"], lw["w_up"], out_parts=2, carry=("gather", [n_up]))
            s["z"], s["a"], (g_down,) = ffn_act_fwd(f"l{i}_ffn_act", s["u"], f_cw[i], cb_p[i], n_lat, carry=("gather", [n_down]))
            s["f"], (g_in,) = mm_nn(f"l{i}_ffn_down", s["a"], lw["w_down"], carry=("gather", [n_in]))
            layer_w.append(operands(g_in, rode[-1], g_up, g_down))
        else:
            s["u"] = mm_nn(f"l{i}_ffn_up", s["hf"], lw["w_up"], out_parts=2)
            s["z"], s["a"] = ffn_act_fwd(f"l{i}_ffn_act", s["u"], f_cw[i], cb_p[i], n_lat)
            s["f"] = mm_nn(f"l{i}_ffn_down", s["a"], lw["w_down"])
        prev = (s["x1"], s["f"], g2[i])
        saved.append(s)

    dx, loss_local, df, dg2 = loss_head("loss_head", prev[0], prev[1], prev[2], loss_target[0])
    loss = lax.psum(loss_local[0, 0], AXES)

    big = {}
    small = {}
    dmod = [None] * depth
    zeros_like_param = lambda p: [None] * p.shape[0]
    for nm in ("na_w_qkv", "na_w_o", "gm_w_in", "gm_w_out", "sc_w_in", "sc_w_out", "ffn_w_up", "ffn_w_down"):
        big[nm] = zeros_like_param(weights[nm])
    for nm in ("norm_mix_g", "norm_ffn_g", "ffn_conv_w", "ffn_conv_b", "na_q_g", "na_k_g", "na_rpb", "gm_v_g", "gm_w_s", "gm_b_s", "sc_conv_w"):
        small[nm] = zeros_like_param(weights[nm])
    landed, pending = {}, []

    def take(wanted):
        keys = [k for k in pending if wanted(k)]
        for k in keys:
            pending.remove(k)
        return keys

    def riding(keys, fn, *args, **kw):
        if not keys:
            return fn(*args, **kw)
        *res, got = fn(*args, carry=("scatter", [big[nm][idx] for nm, idx in keys]), **kw)
        landed.update(zip(keys, got))
        return res[0] if len(res) == 1 else tuple(res)

    for i in reversed(range(depth)):
        mixer, j = i % N_MIXERS, i // N_MIXERS
        nm_in, nm_out = mixer_weights[mixer]
        s, lw = saved[i], layer_w[i]
        da = riding(take(lambda k: k[0] in [m[1] for m in mixer_weights]), mm_nt, f"l{i}_b_ffn_down_x", df, lw["w_down"])
        dwd = mm_tn(f"l{i}_b_ffn_down_w", s["a"], df)
        big["ffn_w_down"][i] = dwd.reshape(4, nup_p, d)[:, :nup].reshape(N_DEV, fdim // N_DEV, d)
        du, dcw, dcb = riding(take(lambda k: k[0] in [m[0] for m in mixer_weights]), ffn_act_bwd,
                              f"l{i}_b_ffn_act", s["z"], s["u"], da, f_cw[i], n_lat)
        small["ffn_conv_w"][i] = dcw.transpose(1, 0, 2).reshape(3, N_DEV, nup_p)[:, :, :nup].reshape(3, 2 * fdim)
        small["ffn_conv_b"][i] = dcb.reshape(N_DEV, nup_p)[:, :nup].reshape(2 * fdim)
        dhf = riding([("ffn_w_down", i)], mm_nt, f"l{i}_b_ffn_up_x", du, lw["w_up"])
        big["ffn_w_up"][i] = riding(take(lambda k: True), mm_tn, f"l{i}_b_ffn_up_w", s["hf"], du, out_parts=N_DEV)
        pending.append(("ffn_w_up", i))
        dx1, dsh2, dsc2, dgf, dy, dg1 = rms_mod_bwd(f"l{i}_b_norm_ffn", s["x1"], dhf, norm_ffn_g[i:i + 1], sc2[i], dx, n_lat,
                                                    branch=(s["y"], g1[i]))
        small["norm_ffn_g"][i] = dgf[0]
        do = mm_nt(f"l{i}_b_mix_out_x", dy, lw["w_out"], out_dtype=BF16 if mixer == 0 else F32)
        big[nm_out][j] = mm_tn(f"l{i}_b_mix_out_w", s["o"], dy).reshape(N_DEV, -1, d)
        pending.append((nm_out, j))
        if mixer == 0:
            dq, dk, dv, dbias = riding(take(lambda k: True), na_attention_bwd,
                                       f"l{i}_b_attn", s["q"], s["k"], s["v"], s["bias"], do, geo, n_lat, hd)
            small["na_rpb"][j] = na_rpb_grad(f"l{i}_b_rpb", dbias, geo, n_heads)
            dt, dqg, dkg = qk_norm_bwd(f"l{i}_b_qk_norm", s["qkv"], dq, dk, dv, na_q_g[j:j + 1], na_k_g[j:j + 1], hd)
            small["na_q_g"][j], small["na_k_g"][j] = dqg[0], dkg[0]
        elif mixer == 1:
            dt, dvg, dws, dsv = gmlp_gate_bwd(f"l{i}_b_gm_gate", s["t"], do, gm_v_g[j:j + 1], gm_w_s[j], gm_b_s[j][:, :, None])
            groups, width = gm_w_s.shape[1], dsv.shape[1]
            group_of = np.zeros((width, LANE), np.float32)
            group_of[np.arange(width), np.arange(width) // (width // groups)] = 1.0
            dbs = _mm(f"l{i}_b_gm_bs", dsv, jnp.asarray(group_of), "nn", (GM_CHUNK, LANE), F32,
                      (GM_CHUNK, LANE, _tile(width, 2048, LANE)), exact=True)[:, :groups].T
            small["gm_v_g"][j], small["gm_w_s"][j], small["gm_b_s"][j] = dvg[0], dws, dbs
        else:
            dt, dscw = sc_gate_bwd(f"l{i}_b_sc_gate", s["t"], do, s_cw[j], n_lat)
            small["sc_conv_w"][j] = dscw
            dt = dt.transpose(1, 0, 2).reshape(t, 3 * d)
        big[nm_in][j] = mm_tn(f"l{i}_b_mix_in_w", s["h"], dt, out_parts=N_DEV)
        dh = riding([(nm_in, j)], mm_nt, f"l{i}_b_mix_in_x", dt, lw["w_in"])
        dmod_ffn = [dsh2, dsc2, dg2]
        if i > 0:
            dx, dsh1, dsc1, dgm, df, dg2 = rms_mod_bwd(f"l{i}_b_norm_mix", s["x"], dh, norm_mix_g[i:i + 1], sc1[i], dx1, n_lat,
                                                       branch=(saved[i - 1]["f"], g2[i - 1]))
        else:
            dx, dsh1, dsc1, dgm = riding(take(lambda k: True), rms_mod_bwd,
                                         f"l{i}_b_norm_mix", s["x"], dh, norm_mix_g[i:i + 1], sc1[i], dx1, n_lat)
        small["norm_mix_g"][i] = dgm[0]
        dmod[i] = jnp.stack([dsh1, dsc1, dg1] + dmod_ffn, axis=1)
    grad_x = dx[:n_lat][None]

    small = {nm: jnp.stack(v) for nm, v in small.items()}
    small["dmod"] = jnp.stack(dmod)
    pack = _Pack({nm: v.shape for nm, v in small.items()})
    (small_all,) = all_gather("gather_small", [pack.pack(small)])
    small_sum = pack.unpack(sum_parts("sum_small", small_all))
    dmod_all = pack.unpack(small_all, lead=(N_DEV,))["dmod"]
    dmod_ctx = small_sum["dmod"][:, 1].reshape(depth, N_MOD * d)
    grads = {nm: small_sum[nm] for nm in small if nm != "dmod"}
    grads["b_ada"] = (small_sum["dmod"][:, 0] + small_sum["dmod"][:, 1]).reshape(depth, N_MOD * d)
    my_cols = lambda a, width: lax.dynamic_slice_in_dim(a, me * width, width, axis=-1)
    grads["ffn_conv_w"] = my_cols(grads["ffn_conv_w"], nup)
    grads["sc_conv_w"] = my_cols(grads["sc_conv_w"], sc_conv_w.shape[-1])

    drows = jnp.concatenate([dmod_all[:, :, 0], dmod_all[:, :, 1]]).reshape(2 * N_DEV, depth, N_MOD * d)
    drows = my_cols(drows, ncol).reshape(2 * N_DEV, depth * ncol)
    cond2 = jnp.concatenate([c_all.reshape(N_DEV, d), jnp.broadcast_to(c_ctx[None], (N_DEV, d))])
    g_w_ada = mm_tn("ada_bwd_w", cond2, drows, out_parts=depth, out_dtype=F32, a_silu=True)
    dctx_rows = jnp.pad(my_cols(dmod_ctx, ncol).reshape(1, depth * ncol), ((0, 15), (0, 0)))
    dcc = mm_nt("ada_bwd_c", dctx_rows, w_ada)[0:1]
    (dcc_all,) = all_gather("gather_c_ctx", [jnp.pad(dcc.reshape(-1, LANE), ((0, (-d // LANE) % 8), (0, 0)))])
    dcc_sum = sum_parts("sum_c_ctx", dcc_all).reshape(-1)[:d]
    sig = jax.nn.sigmoid(c_ctx)
    grads["c_ctx"] = dcc_sum * (sig * (1.0 + c_ctx * (1.0 - sig)))

    assert not pending
    out = {nm: [None] * 4 for nm in names}
    per_big = {nm: [] for nm in big}
    for nm, idx in [(nm, idx) for nm in big for idx in range(len(big[nm]))]:
        w2, p = weights[nm][idx], landed[nm, idx]
        per_big[nm].append(adamw(f"adamw_{nm}{idx}", p.reshape(N_DEV, w2.shape[0], -1), w2, mom_m[nm][idx], mom_v[nm][idx]))
    for nm, res in per_big.items():
        out[nm] = [jnp.stack([r[q] for r in res]) for q in range(4)]
    res = [adamw(f"adamw_w_ada{i}", g_w_ada[i][None], w_ada[i], m_w_ada[i], v_w_ada[i]) for i in range(depth)]
    out["w_ada"] = [jnp.stack([r[q] for r in res]) for q in range(4)]
    small_names = [nm for nm in names if nm not in big and nm != "w_ada"]
    spack = _Pack({nm: weights[nm].shape for nm in small_names})
    flat = [spack.pack({nm: src[nm] for nm in small_names}) for src in (grads, weights, mom_m, mom_v)]
    res = adamw("adamw_small", flat[0][None], flat[1], flat[2], flat[3])
    res = [spack.unpack(r) for r in res]
    for nm in small_names:
        out[nm] = [grads[nm].reshape(weights[nm].shape)] + [res[q][nm] for q in range(1, 4)]

    return (loss, grad_x, *[out[nm][0] for nm in names], *[out[nm][1] for nm in names],
            *[out[nm][2] for nm in names], *[out[nm][3] for nm in names])
```
